```python
import math
import jax, jax.numpy as jnp
from jax import lax
import numpy as np

D_MODEL = 2048
BATCH = 4
SEQ = 2048
DEPTH = 1
DEC_BATCH = 128
DEC_SEQ = 4
PAST_LEN = 16384
PAGE_SIZE = 128

N_MEM = 256
EPS = 1e-6
D_INNER = D_MODEL
SSD_HEAD_DIM = 64
SSD_HEADS = D_INNER // SSD_HEAD_DIM
D_STATE = 128
N_GROUPS = 4
HEADS_PER_GROUP = SSD_HEADS // N_GROUPS
SSD_CONV = 4
CONV_DIM = D_INNER + 2 * N_GROUPS * D_STATE
SSD_CHUNK = 128
GLA_HEADS = 4
GLA_KEY_DIM = D_MODEL // 2
GLA_VAL_DIM = D_MODEL
GLA_HEAD_K = GLA_KEY_DIM // GLA_HEADS
GLA_HEAD_V = GLA_VAL_DIM // GLA_HEADS
GATE_RANK = 16
GATE_TAU = 16.0
GLA_CHUNK = 64
CROSS_HEADS = 4
CROSS_HEAD_DIM = D_MODEL // CROSS_HEADS
FFN_DIM = 5632
FFN_CONV = 3
IN_SIZES = (D_INNER, CONV_DIM, SSD_HEADS, GLA_KEY_DIM, GLA_KEY_DIM, GLA_VAL_DIM, GLA_VAL_DIM, GATE_RANK, D_MODEL, D_MODEL)
IN_DIM = D_INNER + CONV_DIM + SSD_HEADS + 2 * GLA_KEY_DIM + 2 * GLA_VAL_DIM + GATE_RANK + 2 * D_MODEL

kernel_name = "hybrid_ssd_gla_memx_convffn_step"


def _rmsnorm(x, g):
    x32 = x.astype(jnp.float32)
    y = x32 * lax.rsqrt(jnp.mean(x32 * x32, axis=-1, keepdims=True) + EPS)
    return (y * g.astype(jnp.float32)).astype(x.dtype)


def _split(t, sizes):
    idx = np.cumsum(np.array(sizes))[:-1].tolist()
    return jnp.split(t, idx, axis=-1)


def _causal_dwconv(u, prev, w, b):
    L = u.shape[1]
    full = jnp.concatenate([prev.astype(u.dtype), u], axis=1)
    out = b
    for k in range(w.shape[0]):
        out = out + full[:, k:k + L] * w[k]
    return out, full[:, L:]


def _chunk_len(L, c):
    return c if L % c == 0 else L


def _to_chunks(t, c):
    B, L = t.shape[:2]
    return jnp.moveaxis(t.reshape((B, L // c, c) + t.shape[2:]), 1, 0)


def _from_chunks(t):
    nc, B, c = t.shape[:3]
    return jnp.moveaxis(t, 0, 1).reshape((B, nc * c) + t.shape[3:])


def _ssd_scan(x, dt, A, Bm, Cm, S0):
    Bsz, L = x.shape[:2]
    c = _chunk_len(L, SSD_CHUNK)
    xg = x.reshape(Bsz, L, N_GROUPS, HEADS_PER_GROUP, SSD_HEAD_DIM)
    dtg = dt.reshape(Bsz, L, N_GROUPS, HEADS_PER_GROUP)
    Ag = A.reshape(N_GROUPS, HEADS_PER_GROUP)
    S = S0.reshape(Bsz, N_GROUPS, HEADS_PER_GROUP, SSD_HEAD_DIM, D_STATE)
    mask = jnp.tril(jnp.ones((c, c), dtype=bool))

    def step(S, inp):
        xc, dtc, Bc, Cc = inp
        acum = jnp.cumsum(dtc * Ag, axis=1)
        acum_t = jnp.moveaxis(acum, 1, -1)
        diff = acum_t[..., :, None] - acum_t[..., None, :]
        decay = jnp.exp(jnp.where(mask, diff, -jnp.inf))
        cb = jnp.einsum('bign,bjgn->bgij', Cc, Bc)
        m = cb[:, :, None] * decay * jnp.moveaxis(dtc, 1, -1)[..., None, :]
        y = jnp.einsum('bgrij,bjgrp->bigrp', m, xc)
        y = y + jnp.einsum('bign,bgrpn->bigrp', Cc, S) * jnp.exp(acum)[..., None]
        last = acum[:, -1]
        w = jnp.exp(last[:, None] - acum) * dtc
        S = jnp.exp(last)[..., None, None] * S + jnp.einsum('bjgr,bjgn,bjgrp->bgrpn', w, Bc, xc)
        return S, y

    S, ys = lax.scan(step, S, (_to_chunks(xg, c), _to_chunks(dtg, c), _to_chunks(Bm, c), _to_chunks(Cm, c)))
    y = _from_chunks(ys).reshape(Bsz, L, SSD_HEADS, SSD_HEAD_DIM)
    return y, S.reshape(Bsz, SSD_HEADS, SSD_HEAD_DIM, D_STATE)


def _gla_scan(q, k, v, g, S0):
    L = q.shape[1]
    c = _chunk_len(L, GLA_CHUNK)
    mask = jnp.tril(jnp.ones((c, c), dtype=bool))

    def step(S, inp):
        qc, kc, vc, gc = inp
        b = jnp.cumsum(gc, axis=1)
        qe = qc * jnp.exp(b)
        ke = kc * jnp.exp(-b)
        att = jnp.where(mask, jnp.einsum('bihk,bjhk->bhij', qe, ke), 0.0)
        o = jnp.einsum('bhij,bjhv->bihv', att, vc) + jnp.einsum('bihk,bhkv->bihv', qe, S)
        last = b[:, -1]
        S = jnp.exp(last)[..., None] * S + jnp.einsum('bjhk,bjhv->bhkv', kc * jnp.exp(last[:, None] - b), vc)
        return S, o

    S, os_ = lax.scan(step, S0, (_to_chunks(q, c), _to_chunks(k, c), _to_chunks(v, c), _to_chunks(g, c)))
    return _from_chunks(os_), S


def _mem_kv(mem, norm_mem, w_ck, w_cv):
    Bsz = mem.shape[0]
    mn = _rmsnorm(mem, norm_mem)
    k = (mn @ w_ck).reshape(Bsz, N_MEM, CROSS_HEADS, CROSS_HEAD_DIM)
    v = (mn @ w_cv).reshape(Bsz, N_MEM, CROSS_HEADS, CROSS_HEAD_DIM)
    return k, v


def _cross_attend(hn, mem_k, mem_v, w_cq, w_co):
    Bsz, L, _ = hn.shape
    q = (hn @ w_cq).reshape(Bsz, L, CROSS_HEADS, CROSS_HEAD_DIM).astype(jnp.float32)
    s = jnp.einsum('blhd,bmhd->bhlm', q, mem_k.astype(jnp.float32)) * (CROSS_HEAD_DIM ** -0.5)
    pr = jax.nn.softmax(s, axis=-1)
    o = jnp.einsum('bhlm,bmhd->blhd', pr, mem_v.astype(jnp.float32)).astype(hn.dtype)
    return o.reshape(Bsz, L, D_MODEL) @ w_co


def _layer(x, mem_k, mem_v, ssd_conv, ssd_state, gla_state, ffn_conv, p):
    f32 = jnp.float32
    Bsz, L, _ = x.shape
    dtype = x.dtype
    xn = _rmsnorm(x, p['norm_mix'])
    z, xbc, dt_raw, q, k, v, r, g_lr, gate_a, gate_b = _split(xn @ p['w_in'], IN_SIZES)

    xbc, ssd_conv_new = _causal_dwconv(xbc, ssd_conv, p['ssd_conv_w'], p['ssd_conv_b'])
    xbc = jax.nn.silu(xbc)
    xs, Bm, Cm = _split(xbc, (D_INNER, N_GROUPS * D_STATE, N_GROUPS * D_STATE))
    dt = jax.nn.softplus(dt_raw.astype(f32) + p['ssd_dt_bias'].astype(f32))
    A = -jnp.exp(p['ssd_A_log'].astype(f32))
    xs_h = xs.astype(f32).reshape(Bsz, L, SSD_HEADS, SSD_HEAD_DIM)
    y_ssd, ssd_state_new = _ssd_scan(xs_h, dt, A,
                                     Bm.astype(f32).reshape(Bsz, L, N_GROUPS, D_STATE),
                                     Cm.astype(f32).reshape(Bsz, L, N_GROUPS, D_STATE),
                                     ssd_state.astype(f32))
    y_ssd = y_ssd + p['ssd_D'].astype(f32)[:, None] * xs_h
    u = y_ssd.reshape(Bsz, L, N_GROUPS, D_INNER // N_GROUPS) * jax.nn.silu(z.astype(f32)).reshape(Bsz, L, N_GROUPS, D_INNER // N_GROUPS)
    u = u * lax.rsqrt(jnp.mean(u * u, axis=-1, keepdims=True) + EPS)
    u = (u.reshape(Bsz, L, D_INNER) * p['ssd_norm'].astype(f32)).astype(dtype)
    branch_a = u @ p['w_ssd_out']

    qh = q.astype(f32).reshape(Bsz, L, GLA_HEADS, GLA_HEAD_K) * (GLA_HEAD_K ** -0.5)
    kh = k.astype(f32).reshape(Bsz, L, GLA_HEADS, GLA_HEAD_K)
    vh = v.astype(f32).reshape(Bsz, L, GLA_HEADS, GLA_HEAD_V)
    glog = jax.nn.log_sigmoid((g_lr @ p['w_gla_gate']).astype(f32) + p['b_gla_gate'].astype(f32)) / GATE_TAU
    glog = glog.reshape(Bsz, L, GLA_HEADS, GLA_HEAD_K)
    o, gla_state_new = _gla_scan(qh, kh, vh, glog, gla_state.astype(f32))
    o = o * lax.rsqrt(jnp.mean(o * o, axis=-1, keepdims=True) + EPS) * p['gla_norm'].astype(f32)
    o = (o.reshape(Bsz, L, GLA_VAL_DIM) * jax.nn.silu(r.astype(f32))).astype(dtype)
    branch_b = o @ p['w_gla_out']

    merged = jax.nn.sigmoid(gate_a) * branch_a + jax.nn.sigmoid(gate_b) * branch_b
    h = x + merged @ p['w_mix_out']

    h = h + _cross_attend(_rmsnorm(h, p['norm_cross']), mem_k, mem_v, p['w_cq'], p['w_co'])

    up = _rmsnorm(h, p['norm_ffn']) @ p['w_up']
    up, ffn_conv_new = _causal_dwconv(up, ffn_conv, p['ffn_conv_w'], p['ffn_conv_b'])
    a, gt = jnp.split(up, 2, axis=-1)
    h = h + (jax.nn.silu(gt) * a) @ p['w_down']
    return h, ssd_conv_new, ssd_state_new, gla_state_new, ffn_conv_new


def setup_inputs(seed: int = 0) -> dict:
    key = jax.random.key(seed)
    ks = iter(list(jax.random.split(key, 48)))
    f32 = jnp.float32

    def nrm(shape, scale=1.0):
        return scale * jax.random.normal(next(ks), shape, f32)

    def dense(fi, fo):
        return jax.random.normal(next(ks), (fi, fo), f32) * (fi ** -0.5)

    def gain(n):
        return 1.0 + 0.02 * jax.random.normal(next(ks), (n,), f32)

    d = {}
    d['x_prompt'] = nrm((BATCH, SEQ, D_MODEL))
    d['x_sample'] = nrm((DEC_BATCH, DEC_SEQ, D_MODEL))
    d['cache_mem_k'] = nrm((DEC_BATCH, N_MEM, CROSS_HEADS, CROSS_HEAD_DIM))
    d['cache_mem_v'] = nrm((DEC_BATCH, N_MEM, CROSS_HEADS, CROSS_HEAD_DIM))
    d['state_ssd_conv'] = nrm((DEC_BATCH, SSD_CONV - 1, CONV_DIM))
    d['state_ssd'] = nrm((DEC_BATCH, SSD_HEADS, SSD_HEAD_DIM, D_STATE), 0.5)
    d['state_gla'] = nrm((DEC_BATCH, GLA_HEADS, GLA_HEAD_K, GLA_HEAD_V), 0.5)
    d['state_ffn_conv'] = nrm((DEC_BATCH, FFN_CONV - 1, 2 * FFN_DIM))
    d['mem_prompt'] = nrm((BATCH, N_MEM, D_MODEL))
    d['norm_mix'] = gain(D_MODEL)
    d['w_in'] = dense(D_MODEL, IN_DIM)
    d['ssd_conv_w'] = nrm((SSD_CONV, CONV_DIM), SSD_CONV ** -0.5)
    d['ssd_conv_b'] = nrm((CONV_DIM,), 0.02)
    dt0 = jnp.exp(jax.random.uniform(next(ks), (SSD_HEADS,), f32, math.log(1e-3), math.log(1e-1)))
    d['ssd_dt_bias'] = dt0 + jnp.log(-jnp.expm1(-dt0))
    d['ssd_A_log'] = jnp.log(jax.random.uniform(next(ks), (SSD_HEADS,), f32, 1.0, 16.0))
    d['ssd_D'] = 1.0 + nrm((SSD_HEADS,), 0.1)
    d['ssd_norm'] = gain(D_INNER)
    d['w_ssd_out'] = dense(D_INNER, D_MODEL)
    d['w_gla_gate'] = dense(GATE_RANK, GLA_KEY_DIM)
    d['b_gla_gate'] = nrm((GLA_KEY_DIM,), 0.1)
    d['gla_norm'] = gain(GLA_HEAD_V)
    d['w_gla_out'] = dense(GLA_VAL_DIM, D_MODEL)
    d['w_mix_out'] = dense(D_MODEL, D_MODEL)
    d['norm_cross'] = gain(D_MODEL)
    d['norm_mem'] = gain(D_MODEL)
    d['w_cq'] = dense(D_MODEL, D_MODEL)
    d['w_ck'] = dense(D_MODEL, D_MODEL)
    d['w_cv'] = dense(D_MODEL, D_MODEL)
    d['w_co'] = dense(D_MODEL, D_MODEL)
    d['norm_ffn'] = gain(D_MODEL)
    d['w_up'] = dense(D_MODEL, 2 * FFN_DIM)
    d['ffn_conv_w'] = nrm((FFN_CONV, 2 * FFN_DIM), FFN_CONV ** -0.5)
    d['ffn_conv_b'] = nrm((2 * FFN_DIM,), 0.02)
    d['w_down'] = dense(FFN_DIM, D_MODEL)
    d['norm_final'] = gain(D_MODEL)
    return d


def reference(x_prompt, x_sample, cache_mem_k, cache_mem_v, state_ssd_conv, state_ssd, state_gla, state_ffn_conv,
              mem_prompt, norm_mix, w_in, ssd_conv_w, ssd_conv_b, ssd_dt_bias, ssd_A_log, ssd_D, ssd_norm, w_ssd_out,
              w_gla_gate, b_gla_gate, gla_norm, w_gla_out, w_mix_out, norm_cross, norm_mem, w_cq, w_ck, w_cv, w_co,
              norm_ffn, w_up, ffn_conv_w, ffn_conv_b, w_down, norm_final):
    p = dict(norm_mix=norm_mix, w_in=w_in, ssd_conv_w=ssd_conv_w, ssd_conv_b=ssd_conv_b, ssd_dt_bias=ssd_dt_bias,
             ssd_A_log=ssd_A_log, ssd_D=ssd_D, ssd_norm=ssd_norm, w_ssd_out=w_ssd_out, w_gla_gate=w_gla_gate,
             b_gla_gate=b_gla_gate, gla_norm=gla_norm, w_gla_out=w_gla_out, w_mix_out=w_mix_out,
             norm_cross=norm_cross, w_cq=w_cq, w_co=w_co, norm_ffn=norm_ffn, w_up=w_up,
             ffn_conv_w=ffn_conv_w, ffn_conv_b=ffn_conv_b, w_down=w_down)
    nb = x_prompt.shape[0]
    dtype = x_prompt.dtype
    p_mem_k, p_mem_v = _mem_kv(mem_prompt, norm_mem, w_ck, w_cv)
    hp, p_ssd_conv, p_ssd, p_gla, p_ffn_conv = _layer(
        x_prompt, p_mem_k, p_mem_v,
        jnp.zeros((nb, SSD_CONV - 1, CONV_DIM), dtype),
        jnp.zeros((nb, SSD_HEADS, SSD_HEAD_DIM, D_STATE), jnp.float32),
        jnp.zeros((nb, GLA_HEADS, GLA_HEAD_K, GLA_HEAD_V), jnp.float32),
        jnp.zeros((nb, FFN_CONV - 1, 2 * FFN_DIM), dtype), p)
    y_prompt = _rmsnorm(hp, norm_final)
    hs, s_ssd_conv, s_ssd, s_gla, s_ffn_conv = _layer(
        x_sample, cache_mem_k, cache_mem_v, state_ssd_conv, state_ssd, state_gla, state_ffn_conv, p)
    y_sample = _rmsnorm(hs, norm_final)
    return (y_prompt, y_sample, p_ssd_conv, p_ssd, p_gla, p_ffn_conv, p_mem_k, p_mem_v,
            s_ssd_conv, s_ssd, s_gla, s_ffn_conv)
```

```python
import functools

import jax
import jax.numpy as jnp
from jax import lax
from jax.experimental import pallas as pl
from jax.experimental.pallas import tpu as pltpu

F32 = jnp.float32
BF16 = jnp.bfloat16
HI = lax.Precision.HIGHEST
EPS = 1e-6
NEG_BIG = -1e30

D_MODEL = 2048
SSD_HEAD_DIM = 64
SSD_HEADS = 32
D_STATE = 128
N_GROUPS = 4
GROUP_W = D_MODEL // N_GROUPS
BC_W = N_GROUPS * D_STATE
CONV_DIM = D_MODEL + 2 * BC_W
SSD_CONV = 4
SSD_CHUNK = 128
GLA_HEADS = 4
GLA_KEY_DIM = 1024
GLA_HEAD_K = 256
GLA_HEAD_V = 512
GATE_RANK = 16
GATE_TAU = 16.0
GLA_CHUNK = 64
CROSS_HEADS = 4
CROSS_HEAD_DIM = 512
FFN_CONV = 3
LANE = 128

COL_Z = 0
COL_GA = 2048
COL_GB = 4096
COL_V = 6144
COL_R = 8192
COL_XBC = 10240
COL_Q = 13312
COL_K = 14336
COL_DT = 15360
COL_GLR = 15488
NPROJ = 15872

VMEM_LIMIT = 56 * 1024 * 1024


def _cp(*sem):
    return pltpu.CompilerParams(dimension_semantics=sem, vmem_limit_bytes=VMEM_LIMIT)


def _dot(a, b, prec=None):
    return jnp.dot(a, b, preferred_element_type=F32, precision=prec)


def _dot_nt(a, b):
    return lax.dot_general(a, b, (((1,), (1,)), ((), ())), preferred_element_type=F32)


def _sigmoid(x):
    return 1.0 / (1.0 + jnp.exp(-x))


def _silu(x):
    return x * _sigmoid(x)


def _softplus(x):
    return jnp.maximum(x, 0.0) + jnp.log(1.0 + jnp.exp(-jnp.abs(x)))


def _rms(x, g):
    ms = jnp.mean(x * x, axis=-1, keepdims=True)
    return x * lax.rsqrt(ms + EPS) * g


def _rmsnorm_kernel(x_ref, g_ref, o_ref):
    o_ref[...] = _rms(x_ref[...], g_ref[...]).astype(o_ref.dtype)


def rmsnorm_cast(x2, g):
    m, d = x2.shape
    tm = min(m, 512)
    return pl.pallas_call(
        _rmsnorm_kernel,
        grid=(m // tm,),
        in_specs=[pl.BlockSpec((tm, d), lambda i: (i, 0)), pl.BlockSpec((1, d), lambda i: (0, 0))],
        out_specs=pl.BlockSpec((tm, d), lambda i: (i, 0)),
        out_shape=jax.ShapeDtypeStruct((m, d), BF16),
        compiler_params=_cp("parallel"),
        name="rmsnorm_cast",
    )(x2, g.reshape(1, d))


def _mm_kernel(a_ref, w_ref, o_ref):
    o_ref[...] = _dot(a_ref[...], w_ref[...]).astype(o_ref.dtype)


def matmul(a, w, out_dtype=F32, tn=512, name="matmul"):
    m, k = a.shape
    n = w.shape[1]
    tm = min(m, 1024)
    return pl.pallas_call(
        _mm_kernel,
        grid=(m // tm, n // tn),
        in_specs=[pl.BlockSpec((tm, k), lambda i, j: (i, 0)), pl.BlockSpec((k, tn), lambda i, j: (0, j))],
        out_specs=pl.BlockSpec((tm, tn), lambda i, j: (i, j)),
        out_shape=jax.ShapeDtypeStruct((m, n), out_dtype),
        compiler_params=_cp("parallel", "arbitrary"),
        name=name,
    )(a, w)


def _mm_res_norm_kernel(a_ref, w_ref, res_ref, g_ref, *rest, nk, emit_h):
    if emit_h:
        h_ref, n_ref, acc_ref = rest
    else:
        n_ref, acc_ref = rest
    k = pl.program_id(1)

    @pl.when(k == 0)
    def _():
        acc_ref[...] = jnp.zeros_like(acc_ref)

    acc_ref[...] += _dot(a_ref[...], w_ref[...])

    @pl.when(k == nk - 1)
    def _():
        h = res_ref[...] + acc_ref[...]
        if emit_h:
            h_ref[...] = h
        n_ref[...] = _rms(h, g_ref[...]).astype(n_ref.dtype)


def mm_res_norm(a, w, res, g, emit_h, norm_dtype, name):
    m, kdim = a.shape
    n = w.shape[1]
    tm = min(m, 512)
    tk = 2048 if kdim == 2048 else 512
    nk = kdim // tk
    out_shape = [jax.ShapeDtypeStruct((m, n), norm_dtype)]
    out_specs = [pl.BlockSpec((tm, n), lambda i, k: (i, 0))]
    if emit_h:
        out_shape = [jax.ShapeDtypeStruct((m, n), F32)] + out_shape
        out_specs = [pl.BlockSpec((tm, n), lambda i, k: (i, 0))] + out_specs
    outs = pl.pallas_call(
        functools.partial(_mm_res_norm_kernel, nk=nk, emit_h=emit_h),
        grid=(m // tm, nk),
        in_specs=[
            pl.BlockSpec((tm, tk), lambda i, k: (i, k)),
            pl.BlockSpec((tk, n), lambda i, k: (k, 0)),
            pl.BlockSpec((tm, n), lambda i, k: (i, 0)),
            pl.BlockSpec((1, n), lambda i, k: (0, 0)),
        ],
        out_specs=out_specs,
        out_shape=out_shape,
        scratch_shapes=[pltpu.VMEM((tm, n), F32)],
        compiler_params=_cp("parallel", "arbitrary"),
        name=name,
    )(a, w, res, g.reshape(1, n))
    return outs if emit_h else outs[0]


def _merge_kernel(u_ref, o_ref, wa_ref, wb_ref, ga_ref, gb_ref, out_ref):
    a = _dot(u_ref[...], wa_ref[...])
    b = _dot(o_ref[...], wb_ref[...])
    out_ref[...] = (_sigmoid(ga_ref[...]) * a + _sigmoid(gb_ref[...]) * b).astype(out_ref.dtype)


def merge_branches(u, o, wa, wb, proj):
    m, d = u.shape
    tm, tn = min(m, 1024), 512
    return pl.pallas_call(
        _merge_kernel,
        grid=(m // tm, d // tn),
        in_specs=[
            pl.BlockSpec((tm, d), lambda i, j: (i, 0)),
            pl.BlockSpec((tm, d), lambda i, j: (i, 0)),
            pl.BlockSpec((d, tn), lambda i, j: (0, j)),
            pl.BlockSpec((d, tn), lambda i, j: (0, j)),
            pl.BlockSpec((tm, tn), lambda i, j: (i, COL_GA // tn + j)),
            pl.BlockSpec((tm, tn), lambda i, j: (i, COL_GB // tn + j)),
        ],
        out_specs=pl.BlockSpec((tm, tn), lambda i, j: (i, j)),
        out_shape=jax.ShapeDtypeStruct((m, d), BF16),
        compiler_params=_cp("parallel", "arbitrary"),
        name="merge_branches",
    )(u, o, wa, wb, proj, proj)


CONV_PAD = 8


def _conv_kernel(*refs, taps, tl, nstreams, swiglu):
    ins = refs[: 4 * nstreams]
    out_ref = refs[4 * nstreams]
    ns_refs = refs[4 * nstreams + 1 : 4 * nstreams + 1 + nstreams]
    bufs = refs[4 * nstreams + 1 + nstreams :]
    l = pl.program_id(2)
    nl = pl.num_programs(2)
    lo = CONV_PAD - (taps - 1)
    vals = []
    for s in range(nstreams):
        u_ref, st_ref, w_ref, b_ref = ins[4 * s : 4 * s + 4]
        buf = bufs[s]

        @pl.when(l == 0)
        def _():
            buf[:, lo:CONV_PAD, :] = st_ref[...]

        buf[:, CONV_PAD : CONV_PAD + tl, :] = u_ref[...]
        acc = b_ref[...]
        for k in range(taps):
            acc = acc + buf[:, lo + k : lo + k + tl, :] * w_ref[k : k + 1, :]
        vals.append(acc)
        tail = buf[:, CONV_PAD + tl - (taps - 1) : CONV_PAD + tl, :]
        buf[:, lo:CONV_PAD, :] = tail

        @pl.when(l == nl - 1)
        def _():
            ns_refs[s][...] = tail

    if swiglu:
        out_ref[...] = (_silu(vals[1]) * vals[0]).astype(out_ref.dtype)
    else:
        out_ref[...] = _silu(vals[0]).astype(out_ref.dtype)


def causal_conv(u3, col_offs, width, states, ws, bs, taps, swiglu, out_dtype, name):
    nb, seq, _ = u3.shape
    tc = 512
    tl = min(seq, 512)
    sb = 1 if seq >= 8 else 16
    nstreams = len(col_offs)
    in_specs, args = [], []
    for s in range(nstreams):
        cb = col_offs[s] // tc
        in_specs += [
            pl.BlockSpec((sb, tl, tc), lambda b, c, l, cb=cb: (b, l, cb + c)),
            pl.BlockSpec((sb, taps - 1, tc), lambda b, c, l: (b, 0, c)),
            pl.BlockSpec((taps, tc), lambda b, c, l: (0, c)),
            pl.BlockSpec((1, tc), lambda b, c, l: (0, c)),
        ]
        args += [u3, states[s], ws[s], bs[s].reshape(1, width)]
    out_specs = [pl.BlockSpec((sb, tl, tc), lambda b, c, l: (b, l, c))]
    out_shape = [jax.ShapeDtypeStruct((nb, seq, width), out_dtype)]
    for s in range(nstreams):
        out_specs.append(pl.BlockSpec((sb, taps - 1, tc), lambda b, c, l: (b, 0, c)))
        out_shape.append(jax.ShapeDtypeStruct((nb, taps - 1, width), F32))
    return pl.pallas_call(
        functools.partial(_conv_kernel, taps=taps, tl=tl, nstreams=nstreams, swiglu=swiglu),
        grid=(nb // sb, width // tc, seq // tl),
        in_specs=in_specs,
        out_specs=out_specs,
        out_shape=out_shape,
        scratch_shapes=[pltpu.VMEM((sb, CONV_PAD + tl, tc), F32) for _ in range(nstreams)],
        compiler_params=_cp("parallel", "parallel", "arbitrary"),
        name=name,
    )(*args)


def _ssd_gate_norm(y, z, nrm):
    ug = y * _silu(z)
    outs = []
    for g in range(N_GROUPS):
        ugg = ug[:, g * GROUP_W : (g + 1) * GROUP_W]
        ms = jnp.mean(ugg * ugg, axis=-1, keepdims=True)
        outs.append(ugg * lax.rsqrt(ms + EPS))
    return jnp.concatenate(outs, axis=1) * nrm


def _gla_out_norm(o, r, nrm):
    outs = []
    for h in range(GLA_HEADS):
        oh = o[:, h * GLA_HEAD_V : (h + 1) * GLA_HEAD_V]
        rh = r[:, h * GLA_HEAD_V : (h + 1) * GLA_HEAD_V]
        outs.append(_rms(oh, nrm) * _silu(rh))
    return jnp.concatenate(outs, axis=1)


def _ssd_scan_kernel(xs_ref, b_ref, c_ref, dt_ref, z_ref, dtb_ref, alog_ref, dexp_ref, nrm_ref, e_ref,
                     u_ref, sout_ref, st_ref, *, nc):
    c = pl.program_id(1)
    q = SSD_CHUNK

    @pl.when(c == 0)
    def _():
        st_ref[...] = jnp.zeros_like(st_ref)

    xs = xs_ref[0]
    bm = b_ref[0]
    cm = c_ref[0]
    dt = _softplus(dt_ref[0] + dtb_ref[...])
    a = dt * (-jnp.exp(alog_ref[...]))
    row = lax.broadcasted_iota(jnp.int32, (q, q), 0)
    col = lax.broadcasted_iota(jnp.int32, (q, q), 1)
    tril = row >= col
    acum = _dot(tril.astype(F32), a, HI)
    acum_t = acum.T
    dt_t = dt.T
    last = acum[q - 1 : q, :]
    e_mat = e_ref[...]
    eexp = _dot(jnp.exp(acum), e_mat, HI)
    wexp = _dot(jnp.exp(last - acum) * dt, e_mat, HI)
    s_bf = st_ref[...].astype(BF16)
    cb16 = cm.astype(BF16)
    bb16 = bm.astype(BF16)
    x16 = xs.astype(BF16)
    xw16 = (xs * wexp).astype(BF16)
    lane_lo = lax.broadcasted_iota(jnp.int32, (q, LANE), 1) < SSD_HEAD_DIM
    ys = []
    for g in range(N_GROUPS):
        cg = cb16[:, g * D_STATE : (g + 1) * D_STATE]
        bg = bb16[:, g * D_STATE : (g + 1) * D_STATE]
        cb = _dot_nt(cg, bg)
        yoff = _dot(cg, s_bf[:, g * GROUP_W : (g + 1) * GROUP_W])
        pieces = []
        for pr in range(GROUP_W // LANE):
            h0 = g * (SSD_HEADS // N_GROUPS) + 2 * pr
            xp = x16[:, h0 * SSD_HEAD_DIM : h0 * SSD_HEAD_DIM + LANE]
            yh = []
            for h in (h0, h0 + 1):
                diff = acum[:, h : h + 1] - acum_t[h : h + 1, :]
                dec = jnp.exp(jnp.where(tril, diff, NEG_BIG))
                m = (cb * dec * dt_t[h : h + 1, :]).astype(BF16)
                yh.append(_dot(m, xp))
            pieces.append(jnp.where(lane_lo, yh[0], yh[1]))
        sl = slice(g * GROUP_W, (g + 1) * GROUP_W)
        ys.append(jnp.concatenate(pieces, axis=1) + yoff * eexp[:, sl])
        bg_t = bm[:, g * D_STATE : (g + 1) * D_STATE].T.astype(BF16)
        upd = _dot(bg_t, xw16[:, sl])
        st_ref[:, sl] = eexp[q - 1 : q, sl] * st_ref[:, sl] + upd
    y = jnp.concatenate(ys, axis=1) + dexp_ref[...] * xs
    u_ref[0] = _ssd_gate_norm(y, z_ref[0], nrm_ref[...]).astype(u_ref.dtype)

    @pl.when(c == nc - 1)
    def _():
        sout_ref[0] = st_ref[...].T


def ssd_scan(xc3, proj3, p):
    nb, seq, _ = xc3.shape
    nc = seq // SSD_CHUNK
    q = SSD_CHUNK
    vec = lambda n: pl.BlockSpec((1, n), lambda b, c: (0, 0))
    u, s_out = pl.pallas_call(
        functools.partial(_ssd_scan_kernel, nc=nc),
        grid=(nb, nc),
        in_specs=[
            pl.BlockSpec((1, q, D_MODEL), lambda b, c: (b, c, 0)),
            pl.BlockSpec((1, q, BC_W), lambda b, c: (b, c, D_MODEL // BC_W)),
            pl.BlockSpec((1, q, BC_W), lambda b, c: (b, c, D_MODEL // BC_W + 1)),
            pl.BlockSpec((1, q, LANE), lambda b, c: (b, c, COL_DT // LANE)),
            pl.BlockSpec((1, q, D_MODEL), lambda b, c: (b, c, COL_Z // D_MODEL)),
            vec(LANE), vec(LANE), vec(D_MODEL), vec(D_MODEL),
            pl.BlockSpec((LANE, D_MODEL), lambda b, c: (0, 0)),
        ],
        out_specs=[
            pl.BlockSpec((1, q, D_MODEL), lambda b, c: (b, c, 0)),
            pl.BlockSpec((1, D_MODEL, D_STATE), lambda b, c: (b, 0, 0)),
        ],
        out_shape=[
            jax.ShapeDtypeStruct((nb, seq, D_MODEL), BF16),
            jax.ShapeDtypeStruct((nb, D_MODEL, D_STATE), F32),
        ],
        scratch_shapes=[pltpu.VMEM((D_STATE, D_MODEL), F32)],
        compiler_params=_cp("parallel", "arbitrary"),
        name="ssd_scan",
    )(xc3, xc3, xc3, proj3, proj3, p["dt_bias"], p["a_log"], p["d_exp"], p["ssd_norm"], p["e_head"])
    return u.reshape(nb * seq, D_MODEL), s_out.reshape(nb, SSD_HEADS, SSD_HEAD_DIM, D_STATE)


def _gla_gate_log(glr, wg, bg):
    x = _dot(glr.astype(BF16), wg) + bg
    return -_softplus(-x) / GATE_TAU


def _gla_scan_kernel(q_ref, k_ref, v_ref, r_ref, glr_ref, wg_ref, bg_ref, nrm_ref,
                     o_ref, sout_ref, st_ref, *, nc):
    c = pl.program_id(1)
    q = GLA_CHUNK

    @pl.when(c == 0)
    def _():
        st_ref[...] = jnp.zeros_like(st_ref)

    glog = _gla_gate_log(glr_ref[0], wg_ref[...], bg_ref[...])
    row = lax.broadcasted_iota(jnp.int32, (q, q), 0)
    col = lax.broadcasted_iota(jnp.int32, (q, q), 1)
    tril = row >= col
    bc = _dot(tril.astype(F32), glog, HI)
    last = bc[q - 1 : q, :]
    kk = k_ref[0]
    qe = q_ref[0] * (GLA_HEAD_K ** -0.5) * jnp.exp(bc)
    ke = kk * jnp.exp(-bc)
    kd = kk * jnp.exp(last - bc)
    elast = jnp.exp(last)
    v16 = v_ref[0].astype(BF16)
    zeros_v = jnp.zeros((q, GLA_HEAD_V), BF16)
    outs = []
    for h in range(GLA_HEADS):
        ks = slice(h * GLA_HEAD_K, (h + 1) * GLA_HEAD_K)
        vs = slice(h * GLA_HEAD_V, (h + 1) * GLA_HEAD_V)
        qh = qe[:, ks].astype(BF16)
        kh = ke[:, ks].astype(BF16)
        att = jnp.where(tril, _dot_nt(qh, kh), 0.0)
        s_h = st_ref[ks, :]
        outs.append(_dot(att.astype(BF16), v16[:, vs]) + _dot(qh, s_h.astype(BF16)))
        xt = jnp.concatenate([kd[:, ks], jnp.broadcast_to(elast[:, ks], (q, GLA_HEAD_K))], axis=0).T
        v2 = jnp.concatenate([v16[:, vs], zeros_v], axis=0)
        st_ref[ks, :] = xt[:, q : q + 1] * s_h + _dot(xt.astype(BF16), v2)
    o = jnp.concatenate(outs, axis=1)
    o_ref[0] = _gla_out_norm(o, r_ref[0], nrm_ref[...]).astype(o_ref.dtype)

    @pl.when(c == nc - 1)
    def _():
        sout_ref[0] = st_ref[...]


def gla_scan(proj3, p):
    nb, seq, _ = proj3.shape
    q = GLA_CHUNK
    nc = seq // q
    o, s_out = pl.pallas_call(
        functools.partial(_gla_scan_kernel, nc=nc),
        grid=(nb, nc),
        in_specs=[
            pl.BlockSpec((1, q, GLA_KEY_DIM), lambda b, c: (b, c, COL_Q // GLA_KEY_DIM)),
            pl.BlockSpec((1, q, GLA_KEY_DIM), lambda b, c: (b, c, COL_K // GLA_KEY_DIM)),
            pl.BlockSpec((1, q, D_MODEL), lambda b, c: (b, c, COL_V // D_MODEL)),
            pl.BlockSpec((1, q, D_MODEL), lambda b, c: (b, c, COL_R // D_MODEL)),
            pl.BlockSpec((1, q, LANE), lambda b, c: (b, c, COL_GLR // LANE)),
            pl.BlockSpec((LANE, GLA_KEY_DIM), lambda b, c: (0, 0)),
            pl.BlockSpec((1, GLA_KEY_DIM), lambda b, c: (0, 0)),
            pl.BlockSpec((1, GLA_HEAD_V), lambda b, c: (0, 0)),
        ],
        out_specs=[
            pl.BlockSpec((1, q, D_MODEL), lambda b, c: (b, c, 0)),
            pl.BlockSpec((1, GLA_KEY_DIM, GLA_HEAD_V), lambda b, c: (b, 0, 0)),
        ],
        out_shape=[
            jax.ShapeDtypeStruct((nb, seq, D_MODEL), BF16),
            jax.ShapeDtypeStruct((nb, GLA_KEY_DIM, GLA_HEAD_V), F32),
        ],
        scratch_shapes=[pltpu.VMEM((GLA_KEY_DIM, GLA_HEAD_V), F32)],
        compiler_params=_cp("parallel", "arbitrary"),
        name="gla_scan",
    )(proj3, proj3, proj3, proj3, proj3, p["w_gate"], p["b_gate"], p["gla_norm"])
    return o.reshape(nb * seq, D_MODEL), s_out.reshape(nb, GLA_HEADS, GLA_HEAD_K, GLA_HEAD_V)


TOK_BLOCK = 128


def _row_shift(x, d, tpos):
    return jnp.where(tpos >= d, pltpu.roll(x, d, 0), 0.0)


def _seq_cumsum_and_last(a, seq, tpos):
    nrows = a.shape[0]
    acum = a
    for d in range(1, seq):
        acum = acum + _row_shift(a, d, tpos)
    last = jnp.where(tpos == seq - 1, acum, 0.0)
    for d in range(1, seq):
        last = last + jnp.where(tpos == seq - 1 - d, pltpu.roll(acum, nrows - d, 0), 0.0)
    return acum, last


def _ssd_step_pre_kernel(xs_ref, b_ref, c_ref, dt_ref, dtb_ref, alog_ref, dexp_ref, e_ref, gh_ref,
                         ypart_ref, eexp_ref, xwt_ref, elt_ref, *, seq):
    nrows = xs_ref.shape[0]
    xs = xs_ref[...]
    bm = b_ref[...]
    cm = c_ref[...]
    dt = _softplus(dt_ref[...] + dtb_ref[...])
    a = dt * (-jnp.exp(alog_ref[...]))
    pos = lambda w: lax.broadcasted_iota(jnp.int32, (nrows, w), 0) % seq
    t_h, t_c, t_x = pos(LANE), pos(BC_W), pos(D_MODEL)
    acum, last = _seq_cumsum_and_last(a, seq, t_h)
    e_mat = e_ref[...]
    eexp_ref[...] = _dot(jnp.exp(acum), e_mat, HI)
    wexp = _dot(jnp.exp(last - acum) * dt, e_mat, HI)
    xwt_ref[...] = (xs * wexp).T
    elt_ref[...] = _dot(jnp.exp(last), e_mat, HI).T
    y = dexp_ref[...] * xs
    for d in range(seq):
        if d == 0:
            cbh = _dot(cm * bm, gh_ref[...], HI)
            coef = dt
            xd = xs
        else:
            cbh = _dot(cm * _row_shift(bm, d, t_c), gh_ref[...], HI)
            coef = jnp.where(t_h >= d, jnp.exp(acum - pltpu.roll(acum, d, 0)) * pltpu.roll(dt, d, 0), 0.0)
            xd = _row_shift(xs, d, t_x)
        y = y + _dot(cbh * coef, e_mat, HI) * xd
    ypart_ref[...] = y


def _ssd_step_state_kernel(st_ref, c_ref, b_ref, xwt_ref, elt_ref, ypart_ref, eexp_ref, z_ref, nrm_ref,
                           u_ref, so_ref, *, sb, seq):
    i = pl.program_id(0)
    rows = sb * seq
    steps_per_block = TOK_BLOCK // rows
    base = (i % steps_per_block) * rows
    c16 = c_ref[...].astype(BF16)
    btok = b_ref[...]
    tok = lax.broadcasted_iota(jnp.int32, (TOK_BLOCK, LANE), 0)
    rsel = lax.broadcasted_iota(jnp.int32, (rows, GROUP_W), 0)
    zeros = jnp.zeros((TOK_BLOCK, LANE), F32)
    yoff = [jnp.zeros((rows, GROUP_W), F32) for _ in range(N_GROUPS)]
    for s in range(sb):
        lo = base + seq * s
        own = (tok >= lo) & (tok < lo + seq)
        first = jnp.where(tok == lo, 1.0, 0.0)
        mine = (rsel >= seq * s) & (rsel < seq * (s + 1))
        for g in range(N_GROUPS):
            sl = slice(g * GROUP_W, (g + 1) * GROUP_W)
            s_g = st_ref[s, sl, :]
            yo = _dot_nt(c16[:, g * D_STATE : (g + 1) * D_STATE], s_g.astype(BF16))
            yoff[g] = jnp.where(mine, yo, yoff[g])
            bsel = jnp.where(own, btok[:, g * D_STATE : (g + 1) * D_STATE], 0.0)
            rhs = jnp.concatenate(
                [jnp.concatenate([bsel, zeros], axis=1), jnp.concatenate([zeros, first], axis=1)], axis=0)
            lhs = jnp.concatenate([xwt_ref[sl, :], elt_ref[sl, :]], axis=1)
            res = _dot(lhs, rhs, HI)
            so_ref[s, sl, :] = res[:, LANE:] * s_g + res[:, :LANE]
    y = ypart_ref[...] + jnp.concatenate(yoff, axis=1) * eexp_ref[...]
    u_ref[...] = _ssd_gate_norm(y, z_ref[...], nrm_ref[...]).astype(u_ref.dtype)


def ssd_step(xc2, proj2, state, p, seq):
    ntok = xc2.shape[0]
    nseq = ntok // seq
    full = lambda shape: pl.BlockSpec(shape, lambda i: (0,) * len(shape))
    ypart, eexp, xwt, elt = pl.pallas_call(
        functools.partial(_ssd_step_pre_kernel, seq=seq),
        grid=(1,),
        in_specs=[
            pl.BlockSpec((ntok, D_MODEL), lambda i: (0, 0)),
            pl.BlockSpec((ntok, BC_W), lambda i: (0, D_MODEL // BC_W)),
            pl.BlockSpec((ntok, BC_W), lambda i: (0, D_MODEL // BC_W + 1)),
            pl.BlockSpec((ntok, LANE), lambda i: (0, COL_DT // LANE)),
            full((1, LANE)), full((1, LANE)), full((1, D_MODEL)),
            full((LANE, D_MODEL)), full((BC_W, LANE)),
        ],
        out_specs=[full((ntok, D_MODEL)), full((ntok, D_MODEL)), full((D_MODEL, ntok)), full((D_MODEL, ntok))],
        out_shape=[
            jax.ShapeDtypeStruct((ntok, D_MODEL), F32),
            jax.ShapeDtypeStruct((ntok, D_MODEL), F32),
            jax.ShapeDtypeStruct((D_MODEL, ntok), F32),
            jax.ShapeDtypeStruct((D_MODEL, ntok), F32),
        ],
        compiler_params=_cp("arbitrary"),
        name="ssd_step_pre",
    )(xc2, xc2, xc2, proj2, p["dt_bias"], p["a_log"], p["d_exp"], p["e_head"], p["g_head"])

    sb = 4
    rows = sb * seq
    spb = TOK_BLOCK // rows
    st3 = state.reshape(nseq, D_MODEL, D_STATE)
    u, s_new = pl.pallas_call(
        functools.partial(_ssd_step_state_kernel, sb=sb, seq=seq),
        grid=(nseq // sb,),
        in_specs=[
            pl.BlockSpec((sb, D_MODEL, D_STATE), lambda i: (i, 0, 0)),
            pl.BlockSpec((rows, BC_W), lambda i: (i, D_MODEL // BC_W + 1)),
            pl.BlockSpec((TOK_BLOCK, BC_W), lambda i: (i // spb, D_MODEL // BC_W)),
            pl.BlockSpec((D_MODEL, TOK_BLOCK), lambda i: (0, i // spb)),
            pl.BlockSpec((D_MODEL, TOK_BLOCK), lambda i: (0, i // spb)),
            pl.BlockSpec((rows, D_MODEL), lambda i: (i, 0)),
            pl.BlockSpec((rows, D_MODEL), lambda i: (i, 0)),
            pl.BlockSpec((rows, D_MODEL), lambda i: (i, COL_Z // D_MODEL)),
            pl.BlockSpec((1, D_MODEL), lambda i: (0, 0)),
        ],
        out_specs=[
            pl.BlockSpec((rows, D_MODEL), lambda i: (i, 0)),
            pl.BlockSpec((sb, D_MODEL, D_STATE), lambda i: (i, 0, 0)),
        ],
        out_shape=[
            jax.ShapeDtypeStruct((ntok, D_MODEL), BF16),
            jax.ShapeDtypeStruct((nseq, D_MODEL, D_STATE), F32),
        ],
        compiler_params=_cp("parallel"),
        name="ssd_step_state",
    )(st3, xc2, xc2, xwt, elt, ypart, eexp, proj2, p["ssd_norm"])
    return u, s_new.reshape(nseq, SSD_HEADS, SSD_HEAD_DIM, D_STATE)


def _gla_step_pre_kernel(q_ref, k_ref, v_ref, glr_ref, wg_ref, bg_ref, gv_ref,
                         oin_ref, qe_ref, kdt_ref, elt_ref, *, seq):
    nrows = q_ref.shape[0]
    glog = _gla_gate_log(glr_ref[...], wg_ref[...], bg_ref[...])
    pos = lambda w: lax.broadcasted_iota(jnp.int32, (nrows, w), 0) % seq
    t_k, t_v = pos(GLA_KEY_DIM), pos(D_MODEL)
    bc, last = _seq_cumsum_and_last(glog, seq, t_k)
    kk = k_ref[...]
    qe = q_ref[...] * (GLA_HEAD_K ** -0.5) * jnp.exp(bc)
    ke = kk * jnp.exp(-bc)
    qe_ref[...] = qe
    kdt_ref[...] = (kk * jnp.exp(last - bc)).T
    elt_ref[...] = jnp.exp(last).T
    v = v_ref[...]
    gv = gv_ref[...]
    o = jnp.zeros((nrows, D_MODEL), F32)
    for d in range(seq):
        ked = ke if d == 0 else _row_shift(ke, d, t_k)
        vd = v if d == 0 else _row_shift(v, d, t_v)
        att = _dot((qe * ked).astype(BF16), gv)
        o = o + att.astype(BF16).astype(F32) * vd.astype(BF16).astype(F32)
    oin_ref[...] = o


def _gla_step_state_kernel(st_ref, qe_ref, v_ref, kdt_ref, elt_ref, oin_ref, r_ref, nrm_ref,
                           o_ref, so_ref, *, sb, seq):
    i = pl.program_id(0)
    rows = sb * seq
    spb = TOK_BLOCK // rows
    base = (i % spb) * rows
    qe16 = qe_ref[...].astype(BF16)
    vtok = v_ref[...]
    tokv = lax.broadcasted_iota(jnp.int32, (TOK_BLOCK, GLA_HEAD_V), 0)
    tok = lax.broadcasted_iota(jnp.int32, (TOK_BLOCK, LANE), 0)
    rsel = lax.broadcasted_iota(jnp.int32, (rows, GLA_HEAD_V), 0)
    ooff = [jnp.zeros((rows, GLA_HEAD_V), F32) for _ in range(GLA_HEADS)]
    for s in range(sb):
        lo = base + seq * s
        own = (tokv >= lo) & (tokv < lo + seq)
        first = jnp.where(tok == lo, 1.0, 0.0)
        mine = (rsel >= seq * s) & (rsel < seq * (s + 1))
        for h in range(GLA_HEADS):
            ks = slice(h * GLA_HEAD_K, (h + 1) * GLA_HEAD_K)
            vs = slice(h * GLA_HEAD_V, (h + 1) * GLA_HEAD_V)
            s_h = st_ref[s, ks, :]
            oo = _dot(qe16[:, ks], s_h.astype(BF16))
            ooff[h] = jnp.where(mine, oo, ooff[h])
            vsel = jnp.where(own, vtok[:, vs], 0.0).astype(BF16)
            upd = _dot(kdt_ref[ks, :].astype(BF16), vsel)
            ecol = _dot(elt_ref[ks, :], first, HI)
            so_ref[s, ks, :] = jnp.concatenate([ecol] * (GLA_HEAD_V // LANE), axis=1) * s_h + upd
    o = oin_ref[...] + jnp.concatenate(ooff, axis=1)
    o_ref[0] = _gla_out_norm(o, r_ref[...], nrm_ref[...]).astype(o_ref.dtype)


def gla_step(proj2, state, p, seq):
    ntok = proj2.shape[0]
    nseq = ntok // seq
    full = lambda shape: pl.BlockSpec(shape, lambda i: (0,) * len(shape))
    oin, qe, kdt, elt = pl.pallas_call(
        functools.partial(_gla_step_pre_kernel, seq=seq),
        grid=(1,),
        in_specs=[
            pl.BlockSpec((ntok, GLA_KEY_DIM), lambda i: (0, COL_Q // GLA_KEY_DIM)),
            pl.BlockSpec((ntok, GLA_KEY_DIM), lambda i: (0, COL_K // GLA_KEY_DIM)),
            pl.BlockSpec((ntok, D_MODEL), lambda i: (0, COL_V // D_MODEL)),
            pl.BlockSpec((ntok, LANE), lambda i: (0, COL_GLR // LANE)),
            full((LANE, GLA_KEY_DIM)), full((1, GLA_KEY_DIM)), full((GLA_KEY_DIM, D_MODEL)),
        ],
        out_specs=[full((ntok, D_MODEL)), full((ntok, GLA_KEY_DIM)), full((GLA_KEY_DIM, ntok)),
                   full((GLA_KEY_DIM, ntok))],
        out_shape=[
            jax.ShapeDtypeStruct((ntok, D_MODEL), F32),
            jax.ShapeDtypeStruct((ntok, GLA_KEY_DIM), F32),
            jax.ShapeDtypeStruct((GLA_KEY_DIM, ntok), F32),
            jax.ShapeDtypeStruct((GLA_KEY_DIM, ntok), F32),
        ],
        compiler_params=_cp("arbitrary"),
        name="gla_step_pre",
    )(proj2, proj2, proj2, proj2, p["w_gate"], p["b_gate"], p["g_val"])

    sb = 2
    rows = sb * seq
    spb = TOK_BLOCK // rows
    st3 = state.reshape(nseq, GLA_KEY_DIM, GLA_HEAD_V)
    o, s_new = pl.pallas_call(
        functools.partial(_gla_step_state_kernel, sb=sb, seq=seq),
        grid=(nseq // sb,),
        in_specs=[
            pl.BlockSpec((sb, GLA_KEY_DIM, GLA_HEAD_V), lambda i: (i, 0, 0)),
            pl.BlockSpec((rows, GLA_KEY_DIM), lambda i: (i, 0)),
            pl.BlockSpec((TOK_BLOCK, D_MODEL), lambda i: (i // spb, COL_V // D_MODEL)),
            pl.BlockSpec((GLA_KEY_DIM, TOK_BLOCK), lambda i: (0, i // spb)),
            pl.BlockSpec((GLA_KEY_DIM, TOK_BLOCK), lambda i: (0, i // spb)),
            pl.BlockSpec((rows, D_MODEL), lambda i: (i, 0)),
            pl.BlockSpec((rows, D_MODEL), lambda i: (i, COL_R // D_MODEL)),
            pl.BlockSpec((1, GLA_HEAD_V), lambda i: (0, 0)),
        ],
        out_specs=[
            pl.BlockSpec((1, rows, D_MODEL), lambda i: (i, 0, 0)),
            pl.BlockSpec((sb, GLA_KEY_DIM, GLA_HEAD_V), lambda i: (i, 0, 0)),
        ],
        out_shape=[
            jax.ShapeDtypeStruct((nseq // sb, rows, D_MODEL), BF16),
            jax.ShapeDtypeStruct((nseq, GLA_KEY_DIM, GLA_HEAD_V), F32),
        ],
        compiler_params=_cp("parallel"),
        name="gla_step_state",
    )(st3, qe, proj2, kdt, elt, oin, proj2, p["gla_norm"])
    return o.reshape(ntok, D_MODEL), s_new.reshape(nseq, GLA_HEADS, GLA_HEAD_K, GLA_HEAD_V)


def _xattn_kernel(q_ref, k_ref, v_ref, o_ref, *, nseq, tl):
    rows = nseq * tl
    q16 = q_ref[0].astype(BF16)
    rsel = lax.broadcasted_iota(jnp.int32, (rows, CROSS_HEAD_DIM), 0)
    outs = [jnp.zeros((rows, CROSS_HEAD_DIM), F32) for _ in range(CROSS_HEADS)]
    for s in range(nseq):
        k16 = k_ref[s].astype(BF16)
        v16 = v_ref[s].astype(BF16)
        for h in range(CROSS_HEADS):
            hs = slice(h * CROSS_HEAD_DIM, (h + 1) * CROSS_HEAD_DIM)
            sc = _dot_nt(q16[:, hs], k16[:, hs]) * (CROSS_HEAD_DIM ** -0.5)
            e = jnp.exp(sc - jnp.max(sc, axis=-1, keepdims=True))
            pr = e / jnp.sum(e, axis=-1, keepdims=True)
            oh = _dot(pr.astype(BF16), v16[:, hs])
            if nseq == 1:
                outs[h] = oh
            else:
                outs[h] = jnp.where((rsel >= tl * s) & (rsel < tl * (s + 1)), oh, outs[h])
    o_ref[0] = jnp.concatenate(outs, axis=1).astype(o_ref.dtype)


def cross_attend(q2, mem_k, mem_v, nb, seq):
    n_mem = mem_k.shape[1]
    if seq >= 8:
        nseq, tl = 1, min(seq, 512)
    else:
        nseq, tl = 8 // seq, seq
    rows = nseq * tl
    lt = seq // tl
    nblk = nb * seq // rows
    q3 = q2.reshape(nblk, rows, D_MODEL)
    o = pl.pallas_call(
        functools.partial(_xattn_kernel, nseq=nseq, tl=tl),
        grid=(nblk,),
        in_specs=[
            pl.BlockSpec((1, rows, D_MODEL), lambda i: (i, 0, 0)),
            pl.BlockSpec((nseq, n_mem, D_MODEL), lambda i: (i // lt, 0, 0)),
            pl.BlockSpec((nseq, n_mem, D_MODEL), lambda i: (i // lt, 0, 0)),
        ],
        out_specs=pl.BlockSpec((1, rows, D_MODEL), lambda i: (i, 0, 0)),
        out_shape=jax.ShapeDtypeStruct((nblk, rows, D_MODEL), BF16),
        compiler_params=_cp("parallel"),
        name="cross_attend",
    )(q3, mem_k, mem_v)
    return o.reshape(nb * seq, D_MODEL)


def _pack_w_in(w_in):
    sizes = (D_MODEL, CONV_DIM, SSD_HEADS, GLA_KEY_DIM, GLA_KEY_DIM, D_MODEL, D_MODEL, GATE_RANK, D_MODEL, D_MODEL)
    offs = [0]
    for s in sizes:
        offs.append(offs[-1] + s)
    z, xbc, dt, q, k, v, r, glr, ga, gb = [w_in[:, offs[i] : offs[i + 1]] for i in range(len(sizes))]
    pad = lambda a, n: jnp.pad(a, ((0, 0), (0, n - a.shape[1])))
    parts = [z, ga, gb, v, r, xbc, q, k, pad(dt, LANE), pad(glr, LANE)]
    used = sum(a.shape[1] for a in parts)
    parts.append(jnp.zeros((w_in.shape[0], NPROJ - used), w_in.dtype))
    return jnp.concatenate(parts, axis=1).astype(BF16)


def _params(ssd_dt_bias, ssd_A_log, ssd_D, ssd_norm, w_gla_gate, b_gla_gate, gla_norm):
    padv = lambda a: jnp.pad(a.astype(F32), (0, LANE - a.shape[0])).reshape(1, LANE)
    head_of_chan = jnp.arange(D_MODEL, dtype=jnp.int32) // SSD_HEAD_DIM
    e_head = (jnp.arange(LANE, dtype=jnp.int32)[:, None] == head_of_chan[None, :]).astype(F32)
    group_of_bc = jnp.arange(BC_W, dtype=jnp.int32) // D_STATE
    lane_h = jnp.arange(LANE, dtype=jnp.int32)
    g_head = ((lane_h[None, :] // (SSD_HEADS // N_GROUPS) == group_of_bc[:, None])
              & (lane_h[None, :] < SSD_HEADS)).astype(F32)
    khead = jnp.arange(GLA_KEY_DIM, dtype=jnp.int32) // GLA_HEAD_K
    vhead = jnp.arange(D_MODEL, dtype=jnp.int32) // GLA_HEAD_V
    g_val = (khead[:, None] == vhead[None, :]).astype(BF16)
    return dict(
        dt_bias=padv(ssd_dt_bias), a_log=padv(ssd_A_log),
        d_exp=jnp.repeat(ssd_D.astype(F32), SSD_HEAD_DIM).reshape(1, D_MODEL),
        ssd_norm=ssd_norm.astype(F32).reshape(1, D_MODEL),
        e_head=e_head, g_head=g_head, g_val=g_val,
        w_gate=jnp.pad(w_gla_gate, ((0, LANE - GATE_RANK), (0, 0))).astype(BF16),
        b_gate=b_gla_gate.astype(F32).reshape(1, GLA_KEY_DIM),
        gla_norm=gla_norm.astype(F32).reshape(1, GLA_HEAD_V),
    )


def _layer(x3, mem_k, mem_v, ssd_conv, ssd_state, gla_state, ffn_conv, w, p, long_seq):
    nb, seq, d = x3.shape
    ntok = nb * seq
    ffn = w["w_down"].shape[0]
    x2 = x3.reshape(ntok, d)
    xn = rmsnorm_cast(x2, w["norm_mix"])
    proj = matmul(xn, w["w_in"], name="in_proj")
    proj3 = proj.reshape(nb, seq, NPROJ)
    xc3, ssd_conv_new = causal_conv(proj3, [COL_XBC], CONV_DIM, [ssd_conv], [w["ssd_conv_w"]], [w["ssd_conv_b"]],
                                    SSD_CONV, False, F32, "ssd_conv")
    if long_seq:
        u, ssd_new = ssd_scan(xc3, proj3, p)
        o, gla_new = gla_scan(proj3, p)
    else:
        u, ssd_new = ssd_step(xc3.reshape(ntok, CONV_DIM), proj, ssd_state, p, seq)
        o, gla_new = gla_step(proj, gla_state, p, seq)
    merged = merge_branches(u, o, w["w_ssd_out"], w["w_gla_out"], proj)
    h, hn = mm_res_norm(merged, w["w_mix_out"], x2, w["norm_cross"], True, BF16, "mix_out")
    qc = matmul(hn, w["w_cq"], out_dtype=BF16 if long_seq else F32, name="cross_q")
    att = cross_attend(qc, mem_k, mem_v, nb, seq)
    h2, hn2 = mm_res_norm(att, w["w_co"], h, w["norm_ffn"], True, BF16, "cross_out")
    up = matmul(hn2, w["w_up"], name="ffn_up").reshape(nb, seq, 2 * ffn)
    cw, cbias = w["ffn_conv_w"], w["ffn_conv_b"]
    act, fa, fg = causal_conv(up, [0, ffn], ffn, [ffn_conv[:, :, :ffn], ffn_conv[:, :, ffn:]],
                              [cw[:, :ffn], cw[:, ffn:]], [cbias[:ffn], cbias[ffn:]],
                              FFN_CONV, True, BF16, "ffn_conv")
    y = mm_res_norm(act.reshape(ntok, ffn), w["w_down"], h2, w["norm_final"], False, F32, "ffn_down")
    return y.reshape(nb, seq, d), ssd_conv_new, ssd_new, gla_new, jnp.concatenate([fa, fg], axis=-1)


def kernel(x_prompt, x_sample, cache_mem_k, cache_mem_v, state_ssd_conv, state_ssd, state_gla, state_ffn_conv, mem_prompt, norm_mix, w_in, ssd_conv_w, ssd_conv_b, ssd_dt_bias, ssd_A_log, ssd_D, ssd_norm, w_ssd_out, w_gla_gate, b_gla_gate, gla_norm, w_gla_out, w_mix_out, norm_cross, norm_mem, w_cq, w_ck, w_cv, w_co, norm_ffn, w_up, ffn_conv_w, ffn_conv_b, w_down, norm_final):
    nb, seq, d = x_prompt.shape
    n_mem = mem_prompt.shape[1]
    ffn2 = w_up.shape[1]
    w = dict(
        norm_mix=norm_mix, norm_cross=norm_cross, norm_ffn=norm_ffn, norm_final=norm_final,
        w_in=_pack_w_in(w_in), ssd_conv_w=ssd_conv_w, ssd_conv_b=ssd_conv_b,
        w_ssd_out=w_ssd_out.astype(BF16), w_gla_out=w_gla_out.astype(BF16), w_mix_out=w_mix_out.astype(BF16),
        w_cq=w_cq.astype(BF16), w_co=w_co.astype(BF16), w_up=w_up.astype(BF16), w_down=w_down.astype(BF16),
        ffn_conv_w=ffn_conv_w, ffn_conv_b=ffn_conv_b,
    )
    p = _params(ssd_dt_bias, ssd_A_log, ssd_D, ssd_norm, w_gla_gate, b_gla_gate, gla_norm)

    mn = rmsnorm_cast(mem_prompt.reshape(nb * n_mem, d), norm_mem)
    p_mem_k = matmul(mn, w_ck.astype(BF16), name="mem_k").reshape(nb, n_mem, d)
    p_mem_v = matmul(mn, w_cv.astype(BF16), name="mem_v").reshape(nb, n_mem, d)
    zeros_ssd_conv = jnp.zeros((nb, SSD_CONV - 1, CONV_DIM), F32)
    zeros_ffn_conv = jnp.zeros((nb, FFN_CONV - 1, ffn2), F32)
    y_prompt, p_ssd_conv, p_ssd, p_gla, p_ffn_conv = _layer(
        x_prompt, p_mem_k, p_mem_v, zeros_ssd_conv, None, None, zeros_ffn_conv, w, p, True)

    ns = x_sample.shape[0]
    y_sample, s_ssd_conv, s_ssd, s_gla, s_ffn_conv = _layer(
        x_sample, cache_mem_k.reshape(ns, n_mem, d), cache_mem_v.reshape(ns, n_mem, d),
        state_ssd_conv, state_ssd, state_gla, state_ffn_conv, w, p, False)

    head_shape = (n_mem, CROSS_HEADS, CROSS_HEAD_DIM)
    return (y_prompt, y_sample, p_ssd_conv, p_ssd, p_gla, p_ffn_conv,
            p_mem_k.reshape((nb,) + head_shape), p_mem_v.reshape((nb,) + head_shape),
            s_ssd_conv, s_ssd, s_gla, s_ffn_conv)
```

```python
import functools

import jax
import jax.numpy as jnp
from jax import lax
from jax.experimental import pallas as pl
from jax.experimental.pallas import tpu as pltpu

F32 = jnp.float32
BF16 = jnp.bfloat16
HI = lax.Precision.HIGHEST
EPS = 1e-6
NEG_BIG = -1e30

D_MODEL = 2048
SSD_HEAD_DIM = 64
SSD_HEADS = 32
D_STATE = 128
N_GROUPS = 4
GROUP_W = D_MODEL // N_GROUPS
BC_W = N_GROUPS * D_STATE
CONV_DIM = D_MODEL + 2 * BC_W
SSD_CONV = 4
SSD_CHUNK = 128
GLA_HEADS = 4
GLA_KEY_DIM = 1024
GLA_HEAD_K = 256
GLA_HEAD_V = 512
GATE_RANK = 16
GATE_TAU = 16.0
GLA_CHUNK = 64
CROSS_HEADS = 4
CROSS_HEAD_DIM = 512
FFN_CONV = 3
LANE = 128

COL_Z = 0
COL_GA = 2048
COL_GB = 4096
COL_V = 6144
COL_R = 8192
COL_XBC = 10240
COL_Q = 13312
COL_K = 14336
COL_DT = 15360
COL_GLR = 15488
NPROJ = 15872

VMEM_LIMIT = 56 * 1024 * 1024


def _cp(*sem):
    return pltpu.CompilerParams(dimension_semantics=sem, vmem_limit_bytes=VMEM_LIMIT)


def _dot(a, b, prec=None):
    return jnp.dot(a, b, preferred_element_type=F32, precision=prec)


def _dot_nt(a, b):
    return lax.dot_general(a, b, (((1,), (1,)), ((), ())), preferred_element_type=F32)


def _sigmoid(x):
    return 1.0 / (1.0 + jnp.exp(-x))


def _silu(x):
    return x * _sigmoid(x)


def _softplus(x):
    return jnp.maximum(x, 0.0) + jnp.log(1.0 + jnp.exp(-jnp.abs(x)))


def _rms(x, g):
    ms = jnp.mean(x * x, axis=-1, keepdims=True)
    return x * lax.rsqrt(ms + EPS) * g


def _rmsnorm_kernel(x_ref, g_ref, o_ref):
    o_ref[...] = _rms(x_ref[...], g_ref[...]).astype(o_ref.dtype)


def rmsnorm_cast(x2, g):
    m, d = x2.shape
    tm = min(m, 512)
    return pl.pallas_call(
        _rmsnorm_kernel,
        grid=(m // tm,),
        in_specs=[pl.BlockSpec((tm, d), lambda i: (i, 0)), pl.BlockSpec((1, d), lambda i: (0, 0))],
        out_specs=pl.BlockSpec((tm, d), lambda i: (i, 0)),
        out_shape=jax.ShapeDtypeStruct((m, d), BF16),
        compiler_params=_cp("parallel"),
        name="rmsnorm_cast",
    )(x2, g.reshape(1, d))


def _mm_kernel(a_ref, w_ref, o_ref):
    o_ref[...] = _dot(a_ref[...], w_ref[...]).astype(o_ref.dtype)


def matmul(a, w, out_dtype=F32, tn=512, name="matmul"):
    m, k = a.shape
    n = w.shape[1]
    tm = min(m, 1024)
    return pl.pallas_call(
        _mm_kernel,
        grid=(m // tm, n // tn),
        in_specs=[pl.BlockSpec((tm, k), lambda i, j: (i, 0)), pl.BlockSpec((k, tn), lambda i, j: (0, j))],
        out_specs=pl.BlockSpec((tm, tn), lambda i, j: (i, j)),
        out_shape=jax.ShapeDtypeStruct((m, n), out_dtype),
        compiler_params=_cp("parallel", "arbitrary"),
        name=name,
    )(a, w)


def _mm_res_norm_kernel(a_ref, w_ref, res_ref, g_ref, *rest, nk, emit_h):
    h_ref = rest[0] if emit_h else None
    n_ref = rest[1] if emit_h else rest[0]

    def finish(acc):
        h = res_ref[...] + acc
        if emit_h:
            h_ref[...] = h
        n_ref[...] = _rms(h, g_ref[...]).astype(n_ref.dtype)

    if nk == 1:
        finish(_dot(a_ref[...], w_ref[...]))
        return
    acc_ref = rest[-1]
    k = pl.program_id(1)

    @pl.when(k == 0)
    def _():
        acc_ref[...] = jnp.zeros_like(acc_ref)

    acc_ref[...] += _dot(a_ref[...], w_ref[...])

    @pl.when(k == nk - 1)
    def _():
        finish(acc_ref[...])


def mm_res_norm(a, w, res, g, emit_h, norm_dtype, name):
    m, kdim = a.shape
    n = w.shape[1]
    tm = min(m, 512)
    nk = next(c for c in range(1, kdim // LANE + 1)
              if kdim % c == 0 and (kdim // c) % LANE == 0 and kdim // c <= 2048)
    tk = kdim // nk
    out_shape = [jax.ShapeDtypeStruct((m, n), norm_dtype)]
    out_specs = [pl.BlockSpec((tm, n), lambda i, k: (i, 0))]
    if emit_h:
        out_shape = [jax.ShapeDtypeStruct((m, n), F32)] + out_shape
        out_specs = [pl.BlockSpec((tm, n), lambda i, k: (i, 0))] + out_specs
    outs = pl.pallas_call(
        functools.partial(_mm_res_norm_kernel, nk=nk, emit_h=emit_h),
        grid=(m // tm, nk),
        in_specs=[
            pl.BlockSpec((tm, tk), lambda i, k: (i, k)),
            pl.BlockSpec((tk, n), lambda i, k: (k, 0)),
            pl.BlockSpec((tm, n), lambda i, k: (i, 0)),
            pl.BlockSpec((1, n), lambda i, k: (0, 0)),
        ],
        out_specs=out_specs,
        out_shape=out_shape,
        scratch_shapes=[pltpu.VMEM((tm, n), F32)] if nk > 1 else [],
        compiler_params=_cp("parallel", "arbitrary"),
        name=name,
    )(a, w, res, g.reshape(1, n))
    return outs if emit_h else outs[0]


def _merge_kernel(u_ref, o_ref, wa_ref, wb_ref, ga_ref, gb_ref, out_ref):
    a = _dot(u_ref[...], wa_ref[...])
    b = _dot(o_ref[...], wb_ref[...])
    out_ref[...] = (_sigmoid(ga_ref[...]) * a + _sigmoid(gb_ref[...]) * b).astype(out_ref.dtype)


def merge_branches(u, o, wa, wb, proj):
    m, d = u.shape
    tm, tn = min(m, 1024), 512
    return pl.pallas_call(
        _merge_kernel,
        grid=(m // tm, d // tn),
        in_specs=[
            pl.BlockSpec((tm, d), lambda i, j: (i, 0)),
            pl.BlockSpec((tm, d), lambda i, j: (i, 0)),
            pl.BlockSpec((d, tn), lambda i, j: (0, j)),
            pl.BlockSpec((d, tn), lambda i, j: (0, j)),
            pl.BlockSpec((tm, tn), lambda i, j: (i, COL_GA // tn + j)),
            pl.BlockSpec((tm, tn), lambda i, j: (i, COL_GB // tn + j)),
        ],
        out_specs=pl.BlockSpec((tm, tn), lambda i, j: (i, j)),
        out_shape=jax.ShapeDtypeStruct((m, d), BF16),
        compiler_params=_cp("parallel", "arbitrary"),
        name="merge_branches",
    )(u, o, wa, wb, proj, proj)


CONV_PAD = 8


def _conv_kernel(*refs, taps, tl, nstreams, swiglu):
    ins = refs[: 4 * nstreams]
    out_ref = refs[4 * nstreams]
    ns_refs = refs[4 * nstreams + 1 : 4 * nstreams + 1 + nstreams]
    bufs = refs[4 * nstreams + 1 + nstreams :]
    l = pl.program_id(2)
    nl = pl.num_programs(2)
    lo = CONV_PAD - (taps - 1)
    vals = []
    for s in range(nstreams):
        u_ref, st_ref, w_ref, b_ref = ins[4 * s : 4 * s + 4]
        buf = bufs[s]

        @pl.when(l == 0)
        def _():
            buf[:, lo:CONV_PAD, :] = st_ref[...]

        buf[:, CONV_PAD : CONV_PAD + tl, :] = u_ref[...]
        acc = b_ref[...]
        for k in range(taps):
            acc = acc + buf[:, lo + k : lo + k + tl, :] * w_ref[k : k + 1, :]
        vals.append(acc)
        tail = buf[:, CONV_PAD + tl - (taps - 1) : CONV_PAD + tl, :]
        buf[:, lo:CONV_PAD, :] = tail

        @pl.when(l == nl - 1)
        def _():
            ns_refs[s][...] = tail

    if swiglu:
        out_ref[...] = (_silu(vals[1]) * vals[0]).astype(out_ref.dtype)
    else:
        out_ref[...] = _silu(vals[0]).astype(out_ref.dtype)


def causal_conv(u3, col_offs, width, states, ws, bs, taps, swiglu, out_dtype, name):
    nb, seq, _ = u3.shape
    tc = 512
    tl = min(seq, 512)
    sb = 1 if seq >= 8 else 16
    nstreams = len(col_offs)
    in_specs, args = [], []
    for s in range(nstreams):
        cb = col_offs[s] // tc
        in_specs += [
            pl.BlockSpec((sb, tl, tc), lambda b, c, l, cb=cb: (b, l, cb + c)),
            pl.BlockSpec((sb, taps - 1, tc), lambda b, c, l: (b, 0, c)),
            pl.BlockSpec((taps, tc), lambda b, c, l: (0, c)),
            pl.BlockSpec((1, tc), lambda b, c, l: (0, c)),
        ]
        args += [u3, states[s], ws[s], bs[s].reshape(1, width)]
    out_specs = [pl.BlockSpec((sb, tl, tc), lambda b, c, l: (b, l, c))]
    out_shape = [jax.ShapeDtypeStruct((nb, seq, width), out_dtype)]
    for s in range(nstreams):
        out_specs.append(pl.BlockSpec((sb, taps - 1, tc), lambda b, c, l: (b, 0, c)))
        out_shape.append(jax.ShapeDtypeStruct((nb, taps - 1, width), F32))
    return pl.pallas_call(
        functools.partial(_conv_kernel, taps=taps, tl=tl, nstreams=nstreams, swiglu=swiglu),
        grid=(nb // sb, width // tc, seq // tl),
        in_specs=in_specs,
        out_specs=out_specs,
        out_shape=out_shape,
        scratch_shapes=[pltpu.VMEM((sb, CONV_PAD + tl, tc), F32) for _ in range(nstreams)],
        compiler_params=_cp("parallel", "parallel", "arbitrary"),
        name=name,
    )(*args)


def _ffn_up_kernel(hn_ref, wa_ref, wg_ref, cwa_ref, cwg_ref, cba_ref, cbg_ref, sta_ref, stg_ref,
                   act_ref, nsa_ref, nsg_ref, bufa, bufg, *, tm):
    m = pl.program_id(2)
    nm = pl.num_programs(2)
    lo = CONV_PAD - (FFN_CONV - 1)
    hn = hn_ref[0]
    streams = ((wa_ref, cwa_ref, cba_ref, sta_ref, nsa_ref, bufa), (wg_ref, cwg_ref, cbg_ref, stg_ref, nsg_ref, bufg))
    vals = []
    for w_ref, cw_ref, cb_ref, st_ref, ns_ref, buf in streams:

        @pl.when(m == 0)
        def _():
            buf[lo:CONV_PAD, :] = st_ref[0]

        buf[CONV_PAD : CONV_PAD + tm, :] = _dot(hn, w_ref[...])
        acc = cb_ref[...]
        for k in range(FFN_CONV):
            acc = acc + buf[lo + k : lo + k + tm, :] * cw_ref[k : k + 1, :]
        vals.append(acc)
        tail = buf[CONV_PAD + tm - (FFN_CONV - 1) : CONV_PAD + tm, :]
        buf[lo:CONV_PAD, :] = tail

        @pl.when(m == nm - 1)
        def _():
            ns_ref[0] = tail

    act_ref[0] = (_silu(vals[1]) * vals[0]).astype(act_ref.dtype)


def ffn_up_conv_act(hn3, w_up, conv_w, conv_b, conv_state):
    nb, seq, d = hn3.shape
    ffn = w_up.shape[1] // 2
    tm, tn = min(seq, 1024), 512
    nn = ffn // tn
    half = lambda off: (lambda b, n, m: (0, off + n))
    st_spec = lambda off: pl.BlockSpec((1, FFN_CONV - 1, tn), lambda b, n, m: (b, 0, off + n))
    cb2 = conv_b.reshape(1, 2 * ffn)
    act, nsa, nsg = pl.pallas_call(
        functools.partial(_ffn_up_kernel, tm=tm),
        grid=(nb, nn, seq // tm),
        in_specs=[
            pl.BlockSpec((1, tm, d), lambda b, n, m: (b, m, 0)),
            pl.BlockSpec((d, tn), half(0)), pl.BlockSpec((d, tn), half(nn)),
            pl.BlockSpec((FFN_CONV, tn), half(0)), pl.BlockSpec((FFN_CONV, tn), half(nn)),
            pl.BlockSpec((1, tn), half(0)), pl.BlockSpec((1, tn), half(nn)),
            st_spec(0), st_spec(nn),
        ],
        out_specs=[
            pl.BlockSpec((1, tm, tn), lambda b, n, m: (b, m, n)),
            pl.BlockSpec((1, FFN_CONV - 1, tn), lambda b, n, m: (b, 0, n)),
            pl.BlockSpec((1, FFN_CONV - 1, tn), lambda b, n, m: (b, 0, n)),
        ],
        out_shape=[
            jax.ShapeDtypeStruct((nb, seq, ffn), BF16),
            jax.ShapeDtypeStruct((nb, FFN_CONV - 1, ffn), F32),
            jax.ShapeDtypeStruct((nb, FFN_CONV - 1, ffn), F32),
        ],
        scratch_shapes=[pltpu.VMEM((CONV_PAD + tm, tn), F32), pltpu.VMEM((CONV_PAD + tm, tn), F32)],
        compiler_params=_cp("parallel", "parallel", "arbitrary"),
        name="ffn_up_conv_act",
    )(hn3, w_up, w_up, conv_w, conv_w, cb2, cb2, conv_state, conv_state)
    return act, jnp.concatenate([nsa, nsg], axis=-1)


def _ssd_gate_norm(y, z, nrm):
    ug = y * _silu(z)
    outs = []
    for g in range(N_GROUPS):
        ugg = ug[:, g * GROUP_W : (g + 1) * GROUP_W]
        ms = jnp.mean(ugg * ugg, axis=-1, keepdims=True)
        outs.append(ugg * lax.rsqrt(ms + EPS))
    return jnp.concatenate(outs, axis=1) * nrm


def _gla_out_norm(o, r, nrm):
    outs = []
    for h in range(GLA_HEADS):
        oh = o[:, h * GLA_HEAD_V : (h + 1) * GLA_HEAD_V]
        rh = r[:, h * GLA_HEAD_V : (h + 1) * GLA_HEAD_V]
        outs.append(_rms(oh, nrm) * _silu(rh))
    return jnp.concatenate(outs, axis=1)


def _ssd_scan_kernel(xs_ref, b_ref, c_ref, dt_ref, z_ref, cst_ref, cw_ref, cbias_ref,
                     dtb_ref, alog_ref, dexp_ref, nrm_ref, e_ref,
                     u_ref, sout_ref, cso_ref, st_ref, cbuf, *, nc):
    c = pl.program_id(1)
    q = SSD_CHUNK
    lo = CONV_PAD - (SSD_CONV - 1)

    @pl.when(c == 0)
    def _():
        st_ref[...] = jnp.zeros_like(st_ref)
        cbuf[lo:CONV_PAD, :] = cst_ref[0]

    cbuf[CONV_PAD : CONV_PAD + q, 0:D_MODEL] = xs_ref[0]
    cbuf[CONV_PAD : CONV_PAD + q, D_MODEL : D_MODEL + BC_W] = b_ref[0]
    cbuf[CONV_PAD : CONV_PAD + q, D_MODEL + BC_W : CONV_DIM] = c_ref[0]
    acc = cbias_ref[...]
    for k in range(SSD_CONV):
        acc = acc + cbuf[lo + k : lo + k + q, :] * cw_ref[k : k + 1, :]
    xbc = _silu(acc)
    tail = cbuf[CONV_PAD + q - (SSD_CONV - 1) : CONV_PAD + q, :]
    cbuf[lo:CONV_PAD, :] = tail

    @pl.when(c == nc - 1)
    def _():
        cso_ref[0] = tail

    xs = xbc[:, 0:D_MODEL]
    bm = xbc[:, D_MODEL : D_MODEL + BC_W]
    cm = xbc[:, D_MODEL + BC_W : CONV_DIM]
    dt = _softplus(dt_ref[0] + dtb_ref[...])
    a = dt * (-jnp.exp(alog_ref[...]))
    row = lax.broadcasted_iota(jnp.int32, (q, q), 0)
    col = lax.broadcasted_iota(jnp.int32, (q, q), 1)
    tril = row >= col
    acum = _dot(tril.astype(F32), a, HI)
    acum_t = acum.T
    dt_t = dt.T
    last = acum[q - 1 : q, :]
    e_mat = e_ref[...]
    eexp = _dot(jnp.exp(acum), e_mat, HI)
    wexp = _dot(jnp.exp(last - acum) * dt, e_mat, HI)
    s_bf = st_ref[...].astype(BF16)
    cb16 = cm.astype(BF16)
    bb16 = bm.astype(BF16)
    x16 = xs.astype(BF16)
    xw16 = (xs * wexp).astype(BF16)
    lane_lo = lax.broadcasted_iota(jnp.int32, (q, LANE), 1) < SSD_HEAD_DIM
    ys = []
    for g in range(N_GROUPS):
        cg = cb16[:, g * D_STATE : (g + 1) * D_STATE]
        bg = bb16[:, g * D_STATE : (g + 1) * D_STATE]
        cb = _dot_nt(cg, bg)
        yoff = _dot(cg, s_bf[:, g * GROUP_W : (g + 1) * GROUP_W])
        pieces = []
        for pr in range(GROUP_W // LANE):
            h0 = g * (SSD_HEADS // N_GROUPS) + 2 * pr
            xp = x16[:, h0 * SSD_HEAD_DIM : h0 * SSD_HEAD_DIM + LANE]
            yh = []
            for h in (h0, h0 + 1):
                diff = acum[:, h : h + 1] - acum_t[h : h + 1, :]
                dec = jnp.exp(jnp.where(tril, diff, NEG_BIG))
                m = (cb * dec * dt_t[h : h + 1, :]).astype(BF16)
                yh.append(_dot(m, xp))
            pieces.append(jnp.where(lane_lo, yh[0], yh[1]))
        sl = slice(g * GROUP_W, (g + 1) * GROUP_W)
        ys.append(jnp.concatenate(pieces, axis=1) + yoff * eexp[:, sl])
        bg_t = bm[:, g * D_STATE : (g + 1) * D_STATE].T.astype(BF16)
        upd = _dot(bg_t, xw16[:, sl])
        st_ref[:, sl] = eexp[q - 1 : q, sl] * st_ref[:, sl] + upd
    y = jnp.concatenate(ys, axis=1) + dexp_ref[...] * xs
    u_ref[0] = _ssd_gate_norm(y, z_ref[0], nrm_ref[...]).astype(u_ref.dtype)

    @pl.when(c == nc - 1)
    def _():
        sout_ref[0] = st_ref[...].T


def ssd_scan(proj3, conv_state, conv_w, conv_b, p):
    nb, seq, _ = proj3.shape
    nc = seq // SSD_CHUNK
    q = SSD_CHUNK
    vec = lambda n: pl.BlockSpec((1, n), lambda b, c: (0, 0))
    u, s_out, conv_new = pl.pallas_call(
        functools.partial(_ssd_scan_kernel, nc=nc),
        grid=(nb, nc),
        in_specs=[
            pl.BlockSpec((1, q, D_MODEL), lambda b, c: (b, c, COL_XBC // D_MODEL)),
            pl.BlockSpec((1, q, BC_W), lambda b, c: (b, c, (COL_XBC + D_MODEL) // BC_W)),
            pl.BlockSpec((1, q, BC_W), lambda b, c: (b, c, (COL_XBC + D_MODEL) // BC_W + 1)),
            pl.BlockSpec((1, q, LANE), lambda b, c: (b, c, COL_DT // LANE)),
            pl.BlockSpec((1, q, D_MODEL), lambda b, c: (b, c, COL_Z // D_MODEL)),
            pl.BlockSpec((1, SSD_CONV - 1, CONV_DIM), lambda b, c: (b, 0, 0)),
            pl.BlockSpec((SSD_CONV, CONV_DIM), lambda b, c: (0, 0)),
            vec(CONV_DIM),
            vec(LANE), vec(LANE), vec(D_MODEL), vec(D_MODEL),
            pl.BlockSpec((LANE, D_MODEL), lambda b, c: (0, 0)),
        ],
        out_specs=[
            pl.BlockSpec((1, q, D_MODEL), lambda b, c: (b, c, 0)),
            pl.BlockSpec((1, D_MODEL, D_STATE), lambda b, c: (b, 0, 0)),
            pl.BlockSpec((1, SSD_CONV - 1, CONV_DIM), lambda b, c: (b, 0, 0)),
        ],
        out_shape=[
            jax.ShapeDtypeStruct((nb, seq, D_MODEL), BF16),
            jax.ShapeDtypeStruct((nb, D_MODEL, D_STATE), F32),
            jax.ShapeDtypeStruct((nb, SSD_CONV - 1, CONV_DIM), F32),
        ],
        scratch_shapes=[pltpu.VMEM((D_STATE, D_MODEL), F32), pltpu.VMEM((CONV_PAD + q, CONV_DIM), F32)],
        compiler_params=_cp("parallel", "arbitrary"),
        name="ssd_scan",
    )(proj3, proj3, proj3, proj3, proj3, conv_state, conv_w, conv_b.reshape(1, CONV_DIM),
      p["dt_bias"], p["a_log"], p["d_exp"], p["ssd_norm"], p["e_head"])
    return u.reshape(nb * seq, D_MODEL), s_out.reshape(nb, SSD_HEADS, SSD_HEAD_DIM, D_STATE), conv_new


def _gla_gate_log(glr, wg, bg):
    x = _dot(glr.astype(BF16), wg) + bg
    return -_softplus(-x) / GATE_TAU


def _gla_scan_kernel(q_ref, k_ref, v_ref, r_ref, glr_ref, wg_ref, bg_ref, nrm_ref,
                     o_ref, sout_ref, st_ref, *, nc):
    c = pl.program_id(1)
    q = GLA_CHUNK

    @pl.when(c == 0)
    def _():
        st_ref[...] = jnp.zeros_like(st_ref)

    glog = _gla_gate_log(glr_ref[0], wg_ref[...], bg_ref[...])
    row = lax.broadcasted_iota(jnp.int32, (q, q), 0)
    col = lax.broadcasted_iota(jnp.int32, (q, q), 1)
    tril = row >= col
    bc = _dot(tril.astype(F32), glog, HI)
    last = bc[q - 1 : q, :]
    kk = k_ref[0]
    qe = q_ref[0] * (GLA_HEAD_K ** -0.5) * jnp.exp(bc)
    ke = kk * jnp.exp(-bc)
    kd = kk * jnp.exp(last - bc)
    elast = jnp.exp(last)
    v16 = v_ref[0].astype(BF16)
    zeros_v = jnp.zeros((q, GLA_HEAD_V), BF16)
    outs = []
    for h in range(GLA_HEADS):
        ks = slice(h * GLA_HEAD_K, (h + 1) * GLA_HEAD_K)
        vs = slice(h * GLA_HEAD_V, (h + 1) * GLA_HEAD_V)
        qh = qe[:, ks].astype(BF16)
        kh = ke[:, ks].astype(BF16)
        att = jnp.where(tril, _dot_nt(qh, kh), 0.0)
        s_h = st_ref[ks, :]
        outs.append(_dot(att.astype(BF16), v16[:, vs]) + _dot(qh, s_h.astype(BF16)))
        xt = jnp.concatenate([kd[:, ks], jnp.broadcast_to(elast[:, ks], (q, GLA_HEAD_K))], axis=0).T
        v2 = jnp.concatenate([v16[:, vs], zeros_v], axis=0)
        st_ref[ks, :] = xt[:, q : q + 1] * s_h + _dot(xt.astype(BF16), v2)
    o = jnp.concatenate(outs, axis=1)
    o_ref[0] = _gla_out_norm(o, r_ref[0], nrm_ref[...]).astype(o_ref.dtype)

    @pl.when(c == nc - 1)
    def _():
        sout_ref[0] = st_ref[...]


def gla_scan(proj3, p):
    nb, seq, _ = proj3.shape
    q = GLA_CHUNK
    nc = seq // q
    o, s_out = pl.pallas_call(
        functools.partial(_gla_scan_kernel, nc=nc),
        grid=(nb, nc),
        in_specs=[
            pl.BlockSpec((1, q, GLA_KEY_DIM), lambda b, c: (b, c, COL_Q // GLA_KEY_DIM)),
            pl.BlockSpec((1, q, GLA_KEY_DIM), lambda b, c: (b, c, COL_K // GLA_KEY_DIM)),
            pl.BlockSpec((1, q, D_MODEL), lambda b, c: (b, c, COL_V // D_MODEL)),
            pl.BlockSpec((1, q, D_MODEL), lambda b, c: (b, c, COL_R // D_MODEL)),
            pl.BlockSpec((1, q, LANE), lambda b, c: (b, c, COL_GLR // LANE)),
            pl.BlockSpec((LANE, GLA_KEY_DIM), lambda b, c: (0, 0)),
            pl.BlockSpec((1, GLA_KEY_DIM), lambda b, c: (0, 0)),
            pl.BlockSpec((1, GLA_HEAD_V), lambda b, c: (0, 0)),
        ],
        out_specs=[
            pl.BlockSpec((1, q, D_MODEL), lambda b, c: (b, c, 0)),
            pl.BlockSpec((1, GLA_KEY_DIM, GLA_HEAD_V), lambda b, c: (b, 0, 0)),
        ],
        out_shape=[
            jax.ShapeDtypeStruct((nb, seq, D_MODEL), BF16),
            jax.ShapeDtypeStruct((nb, GLA_KEY_DIM, GLA_HEAD_V), F32),
        ],
        scratch_shapes=[pltpu.VMEM((GLA_KEY_DIM, GLA_HEAD_V), F32)],
        compiler_params=_cp("parallel", "arbitrary"),
        name="gla_scan",
    )(proj3, proj3, proj3, proj3, proj3, p["w_gate"], p["b_gate"], p["gla_norm"])
    return o.reshape(nb * seq, D_MODEL), s_out.reshape(nb, GLA_HEADS, GLA_HEAD_K, GLA_HEAD_V)


TOK_BLOCK = 128


def _row_shift(x, d, tpos):
    return jnp.where(tpos >= d, pltpu.roll(x, d, 0), 0.0)


def _seq_cumsum_and_last(a, seq, tpos):
    nrows = a.shape[0]
    acum = a
    for d in range(1, seq):
        acum = acum + _row_shift(a, d, tpos)
    last = jnp.where(tpos == seq - 1, acum, 0.0)
    for d in range(1, seq):
        last = last + jnp.where(tpos == seq - 1 - d, pltpu.roll(acum, nrows - d, 0), 0.0)
    return acum, last


def _ssd_step_pre_kernel(xs_ref, b_ref, c_ref, dt_ref, dtb_ref, alog_ref, dexp_ref, e_ref, gh_ref,
                         ypart_ref, eexp_ref, xwt_ref, elt_ref, *, seq):
    nrows = xs_ref.shape[0]
    xs = xs_ref[...]
    bm = b_ref[...]
    cm = c_ref[...]
    dt = _softplus(dt_ref[...] + dtb_ref[...])
    a = dt * (-jnp.exp(alog_ref[...]))
    pos = lambda w: lax.broadcasted_iota(jnp.int32, (nrows, w), 0) % seq
    t_h, t_c, t_x = pos(LANE), pos(BC_W), pos(D_MODEL)
    acum, last = _seq_cumsum_and_last(a, seq, t_h)
    e_mat = e_ref[...]
    eexp_ref[...] = _dot(jnp.exp(acum), e_mat, HI)
    wexp = _dot(jnp.exp(last - acum) * dt, e_mat, HI)
    xwt_ref[...] = (xs * wexp).T
    elt_ref[...] = _dot(jnp.exp(last), e_mat, HI).T
    y = dexp_ref[...] * xs
    for d in range(seq):
        if d == 0:
            cbh = _dot(cm * bm, gh_ref[...], HI)
            coef = dt
            xd = xs
        else:
            cbh = _dot(cm * _row_shift(bm, d, t_c), gh_ref[...], HI)
            coef = jnp.where(t_h >= d, jnp.exp(acum - pltpu.roll(acum, d, 0)) * pltpu.roll(dt, d, 0), 0.0)
            xd = _row_shift(xs, d, t_x)
        y = y + _dot(cbh * coef, e_mat, HI) * xd
    ypart_ref[...] = y


def _ssd_step_state_kernel(st_ref, c_ref, b_ref, xwt_ref, elt_ref, ypart_ref, eexp_ref, z_ref, nrm_ref,
                           u_ref, so_ref, *, sb, seq):
    i = pl.program_id(0)
    rows = sb * seq
    steps_per_block = TOK_BLOCK // rows
    base = (i % steps_per_block) * rows
    c16 = c_ref[...].astype(BF16)
    btok = b_ref[...]
    tok = lax.broadcasted_iota(jnp.int32, (TOK_BLOCK, LANE), 0)
    rsel = lax.broadcasted_iota(jnp.int32, (rows, GROUP_W), 0)
    zeros = jnp.zeros((TOK_BLOCK, LANE), F32)
    yoff = [jnp.zeros((rows, GROUP_W), F32) for _ in range(N_GROUPS)]
    for s in range(sb):
        lo = base + seq * s
        own = (tok >= lo) & (tok < lo + seq)
        first = jnp.where(tok == lo, 1.0, 0.0)
        mine = (rsel >= seq * s) & (rsel < seq * (s + 1))
        for g in range(N_GROUPS):
            sl = slice(g * GROUP_W, (g + 1) * GROUP_W)
            s_g = st_ref[s, sl, :]
            yo = _dot_nt(c16[:, g * D_STATE : (g + 1) * D_STATE], s_g.astype(BF16))
            yoff[g] = jnp.where(mine, yo, yoff[g])
            bsel = jnp.where(own, btok[:, g * D_STATE : (g + 1) * D_STATE], 0.0)
            rhs = jnp.concatenate(
                [jnp.concatenate([bsel, zeros], axis=1), jnp.concatenate([zeros, first], axis=1)], axis=0)
            lhs = jnp.concatenate([xwt_ref[sl, :], elt_ref[sl, :]], axis=1)
            res = _dot(lhs, rhs, HI)
            so_ref[s, sl, :] = res[:, LANE:] * s_g + res[:, :LANE]
    y = ypart_ref[...] + jnp.concatenate(yoff, axis=1) * eexp_ref[...]
    u_ref[...] = _ssd_gate_norm(y, z_ref[...], nrm_ref[...]).astype(u_ref.dtype)


def ssd_step(xc2, proj2, state, p, seq):
    ntok = xc2.shape[0]
    nseq = ntok // seq
    full = lambda shape: pl.BlockSpec(shape, lambda i: (0,) * len(shape))
    ypart, eexp, xwt, elt = pl.pallas_call(
        functools.partial(_ssd_step_pre_kernel, seq=seq),
        grid=(1,),
        in_specs=[
            pl.BlockSpec((ntok, D_MODEL), lambda i: (0, 0)),
            pl.BlockSpec((ntok, BC_W), lambda i: (0, D_MODEL // BC_W)),
            pl.BlockSpec((ntok, BC_W), lambda i: (0, D_MODEL // BC_W + 1)),
            pl.BlockSpec((ntok, LANE), lambda i: (0, COL_DT // LANE)),
            full((1, LANE)), full((1, LANE)), full((1, D_MODEL)),
            full((LANE, D_MODEL)), full((BC_W, LANE)),
        ],
        out_specs=[full((ntok, D_MODEL)), full((ntok, D_MODEL)), full((D_MODEL, ntok)), full((D_MODEL, ntok))],
        out_shape=[
            jax.ShapeDtypeStruct((ntok, D_MODEL), F32),
            jax.ShapeDtypeStruct((ntok, D_MODEL), F32),
            jax.ShapeDtypeStruct((D_MODEL, ntok), F32),
            jax.ShapeDtypeStruct((D_MODEL, ntok), F32),
        ],
        compiler_params=_cp("arbitrary"),
        name="ssd_step_pre",
    )(xc2, xc2, xc2, proj2, p["dt_bias"], p["a_log"], p["d_exp"], p["e_head"], p["g_head"])

    sb = 4
    rows = sb * seq
    spb = TOK_BLOCK // rows
    st3 = state.reshape(nseq, D_MODEL, D_STATE)
    u, s_new = pl.pallas_call(
        functools.partial(_ssd_step_state_kernel, sb=sb, seq=seq),
        grid=(nseq // sb,),
        in_specs=[
            pl.BlockSpec((sb, D_MODEL, D_STATE), lambda i: (i, 0, 0)),
            pl.BlockSpec((rows, BC_W), lambda i: (i, D_MODEL // BC_W + 1)),
            pl.BlockSpec((TOK_BLOCK, BC_W), lambda i: (i // spb, D_MODEL // BC_W)),
            pl.BlockSpec((D_MODEL, TOK_BLOCK), lambda i: (0, i // spb)),
            pl.BlockSpec((D_MODEL, TOK_BLOCK), lambda i: (0, i // spb)),
            pl.BlockSpec((rows, D_MODEL), lambda i: (i, 0)),
            pl.BlockSpec((rows, D_MODEL), lambda i: (i, 0)),
            pl.BlockSpec((rows, D_MODEL), lambda i: (i, COL_Z // D_MODEL)),
            pl.BlockSpec((1, D_MODEL), lambda i: (0, 0)),
        ],
        out_specs=[
            pl.BlockSpec((rows, D_MODEL), lambda i: (i, 0)),
            pl.BlockSpec((sb, D_MODEL, D_STATE), lambda i: (i, 0, 0)),
        ],
        out_shape=[
            jax.ShapeDtypeStruct((ntok, D_MODEL), BF16),
            jax.ShapeDtypeStruct((nseq, D_MODEL, D_STATE), F32),
        ],
        compiler_params=_cp("parallel"),
        name="ssd_step_state",
    )(st3, xc2, xc2, xwt, elt, ypart, eexp, proj2, p["ssd_norm"])
    return u, s_new.reshape(nseq, SSD_HEADS, SSD_HEAD_DIM, D_STATE)


def _gla_step_pre_kernel(q_ref, k_ref, v_ref, glr_ref, wg_ref, bg_ref, gv_ref,
                         oin_ref, qe_ref, kdt_ref, elt_ref, *, seq):
    nrows = q_ref.shape[0]
    glog = _gla_gate_log(glr_ref[...], wg_ref[...], bg_ref[...])
    pos = lambda w: lax.broadcasted_iota(jnp.int32, (nrows, w), 0) % seq
    t_k, t_v = pos(GLA_KEY_DIM), pos(D_MODEL)
    bc, last = _seq_cumsum_and_last(glog, seq, t_k)
    kk = k_ref[...]
    qe = q_ref[...] * (GLA_HEAD_K ** -0.5) * jnp.exp(bc)
    ke = kk * jnp.exp(-bc)
    qe_ref[...] = qe
    kdt_ref[...] = (kk * jnp.exp(last - bc)).T
    elt_ref[...] = jnp.exp(last).T
    v = v_ref[...]
    gv = gv_ref[...]
    o = jnp.zeros((nrows, D_MODEL), F32)
    for d in range(seq):
        ked = ke if d == 0 else _row_shift(ke, d, t_k)
        vd = v if d == 0 else _row_shift(v, d, t_v)
        att = _dot((qe * ked).astype(BF16), gv)
        o = o + att * vd
    oin_ref[...] = o


def _gla_step_state_kernel(st_ref, qe_ref, v_ref, kdt_ref, elt_ref, oin_ref, r_ref, nrm_ref,
                           o_ref, so_ref, *, sb, seq):
    i = pl.program_id(0)
    rows = sb * seq
    spb = TOK_BLOCK // rows
    base = (i % spb) * rows
    qe16 = qe_ref[...].astype(BF16)
    vtok = v_ref[...]
    tokv = lax.broadcasted_iota(jnp.int32, (TOK_BLOCK, GLA_HEAD_V), 0)
    tok = lax.broadcasted_iota(jnp.int32, (TOK_BLOCK, LANE), 0)
    rsel = lax.broadcasted_iota(jnp.int32, (rows, GLA_HEAD_V), 0)
    ooff = [jnp.zeros((rows, GLA_HEAD_V), F32) for _ in range(GLA_HEADS)]
    for s in range(sb):
        lo = base + seq * s
        own = (tokv >= lo) & (tokv < lo + seq)
        first = jnp.where(tok == lo, 1.0, 0.0)
        mine = (rsel >= seq * s) & (rsel < seq * (s + 1))
        for h in range(GLA_HEADS):
            ks = slice(h * GLA_HEAD_K, (h + 1) * GLA_HEAD_K)
            vs = slice(h * GLA_HEAD_V, (h + 1) * GLA_HEAD_V)
            s_h = st_ref[s, ks, :]
            oo = _dot(qe16[:, ks], s_h.astype(BF16))
            ooff[h] = jnp.where(mine, oo, ooff[h])
            vsel = jnp.where(own, vtok[:, vs], 0.0).astype(BF16)
            upd = _dot(kdt_ref[ks, :].astype(BF16), vsel)
            ecol = _dot(elt_ref[ks, :], first, HI)
            so_ref[s, ks, :] = jnp.concatenate([ecol] * (GLA_HEAD_V // LANE), axis=1) * s_h + upd
    o = oin_ref[...] + jnp.concatenate(ooff, axis=1)
    o_ref[0] = _gla_out_norm(o, r_ref[...], nrm_ref[...]).astype(o_ref.dtype)


def gla_step(proj2, state, p, seq):
    ntok = proj2.shape[0]
    nseq = ntok // seq
    full = lambda shape: pl.BlockSpec(shape, lambda i: (0,) * len(shape))
    oin, qe, kdt, elt = pl.pallas_call(
        functools.partial(_gla_step_pre_kernel, seq=seq),
        grid=(1,),
        in_specs=[
            pl.BlockSpec((ntok, GLA_KEY_DIM), lambda i: (0, COL_Q // GLA_KEY_DIM)),
            pl.BlockSpec((ntok, GLA_KEY_DIM), lambda i: (0, COL_K // GLA_KEY_DIM)),
            pl.BlockSpec((ntok, D_MODEL), lambda i: (0, COL_V // D_MODEL)),
            pl.BlockSpec((ntok, LANE), lambda i: (0, COL_GLR // LANE)),
            full((LANE, GLA_KEY_DIM)), full((1, GLA_KEY_DIM)), full((GLA_KEY_DIM, D_MODEL)),
        ],
        out_specs=[full((ntok, D_MODEL)), full((ntok, GLA_KEY_DIM)), full((GLA_KEY_DIM, ntok)),
                   full((GLA_KEY_DIM, ntok))],
        out_shape=[
            jax.ShapeDtypeStruct((ntok, D_MODEL), F32),
            jax.ShapeDtypeStruct((ntok, GLA_KEY_DIM), F32),
            jax.ShapeDtypeStruct((GLA_KEY_DIM, ntok), F32),
            jax.ShapeDtypeStruct((GLA_KEY_DIM, ntok), F32),
        ],
        compiler_params=_cp("arbitrary"),
        name="gla_step_pre",
    )(proj2, proj2, proj2, proj2, p["w_gate"], p["b_gate"], p["g_val"])

    sb = 2
    rows = sb * seq
    spb = TOK_BLOCK // rows
    st3 = state.reshape(nseq, GLA_KEY_DIM, GLA_HEAD_V)
    o, s_new = pl.pallas_call(
        functools.partial(_gla_step_state_kernel, sb=sb, seq=seq),
        grid=(nseq // sb,),
        in_specs=[
            pl.BlockSpec((sb, GLA_KEY_DIM, GLA_HEAD_V), lambda i: (i, 0, 0)),
            pl.BlockSpec((rows, GLA_KEY_DIM), lambda i: (i, 0)),
            pl.BlockSpec((TOK_BLOCK, D_MODEL), lambda i: (i // spb, COL_V // D_MODEL)),
            pl.BlockSpec((GLA_KEY_DIM, TOK_BLOCK), lambda i: (0, i // spb)),
            pl.BlockSpec((GLA_KEY_DIM, TOK_BLOCK), lambda i: (0, i // spb)),
            pl.BlockSpec((rows, D_MODEL), lambda i: (i, 0)),
            pl.BlockSpec((rows, D_MODEL), lambda i: (i, COL_R // D_MODEL)),
            pl.BlockSpec((1, GLA_HEAD_V), lambda i: (0, 0)),
        ],
        out_specs=[
            pl.BlockSpec((1, rows, D_MODEL), lambda i: (i, 0, 0)),
            pl.BlockSpec((sb, GLA_KEY_DIM, GLA_HEAD_V), lambda i: (i, 0, 0)),
        ],
        out_shape=[
            jax.ShapeDtypeStruct((nseq // sb, rows, D_MODEL), BF16),
            jax.ShapeDtypeStruct((nseq, GLA_KEY_DIM, GLA_HEAD_V), F32),
        ],
        compiler_params=_cp("parallel"),
        name="gla_step_state",
    )(st3, qe, proj2, kdt, elt, oin, proj2, p["gla_norm"])
    return o.reshape(ntok, D_MODEL), s_new.reshape(nseq, GLA_HEADS, GLA_HEAD_K, GLA_HEAD_V)


def _xattn_kernel(q_ref, k_ref, v_ref, o_ref, *, nseq, tl, split_heads):
    rows = nseq * tl
    q16 = q_ref[0].astype(BF16)
    rsel = lax.broadcasted_iota(jnp.int32, (rows, CROSS_HEAD_DIM), 0)
    outs = [jnp.zeros((rows, CROSS_HEAD_DIM), F32) for _ in range(CROSS_HEADS)]
    for s in range(nseq):
        for h in range(CROSS_HEADS):
            hs = slice(h * CROSS_HEAD_DIM, (h + 1) * CROSS_HEAD_DIM)
            if split_heads:
                k16 = k_ref[s, :, h, :].astype(BF16)
                v16 = v_ref[s, :, h, :].astype(BF16)
            else:
                k16 = k_ref[s, :, hs].astype(BF16)
                v16 = v_ref[s, :, hs].astype(BF16)
            sc = _dot_nt(q16[:, hs], k16) * (CROSS_HEAD_DIM ** -0.5)
            e = jnp.exp(sc - jnp.max(sc, axis=-1, keepdims=True))
            pr = e / jnp.sum(e, axis=-1, keepdims=True)
            oh = _dot(pr.astype(BF16), v16)
            if nseq == 1:
                outs[h] = oh
            else:
                outs[h] = jnp.where((rsel >= tl * s) & (rsel < tl * (s + 1)), oh, outs[h])
    o_ref[0] = jnp.concatenate(outs, axis=1).astype(o_ref.dtype)


def cross_attend(q2, mem_k, mem_v, nb, seq):
    n_mem = mem_k.shape[1]
    split_heads = mem_k.ndim == 4
    if seq >= 8:
        nseq, tl = 1, min(seq, 512)
    else:
        nseq, tl = 8 // seq, seq
    rows = nseq * tl
    lt = seq // tl
    nblk = nb * seq // rows
    q3 = q2.reshape(nblk, rows, D_MODEL)
    if split_heads:
        kv_spec = pl.BlockSpec((nseq, n_mem, CROSS_HEADS, CROSS_HEAD_DIM), lambda i: (i // lt, 0, 0, 0))
    else:
        kv_spec = pl.BlockSpec((nseq, n_mem, D_MODEL), lambda i: (i // lt, 0, 0))
    o = pl.pallas_call(
        functools.partial(_xattn_kernel, nseq=nseq, tl=tl, split_heads=split_heads),
        grid=(nblk,),
        in_specs=[pl.BlockSpec((1, rows, D_MODEL), lambda i: (i, 0, 0)), kv_spec, kv_spec],
        out_specs=pl.BlockSpec((1, rows, D_MODEL), lambda i: (i, 0, 0)),
        out_shape=jax.ShapeDtypeStruct((nblk, rows, D_MODEL), BF16),
        compiler_params=_cp("parallel"),
        name="cross_attend",
    )(q3, mem_k, mem_v)
    return o.reshape(nb * seq, D_MODEL)


def _pack_w_in(w_in):
    sizes = (D_MODEL, CONV_DIM, SSD_HEADS, GLA_KEY_DIM, GLA_KEY_DIM, D_MODEL, D_MODEL, GATE_RANK, D_MODEL, D_MODEL)
    offs = [0]
    for s in sizes:
        offs.append(offs[-1] + s)
    z, xbc, dt, q, k, v, r, glr, ga, gb = [w_in[:, offs[i] : offs[i + 1]] for i in range(len(sizes))]
    pad = lambda a, n: jnp.pad(a, ((0, 0), (0, n - a.shape[1])))
    parts = [z, ga, gb, v, r, xbc, q, k, pad(dt, LANE), pad(glr, LANE)]
    used = sum(a.shape[1] for a in parts)
    parts.append(jnp.zeros((w_in.shape[0], NPROJ - used), w_in.dtype))
    return jnp.concatenate(parts, axis=1).astype(BF16)


def _params(ssd_dt_bias, ssd_A_log, ssd_D, ssd_norm, w_gla_gate, b_gla_gate, gla_norm):
    padv = lambda a: jnp.pad(a.astype(F32), (0, LANE - a.shape[0])).reshape(1, LANE)
    head_of_chan = jnp.arange(D_MODEL, dtype=jnp.int32) // SSD_HEAD_DIM
    e_head = (jnp.arange(LANE, dtype=jnp.int32)[:, None] == head_of_chan[None, :]).astype(F32)
    group_of_bc = jnp.arange(BC_W, dtype=jnp.int32) // D_STATE
    lane_h = jnp.arange(LANE, dtype=jnp.int32)
    g_head = ((lane_h[None, :] // (SSD_HEADS // N_GROUPS) == group_of_bc[:, None])
              & (lane_h[None, :] < SSD_HEADS)).astype(F32)
    khead = jnp.arange(GLA_KEY_DIM, dtype=jnp.int32) // GLA_HEAD_K
    vhead = jnp.arange(D_MODEL, dtype=jnp.int32) // GLA_HEAD_V
    g_val = (khead[:, None] == vhead[None, :]).astype(BF16)
    return dict(
        dt_bias=padv(ssd_dt_bias), a_log=padv(ssd_A_log),
        d_exp=jnp.repeat(ssd_D.astype(F32), SSD_HEAD_DIM).reshape(1, D_MODEL),
        ssd_norm=ssd_norm.astype(F32).reshape(1, D_MODEL),
        e_head=e_head, g_head=g_head, g_val=g_val,
        w_gate=jnp.pad(w_gla_gate, ((0, LANE - GATE_RANK), (0, 0))).astype(BF16),
        b_gate=b_gla_gate.astype(F32).reshape(1, GLA_KEY_DIM),
        gla_norm=gla_norm.astype(F32).reshape(1, GLA_HEAD_V),
    )


def _layer(x3, mem_k, mem_v, ssd_conv, ssd_state, gla_state, ffn_conv, w, p, long_seq):
    nb, seq, d = x3.shape
    ntok = nb * seq
    ffn = w["w_down"].shape[0]
    x2 = x3.reshape(ntok, d)
    xn = rmsnorm_cast(x2, w["norm_mix"])
    proj = matmul(xn, w["w_in"], name="in_proj")
    proj3 = proj.reshape(nb, seq, NPROJ)
    if long_seq:
        u, ssd_new, ssd_conv_new = ssd_scan(proj3, ssd_conv, w["ssd_conv_w"], w["ssd_conv_b"], p)
        o, gla_new = gla_scan(proj3, p)
    else:
        xc3, ssd_conv_new = causal_conv(proj3, [COL_XBC], CONV_DIM, [ssd_conv], [w["ssd_conv_w"]],
                                        [w["ssd_conv_b"]], SSD_CONV, False, F32, "ssd_conv")
        u, ssd_new = ssd_step(xc3.reshape(ntok, CONV_DIM), proj, ssd_state, p, seq)
        o, gla_new = gla_step(proj, gla_state, p, seq)
    merged = merge_branches(u, o, w["w_ssd_out"], w["w_gla_out"], proj)
    h, hn = mm_res_norm(merged, w["w_mix_out"], x2, w["norm_cross"], True, BF16, "mix_out")
    qc = matmul(hn, w["w_cq"], out_dtype=BF16 if long_seq else F32, name="cross_q")
    att = cross_attend(qc, mem_k, mem_v, nb, seq)
    h2, hn2 = mm_res_norm(att, w["w_co"], h, w["norm_ffn"], True, BF16, "cross_out")
    cw, cbias = w["ffn_conv_w"], w["ffn_conv_b"]
    if long_seq:
        act, ffn_conv_new = ffn_up_conv_act(hn2.reshape(nb, seq, d), w["w_up"], cw, cbias, ffn_conv)
    else:
        up = matmul(hn2, w["w_up"], name="ffn_up").reshape(nb, seq, 2 * ffn)
        act, fa, fg = causal_conv(up, [0, ffn], ffn, [ffn_conv[:, :, :ffn], ffn_conv[:, :, ffn:]],
                                  [cw[:, :ffn], cw[:, ffn:]], [cbias[:ffn], cbias[ffn:]],
                                  FFN_CONV, True, BF16, "ffn_conv")
        ffn_conv_new = jnp.concatenate([fa, fg], axis=-1)
    y = mm_res_norm(act.reshape(ntok, ffn), w["w_down"], h2, w["norm_final"], False, F32, "ffn_down")
    return y.reshape(nb, seq, d), ssd_conv_new, ssd_new, gla_new, ffn_conv_new


def kernel(x_prompt, x_sample, cache_mem_k, cache_mem_v, state_ssd_conv, state_ssd, state_gla, state_ffn_conv, mem_prompt, norm_mix, w_in, ssd_conv_w, ssd_conv_b, ssd_dt_bias, ssd_A_log, ssd_D, ssd_norm, w_ssd_out, w_gla_gate, b_gla_gate, gla_norm, w_gla_out, w_mix_out, norm_cross, norm_mem, w_cq, w_ck, w_cv, w_co, norm_ffn, w_up, ffn_conv_w, ffn_conv_b, w_down, norm_final):
    nb, seq, d = x_prompt.shape
    n_mem = mem_prompt.shape[1]
    ffn2 = w_up.shape[1]
    w = dict(
        norm_mix=norm_mix, norm_cross=norm_cross, norm_ffn=norm_ffn, norm_final=norm_final,
        w_in=_pack_w_in(w_in), ssd_conv_w=ssd_conv_w, ssd_conv_b=ssd_conv_b,
        w_ssd_out=w_ssd_out.astype(BF16), w_gla_out=w_gla_out.astype(BF16), w_mix_out=w_mix_out.astype(BF16),
        w_cq=w_cq.astype(BF16), w_co=w_co.astype(BF16), w_up=w_up.astype(BF16), w_down=w_down.astype(BF16),
        ffn_conv_w=ffn_conv_w, ffn_conv_b=ffn_conv_b,
    )
    p = _params(ssd_dt_bias, ssd_A_log, ssd_D, ssd_norm, w_gla_gate, b_gla_gate, gla_norm)

    mn = rmsnorm_cast(mem_prompt.reshape(nb * n_mem, d), norm_mem)
    p_mem_k = matmul(mn, w_ck.astype(BF16), name="mem_k").reshape(nb, n_mem, d)
    p_mem_v = matmul(mn, w_cv.astype(BF16), name="mem_v").reshape(nb, n_mem, d)
    zeros_ssd_conv = jnp.zeros((nb, SSD_CONV - 1, CONV_DIM), F32)
    zeros_ffn_conv = jnp.zeros((nb, FFN_CONV - 1, ffn2), F32)
    y_prompt, p_ssd_conv, p_ssd, p_gla, p_ffn_conv = _layer(
        x_prompt, p_mem_k, p_mem_v, zeros_ssd_conv, None, None, zeros_ffn_conv, w, p, True)

    ns = x_sample.shape[0]
    y_sample, s_ssd_conv, s_ssd, s_gla, s_ffn_conv = _layer(
        x_sample, cache_mem_k, cache_mem_v, state_ssd_conv, state_ssd, state_gla, state_ffn_conv, w, p, False)

    head_shape = (n_mem, CROSS_HEADS, CROSS_HEAD_DIM)
    return (y_prompt, y_sample, p_ssd_conv, p_ssd, p_gla, p_ffn_conv,
            p_mem_k.reshape((nb,) + head_shape), p_mem_v.reshape((nb,) + head_shape),
            s_ssd_conv, s_ssd, s_gla, s_ffn_conv)
```

```python
import functools

import jax
import jax.numpy as jnp
from jax import lax
from jax.experimental import pallas as pl
from jax.experimental.pallas import tpu as pltpu

F32 = jnp.float32
BF16 = jnp.bfloat16
HI = lax.Precision.HIGHEST
EPS = 1e-6
NEG_BIG = -1e30

D_MODEL = 2048
SSD_HEAD_DIM = 64
SSD_HEADS = 32
D_STATE = 128
N_GROUPS = 4
GROUP_W = D_MODEL // N_GROUPS
BC_W = N_GROUPS * D_STATE
CONV_DIM = D_MODEL + 2 * BC_W
SSD_CONV = 4
SSD_CHUNK = 128
GLA_HEADS = 4
GLA_KEY_DIM = 1024
GLA_HEAD_K = 256
GLA_HEAD_V = 512
GATE_RANK = 16
GATE_TAU = 16.0
GLA_CHUNK = 64
CROSS_HEADS = 4
CROSS_HEAD_DIM = 512
FFN_CONV = 3
LANE = 128

COL_Z = 0
COL_GA = 2048
COL_GB = 4096
COL_V = 6144
COL_R = 8192
COL_XBC = 10240
COL_Q = 13312
COL_K = 14336
COL_DT = 15360
COL_GLR = 15488
NPROJ = 15872

VMEM_LIMIT = 56 * 1024 * 1024


def _cp(*sem):
    return pltpu.CompilerParams(dimension_semantics=sem, vmem_limit_bytes=VMEM_LIMIT)


def _dot(a, b, prec=None):
    return jnp.dot(a, b, preferred_element_type=F32, precision=prec)


def _dot_nt(a, b):
    return lax.dot_general(a, b, (((1,), (1,)), ((), ())), preferred_element_type=F32)


def _sigmoid(x):
    return 1.0 / (1.0 + jnp.exp(-x))


def _silu(x):
    return x * _sigmoid(x)


def _softplus(x):
    return jnp.maximum(x, 0.0) + jnp.log(1.0 + jnp.exp(-jnp.abs(x)))


def _rms(x, g):
    ms = jnp.mean(x * x, axis=-1, keepdims=True)
    return x * lax.rsqrt(ms + EPS) * g


def _rmsnorm_kernel(x_ref, g_ref, o_ref):
    o_ref[...] = _rms(x_ref[...], g_ref[...]).astype(o_ref.dtype)


def rmsnorm_cast(x2, g):
    m, d = x2.shape
    tm = min(m, 512)
    return pl.pallas_call(
        _rmsnorm_kernel,
        grid=(m // tm,),
        in_specs=[pl.BlockSpec((tm, d), lambda i: (i, 0)), pl.BlockSpec((1, d), lambda i: (0, 0))],
        out_specs=pl.BlockSpec((tm, d), lambda i: (i, 0)),
        out_shape=jax.ShapeDtypeStruct((m, d), BF16),
        compiler_params=_cp("parallel"),
        name="rmsnorm_cast",
    )(x2, g.reshape(1, d))


def _mm_kernel(a_ref, w_ref, o_ref):
    o_ref[...] = _dot(a_ref[...], w_ref[...]).astype(o_ref.dtype)


def matmul(a, w, out_dtype=F32, tn=512, name="matmul"):
    m, k = a.shape
    n = w.shape[1]
    tm = min(m, 1024)
    return pl.pallas_call(
        _mm_kernel,
        grid=(m // tm, n // tn),
        in_specs=[pl.BlockSpec((tm, k), lambda i, j: (i, 0)), pl.BlockSpec((k, tn), lambda i, j: (0, j))],
        out_specs=pl.BlockSpec((tm, tn), lambda i, j: (i, j)),
        out_shape=jax.ShapeDtypeStruct((m, n), out_dtype),
        compiler_params=_cp("parallel", "arbitrary"),
        name=name,
    )(a, w)


def _mm_res_norm_kernel(a_ref, w_ref, res_ref, g_ref, *out_refs):
    h = res_ref[...] + _dot(a_ref[...], w_ref[...])
    if len(out_refs) == 2:
        out_refs[0][...] = h
    out_refs[-1][...] = _rms(h, g_ref[...]).astype(out_refs[-1].dtype)


def _mm_res_norm_coltile_kernel(a_ref, w_ref, res_ref, g_ref, n_ref, h_buf, ssq_ref, *, tn):
    j = pl.program_id(1)
    nj = pl.num_programs(1)
    h = res_ref[...] + _dot(a_ref[...], w_ref[...])
    h_buf[:, pl.ds(pl.multiple_of(j * tn, tn), tn)] = h
    part = jnp.sum(h * h, axis=-1, keepdims=True)

    @pl.when(j == 0)
    def _():
        ssq_ref[...] = part

    @pl.when(j > 0)
    def _():
        ssq_ref[...] += part

    @pl.when(j == nj - 1)
    def _():
        hh = h_buf[...]
        scale = lax.rsqrt(ssq_ref[...] * (1.0 / hh.shape[1]) + EPS)
        n_ref[...] = (hh * scale * g_ref[...]).astype(n_ref.dtype)


def mm_res_norm(a, w, res, g, emit_h, norm_dtype, name):
    m, kdim = a.shape
    n = w.shape[1]
    tm = min(m, 512)
    if kdim > 2048:
        assert not emit_h
        tn = 512
        return pl.pallas_call(
            functools.partial(_mm_res_norm_coltile_kernel, tn=tn),
            grid=(m // tm, n // tn),
            in_specs=[
                pl.BlockSpec((tm, kdim), lambda i, j: (i, 0)),
                pl.BlockSpec((kdim, tn), lambda i, j: (0, j)),
                pl.BlockSpec((tm, tn), lambda i, j: (i, j)),
                pl.BlockSpec((1, n), lambda i, j: (0, 0)),
            ],
            out_specs=pl.BlockSpec((tm, n), lambda i, j: (i, 0)),
            out_shape=jax.ShapeDtypeStruct((m, n), norm_dtype),
            scratch_shapes=[pltpu.VMEM((tm, n), F32), pltpu.VMEM((tm, 1), F32)],
            compiler_params=_cp("parallel", "arbitrary"),
            name=name,
        )(a, w, res, g.reshape(1, n))
    row_tile = lambda width: pl.BlockSpec((tm, width), lambda i: (i, 0))
    out_shape = [jax.ShapeDtypeStruct((m, n), norm_dtype)]
    out_specs = [row_tile(n)]
    if emit_h:
        out_shape = [jax.ShapeDtypeStruct((m, n), F32)] + out_shape
        out_specs = [row_tile(n)] + out_specs
    outs = pl.pallas_call(
        _mm_res_norm_kernel,
        grid=(m // tm,),
        in_specs=[row_tile(kdim), pl.BlockSpec((kdim, n), lambda i: (0, 0)), row_tile(n),
                  pl.BlockSpec((1, n), lambda i: (0, 0))],
        out_specs=out_specs,
        out_shape=out_shape,
        compiler_params=_cp("parallel"),
        name=name,
    )(a, w, res, g.reshape(1, n))
    return outs if emit_h else outs[0]


def _merge_kernel(u_ref, o_ref, wa_ref, wb_ref, ga_ref, gb_ref, out_ref):
    a = _dot(u_ref[...], wa_ref[...])
    b = _dot(o_ref[...], wb_ref[...])
    out_ref[...] = (_sigmoid(ga_ref[...]) * a + _sigmoid(gb_ref[...]) * b).astype(out_ref.dtype)


def merge_branches(u, o, wa, wb, proj):
    m, d = u.shape
    tm, tn = min(m, 1024), 512
    return pl.pallas_call(
        _merge_kernel,
        grid=(m // tm, d // tn),
        in_specs=[
            pl.BlockSpec((tm, d), lambda i, j: (i, 0)),
            pl.BlockSpec((tm, d), lambda i, j: (i, 0)),
            pl.BlockSpec((d, tn), lambda i, j: (0, j)),
            pl.BlockSpec((d, tn), lambda i, j: (0, j)),
            pl.BlockSpec((tm, tn), lambda i, j: (i, COL_GA // tn + j)),
            pl.BlockSpec((tm, tn), lambda i, j: (i, COL_GB // tn + j)),
        ],
        out_specs=pl.BlockSpec((tm, tn), lambda i, j: (i, j)),
        out_shape=jax.ShapeDtypeStruct((m, d), BF16),
        compiler_params=_cp("parallel", "arbitrary"),
        name="merge_branches",
    )(u, o, wa, wb, proj, proj)


CONV_PAD = 8


def _conv_kernel(*refs, taps, tl, nstreams, swiglu):
    ins = refs[: 4 * nstreams]
    out_ref = refs[4 * nstreams]
    ns_refs = refs[4 * nstreams + 1 : 4 * nstreams + 1 + nstreams]
    bufs = refs[4 * nstreams + 1 + nstreams :]
    l = pl.program_id(2)
    nl = pl.num_programs(2)
    lo = CONV_PAD - (taps - 1)
    vals = []
    for s in range(nstreams):
        u_ref, st_ref, w_ref, b_ref = ins[4 * s : 4 * s + 4]
        buf = bufs[s]

        @pl.when(l == 0)
        def _():
            buf[:, lo:CONV_PAD, :] = st_ref[...]

        buf[:, CONV_PAD : CONV_PAD + tl, :] = u_ref[...]
        acc = b_ref[...]
        for k in range(taps):
            acc = acc + buf[:, lo + k : lo + k + tl, :] * w_ref[k : k + 1, :]
        vals.append(acc)
        tail = buf[:, CONV_PAD + tl - (taps - 1) : CONV_PAD + tl, :]
        buf[:, lo:CONV_PAD, :] = tail

        @pl.when(l == nl - 1)
        def _():
            ns_refs[s][...] = tail

    if swiglu:
        out_ref[...] = (_silu(vals[1]) * vals[0]).astype(out_ref.dtype)
    else:
        out_ref[...] = _silu(vals[0]).astype(out_ref.dtype)


def causal_conv(u3, col_offs, width, states, ws, bs, taps, swiglu, out_dtype, name):
    nb, seq, _ = u3.shape
    tc = 512
    tl = min(seq, 512)
    sb = 1 if seq >= 8 else 16
    nstreams = len(col_offs)
    in_specs, args = [], []
    for s in range(nstreams):
        cb = col_offs[s] // tc
        in_specs += [
            pl.BlockSpec((sb, tl, tc), lambda b, c, l, cb=cb: (b, l, cb + c)),
            pl.BlockSpec((sb, taps - 1, tc), lambda b, c, l: (b, 0, c)),
            pl.BlockSpec((taps, tc), lambda b, c, l: (0, c)),
            pl.BlockSpec((1, tc), lambda b, c, l: (0, c)),
        ]
        args += [u3, states[s], ws[s], bs[s].reshape(1, width)]
    out_specs = [pl.BlockSpec((sb, tl, tc), lambda b, c, l: (b, l, c))]
    out_shape = [jax.ShapeDtypeStruct((nb, seq, width), out_dtype)]
    for s in range(nstreams):
        out_specs.append(pl.BlockSpec((sb, taps - 1, tc), lambda b, c, l: (b, 0, c)))
        out_shape.append(jax.ShapeDtypeStruct((nb, taps - 1, width), F32))
    return pl.pallas_call(
        functools.partial(_conv_kernel, taps=taps, tl=tl, nstreams=nstreams, swiglu=swiglu),
        grid=(nb // sb, width // tc, seq // tl),
        in_specs=in_specs,
        out_specs=out_specs,
        out_shape=out_shape,
        scratch_shapes=[pltpu.VMEM((sb, CONV_PAD + tl, tc), F32) for _ in range(nstreams)],
        compiler_params=_cp("parallel", "parallel", "arbitrary"),
        name=name,
    )(*args)


def _ffn_up_kernel(hn_ref, wa_ref, wg_ref, cwa_ref, cwg_ref, cba_ref, cbg_ref, sta_ref, stg_ref,
                   act_ref, nsa_ref, nsg_ref, bufa, bufg, *, tm):
    m = pl.program_id(2)
    nm = pl.num_programs(2)
    lo = CONV_PAD - (FFN_CONV - 1)
    hn = hn_ref[0]
    streams = ((wa_ref, cwa_ref, cba_ref, bufa), (wg_ref, cwg_ref, cbg_ref, bufg))

    @pl.when(m == 0)
    def _():
        bufa[lo:CONV_PAD, :] = sta_ref[0]
        bufg[lo:CONV_PAD, :] = stg_ref[0]

    vals, tails = [], []
    for w_ref, cw_ref, cb_ref, buf in streams:
        buf[CONV_PAD : CONV_PAD + tm, :] = _dot(hn, w_ref[...])
        acc = cb_ref[...]
        for k in range(FFN_CONV):
            acc = acc + buf[lo + k : lo + k + tm, :] * cw_ref[k : k + 1, :]
        vals.append(acc)
        tail = buf[CONV_PAD + tm - (FFN_CONV - 1) : CONV_PAD + tm, :]
        buf[lo:CONV_PAD, :] = tail
        tails.append(tail)
    act_ref[0] = (_silu(vals[1]) * vals[0]).astype(act_ref.dtype)

    @pl.when(m == nm - 1)
    def _():
        nsa_ref[0] = tails[0]
        nsg_ref[0] = tails[1]


def ffn_up_conv_act(hn3, w_up, conv_w, conv_b, conv_state):
    nb, seq, d = hn3.shape
    ffn = w_up.shape[1] // 2
    tm, tn = min(seq, 1024), 512
    nn = ffn // tn
    half = lambda off: (lambda b, n, m: (0, off + n))
    st_spec = lambda off: pl.BlockSpec((1, FFN_CONV - 1, tn), lambda b, n, m: (b, 0, off + n))
    cb2 = conv_b.reshape(1, 2 * ffn)
    act, nsa, nsg = pl.pallas_call(
        functools.partial(_ffn_up_kernel, tm=tm),
        grid=(nb, nn, seq // tm),
        in_specs=[
            pl.BlockSpec((1, tm, d), lambda b, n, m: (b, m, 0)),
            pl.BlockSpec((d, tn), half(0)), pl.BlockSpec((d, tn), half(nn)),
            pl.BlockSpec((FFN_CONV, tn), half(0)), pl.BlockSpec((FFN_CONV, tn), half(nn)),
            pl.BlockSpec((1, tn), half(0)), pl.BlockSpec((1, tn), half(nn)),
            st_spec(0), st_spec(nn),
        ],
        out_specs=[
            pl.BlockSpec((1, tm, tn), lambda b, n, m: (b, m, n)),
            pl.BlockSpec((1, FFN_CONV - 1, tn), lambda b, n, m: (b, 0, n)),
            pl.BlockSpec((1, FFN_CONV - 1, tn), lambda b, n, m: (b, 0, n)),
        ],
        out_shape=[
            jax.ShapeDtypeStruct((nb, seq, ffn), BF16),
            jax.ShapeDtypeStruct((nb, FFN_CONV - 1, ffn), F32),
            jax.ShapeDtypeStruct((nb, FFN_CONV - 1, ffn), F32),
        ],
        scratch_shapes=[pltpu.VMEM((CONV_PAD + tm, tn), F32), pltpu.VMEM((CONV_PAD + tm, tn), F32)],
        compiler_params=_cp("parallel", "parallel", "arbitrary"),
        name="ffn_up_conv_act",
    )(hn3, w_up, w_up, conv_w, conv_w, cb2, cb2, conv_state, conv_state)
    return act, jnp.concatenate([nsa, nsg], axis=-1)


def _ssd_gate_norm(y, z, nrm):
    ug = y * _silu(z)
    outs = []
    for g in range(N_GROUPS):
        ugg = ug[:, g * GROUP_W : (g + 1) * GROUP_W]
        ms = jnp.mean(ugg * ugg, axis=-1, keepdims=True)
        outs.append(ugg * lax.rsqrt(ms + EPS))
    return jnp.concatenate(outs, axis=1) * nrm


def _gla_out_norm(o, r, nrm):
    outs = []
    for h in range(GLA_HEADS):
        oh = o[:, h * GLA_HEAD_V : (h + 1) * GLA_HEAD_V]
        rh = r[:, h * GLA_HEAD_V : (h + 1) * GLA_HEAD_V]
        outs.append(_rms(oh, nrm) * _silu(rh))
    return jnp.concatenate(outs, axis=1)


def _ssd_scan_kernel(xs_ref, b_ref, c_ref, dt_ref, z_ref, cst_ref, cw_ref, cbias_ref,
                     dtb_ref, alog_ref, dexp_ref, nrm_ref, e_ref,
                     u_ref, sout_ref, cso_ref, st_ref, cbuf, *, nc):
    c = pl.program_id(1)
    q = SSD_CHUNK
    lo = CONV_PAD - (SSD_CONV - 1)

    @pl.when(c == 0)
    def _():
        st_ref[...] = jnp.zeros_like(st_ref)
        cbuf[lo:CONV_PAD, :] = cst_ref[0]

    cbuf[CONV_PAD : CONV_PAD + q, 0:D_MODEL] = xs_ref[0]
    cbuf[CONV_PAD : CONV_PAD + q, D_MODEL : D_MODEL + BC_W] = b_ref[0]
    cbuf[CONV_PAD : CONV_PAD + q, D_MODEL + BC_W : CONV_DIM] = c_ref[0]
    acc = cbias_ref[...]
    for k in range(SSD_CONV):
        acc = acc + cbuf[lo + k : lo + k + q, :] * cw_ref[k : k + 1, :]
    xbc = _silu(acc)
    tail = cbuf[CONV_PAD + q - (SSD_CONV - 1) : CONV_PAD + q, :]
    cbuf[lo:CONV_PAD, :] = tail
    xs = xbc[:, 0:D_MODEL]
    bm = xbc[:, D_MODEL : D_MODEL + BC_W]
    cm = xbc[:, D_MODEL + BC_W : CONV_DIM]
    dt = _softplus(dt_ref[0] + dtb_ref[...])
    a = dt * (-jnp.exp(alog_ref[...]))
    row = lax.broadcasted_iota(jnp.int32, (q, q), 0)
    col = lax.broadcasted_iota(jnp.int32, (q, q), 1)
    tril = row >= col
    acum = _dot(tril.astype(F32), a, HI)
    acum_t = acum.T
    dt_t = dt.T
    last = acum[q - 1 : q, :]
    e_mat = e_ref[...]
    eexp = _dot(jnp.exp(acum), e_mat, HI)
    wexp = _dot(jnp.exp(last - acum) * dt, e_mat, HI)
    s_bf = st_ref[...].astype(BF16)
    cb16 = cm.astype(BF16)
    bb16 = bm.astype(BF16)
    x16 = xs.astype(BF16)
    xw16 = (xs * wexp).astype(BF16)
    lane_lo = lax.broadcasted_iota(jnp.int32, (q, LANE), 1) < SSD_HEAD_DIM
    ys = []
    for g in range(N_GROUPS):
        cg = cb16[:, g * D_STATE : (g + 1) * D_STATE]
        bg = bb16[:, g * D_STATE : (g + 1) * D_STATE]
        cb = _dot_nt(cg, bg)
        yoff = _dot(cg, s_bf[:, g * GROUP_W : (g + 1) * GROUP_W])
        pieces = []
        for pr in range(GROUP_W // LANE):
            h0 = g * (SSD_HEADS // N_GROUPS) + 2 * pr
            xp = x16[:, h0 * SSD_HEAD_DIM : h0 * SSD_HEAD_DIM + LANE]
            yh = []
            for h in (h0, h0 + 1):
                diff = acum[:, h : h + 1] - acum_t[h : h + 1, :]
                dec = jnp.exp(jnp.where(tril, diff, NEG_BIG))
                m = (cb * dec * dt_t[h : h + 1, :]).astype(BF16)
                yh.append(_dot(m, xp))
            pieces.append(jnp.where(lane_lo, yh[0], yh[1]))
        sl = slice(g * GROUP_W, (g + 1) * GROUP_W)
        ys.append(jnp.concatenate(pieces, axis=1) + yoff * eexp[:, sl])
        bg_t = bm[:, g * D_STATE : (g + 1) * D_STATE].T.astype(BF16)
        upd = _dot(bg_t, xw16[:, sl])
        st_ref[:, sl] = eexp[q - 1 : q, sl] * st_ref[:, sl] + upd
    y = jnp.concatenate(ys, axis=1) + dexp_ref[...] * xs
    u_ref[0] = _ssd_gate_norm(y, z_ref[0], nrm_ref[...]).astype(u_ref.dtype)

    @pl.when(c == nc - 1)
    def _():
        sout_ref[0] = st_ref[...].T
        cso_ref[0] = tail


def ssd_scan(proj3, conv_state, conv_w, conv_b, p):
    nb, seq, _ = proj3.shape
    nc = seq // SSD_CHUNK
    q = SSD_CHUNK
    vec = lambda n: pl.BlockSpec((1, n), lambda b, c: (0, 0))
    u, s_out, conv_new = pl.pallas_call(
        functools.partial(_ssd_scan_kernel, nc=nc),
        grid=(nb, nc),
        in_specs=[
            pl.BlockSpec((1, q, D_MODEL), lambda b, c: (b, c, COL_XBC // D_MODEL)),
            pl.BlockSpec((1, q, BC_W), lambda b, c: (b, c, (COL_XBC + D_MODEL) // BC_W)),
            pl.BlockSpec((1, q, BC_W), lambda b, c: (b, c, (COL_XBC + D_MODEL) // BC_W + 1)),
            pl.BlockSpec((1, q, LANE), lambda b, c: (b, c, COL_DT // LANE)),
            pl.BlockSpec((1, q, D_MODEL), lambda b, c: (b, c, COL_Z // D_MODEL)),
            pl.BlockSpec((1, SSD_CONV - 1, CONV_DIM), lambda b, c: (b, 0, 0)),
            pl.BlockSpec((SSD_CONV, CONV_DIM), lambda b, c: (0, 0)),
            vec(CONV_DIM),
            vec(LANE), vec(LANE), vec(D_MODEL), vec(D_MODEL),
            pl.BlockSpec((LANE, D_MODEL), lambda b, c: (0, 0)),
        ],
        out_specs=[
            pl.BlockSpec((1, q, D_MODEL), lambda b, c: (b, c, 0)),
            pl.BlockSpec((1, D_MODEL, D_STATE), lambda b, c: (b, 0, 0)),
            pl.BlockSpec((1, SSD_CONV - 1, CONV_DIM), lambda b, c: (b, 0, 0)),
        ],
        out_shape=[
            jax.ShapeDtypeStruct((nb, seq, D_MODEL), BF16),
            jax.ShapeDtypeStruct((nb, D_MODEL, D_STATE), F32),
            jax.ShapeDtypeStruct((nb, SSD_CONV - 1, CONV_DIM), F32),
        ],
        scratch_shapes=[pltpu.VMEM((D_STATE, D_MODEL), F32), pltpu.VMEM((CONV_PAD + q, CONV_DIM), F32)],
        compiler_params=_cp("parallel", "arbitrary"),
        name="ssd_scan",
    )(proj3, proj3, proj3, proj3, proj3, conv_state, conv_w, conv_b.reshape(1, CONV_DIM),
      p["dt_bias"], p["a_log"], p["d_exp"], p["ssd_norm"], p["e_head"])
    return u.reshape(nb * seq, D_MODEL), s_out.reshape(nb, SSD_HEADS, SSD_HEAD_DIM, D_STATE), conv_new


def _gla_gate_log(glr, wg, bg):
    x = _dot(glr.astype(BF16), wg) + bg
    return -_softplus(-x) / GATE_TAU


def _gla_scan_kernel(q_ref, k_ref, v_ref, r_ref, glr_ref, wg_ref, bg_ref, nrm_ref,
                     o_ref, sout_ref, st_ref, *, nc):
    c = pl.program_id(1)
    q = GLA_CHUNK

    @pl.when(c == 0)
    def _():
        st_ref[...] = jnp.zeros_like(st_ref)

    glog = _gla_gate_log(glr_ref[0], wg_ref[...], bg_ref[...])
    row = lax.broadcasted_iota(jnp.int32, (q, q), 0)
    col = lax.broadcasted_iota(jnp.int32, (q, q), 1)
    tril = row >= col
    bc = _dot(tril.astype(F32), glog, HI)
    last = bc[q - 1 : q, :]
    kk = k_ref[0]
    qe = q_ref[0] * (GLA_HEAD_K ** -0.5) * jnp.exp(bc)
    ke = kk * jnp.exp(-bc)
    kd = kk * jnp.exp(last - bc)
    elast = jnp.exp(last)
    v16 = v_ref[0].astype(BF16)
    zeros_v = jnp.zeros((q, GLA_HEAD_V), BF16)
    outs = []
    for h in range(GLA_HEADS):
        ks = slice(h * GLA_HEAD_K, (h + 1) * GLA_HEAD_K)
        vs = slice(h * GLA_HEAD_V, (h + 1) * GLA_HEAD_V)
        qh = qe[:, ks].astype(BF16)
        kh = ke[:, ks].astype(BF16)
        att = jnp.where(tril, _dot_nt(qh, kh), 0.0)
        s_h = st_ref[ks, :]
        outs.append(_dot(att.astype(BF16), v16[:, vs]) + _dot(qh, s_h.astype(BF16)))
        xt = jnp.concatenate([kd[:, ks], jnp.broadcast_to(elast[:, ks], (q, GLA_HEAD_K))], axis=0).T
        v2 = jnp.concatenate([v16[:, vs], zeros_v], axis=0)
        st_ref[ks, :] = xt[:, q : q + 1] * s_h + _dot(xt.astype(BF16), v2)
    o = jnp.concatenate(outs, axis=1)
    o_ref[0] = _gla_out_norm(o, r_ref[0], nrm_ref[...]).astype(o_ref.dtype)

    @pl.when(c == nc - 1)
    def _():
        sout_ref[0] = st_ref[...]


def gla_scan(proj3, p):
    nb, seq, _ = proj3.shape
    q = GLA_CHUNK
    nc = seq // q
    o, s_out = pl.pallas_call(
        functools.partial(_gla_scan_kernel, nc=nc),
        grid=(nb, nc),
        in_specs=[
            pl.BlockSpec((1, q, GLA_KEY_DIM), lambda b, c: (b, c, COL_Q // GLA_KEY_DIM)),
            pl.BlockSpec((1, q, GLA_KEY_DIM), lambda b, c: (b, c, COL_K // GLA_KEY_DIM)),
            pl.BlockSpec((1, q, D_MODEL), lambda b, c: (b, c, COL_V // D_MODEL)),
            pl.BlockSpec((1, q, D_MODEL), lambda b, c: (b, c, COL_R // D_MODEL)),
            pl.BlockSpec((1, q, LANE), lambda b, c: (b, c, COL_GLR // LANE)),
            pl.BlockSpec((LANE, GLA_KEY_DIM), lambda b, c: (0, 0)),
            pl.BlockSpec((1, GLA_KEY_DIM), lambda b, c: (0, 0)),
            pl.BlockSpec((1, GLA_HEAD_V), lambda b, c: (0, 0)),
        ],
        out_specs=[
            pl.BlockSpec((1, q, D_MODEL), lambda b, c: (b, c, 0)),
            pl.BlockSpec((1, GLA_KEY_DIM, GLA_HEAD_V), lambda b, c: (b, 0, 0)),
        ],
        out_shape=[
            jax.ShapeDtypeStruct((nb, seq, D_MODEL), BF16),
            jax.ShapeDtypeStruct((nb, GLA_KEY_DIM, GLA_HEAD_V), F32),
        ],
        scratch_shapes=[pltpu.VMEM((GLA_KEY_DIM, GLA_HEAD_V), F32)],
        compiler_params=_cp("parallel", "arbitrary"),
        name="gla_scan",
    )(proj3, proj3, proj3, proj3, proj3, p["w_gate"], p["b_gate"], p["gla_norm"])
    return o.reshape(nb * seq, D_MODEL), s_out.reshape(nb, GLA_HEADS, GLA_HEAD_K, GLA_HEAD_V)


TOK_BLOCK = 128


def _row_shift(x, d, tpos):
    return jnp.where(tpos >= d, pltpu.roll(x, d, 0), 0.0)


def _seq_cumsum_and_last(a, seq, tpos):
    nrows = a.shape[0]
    acum = a
    for d in range(1, seq):
        acum = acum + _row_shift(a, d, tpos)
    last = jnp.where(tpos == seq - 1, acum, 0.0)
    for d in range(1, seq):
        last = last + jnp.where(tpos == seq - 1 - d, pltpu.roll(acum, nrows - d, 0), 0.0)
    return acum, last


def _ssd_step_pre_kernel(xs_ref, b_ref, c_ref, dt_ref, dtb_ref, alog_ref, dexp_ref, e_ref, gh_ref,
                         ypart_ref, eexp_ref, xwt_ref, el_ref, *, seq):
    nrows = xs_ref.shape[0]
    xs = xs_ref[...]
    bm = b_ref[...]
    cm = c_ref[...]
    dt = _softplus(dt_ref[...] + dtb_ref[...])
    a = dt * (-jnp.exp(alog_ref[...]))
    pos = lambda w: lax.broadcasted_iota(jnp.int32, (nrows, w), 0) % seq
    t_h, t_c, t_x = pos(LANE), pos(BC_W), pos(D_MODEL)
    acum, last = _seq_cumsum_and_last(a, seq, t_h)
    e_mat = e_ref[...]
    eexp_ref[...] = _dot(jnp.exp(acum), e_mat, HI)
    wexp = _dot(jnp.exp(last - acum) * dt, e_mat, HI)
    xwt_ref[...] = (xs * wexp).T.astype(xwt_ref.dtype)
    el_ref[...] = jnp.exp(last)
    y = dexp_ref[...] * xs
    for d in range(seq):
        if d == 0:
            cbh = _dot(cm * bm, gh_ref[...], HI)
            coef = dt
            xd = xs
        else:
            cbh = _dot(cm * _row_shift(bm, d, t_c), gh_ref[...], HI)
            coef = jnp.where(t_h >= d, jnp.exp(acum - pltpu.roll(acum, d, 0)) * pltpu.roll(dt, d, 0), 0.0)
            xd = _row_shift(xs, d, t_x)
        y = y + _dot(cbh * coef, e_mat, HI) * xd
    ypart_ref[...] = y


def _ssd_step_state_kernel(st_ref, c_ref, b_ref, xwt_ref, el_ref, ypart_ref, eexp_ref, z_ref, nrm_ref,
                           u_ref, so_ref, *, sb, seq):
    i = pl.program_id(0)
    rows = sb * seq
    steps_per_block = TOK_BLOCK // rows
    base = (i % steps_per_block) * rows
    c16 = c_ref[...].astype(BF16)
    btok = b_ref[...]
    tok = lax.broadcasted_iota(jnp.int32, (TOK_BLOCK, LANE), 0)
    rsel = lax.broadcasted_iota(jnp.int32, (rows, GROUP_W), 0)
    heads_per_group = SSD_HEADS // N_GROUPS
    yoff = [jnp.zeros((rows, GROUP_W), F32) for _ in range(N_GROUPS)]
    for s in range(sb):
        lo = base + seq * s
        own = (tok >= lo) & (tok < lo + seq)
        mine = (rsel >= seq * s) & (rsel < seq * (s + 1))
        for g in range(N_GROUPS):
            sl = slice(g * GROUP_W, (g + 1) * GROUP_W)
            yo = _dot_nt(c16[:, g * D_STATE : (g + 1) * D_STATE], st_ref[s, sl, :].astype(BF16))
            yoff[g] = jnp.where(mine, yo, yoff[g])
            bsel = jnp.where(own, btok[:, g * D_STATE : (g + 1) * D_STATE], 0.0).astype(BF16)
            upd = _dot(xwt_ref[sl, :], bsel)
            for r in range(heads_per_group):
                h = g * heads_per_group + r
                hs = slice(h * SSD_HEAD_DIM, (h + 1) * SSD_HEAD_DIM)
                so_ref[s, hs, :] = el_ref[seq * s, h] * st_ref[s, hs, :] + upd[r * SSD_HEAD_DIM : (r + 1) * SSD_HEAD_DIM]
    y = ypart_ref[...] + jnp.concatenate(yoff, axis=1) * eexp_ref[...]
    u_ref[...] = _ssd_gate_norm(y, z_ref[...], nrm_ref[...]).astype(u_ref.dtype)


def ssd_step(xc2, proj2, state, p, seq):
    ntok = xc2.shape[0]
    nseq = ntok // seq
    full = lambda shape: pl.BlockSpec(shape, lambda i: (0,) * len(shape))
    ypart, eexp, xwt, elast = pl.pallas_call(
        functools.partial(_ssd_step_pre_kernel, seq=seq),
        grid=(1,),
        in_specs=[
            pl.BlockSpec((ntok, D_MODEL), lambda i: (0, 0)),
            pl.BlockSpec((ntok, BC_W), lambda i: (0, D_MODEL // BC_W)),
            pl.BlockSpec((ntok, BC_W), lambda i: (0, D_MODEL // BC_W + 1)),
            pl.BlockSpec((ntok, LANE), lambda i: (0, COL_DT // LANE)),
            full((1, LANE)), full((1, LANE)), full((1, D_MODEL)),
            full((LANE, D_MODEL)), full((BC_W, LANE)),
        ],
        out_specs=[full((ntok, D_MODEL)), full((ntok, D_MODEL)), full((D_MODEL, ntok)), full((ntok, LANE))],
        out_shape=[
            jax.ShapeDtypeStruct((ntok, D_MODEL), F32),
            jax.ShapeDtypeStruct((ntok, D_MODEL), F32),
            jax.ShapeDtypeStruct((D_MODEL, ntok), BF16),
            jax.ShapeDtypeStruct((ntok, LANE), F32),
        ],
        compiler_params=_cp("arbitrary"),
        name="ssd_step_pre",
    )(xc2, xc2, xc2, proj2, p["dt_bias"], p["a_log"], p["d_exp"], p["e_head"], p["g_head"])

    sb = 4
    rows = sb * seq
    spb = TOK_BLOCK // rows
    st3 = state.reshape(nseq, D_MODEL, D_STATE)
    u, s_new = pl.pallas_call(
        functools.partial(_ssd_step_state_kernel, sb=sb, seq=seq),
        grid=(nseq // sb,),
        in_specs=[
            pl.BlockSpec((sb, D_MODEL, D_STATE), lambda i: (i, 0, 0)),
            pl.BlockSpec((rows, BC_W), lambda i: (i, D_MODEL // BC_W + 1)),
            pl.BlockSpec((TOK_BLOCK, BC_W), lambda i: (i // spb, D_MODEL // BC_W)),
            pl.BlockSpec((D_MODEL, TOK_BLOCK), lambda i: (0, i // spb)),
            pl.BlockSpec((rows, LANE), lambda i: (i, 0), memory_space=pltpu.SMEM),
            pl.BlockSpec((rows, D_MODEL), lambda i: (i, 0)),
            pl.BlockSpec((rows, D_MODEL), lambda i: (i, 0)),
            pl.BlockSpec((rows, D_MODEL), lambda i: (i, COL_Z // D_MODEL)),
            pl.BlockSpec((1, D_MODEL), lambda i: (0, 0)),
        ],
        out_specs=[
            pl.BlockSpec((rows, D_MODEL), lambda i: (i, 0)),
            pl.BlockSpec((sb, D_MODEL, D_STATE), lambda i: (i, 0, 0)),
        ],
        out_shape=[
            jax.ShapeDtypeStruct((ntok, D_MODEL), BF16),
            jax.ShapeDtypeStruct((nseq, D_MODEL, D_STATE), F32),
        ],
        compiler_params=_cp("parallel"),
        name="ssd_step_state",
    )(st3, xc2, xc2, xwt, elast, ypart, eexp, proj2, p["ssd_norm"])
    return u, s_new.reshape(nseq, SSD_HEADS, SSD_HEAD_DIM, D_STATE)


def _gla_step_pre_kernel(q_ref, k_ref, v_ref, glr_ref, wg_ref, bg_ref, gv_ref,
                         oin_ref, qe_ref, kdt_ref, elt_ref, *, seq):
    nrows = q_ref.shape[0]
    glog = _gla_gate_log(glr_ref[...], wg_ref[...], bg_ref[...])
    pos = lambda w: lax.broadcasted_iota(jnp.int32, (nrows, w), 0) % seq
    t_k, t_v = pos(GLA_KEY_DIM), pos(D_MODEL)
    bc, last = _seq_cumsum_and_last(glog, seq, t_k)
    kk = k_ref[...]
    qe = q_ref[...] * (GLA_HEAD_K ** -0.5) * jnp.exp(bc)
    ke = kk * jnp.exp(-bc)
    qe_ref[...] = qe
    kdt_ref[...] = (kk * jnp.exp(last - bc)).T
    elt_ref[...] = jnp.exp(last).T
    v = v_ref[...]
    gv = gv_ref[...]
    o = jnp.zeros((nrows, D_MODEL), F32)
    for d in range(seq):
        ked = ke if d == 0 else _row_shift(ke, d, t_k)
        vd = v if d == 0 else _row_shift(v, d, t_v)
        att = _dot((qe * ked).astype(BF16), gv)
        o = o + att * vd
    oin_ref[...] = o


def _gla_step_state_kernel(st_ref, qe_ref, v_ref, kdt_ref, elt_ref, oin_ref, r_ref, nrm_ref,
                           o_ref, so_ref, *, sb, seq):
    i = pl.program_id(0)
    rows = sb * seq
    spb = TOK_BLOCK // rows
    base = (i % spb) * rows
    qe16 = qe_ref[...].astype(BF16)
    vtok = v_ref[...]
    tokv = lax.broadcasted_iota(jnp.int32, (TOK_BLOCK, GLA_HEAD_V), 0)
    tok = lax.broadcasted_iota(jnp.int32, (TOK_BLOCK, LANE), 0)
    rsel = lax.broadcasted_iota(jnp.int32, (rows, GLA_HEAD_V), 0)
    ooff = [jnp.zeros((rows, GLA_HEAD_V), F32) for _ in range(GLA_HEADS)]
    for s in range(sb):
        lo = base + seq * s
        own = (tokv >= lo) & (tokv < lo + seq)
        first = jnp.where(tok == lo, 1.0, 0.0)
        mine = (rsel >= seq * s) & (rsel < seq * (s + 1))
        for h in range(GLA_HEADS):
            ks = slice(h * GLA_HEAD_K, (h + 1) * GLA_HEAD_K)
            vs = slice(h * GLA_HEAD_V, (h + 1) * GLA_HEAD_V)
            s_h = st_ref[s, ks, :]
            oo = _dot(qe16[:, ks], s_h.astype(BF16))
            ooff[h] = jnp.where(mine, oo, ooff[h])
            vsel = jnp.where(own, vtok[:, vs], 0.0).astype(BF16)
            upd = _dot(kdt_ref[ks, :].astype(BF16), vsel)
            ecol = _dot(elt_ref[ks, :], first, HI)
            so_ref[s, ks, :] = jnp.concatenate([ecol] * (GLA_HEAD_V // LANE), axis=1) * s_h + upd
    o = oin_ref[...] + jnp.concatenate(ooff, axis=1)
    o_ref[0] = _gla_out_norm(o, r_ref[...], nrm_ref[...]).astype(o_ref.dtype)


def gla_step(proj2, state, p, seq):
    ntok = proj2.shape[0]
    nseq = ntok // seq
    full = lambda shape: pl.BlockSpec(shape, lambda i: (0,) * len(shape))
    oin, qe, kdt, elt = pl.pallas_call(
        functools.partial(_gla_step_pre_kernel, seq=seq),
        grid=(1,),
        in_specs=[
            pl.BlockSpec((ntok, GLA_KEY_DIM), lambda i: (0, COL_Q // GLA_KEY_DIM)),
            pl.BlockSpec((ntok, GLA_KEY_DIM), lambda i: (0, COL_K // GLA_KEY_DIM)),
            pl.BlockSpec((ntok, D_MODEL), lambda i: (0, COL_V // D_MODEL)),
            pl.BlockSpec((ntok, LANE), lambda i: (0, COL_GLR // LANE)),
            full((LANE, GLA_KEY_DIM)), full((1, GLA_KEY_DIM)), full((GLA_KEY_DIM, D_MODEL)),
        ],
        out_specs=[full((ntok, D_MODEL)), full((ntok, GLA_KEY_DIM)), full((GLA_KEY_DIM, ntok)),
                   full((GLA_KEY_DIM, ntok))],
        out_shape=[
            jax.ShapeDtypeStruct((ntok, D_MODEL), F32),
            jax.ShapeDtypeStruct((ntok, GLA_KEY_DIM), F32),
            jax.ShapeDtypeStruct((GLA_KEY_DIM, ntok), F32),
            jax.ShapeDtypeStruct((GLA_KEY_DIM, ntok), F32),
        ],
        compiler_params=_cp("arbitrary"),
        name="gla_step_pre",
    )(proj2, proj2, proj2, proj2, p["w_gate"], p["b_gate"], p["g_val"])

    sb = 2
    rows = sb * seq
    spb = TOK_BLOCK // rows
    st3 = state.reshape(nseq, GLA_KEY_DIM, GLA_HEAD_V)
    o, s_new = pl.pallas_call(
        functools.partial(_gla_step_state_kernel, sb=sb, seq=seq),
        grid=(nseq // sb,),
        in_specs=[
            pl.BlockSpec((sb, GLA_KEY_DIM, GLA_HEAD_V), lambda i: (i, 0, 0)),
            pl.BlockSpec((rows, GLA_KEY_DIM), lambda i: (i, 0)),
            pl.BlockSpec((TOK_BLOCK, D_MODEL), lambda i: (i // spb, COL_V // D_MODEL)),
            pl.BlockSpec((GLA_KEY_DIM, TOK_BLOCK), lambda i: (0, i // spb)),
            pl.BlockSpec((GLA_KEY_DIM, TOK_BLOCK), lambda i: (0, i // spb)),
            pl.BlockSpec((rows, D_MODEL), lambda i: (i, 0)),
            pl.BlockSpec((rows, D_MODEL), lambda i: (i, COL_R // D_MODEL)),
            pl.BlockSpec((1, GLA_HEAD_V), lambda i: (0, 0)),
        ],
        out_specs=[
            pl.BlockSpec((1, rows, D_MODEL), lambda i: (i, 0, 0)),
            pl.BlockSpec((sb, GLA_KEY_DIM, GLA_HEAD_V), lambda i: (i, 0, 0)),
        ],
        out_shape=[
            jax.ShapeDtypeStruct((nseq // sb, rows, D_MODEL), BF16),
            jax.ShapeDtypeStruct((nseq, GLA_KEY_DIM, GLA_HEAD_V), F32),
        ],
        compiler_params=_cp("parallel"),
        name="gla_step_state",
    )(st3, qe, proj2, kdt, elt, oin, proj2, p["gla_norm"])
    return o.reshape(ntok, D_MODEL), s_new.reshape(nseq, GLA_HEADS, GLA_HEAD_K, GLA_HEAD_V)


def _softmax_rows(sc):
    e = jnp.exp(sc - jnp.max(sc, axis=-1, keepdims=True))
    return e / jnp.sum(e, axis=-1, keepdims=True)


def _xattn_kernel(q_ref, k_ref, v_ref, o_ref):
    q16 = q_ref[0]
    outs = []
    for h in range(CROSS_HEADS):
        hs = slice(h * CROSS_HEAD_DIM, (h + 1) * CROSS_HEAD_DIM)
        sc = _dot_nt(q16[:, hs], k_ref[0, :, hs].astype(BF16)) * (CROSS_HEAD_DIM ** -0.5)
        outs.append(_dot(_softmax_rows(sc).astype(BF16), v_ref[0, :, hs].astype(BF16)))
    o_ref[0] = jnp.concatenate(outs, axis=1).astype(o_ref.dtype)


def _xattn_step_kernel(q_ref, k_ref, v_ref, o_ref, *, nseq, tl):
    rows = nseq * tl
    n_mem = k_ref.shape[1]
    q = q_ref[0]
    qs = jnp.concatenate([q[:, h * CROSS_HEAD_DIM : (h + 1) * CROSS_HEAD_DIM] for h in range(CROSS_HEADS)],
                         axis=0).astype(BF16)
    shape = (CROSS_HEADS * rows, n_mem * CROSS_HEADS)
    col_head = lax.broadcasted_iota(jnp.int32, shape, 1) % CROSS_HEADS
    row_head = lax.broadcasted_iota(jnp.int32, shape, 0) // rows
    same_head = col_head == row_head
    rsel = lax.broadcasted_iota(jnp.int32, (CROSS_HEADS * rows, CROSS_HEAD_DIM), 0) % rows
    out = jnp.zeros((CROSS_HEADS * rows, CROSS_HEAD_DIM), F32)
    for s in range(nseq):
        kall = k_ref[s].reshape(n_mem * CROSS_HEADS, CROSS_HEAD_DIM).astype(BF16)
        vall = v_ref[s].reshape(n_mem * CROSS_HEADS, CROSS_HEAD_DIM).astype(BF16)
        sc = jnp.where(same_head, _dot_nt(qs, kall) * (CROSS_HEAD_DIM ** -0.5), NEG_BIG)
        oh = _dot(_softmax_rows(sc).astype(BF16), vall)
        out = jnp.where((rsel >= tl * s) & (rsel < tl * (s + 1)), oh, out)
    o_ref[0] = jnp.concatenate([out[h * rows : (h + 1) * rows] for h in range(CROSS_HEADS)],
                               axis=1).astype(o_ref.dtype)


def cross_attend(q2, mem_k, mem_v, nb, seq):
    n_mem = mem_k.shape[1]
    if mem_k.ndim == 3:
        nseq, tl = 1, min(seq, 512)
        body = _xattn_kernel
    else:
        nseq, tl = 8 // seq, seq
        body = functools.partial(_xattn_step_kernel, nseq=nseq, tl=tl)
    rows = nseq * tl
    lt = seq // tl
    nblk = nb * seq // rows
    q3 = q2.reshape(nblk, rows, D_MODEL)
    if mem_k.ndim == 4:
        kv_spec = pl.BlockSpec((nseq, n_mem, CROSS_HEADS, CROSS_HEAD_DIM), lambda i: (i // lt, 0, 0, 0))
    else:
        kv_spec = pl.BlockSpec((nseq, n_mem, D_MODEL), lambda i: (i // lt, 0, 0))
    o = pl.pallas_call(
        body,
        grid=(nblk,),
        in_specs=[pl.BlockSpec((1, rows, D_MODEL), lambda i: (i, 0, 0)), kv_spec, kv_spec],
        out_specs=pl.BlockSpec((1, rows, D_MODEL), lambda i: (i, 0, 0)),
        out_shape=jax.ShapeDtypeStruct((nblk, rows, D_MODEL), BF16),
        compiler_params=_cp("parallel"),
        name="cross_attend",
    )(q3, mem_k, mem_v)
    return o.reshape(nb * seq, D_MODEL)


def _pack_w_in(w_in):
    sizes = (D_MODEL, CONV_DIM, SSD_HEADS, GLA_KEY_DIM, GLA_KEY_DIM, D_MODEL, D_MODEL, GATE_RANK, D_MODEL, D_MODEL)
    offs = [0]
    for s in sizes:
        offs.append(offs[-1] + s)
    z, xbc, dt, q, k, v, r, glr, ga, gb = [w_in[:, offs[i] : offs[i + 1]] for i in range(len(sizes))]
    pad = lambda a, n: jnp.pad(a, ((0, 0), (0, n - a.shape[1])))
    parts = [z, ga, gb, v, r, xbc, q, k, pad(dt, LANE), pad(glr, LANE)]
    used = sum(a.shape[1] for a in parts)
    parts.append(jnp.zeros((w_in.shape[0], NPROJ - used), w_in.dtype))
    return jnp.concatenate(parts, axis=1).astype(BF16)


def _params(ssd_dt_bias, ssd_A_log, ssd_D, ssd_norm, w_gla_gate, b_gla_gate, gla_norm):
    padv = lambda a: jnp.pad(a.astype(F32), (0, LANE - a.shape[0])).reshape(1, LANE)
    head_of_chan = jnp.arange(D_MODEL, dtype=jnp.int32) // SSD_HEAD_DIM
    e_head = (jnp.arange(LANE, dtype=jnp.int32)[:, None] == head_of_chan[None, :]).astype(F32)
    group_of_bc = jnp.arange(BC_W, dtype=jnp.int32) // D_STATE
    lane_h = jnp.arange(LANE, dtype=jnp.int32)
    g_head = ((lane_h[None, :] // (SSD_HEADS // N_GROUPS) == group_of_bc[:, None])
              & (lane_h[None, :] < SSD_HEADS)).astype(F32)
    khead = jnp.arange(GLA_KEY_DIM, dtype=jnp.int32) // GLA_HEAD_K
    vhead = jnp.arange(D_MODEL, dtype=jnp.int32) // GLA_HEAD_V
    g_val = (khead[:, None] == vhead[None, :]).astype(BF16)
    return dict(
        dt_bias=padv(ssd_dt_bias), a_log=padv(ssd_A_log),
        d_exp=jnp.repeat(ssd_D.astype(F32), SSD_HEAD_DIM).reshape(1, D_MODEL),
        ssd_norm=ssd_norm.astype(F32).reshape(1, D_MODEL),
        e_head=e_head, g_head=g_head, g_val=g_val,
        w_gate=jnp.pad(w_gla_gate, ((0, LANE - GATE_RANK), (0, 0))).astype(BF16),
        b_gate=b_gla_gate.astype(F32).reshape(1, GLA_KEY_DIM),
        gla_norm=gla_norm.astype(F32).reshape(1, GLA_HEAD_V),
    )


def _layer(x3, mem_k, mem_v, ssd_conv, ssd_state, gla_state, ffn_conv, w, p, long_seq):
    nb, seq, d = x3.shape
    ntok = nb * seq
    ffn = w["w_down"].shape[0]
    x2 = x3.reshape(ntok, d)
    xn = rmsnorm_cast(x2, w["norm_mix"])
    proj = matmul(xn, w["w_in"], name="in_proj")
    proj3 = proj.reshape(nb, seq, NPROJ)
    if long_seq:
        u, ssd_new, ssd_conv_new = ssd_scan(proj3, ssd_conv, w["ssd_conv_w"], w["ssd_conv_b"], p)
        o, gla_new = gla_scan(proj3, p)
    else:
        xc3, ssd_conv_new = causal_conv(proj3, [COL_XBC], CONV_DIM, [ssd_conv], [w["ssd_conv_w"]],
                                        [w["ssd_conv_b"]], SSD_CONV, False, F32, "ssd_conv")
        u, ssd_new = ssd_step(xc3.reshape(ntok, CONV_DIM), proj, ssd_state, p, seq)
        o, gla_new = gla_step(proj, gla_state, p, seq)
    merged = merge_branches(u, o, w["w_ssd_out"], w["w_gla_out"], proj)
    h, hn = mm_res_norm(merged, w["w_mix_out"], x2, w["norm_cross"], True, BF16, "mix_out")
    qc = matmul(hn, w["w_cq"], out_dtype=BF16 if long_seq else F32, name="cross_q")
    att = cross_attend(qc, mem_k, mem_v, nb, seq)
    h2, hn2 = mm_res_norm(att, w["w_co"], h, w["norm_ffn"], True, BF16, "cross_out")
    cw, cbias = w["ffn_conv_w"], w["ffn_conv_b"]
    if long_seq:
        act, ffn_conv_new = ffn_up_conv_act(hn2.reshape(nb, seq, d), w["w_up"], cw, cbias, ffn_conv)
    else:
        up = matmul(hn2, w["w_up"], name="ffn_up").reshape(nb, seq, 2 * ffn)
        act, fa, fg = causal_conv(up, [0, ffn], ffn, [ffn_conv[:, :, :ffn], ffn_conv[:, :, ffn:]],
                                  [cw[:, :ffn], cw[:, ffn:]], [cbias[:ffn], cbias[ffn:]],
                                  FFN_CONV, True, BF16, "ffn_conv")
        ffn_conv_new = jnp.concatenate([fa, fg], axis=-1)
    y = mm_res_norm(act.reshape(ntok, ffn), w["w_down"], h2, w["norm_final"], False, F32, "ffn_down")
    return y.reshape(nb, seq, d), ssd_conv_new, ssd_new, gla_new, ffn_conv_new


def kernel(x_prompt, x_sample, cache_mem_k, cache_mem_v, state_ssd_conv, state_ssd, state_gla, state_ffn_conv, mem_prompt, norm_mix, w_in, ssd_conv_w, ssd_conv_b, ssd_dt_bias, ssd_A_log, ssd_D, ssd_norm, w_ssd_out, w_gla_gate, b_gla_gate, gla_norm, w_gla_out, w_mix_out, norm_cross, norm_mem, w_cq, w_ck, w_cv, w_co, norm_ffn, w_up, ffn_conv_w, ffn_conv_b, w_down, norm_final):
    nb, seq, d = x_prompt.shape
    n_mem = mem_prompt.shape[1]
    ffn2 = w_up.shape[1]
    w = dict(
        norm_mix=norm_mix, norm_cross=norm_cross, norm_ffn=norm_ffn, norm_final=norm_final,
        w_in=_pack_w_in(w_in), ssd_conv_w=ssd_conv_w, ssd_conv_b=ssd_conv_b,
        w_ssd_out=w_ssd_out.astype(BF16), w_gla_out=w_gla_out.astype(BF16), w_mix_out=w_mix_out.astype(BF16),
        w_cq=w_cq.astype(BF16), w_co=w_co.astype(BF16), w_up=w_up.astype(BF16), w_down=w_down.astype(BF16),
        ffn_conv_w=ffn_conv_w, ffn_conv_b=ffn_conv_b,
    )
    p = _params(ssd_dt_bias, ssd_A_log, ssd_D, ssd_norm, w_gla_gate, b_gla_gate, gla_norm)

    mn = rmsnorm_cast(mem_prompt.reshape(nb * n_mem, d), norm_mem)
    p_mem_k = matmul(mn, w_ck.astype(BF16), name="mem_k").reshape(nb, n_mem, d)
    p_mem_v = matmul(mn, w_cv.astype(BF16), name="mem_v").reshape(nb, n_mem, d)
    zeros_ssd_conv = jnp.zeros((nb, SSD_CONV - 1, CONV_DIM), F32)
    zeros_ffn_conv = jnp.zeros((nb, FFN_CONV - 1, ffn2), F32)
    y_prompt, p_ssd_conv, p_ssd, p_gla, p_ffn_conv = _layer(
        x_prompt, p_mem_k, p_mem_v, zeros_ssd_conv, None, None, zeros_ffn_conv, w, p, True)

    ns = x_sample.shape[0]
    y_sample, s_ssd_conv, s_ssd, s_gla, s_ffn_conv = _layer(
        x_sample, cache_mem_k, cache_mem_v, state_ssd_conv, state_ssd, state_gla, state_ffn_conv, w, p, False)

    head_shape = (n_mem, CROSS_HEADS, CROSS_HEAD_DIM)
    return (y_prompt, y_sample, p_ssd_conv, p_ssd, p_gla, p_ffn_conv,
            p_mem_k.reshape((nb,) + head_shape), p_mem_v.reshape((nb,) + head_shape),
            s_ssd_conv, s_ssd, s_gla, s_ffn_conv)
```

```python
import functools

import jax
import jax.numpy as jnp
from jax import lax
from jax.experimental import pallas as pl
from jax.experimental.pallas import tpu as pltpu

F32 = jnp.float32
BF16 = jnp.bfloat16
HI = lax.Precision.HIGHEST
EPS = 1e-6
NEG_BIG = -1e30

D_MODEL = 2048
SSD_HEAD_DIM = 64
SSD_HEADS = 32
D_STATE = 128
N_GROUPS = 4
GROUP_W = D_MODEL // N_GROUPS
BC_W = N_GROUPS * D_STATE
CONV_DIM = D_MODEL + 2 * BC_W
SSD_CONV = 4
SSD_CHUNK = 128
GLA_HEADS = 4
GLA_KEY_DIM = 1024
GLA_HEAD_K = 256
GLA_HEAD_V = 512
GATE_RANK = 16
GATE_TAU = 16.0
GLA_CHUNK = 64
CROSS_HEADS = 4
CROSS_HEAD_DIM = 512
FFN_CONV = 3
LANE = 128

COL_Z, COL_XBC, COL_DT, N_SSD_PROJ = 0, 2048, 5120, 5632
COL_Q, COL_K, COL_V, COL_R, COL_GLR, N_GLA_PROJ = 0, 1024, 2048, 4096, 6144, 6656
COL_GA, COL_GB = 0, 2048

VMEM_LIMIT = 56 * 1024 * 1024


def _cp(*sem):
    return pltpu.CompilerParams(dimension_semantics=sem, vmem_limit_bytes=VMEM_LIMIT)


def _dot(a, b, prec=None):
    return jnp.dot(a, b, preferred_element_type=F32, precision=prec)


def _dot_nt(a, b):
    return lax.dot_general(a, b, (((1,), (1,)), ((), ())), preferred_element_type=F32)


def _sigmoid(x):
    return 1.0 / (1.0 + jnp.exp(-x))


def _silu(x):
    return x * _sigmoid(x)


def _softplus(x):
    return jnp.maximum(x, 0.0) + jnp.log(1.0 + jnp.exp(-jnp.abs(x)))


def _rms(x, g):
    ms = jnp.mean(x * x, axis=-1, keepdims=True)
    return x * lax.rsqrt(ms + EPS) * g


def _rmsnorm_kernel(x_ref, g_ref, o_ref):
    o_ref[...] = _rms(x_ref[...], g_ref[...]).astype(o_ref.dtype)


def rmsnorm_cast(x2, g):
    m, d = x2.shape
    tm = min(m, 512)
    return pl.pallas_call(
        _rmsnorm_kernel,
        grid=(m // tm,),
        in_specs=[pl.BlockSpec((tm, d), lambda i: (i, 0)), pl.BlockSpec((1, d), lambda i: (0, 0))],
        out_specs=pl.BlockSpec((tm, d), lambda i: (i, 0)),
        out_shape=jax.ShapeDtypeStruct((m, d), BF16),
        compiler_params=_cp("parallel"),
        name="rmsnorm_cast",
    )(x2, g.reshape(1, d))


def _mm_kernel(a_ref, w_ref, o_ref):
    o_ref[...] = _dot(a_ref[...], w_ref[...]).astype(o_ref.dtype)


def matmul(a, w, out_dtype=F32, tm=1024, tn=512, name="matmul"):
    m, k = a.shape
    n = w.shape[1]
    tm = min(m, tm)
    return pl.pallas_call(
        _mm_kernel,
        grid=(m // tm, n // tn),
        in_specs=[pl.BlockSpec((tm, k), lambda i, j: (i, 0)), pl.BlockSpec((k, tn), lambda i, j: (0, j))],
        out_specs=pl.BlockSpec((tm, tn), lambda i, j: (i, j)),
        out_shape=jax.ShapeDtypeStruct((m, n), out_dtype),
        compiler_params=_cp("parallel", "arbitrary"),
        name=name,
    )(a, w)


def _mm_res_norm_kernel(a_ref, w_ref, res_ref, g_ref, *out_refs):
    h = res_ref[...] + _dot(a_ref[...], w_ref[...])
    if len(out_refs) == 2:
        out_refs[0][...] = h
    out_refs[-1][...] = _rms(h, g_ref[...]).astype(out_refs[-1].dtype)


def _mm_res_norm_coltile_kernel(a_ref, w_ref, res_ref, g_ref, n_ref, h_buf, ssq_ref, *, tn):
    j = pl.program_id(1)
    nj = pl.num_programs(1)
    h = res_ref[...] + _dot(a_ref[...], w_ref[...])
    h_buf[:, pl.ds(pl.multiple_of(j * tn, tn), tn)] = h
    part = jnp.sum(h * h, axis=-1, keepdims=True)

    @pl.when(j == 0)
    def _():
        ssq_ref[...] = part

    @pl.when(j > 0)
    def _():
        ssq_ref[...] += part

    @pl.when(j == nj - 1)
    def _():
        hh = h_buf[...]
        scale = lax.rsqrt(ssq_ref[...] * (1.0 / hh.shape[1]) + EPS)
        n_ref[...] = (hh * scale * g_ref[...]).astype(n_ref.dtype)


def mm_res_norm(a, w, res, g, emit_h, norm_dtype, name):
    m, kdim = a.shape
    n = w.shape[1]
    tm = min(m, 512)
    if kdim > 2048:
        assert not emit_h
        tn = 512
        return pl.pallas_call(
            functools.partial(_mm_res_norm_coltile_kernel, tn=tn),
            grid=(m // tm, n // tn),
            in_specs=[
                pl.BlockSpec((tm, kdim), lambda i, j: (i, 0)),
                pl.BlockSpec((kdim, tn), lambda i, j: (0, j)),
                pl.BlockSpec((tm, tn), lambda i, j: (i, j)),
                pl.BlockSpec((1, n), lambda i, j: (0, 0)),
            ],
            out_specs=pl.BlockSpec((tm, n), lambda i, j: (i, 0)),
            out_shape=jax.ShapeDtypeStruct((m, n), norm_dtype),
            scratch_shapes=[pltpu.VMEM((tm, n), F32), pltpu.VMEM((tm, 1), F32)],
            compiler_params=_cp("parallel", "arbitrary"),
            name=name,
        )(a, w, res, g.reshape(1, n))
    row_tile = lambda width: pl.BlockSpec((tm, width), lambda i: (i, 0))
    out_shape = [jax.ShapeDtypeStruct((m, n), norm_dtype)]
    out_specs = [row_tile(n)]
    if emit_h:
        out_shape = [jax.ShapeDtypeStruct((m, n), F32)] + out_shape
        out_specs = [row_tile(n)] + out_specs
    outs = pl.pallas_call(
        _mm_res_norm_kernel,
        grid=(m // tm,),
        in_specs=[row_tile(kdim), pl.BlockSpec((kdim, n), lambda i: (0, 0)), row_tile(n),
                  pl.BlockSpec((1, n), lambda i: (0, 0))],
        out_specs=out_specs,
        out_shape=out_shape,
        compiler_params=_cp("parallel"),
        name=name,
    )(a, w, res, g.reshape(1, n))
    return outs if emit_h else outs[0]


def _merge_kernel(u_ref, o_ref, wa_ref, wb_ref, ga_ref, gb_ref, out_ref):
    a = _dot(u_ref[...], wa_ref[...])
    b = _dot(o_ref[...], wb_ref[...])
    out_ref[...] = (_sigmoid(ga_ref[...]) * a + _sigmoid(gb_ref[...]) * b).astype(out_ref.dtype)


def merge_branches(u, o, wa, wb, proj):
    m, d = u.shape
    tm, tn = min(m, 1024), 512
    return pl.pallas_call(
        _merge_kernel,
        grid=(m // tm, d // tn),
        in_specs=[
            pl.BlockSpec((tm, d), lambda i, j: (i, 0)),
            pl.BlockSpec((tm, d), lambda i, j: (i, 0)),
            pl.BlockSpec((d, tn), lambda i, j: (0, j)),
            pl.BlockSpec((d, tn), lambda i, j: (0, j)),
            pl.BlockSpec((tm, tn), lambda i, j: (i, COL_GA // tn + j)),
            pl.BlockSpec((tm, tn), lambda i, j: (i, COL_GB // tn + j)),
        ],
        out_specs=pl.BlockSpec((tm, tn), lambda i, j: (i, j)),
        out_shape=jax.ShapeDtypeStruct((m, d), BF16),
        compiler_params=_cp("parallel", "arbitrary"),
        name="merge_branches",
    )(u, o, wa, wb, proj, proj)


CONV_PAD = 8


def _short_conv_kernel(*refs, taps, seq, nstreams, swiglu):
    ins = refs[: 4 * nstreams]
    out_ref = refs[4 * nstreams]
    ns_refs = refs[4 * nstreams + 1 :]
    vals = []
    for s in range(nstreams):
        u_ref, st_ref, w_ref, b_ref = ins[4 * s : 4 * s + 4]
        nseq, _, tc = st_ref.shape
        full = jnp.concatenate([st_ref[...], u_ref[...].reshape(nseq, seq, tc)], axis=1)
        acc = b_ref[...]
        for k in range(taps):
            acc = acc + full[:, k : k + seq] * w_ref[k : k + 1, :]
        vals.append(acc)
        ns_refs[s][...] = full[:, seq : seq + taps - 1]
    out = _silu(vals[1]) * vals[0] if swiglu else _silu(vals[0])
    out_ref[...] = out.reshape(out_ref.shape).astype(out_ref.dtype)


def short_conv(u2, seq, col_offs, width, states, ws, bs, taps, swiglu, out_dtype, name):
    ntok = u2.shape[0]
    nseq = ntok // seq
    tc = 512
    nstreams = len(col_offs)
    in_specs, args = [], []
    for s in range(nstreams):
        cb = col_offs[s] // tc
        in_specs += [
            pl.BlockSpec((ntok, tc), lambda c, cb=cb: (0, cb + c)),
            pl.BlockSpec((nseq, taps - 1, tc), lambda c: (0, 0, c)),
            pl.BlockSpec((taps, tc), lambda c: (0, c)),
            pl.BlockSpec((1, tc), lambda c: (0, c)),
        ]
        args += [u2, states[s], ws[s], bs[s].reshape(1, width)]
    out_specs = [pl.BlockSpec((ntok, tc), lambda c: (0, c))]
    out_shape = [jax.ShapeDtypeStruct((ntok, width), out_dtype)]
    for s in range(nstreams):
        out_specs.append(pl.BlockSpec((nseq, taps - 1, tc), lambda c: (0, 0, c)))
        out_shape.append(jax.ShapeDtypeStruct((nseq, taps - 1, width), F32))
    return pl.pallas_call(
        functools.partial(_short_conv_kernel, taps=taps, seq=seq, nstreams=nstreams, swiglu=swiglu),
        grid=(width // tc,),
        in_specs=in_specs,
        out_specs=out_specs,
        out_shape=out_shape,
        compiler_params=_cp("parallel"),
        name=name,
    )(*args)


def _ffn_up_kernel(hn_ref, wa_ref, wg_ref, cwa_ref, cwg_ref, cba_ref, cbg_ref, sta_ref, stg_ref,
                   act_ref, nsa_ref, nsg_ref, bufa, bufg, *, tm):
    m = pl.program_id(2)
    nm = pl.num_programs(2)
    lo = CONV_PAD - (FFN_CONV - 1)
    streams = ((wa_ref, cwa_ref, cba_ref, bufa), (wg_ref, cwg_ref, cbg_ref, bufg))

    @pl.when(m == 0)
    def _():
        bufa[lo:CONV_PAD, :] = sta_ref[0]
        bufg[lo:CONV_PAD, :] = stg_ref[0]

    hn = hn_ref[0]
    vals, tails = [], []
    for w_ref, cw_ref, cb_ref, buf in streams:
        buf[CONV_PAD : CONV_PAD + tm, :] = _dot(hn, w_ref[...])
        acc = cb_ref[...]
        for k in range(FFN_CONV):
            acc = acc + buf[lo + k : lo + k + tm, :] * cw_ref[k : k + 1, :]
        vals.append(acc)
        tail = buf[CONV_PAD + tm - (FFN_CONV - 1) : CONV_PAD + tm, :]
        buf[lo:CONV_PAD, :] = tail
        tails.append(tail)
    act_ref[0] = (_silu(vals[1]) * vals[0]).astype(act_ref.dtype)

    @pl.when(m == nm - 1)
    def _():
        nsa_ref[0] = tails[0]
        nsg_ref[0] = tails[1]


def ffn_up_conv_act(hn3, w_up, conv_w, conv_b, conv_state):
    nb, seq, d = hn3.shape
    ffn = w_up.shape[1] // 2
    tm, tn = min(seq, 1024), 512
    nn = ffn // tn
    half = lambda off: (lambda b, n, m: (0, off + n))
    st_spec = lambda off: pl.BlockSpec((1, FFN_CONV - 1, tn), lambda b, n, m: (b, 0, off + n))
    cb2 = conv_b.reshape(1, 2 * ffn)
    act, nsa, nsg = pl.pallas_call(
        functools.partial(_ffn_up_kernel, tm=tm),
        grid=(nb, nn, seq // tm),
        in_specs=[
            pl.BlockSpec((1, tm, d), lambda b, n, m: (b, m, 0)),
            pl.BlockSpec((d, tn), half(0)), pl.BlockSpec((d, tn), half(nn)),
            pl.BlockSpec((FFN_CONV, tn), half(0)), pl.BlockSpec((FFN_CONV, tn), half(nn)),
            pl.BlockSpec((1, tn), half(0)), pl.BlockSpec((1, tn), half(nn)),
            st_spec(0), st_spec(nn),
        ],
        out_specs=[
            pl.BlockSpec((1, tm, tn), lambda b, n, m: (b, m, n)),
            pl.BlockSpec((1, FFN_CONV - 1, tn), lambda b, n, m: (b, 0, n)),
            pl.BlockSpec((1, FFN_CONV - 1, tn), lambda b, n, m: (b, 0, n)),
        ],
        out_shape=[
            jax.ShapeDtypeStruct((nb, seq, ffn), BF16),
            jax.ShapeDtypeStruct((nb, FFN_CONV - 1, ffn), F32),
            jax.ShapeDtypeStruct((nb, FFN_CONV - 1, ffn), F32),
        ],
        scratch_shapes=[pltpu.VMEM((CONV_PAD + tm, tn), F32), pltpu.VMEM((CONV_PAD + tm, tn), F32)],
        compiler_params=_cp("parallel", "parallel", "arbitrary"),
        name="ffn_up_conv_act",
    )(hn3, w_up, w_up, conv_w, conv_w, cb2, cb2, conv_state, conv_state)
    return act, jnp.concatenate([nsa, nsg], axis=-1)


def _ssd_gate_norm(y, z, nrm):
    ug = y * _silu(z)
    outs = []
    for g in range(N_GROUPS):
        ugg = ug[:, g * GROUP_W : (g + 1) * GROUP_W]
        ms = jnp.mean(ugg * ugg, axis=-1, keepdims=True)
        outs.append(ugg * lax.rsqrt(ms + EPS))
    return jnp.concatenate(outs, axis=1) * nrm


def _gla_out_norm(o, r, nrm):
    outs = []
    for h in range(GLA_HEADS):
        oh = o[:, h * GLA_HEAD_V : (h + 1) * GLA_HEAD_V]
        rh = r[:, h * GLA_HEAD_V : (h + 1) * GLA_HEAD_V]
        outs.append(_rms(oh, nrm) * _silu(rh))
    return jnp.concatenate(outs, axis=1)


def _ssd_scan_kernel(xs_ref, b_ref, c_ref, dt_ref, z_ref, cst_ref, cw_ref, cbias_ref,
                     dtb_ref, alog_ref, dexp_ref, nrm_ref, e_ref,
                     u_ref, sout_ref, cso_ref, st_ref, cbuf, *, nc):
    c = pl.program_id(1)
    q = SSD_CHUNK
    lo = CONV_PAD - (SSD_CONV - 1)

    @pl.when(c == 0)
    def _():
        st_ref[...] = jnp.zeros_like(st_ref)
        cbuf[lo:CONV_PAD, :] = cst_ref[0]

    cbuf[CONV_PAD : CONV_PAD + q, 0:D_MODEL] = xs_ref[0]
    cbuf[CONV_PAD : CONV_PAD + q, D_MODEL : D_MODEL + BC_W] = b_ref[0]
    cbuf[CONV_PAD : CONV_PAD + q, D_MODEL + BC_W : CONV_DIM] = c_ref[0]
    acc = cbias_ref[...]
    for k in range(SSD_CONV):
        acc = acc + cbuf[lo + k : lo + k + q, :] * cw_ref[k : k + 1, :]
    xbc = _silu(acc)
    tail = cbuf[CONV_PAD + q - (SSD_CONV - 1) : CONV_PAD + q, :]
    cbuf[lo:CONV_PAD, :] = tail
    xs = xbc[:, 0:D_MODEL]
    bm = xbc[:, D_MODEL : D_MODEL + BC_W]
    cm = xbc[:, D_MODEL + BC_W : CONV_DIM]
    dt = _softplus(dt_ref[0] + dtb_ref[...])
    a = dt * (-jnp.exp(alog_ref[...]))
    row = lax.broadcasted_iota(jnp.int32, (q, q), 0)
    col = lax.broadcasted_iota(jnp.int32, (q, q), 1)
    tril = row >= col
    acum = _dot(tril.astype(F32), a, HI)
    acum_t = acum.T
    dt_t = dt.T
    last = acum[q - 1 : q, :]
    e_mat = e_ref[...]
    eexp = _dot(jnp.exp(acum), e_mat, HI)
    wexp = _dot(jnp.exp(last - acum) * dt, e_mat, HI)
    s_bf = st_ref[...].astype(BF16)
    cb16 = cm.astype(BF16)
    bb16 = bm.astype(BF16)
    x16 = xs.astype(BF16)
    xw16 = (xs * wexp).astype(BF16)
    lane_lo = lax.broadcasted_iota(jnp.int32, (q, LANE), 1) < SSD_HEAD_DIM
    ys = []
    for g in range(N_GROUPS):
        cg = cb16[:, g * D_STATE : (g + 1) * D_STATE]
        bg = bb16[:, g * D_STATE : (g + 1) * D_STATE]
        cb = _dot_nt(cg, bg)
        yoff = _dot(cg, s_bf[:, g * GROUP_W : (g + 1) * GROUP_W])
        pieces = []
        for pr in range(GROUP_W // LANE):
            h0 = g * (SSD_HEADS // N_GROUPS) + 2 * pr
            xp = x16[:, h0 * SSD_HEAD_DIM : h0 * SSD_HEAD_DIM + LANE]
            yh = []
            for h in (h0, h0 + 1):
                diff = acum[:, h : h + 1] - acum_t[h : h + 1, :]
                dec = jnp.exp(jnp.where(tril, diff, NEG_BIG))
                m = (cb * dec * dt_t[h : h + 1, :]).astype(BF16)
                yh.append(_dot(m, xp))
            pieces.append(jnp.where(lane_lo, yh[0], yh[1]))
        sl = slice(g * GROUP_W, (g + 1) * GROUP_W)
        ys.append(jnp.concatenate(pieces, axis=1) + yoff * eexp[:, sl])
        bg_t = bm[:, g * D_STATE : (g + 1) * D_STATE].T.astype(BF16)
        upd = _dot(bg_t, xw16[:, sl])
        st_ref[:, sl] = eexp[q - 1 : q, sl] * st_ref[:, sl] + upd
    y = jnp.concatenate(ys, axis=1) + dexp_ref[...] * xs
    u_ref[0] = _ssd_gate_norm(y, z_ref[0], nrm_ref[...]).astype(u_ref.dtype)

    @pl.when(c == nc - 1)
    def _():
        sout_ref[0] = st_ref[...].T
        cso_ref[0] = tail


def ssd_scan(proj3, conv_state, conv_w, conv_b, p):
    nb, seq, _ = proj3.shape
    nc = seq // SSD_CHUNK
    q = SSD_CHUNK
    vec = lambda n: pl.BlockSpec((1, n), lambda b, c: (0, 0))
    u, s_out, conv_new = pl.pallas_call(
        functools.partial(_ssd_scan_kernel, nc=nc),
        grid=(nb, nc),
        in_specs=[
            pl.BlockSpec((1, q, D_MODEL), lambda b, c: (b, c, COL_XBC // D_MODEL)),
            pl.BlockSpec((1, q, BC_W), lambda b, c: (b, c, (COL_XBC + D_MODEL) // BC_W)),
            pl.BlockSpec((1, q, BC_W), lambda b, c: (b, c, (COL_XBC + D_MODEL) // BC_W + 1)),
            pl.BlockSpec((1, q, LANE), lambda b, c: (b, c, COL_DT // LANE)),
            pl.BlockSpec((1, q, D_MODEL), lambda b, c: (b, c, COL_Z // D_MODEL)),
            pl.BlockSpec((1, SSD_CONV - 1, CONV_DIM), lambda b, c: (b, 0, 0)),
            pl.BlockSpec((SSD_CONV, CONV_DIM), lambda b, c: (0, 0)),
            vec(CONV_DIM),
            vec(LANE), vec(LANE), vec(D_MODEL), vec(D_MODEL),
            pl.BlockSpec((LANE, D_MODEL), lambda b, c: (0, 0)),
        ],
        out_specs=[
            pl.BlockSpec((1, q, D_MODEL), lambda b, c: (b, c, 0)),
            pl.BlockSpec((1, D_MODEL, D_STATE), lambda b, c: (b, 0, 0)),
            pl.BlockSpec((1, SSD_CONV - 1, CONV_DIM), lambda b, c: (b, 0, 0)),
        ],
        out_shape=[
            jax.ShapeDtypeStruct((nb, seq, D_MODEL), BF16),
            jax.ShapeDtypeStruct((nb, D_MODEL, D_STATE), F32),
            jax.ShapeDtypeStruct((nb, SSD_CONV - 1, CONV_DIM), F32),
        ],
        scratch_shapes=[pltpu.VMEM((D_STATE, D_MODEL), F32), pltpu.VMEM((CONV_PAD + q, CONV_DIM), F32)],
        compiler_params=_cp("parallel", "arbitrary"),
        name="ssd_scan",
    )(proj3, proj3, proj3, proj3, proj3, conv_state, conv_w, conv_b.reshape(1, CONV_DIM),
      p["dt_bias"], p["a_log"], p["d_exp"], p["ssd_norm"], p["e_head"])
    return u.reshape(nb * seq, D_MODEL), s_out.reshape(nb, SSD_HEADS, SSD_HEAD_DIM, D_STATE), conv_new


def _gla_gate_log(glr, wg, bg):
    x = _dot(glr.astype(BF16), wg) + bg
    return -_softplus(-x) / GATE_TAU


def _gla_scan_kernel(q_ref, k_ref, v_ref, r_ref, glr_ref, wg_ref, bg_ref, nrm_ref,
                     o_ref, sout_ref, st_ref, *, nc):
    c = pl.program_id(1)
    q = GLA_CHUNK

    @pl.when(c == 0)
    def _():
        st_ref[...] = jnp.zeros_like(st_ref)

    glog = _gla_gate_log(glr_ref[0], wg_ref[...], bg_ref[...])
    row = lax.broadcasted_iota(jnp.int32, (q, q), 0)
    col = lax.broadcasted_iota(jnp.int32, (q, q), 1)
    tril = row >= col
    bc = _dot(tril.astype(F32), glog, HI)
    last = bc[q - 1 : q, :]
    kk = k_ref[0]
    qe = q_ref[0] * (GLA_HEAD_K ** -0.5) * jnp.exp(bc)
    ke = kk * jnp.exp(-bc)
    kd = kk * jnp.exp(last - bc)
    elast = jnp.exp(last)
    v16 = v_ref[0].astype(BF16)
    zeros_v = jnp.zeros((q, GLA_HEAD_V), BF16)
    outs = []
    for h in range(GLA_HEADS):
        ks = slice(h * GLA_HEAD_K, (h + 1) * GLA_HEAD_K)
        vs = slice(h * GLA_HEAD_V, (h + 1) * GLA_HEAD_V)
        qh = qe[:, ks].astype(BF16)
        kh = ke[:, ks].astype(BF16)
        att = jnp.where(tril, _dot_nt(qh, kh), 0.0)
        s_h = st_ref[ks, :]
        outs.append(_dot(att.astype(BF16), v16[:, vs]) + _dot(qh, s_h.astype(BF16)))
        xt = jnp.concatenate([kd[:, ks], jnp.broadcast_to(elast[:, ks], (q, GLA_HEAD_K))], axis=0).T
        v2 = jnp.concatenate([v16[:, vs], zeros_v], axis=0)
        st_ref[ks, :] = xt[:, q : q + 1] * s_h + _dot(xt.astype(BF16), v2)
    o = jnp.concatenate(outs, axis=1)
    o_ref[0] = _gla_out_norm(o, r_ref[0], nrm_ref[...]).astype(o_ref.dtype)

    @pl.when(c == nc - 1)
    def _():
        sout_ref[0] = st_ref[...]


def gla_scan(proj3, p):
    nb, seq, _ = proj3.shape
    q = GLA_CHUNK
    nc = seq // q
    o, s_out = pl.pallas_call(
        functools.partial(_gla_scan_kernel, nc=nc),
        grid=(nb, nc),
        in_specs=[
            pl.BlockSpec((1, q, GLA_KEY_DIM), lambda b, c: (b, c, COL_Q // GLA_KEY_DIM)),
            pl.BlockSpec((1, q, GLA_KEY_DIM), lambda b, c: (b, c, COL_K // GLA_KEY_DIM)),
            pl.BlockSpec((1, q, D_MODEL), lambda b, c: (b, c, COL_V // D_MODEL)),
            pl.BlockSpec((1, q, D_MODEL), lambda b, c: (b, c, COL_R // D_MODEL)),
            pl.BlockSpec((1, q, LANE), lambda b, c: (b, c, COL_GLR // LANE)),
            pl.BlockSpec((LANE, GLA_KEY_DIM), lambda b, c: (0, 0)),
            pl.BlockSpec((1, GLA_KEY_DIM), lambda b, c: (0, 0)),
            pl.BlockSpec((1, GLA_HEAD_V), lambda b, c: (0, 0)),
        ],
        out_specs=[
            pl.BlockSpec((1, q, D_MODEL), lambda b, c: (b, c, 0)),
            pl.BlockSpec((1, GLA_KEY_DIM, GLA_HEAD_V), lambda b, c: (b, 0, 0)),
        ],
        out_shape=[
            jax.ShapeDtypeStruct((nb, seq, D_MODEL), BF16),
            jax.ShapeDtypeStruct((nb, GLA_KEY_DIM, GLA_HEAD_V), F32),
        ],
        scratch_shapes=[pltpu.VMEM((GLA_KEY_DIM, GLA_HEAD_V), F32)],
        compiler_params=_cp("parallel", "arbitrary"),
        name="gla_scan",
    )(proj3, proj3, proj3, proj3, proj3, p["w_gate"], p["b_gate"], p["gla_norm"])
    return o.reshape(nb * seq, D_MODEL), s_out.reshape(nb, GLA_HEADS, GLA_HEAD_K, GLA_HEAD_V)


TOK_BLOCK = 128


def _row_shift(x, d, tpos):
    return jnp.where(tpos >= d, pltpu.roll(x, d, 0), 0.0)


def _seq_cumsum_and_last(a, seq, tpos):
    nrows = a.shape[0]
    acum = a
    for d in range(1, seq):
        acum = acum + _row_shift(a, d, tpos)
    last = jnp.where(tpos == seq - 1, acum, 0.0)
    for d in range(1, seq):
        last = last + jnp.where(tpos == seq - 1 - d, pltpu.roll(acum, nrows - d, 0), 0.0)
    return acum, last


def _ssd_step_pre_kernel(xs_ref, b_ref, c_ref, dt_ref, dtb_ref, alog_ref, dexp_ref, e_ref, gh_ref,
                         ypart_ref, eexp_ref, xwt_ref, el_ref, *, seq):
    nrows = xs_ref.shape[0]
    xs = xs_ref[...]
    bm = b_ref[...]
    cm = c_ref[...]
    dt = _softplus(dt_ref[...] + dtb_ref[...])
    a = dt * (-jnp.exp(alog_ref[...]))
    pos = lambda w: lax.broadcasted_iota(jnp.int32, (nrows, w), 0) % seq
    t_h, t_c, t_x = pos(LANE), pos(BC_W), pos(D_MODEL)
    acum, last = _seq_cumsum_and_last(a, seq, t_h)
    e_mat = e_ref[...]
    eexp_ref[...] = _dot(jnp.exp(acum), e_mat, HI)
    wexp = _dot(jnp.exp(last - acum) * dt, e_mat, HI)
    xwt_ref[...] = (xs * wexp).T.astype(xwt_ref.dtype)
    el_ref[...] = jnp.exp(last)
    y = dexp_ref[...] * xs
    for d in range(seq):
        if d == 0:
            cbh = _dot(cm * bm, gh_ref[...], HI)
            coef = dt
            xd = xs
        else:
            cbh = _dot(cm * _row_shift(bm, d, t_c), gh_ref[...], HI)
            coef = jnp.where(t_h >= d, jnp.exp(acum - pltpu.roll(acum, d, 0)) * pltpu.roll(dt, d, 0), 0.0)
            xd = _row_shift(xs, d, t_x)
        y = y + _dot(cbh * coef, e_mat, HI) * xd
    ypart_ref[...] = y


def _ssd_step_state_kernel(st_ref, c_ref, b_ref, xwt_ref, el_ref, ypart_ref, eexp_ref, z_ref, nrm_ref,
                           u_ref, so_ref, *, sb, seq):
    i = pl.program_id(0)
    rows = sb * seq
    steps_per_block = TOK_BLOCK // rows
    base = (i % steps_per_block) * rows
    c16 = c_ref[...].astype(BF16)
    btok = b_ref[...]
    tok = lax.broadcasted_iota(jnp.int32, (TOK_BLOCK, LANE), 0)
    rsel = lax.broadcasted_iota(jnp.int32, (rows, GROUP_W), 0)
    heads_per_group = SSD_HEADS // N_GROUPS
    yoff = [jnp.zeros((rows, GROUP_W), F32) for _ in range(N_GROUPS)]
    for s in range(sb):
        lo = base + seq * s
        own = (tok >= lo) & (tok < lo + seq)
        mine = (rsel >= seq * s) & (rsel < seq * (s + 1))
        for g in range(N_GROUPS):
            sl = slice(g * GROUP_W, (g + 1) * GROUP_W)
            yo = _dot_nt(c16[:, g * D_STATE : (g + 1) * D_STATE], st_ref[s, sl, :].astype(BF16))
            yoff[g] = jnp.where(mine, yo, yoff[g])
            bsel = jnp.where(own, btok[:, g * D_STATE : (g + 1) * D_STATE], 0.0).astype(BF16)
            upd = _dot(xwt_ref[sl, :], bsel)
            for r in range(heads_per_group):
                h = g * heads_per_group + r
                hs = slice(h * SSD_HEAD_DIM, (h + 1) * SSD_HEAD_DIM)
                so_ref[s, hs, :] = el_ref[seq * s, h] * st_ref[s, hs, :] + upd[r * SSD_HEAD_DIM : (r + 1) * SSD_HEAD_DIM]
    y = ypart_ref[...] + jnp.concatenate(yoff, axis=1) * eexp_ref[...]
    u_ref[...] = _ssd_gate_norm(y, z_ref[...], nrm_ref[...]).astype(u_ref.dtype)


def ssd_step(xc2, proj2, state, p, seq):
    ntok = xc2.shape[0]
    nseq = ntok // seq
    full = lambda shape: pl.BlockSpec(shape, lambda i: (0,) * len(shape))
    ypart, eexp, xwt, elast = pl.pallas_call(
        functools.partial(_ssd_step_pre_kernel, seq=seq),
        grid=(1,),
        in_specs=[
            pl.BlockSpec((ntok, D_MODEL), lambda i: (0, 0)),
            pl.BlockSpec((ntok, BC_W), lambda i: (0, D_MODEL // BC_W)),
            pl.BlockSpec((ntok, BC_W), lambda i: (0, D_MODEL // BC_W + 1)),
            pl.BlockSpec((ntok, LANE), lambda i: (0, COL_DT // LANE)),
            full((1, LANE)), full((1, LANE)), full((1, D_MODEL)),
            full((LANE, D_MODEL)), full((BC_W, LANE)),
        ],
        out_specs=[full((ntok, D_MODEL)), full((ntok, D_MODEL)), full((D_MODEL, ntok)), full((ntok, LANE))],
        out_shape=[
            jax.ShapeDtypeStruct((ntok, D_MODEL), F32),
            jax.ShapeDtypeStruct((ntok, D_MODEL), F32),
            jax.ShapeDtypeStruct((D_MODEL, ntok), BF16),
            jax.ShapeDtypeStruct((ntok, LANE), F32),
        ],
        compiler_params=_cp("arbitrary"),
        name="ssd_step_pre",
    )(xc2, xc2, xc2, proj2, p["dt_bias"], p["a_log"], p["d_exp"], p["e_head"], p["g_head"])

    sb = 4
    rows = sb * seq
    spb = TOK_BLOCK // rows
    st3 = state.reshape(nseq, D_MODEL, D_STATE)
    u, s_new = pl.pallas_call(
        functools.partial(_ssd_step_state_kernel, sb=sb, seq=seq),
        grid=(nseq // sb,),
        in_specs=[
            pl.BlockSpec((sb, D_MODEL, D_STATE), lambda i: (i, 0, 0)),
            pl.BlockSpec((rows, BC_W), lambda i: (i, D_MODEL // BC_W + 1)),
            pl.BlockSpec((TOK_BLOCK, BC_W), lambda i: (i // spb, D_MODEL // BC_W)),
            pl.BlockSpec((D_MODEL, TOK_BLOCK), lambda i: (0, i // spb)),
            pl.BlockSpec((rows, LANE), lambda i: (i, 0), memory_space=pltpu.SMEM),
            pl.BlockSpec((rows, D_MODEL), lambda i: (i, 0)),
            pl.BlockSpec((rows, D_MODEL), lambda i: (i, 0)),
            pl.BlockSpec((rows, D_MODEL), lambda i: (i, COL_Z // D_MODEL)),
            pl.BlockSpec((1, D_MODEL), lambda i: (0, 0)),
        ],
        out_specs=[
            pl.BlockSpec((rows, D_MODEL), lambda i: (i, 0)),
            pl.BlockSpec((sb, D_MODEL, D_STATE), lambda i: (i, 0, 0)),
        ],
        out_shape=[
            jax.ShapeDtypeStruct((ntok, D_MODEL), BF16),
            jax.ShapeDtypeStruct((nseq, D_MODEL, D_STATE), F32),
        ],
        compiler_params=_cp("parallel"),
        name="ssd_step_state",
    )(st3, xc2, xc2, xwt, elast, ypart, eexp, proj2, p["ssd_norm"])
    return u, s_new.reshape(nseq, SSD_HEADS, SSD_HEAD_DIM, D_STATE)


def _gla_step_pre_kernel(q_ref, k_ref, v_ref, glr_ref, wg_ref, bg_ref, gv_ref,
                         oin_ref, qe_ref, kdt_ref, elt_ref, *, seq):
    nrows = q_ref.shape[0]
    glog = _gla_gate_log(glr_ref[...], wg_ref[...], bg_ref[...])
    pos = lambda w: lax.broadcasted_iota(jnp.int32, (nrows, w), 0) % seq
    t_k, t_v = pos(GLA_KEY_DIM), pos(D_MODEL)
    bc, last = _seq_cumsum_and_last(glog, seq, t_k)
    kk = k_ref[...]
    qe = q_ref[...] * (GLA_HEAD_K ** -0.5) * jnp.exp(bc)
    ke = kk * jnp.exp(-bc)
    qe_ref[...] = qe
    kdt_ref[...] = (kk * jnp.exp(last - bc)).T
    elt_ref[...] = jnp.exp(last).T
    v = v_ref[...]
    gv = gv_ref[...]
    o = jnp.zeros((nrows, D_MODEL), F32)
    for d in range(seq):
        ked = ke if d == 0 else _row_shift(ke, d, t_k)
        vd = v if d == 0 else _row_shift(v, d, t_v)
        att = _dot((qe * ked).astype(BF16), gv)
        o = o + att * vd
    oin_ref[...] = o


def _gla_step_state_kernel(st_ref, qe_ref, v_ref, kdt_ref, elt_ref, oin_ref, r_ref, nrm_ref,
                           o_ref, so_ref, *, sb, seq):
    i = pl.program_id(0)
    rows = sb * seq
    spb = TOK_BLOCK // rows
    base = (i % spb) * rows
    qe16 = qe_ref[...].astype(BF16)
    vtok = v_ref[...]
    tokv = lax.broadcasted_iota(jnp.int32, (TOK_BLOCK, GLA_HEAD_V), 0)
    tok = lax.broadcasted_iota(jnp.int32, (TOK_BLOCK, LANE), 0)
    rsel = lax.broadcasted_iota(jnp.int32, (rows, GLA_HEAD_V), 0)
    ooff = [jnp.zeros((rows, GLA_HEAD_V), F32) for _ in range(GLA_HEADS)]
    for s in range(sb):
        lo = base + seq * s
        own = (tokv >= lo) & (tokv < lo + seq)
        first = jnp.where(tok == lo, 1.0, 0.0)
        mine = (rsel >= seq * s) & (rsel < seq * (s + 1))
        for h in range(GLA_HEADS):
            ks = slice(h * GLA_HEAD_K, (h + 1) * GLA_HEAD_K)
            vs = slice(h * GLA_HEAD_V, (h + 1) * GLA_HEAD_V)
            s_h = st_ref[s, ks, :]
            oo = _dot(qe16[:, ks], s_h.astype(BF16))
            ooff[h] = jnp.where(mine, oo, ooff[h])
            vsel = jnp.where(own, vtok[:, vs], 0.0).astype(BF16)
            upd = _dot(kdt_ref[ks, :].astype(BF16), vsel)
            ecol = _dot(elt_ref[ks, :], first, HI)
            so_ref[s, ks, :] = jnp.concatenate([ecol] * (GLA_HEAD_V // LANE), axis=1) * s_h + upd
    o = oin_ref[...] + jnp.concatenate(ooff, axis=1)
    o_ref[0] = _gla_out_norm(o, r_ref[...], nrm_ref[...]).astype(o_ref.dtype)


def gla_step(proj2, state, p, seq):
    ntok = proj2.shape[0]
    nseq = ntok // seq
    full = lambda shape: pl.BlockSpec(shape, lambda i: (0,) * len(shape))
    oin, qe, kdt, elt = pl.pallas_call(
        functools.partial(_gla_step_pre_kernel, seq=seq),
        grid=(1,),
        in_specs=[
            pl.BlockSpec((ntok, GLA_KEY_DIM), lambda i: (0, COL_Q // GLA_KEY_DIM)),
            pl.BlockSpec((ntok, GLA_KEY_DIM), lambda i: (0, COL_K // GLA_KEY_DIM)),
            pl.BlockSpec((ntok, D_MODEL), lambda i: (0, COL_V // D_MODEL)),
            pl.BlockSpec((ntok, LANE), lambda i: (0, COL_GLR // LANE)),
            full((LANE, GLA_KEY_DIM)), full((1, GLA_KEY_DIM)), full((GLA_KEY_DIM, D_MODEL)),
        ],
        out_specs=[full((ntok, D_MODEL)), full((ntok, GLA_KEY_DIM)), full((GLA_KEY_DIM, ntok)),
                   full((GLA_KEY_DIM, ntok))],
        out_shape=[
            jax.ShapeDtypeStruct((ntok, D_MODEL), F32),
            jax.ShapeDtypeStruct((ntok, GLA_KEY_DIM), F32),
            jax.ShapeDtypeStruct((GLA_KEY_DIM, ntok), F32),
            jax.ShapeDtypeStruct((GLA_KEY_DIM, ntok), F32),
        ],
        compiler_params=_cp("arbitrary"),
        name="gla_step_pre",
    )(proj2, proj2, proj2, proj2, p["w_gate"], p["b_gate"], p["g_val"])

    sb = 4
    rows = sb * seq
    spb = TOK_BLOCK // rows
    st3 = state.reshape(nseq, GLA_KEY_DIM, GLA_HEAD_V)
    o, s_new = pl.pallas_call(
        functools.partial(_gla_step_state_kernel, sb=sb, seq=seq),
        grid=(nseq // sb,),
        in_specs=[
            pl.BlockSpec((sb, GLA_KEY_DIM, GLA_HEAD_V), lambda i: (i, 0, 0)),
            pl.BlockSpec((rows, GLA_KEY_DIM), lambda i: (i, 0)),
            pl.BlockSpec((TOK_BLOCK, D_MODEL), lambda i: (i // spb, COL_V // D_MODEL)),
            pl.BlockSpec((GLA_KEY_DIM, TOK_BLOCK), lambda i: (0, i // spb)),
            pl.BlockSpec((GLA_KEY_DIM, TOK_BLOCK), lambda i: (0, i // spb)),
            pl.BlockSpec((rows, D_MODEL), lambda i: (i, 0)),
            pl.BlockSpec((rows, D_MODEL), lambda i: (i, COL_R // D_MODEL)),
            pl.BlockSpec((1, GLA_HEAD_V), lambda i: (0, 0)),
        ],
        out_specs=[
            pl.BlockSpec((1, rows, D_MODEL), lambda i: (i, 0, 0)),
            pl.BlockSpec((sb, GLA_KEY_DIM, GLA_HEAD_V), lambda i: (i, 0, 0)),
        ],
        out_shape=[
            jax.ShapeDtypeStruct((nseq // sb, rows, D_MODEL), BF16),
            jax.ShapeDtypeStruct((nseq, GLA_KEY_DIM, GLA_HEAD_V), F32),
        ],
        compiler_params=_cp("parallel"),
        name="gla_step_state",
    )(st3, qe, proj2, kdt, elt, oin, proj2, p["gla_norm"])
    return o.reshape(ntok, D_MODEL), s_new.reshape(nseq, GLA_HEADS, GLA_HEAD_K, GLA_HEAD_V)


def _softmax_rows(sc):
    e = jnp.exp(sc - jnp.max(sc, axis=-1, keepdims=True))
    return e / jnp.sum(e, axis=-1, keepdims=True)


def _xattn_kernel(q_ref, k_ref, v_ref, o_ref):
    q16 = q_ref[0]
    outs = []
    for h in range(CROSS_HEADS):
        hs = slice(h * CROSS_HEAD_DIM, (h + 1) * CROSS_HEAD_DIM)
        sc = _dot_nt(q16[:, hs], k_ref[0, :, hs].astype(BF16)) * (CROSS_HEAD_DIM ** -0.5)
        outs.append(_dot(_softmax_rows(sc).astype(BF16), v_ref[0, :, hs].astype(BF16)))
    o_ref[0] = jnp.concatenate(outs, axis=1).astype(o_ref.dtype)


def _xattn_step_kernel(q_ref, k_ref, v_ref, o_ref, *, nseq, tl):
    rows = nseq * tl
    n_mem = k_ref.shape[1]
    q = q_ref[0]
    qs = jnp.concatenate([q[:, h * CROSS_HEAD_DIM : (h + 1) * CROSS_HEAD_DIM] for h in range(CROSS_HEADS)],
                         axis=0).astype(BF16)
    shape = (CROSS_HEADS * rows, n_mem * CROSS_HEADS)
    col_head = lax.broadcasted_iota(jnp.int32, shape, 1) % CROSS_HEADS
    row_head = lax.broadcasted_iota(jnp.int32, shape, 0) // rows
    same_head = col_head == row_head
    rsel = lax.broadcasted_iota(jnp.int32, (CROSS_HEADS * rows, CROSS_HEAD_DIM), 0) % rows
    out = jnp.zeros((CROSS_HEADS * rows, CROSS_HEAD_DIM), F32)
    for s in range(nseq):
        kall = k_ref[s].reshape(n_mem * CROSS_HEADS, CROSS_HEAD_DIM).astype(BF16)
        vall = v_ref[s].reshape(n_mem * CROSS_HEADS, CROSS_HEAD_DIM).astype(BF16)
        sc = jnp.where(same_head, _dot_nt(qs, kall) * (CROSS_HEAD_DIM ** -0.5), NEG_BIG)
        oh = _dot(_softmax_rows(sc).astype(BF16), vall)
        out = jnp.where((rsel >= tl * s) & (rsel < tl * (s + 1)), oh, out)
    o_ref[0] = jnp.concatenate([out[h * rows : (h + 1) * rows] for h in range(CROSS_HEADS)],
                               axis=1).astype(o_ref.dtype)


def cross_attend(q2, mem_k, mem_v, nb, seq):
    n_mem = mem_k.shape[1]
    if mem_k.ndim == 3:
        nseq, tl = 1, min(seq, 512)
        body = _xattn_kernel
    else:
        nseq, tl = 8 // seq, seq
        body = functools.partial(_xattn_step_kernel, nseq=nseq, tl=tl)
    rows = nseq * tl
    lt = seq // tl
    nblk = nb * seq // rows
    q3 = q2.reshape(nblk, rows, D_MODEL)
    if mem_k.ndim == 4:
        kv_spec = pl.BlockSpec((nseq, n_mem, CROSS_HEADS, CROSS_HEAD_DIM), lambda i: (i // lt, 0, 0, 0))
    else:
        kv_spec = pl.BlockSpec((nseq, n_mem, D_MODEL), lambda i: (i // lt, 0, 0))
    o = pl.pallas_call(
        body,
        grid=(nblk,),
        in_specs=[pl.BlockSpec((1, rows, D_MODEL), lambda i: (i, 0, 0)), kv_spec, kv_spec],
        out_specs=pl.BlockSpec((1, rows, D_MODEL), lambda i: (i, 0, 0)),
        out_shape=jax.ShapeDtypeStruct((nblk, rows, D_MODEL), BF16),
        compiler_params=_cp("parallel"),
        name="cross_attend",
    )(q3, mem_k, mem_v)
    return o.reshape(nb * seq, D_MODEL)


def _split_w_in(w_in):
    ssd_end = D_MODEL + CONV_DIM + SSD_HEADS
    gla_end = ssd_end + 2 * GLA_KEY_DIM + 2 * D_MODEL + GATE_RANK
    cut = lambda lo, hi, n: jnp.pad(w_in[:, lo:hi].astype(BF16), ((0, 0), (0, n - (hi - lo))))
    return (cut(0, ssd_end, N_SSD_PROJ), cut(ssd_end, gla_end, N_GLA_PROJ),
            w_in[:, gla_end:].astype(BF16))


def _params(ssd_dt_bias, ssd_A_log, ssd_D, ssd_norm, w_gla_gate, b_gla_gate, gla_norm):
    padv = lambda a: jnp.pad(a.astype(F32), (0, LANE - a.shape[0])).reshape(1, LANE)
    head_of_chan = jnp.arange(D_MODEL, dtype=jnp.int32) // SSD_HEAD_DIM
    e_head = (jnp.arange(LANE, dtype=jnp.int32)[:, None] == head_of_chan[None, :]).astype(F32)
    group_of_bc = jnp.arange(BC_W, dtype=jnp.int32) // D_STATE
    lane_h = jnp.arange(LANE, dtype=jnp.int32)
    g_head = ((lane_h[None, :] // (SSD_HEADS // N_GROUPS) == group_of_bc[:, None])
              & (lane_h[None, :] < SSD_HEADS)).astype(F32)
    khead = jnp.arange(GLA_KEY_DIM, dtype=jnp.int32) // GLA_HEAD_K
    vhead = jnp.arange(D_MODEL, dtype=jnp.int32) // GLA_HEAD_V
    g_val = (khead[:, None] == vhead[None, :]).astype(BF16)
    return dict(
        dt_bias=padv(ssd_dt_bias), a_log=padv(ssd_A_log),
        d_exp=jnp.repeat(ssd_D.astype(F32), SSD_HEAD_DIM).reshape(1, D_MODEL),
        ssd_norm=ssd_norm.astype(F32).reshape(1, D_MODEL),
        e_head=e_head, g_head=g_head, g_val=g_val,
        w_gate=jnp.pad(w_gla_gate, ((0, LANE - GATE_RANK), (0, 0))).astype(BF16),
        b_gate=b_gla_gate.astype(F32).reshape(1, GLA_KEY_DIM),
        gla_norm=gla_norm.astype(F32).reshape(1, GLA_HEAD_V),
    )


def _layer(x3, mem_k, mem_v, ssd_conv, ssd_state, gla_state, ffn_conv, w, p, long_seq):
    nb, seq, d = x3.shape
    ntok = nb * seq
    ffn = w["w_down"].shape[0]
    x2 = x3.reshape(ntok, d)
    xn = rmsnorm_cast(x2, w["norm_mix"])
    w_ssd_in, w_gla_in, w_gate_in = w["w_in"]
    proj_ssd = matmul(xn, w_ssd_in, tm=2048, name="in_proj_ssd")
    proj_gla = matmul(xn, w_gla_in, tm=2048, name="in_proj_gla")
    gates = matmul(xn, w_gate_in, tm=2048, name="in_proj_gates")
    if long_seq:
        u, ssd_new, ssd_conv_new = ssd_scan(proj_ssd.reshape(nb, seq, N_SSD_PROJ), ssd_conv,
                                            w["ssd_conv_w"], w["ssd_conv_b"], p)
        o, gla_new = gla_scan(proj_gla.reshape(nb, seq, N_GLA_PROJ), p)
    else:
        xc2, ssd_conv_new = short_conv(proj_ssd, seq, [COL_XBC], CONV_DIM, [ssd_conv], [w["ssd_conv_w"]],
                                       [w["ssd_conv_b"]], SSD_CONV, False, F32, "ssd_conv")
        u, ssd_new = ssd_step(xc2, proj_ssd, ssd_state, p, seq)
        o, gla_new = gla_step(proj_gla, gla_state, p, seq)
    merged = merge_branches(u, o, w["w_ssd_out"], w["w_gla_out"], gates)
    h, hn = mm_res_norm(merged, w["w_mix_out"], x2, w["norm_cross"], True, BF16, "mix_out")
    qc = matmul(hn, w["w_cq"], out_dtype=BF16 if long_seq else F32, name="cross_q")
    att = cross_attend(qc, mem_k, mem_v, nb, seq)
    h2, hn2 = mm_res_norm(att, w["w_co"], h, w["norm_ffn"], True, BF16, "cross_out")
    cw, cbias = w["ffn_conv_w"], w["ffn_conv_b"]
    if long_seq:
        act, ffn_conv_new = ffn_up_conv_act(hn2.reshape(nb, seq, d), w["w_up"], cw, cbias, ffn_conv)
    else:
        up = matmul(hn2, w["w_up"], name="ffn_up")
        act, fa, fg = short_conv(up, seq, [0, ffn], ffn, [ffn_conv[:, :, :ffn], ffn_conv[:, :, ffn:]],
                                 [cw[:, :ffn], cw[:, ffn:]], [cbias[:ffn], cbias[ffn:]],
                                 FFN_CONV, True, BF16, "ffn_conv")
        ffn_conv_new = jnp.concatenate([fa, fg], axis=-1)
    y = mm_res_norm(act.reshape(ntok, ffn), w["w_down"], h2, w["norm_final"], False, F32, "ffn_down")
    return y.reshape(nb, seq, d), ssd_conv_new, ssd_new, gla_new, ffn_conv_new


def kernel(x_prompt, x_sample, cache_mem_k, cache_mem_v, state_ssd_conv, state_ssd, state_gla, state_ffn_conv, mem_prompt, norm_mix, w_in, ssd_conv_w, ssd_conv_b, ssd_dt_bias, ssd_A_log, ssd_D, ssd_norm, w_ssd_out, w_gla_gate, b_gla_gate, gla_norm, w_gla_out, w_mix_out, norm_cross, norm_mem, w_cq, w_ck, w_cv, w_co, norm_ffn, w_up, ffn_conv_w, ffn_conv_b, w_down, norm_final):
    nb, seq, d = x_prompt.shape
    n_mem = mem_prompt.shape[1]
    ffn2 = w_up.shape[1]
    w = dict(
        norm_mix=norm_mix, norm_cross=norm_cross, norm_ffn=norm_ffn, norm_final=norm_final,
        w_in=_split_w_in(w_in), ssd_conv_w=ssd_conv_w, ssd_conv_b=ssd_conv_b,
        w_ssd_out=w_ssd_out.astype(BF16), w_gla_out=w_gla_out.astype(BF16), w_mix_out=w_mix_out.astype(BF16),
        w_cq=w_cq.astype(BF16), w_co=w_co.astype(BF16), w_up=w_up.astype(BF16), w_down=w_down.astype(BF16),
        ffn_conv_w=ffn_conv_w, ffn_conv_b=ffn_conv_b,
    )
    p = _params(ssd_dt_bias, ssd_A_log, ssd_D, ssd_norm, w_gla_gate, b_gla_gate, gla_norm)

    mn = rmsnorm_cast(mem_prompt.reshape(nb * n_mem, d), norm_mem)
    p_mem_k = matmul(mn, w_ck.astype(BF16), name="mem_k").reshape(nb, n_mem, d)
    p_mem_v = matmul(mn, w_cv.astype(BF16), name="mem_v").reshape(nb, n_mem, d)
    zeros_ssd_conv = jnp.zeros((nb, SSD_CONV - 1, CONV_DIM), F32)
    zeros_ffn_conv = jnp.zeros((nb, FFN_CONV - 1, ffn2), F32)
    y_prompt, p_ssd_conv, p_ssd, p_gla, p_ffn_conv = _layer(
        x_prompt, p_mem_k, p_mem_v, zeros_ssd_conv, None, None, zeros_ffn_conv, w, p, True)

    ns = x_sample.shape[0]
    y_sample, s_ssd_conv, s_ssd, s_gla, s_ffn_conv = _layer(
        x_sample, cache_mem_k, cache_mem_v, state_ssd_conv, state_ssd, state_gla, state_ffn_conv, w, p, False)

    head_shape = (n_mem, CROSS_HEADS, CROSS_HEAD_DIM)
    return (y_prompt, y_sample, p_ssd_conv, p_ssd, p_gla, p_ffn_conv,
            p_mem_k.reshape((nb,) + head_shape), p_mem_v.reshape((nb,) + head_shape),
            s_ssd_conv, s_ssd, s_gla, s_ffn_conv)
```

```python
import functools

import jax
import jax.numpy as jnp
from jax import lax
from jax.experimental import pallas as pl
from jax.experimental.pallas import tpu as pltpu

F32 = jnp.float32
BF16 = jnp.bfloat16
EPS = 1e-6
NEG_BIG = -1e30

D_MODEL = 2048
SSD_HEAD_DIM = 64
SSD_HEADS = 32
D_STATE = 128
N_GROUPS = 4
GROUP_W = D_MODEL // N_GROUPS
BC_W = N_GROUPS * D_STATE
CONV_DIM = D_MODEL + 2 * BC_W
SSD_CONV = 4
SSD_CHUNK = 128
GLA_HEADS = 4
GLA_KEY_DIM = 1024
GLA_HEAD_K = 256
GLA_HEAD_V = 512
GATE_RANK = 16
GATE_TAU = 16.0
GLA_CHUNK = 64
CROSS_HEADS = 4
CROSS_HEAD_DIM = 512
FFN_CONV = 3
LANE = 128

COL_Z, COL_XBC, COL_DT, N_SSD_PROJ = 0, 2048, 5120, 5632
COL_Q, COL_K, COL_V, COL_R, COL_GLR, N_GLA_PROJ = 0, 1024, 2048, 4096, 6144, 6656
COL_GA, COL_GB = 0, 2048

VMEM_LIMIT = 56 * 1024 * 1024


def _cp(*sem):
    return pltpu.CompilerParams(dimension_semantics=sem, vmem_limit_bytes=VMEM_LIMIT)


def _dot(a, b, prec=None):
    return jnp.dot(a, b, preferred_element_type=F32, precision=prec)


def _dot_nt(a, b):
    return lax.dot_general(a, b, (((1,), (1,)), ((), ())), preferred_element_type=F32)


def _split3(x):
    x1 = x.astype(BF16)
    r1 = x - x1.astype(F32)
    x2 = r1.astype(BF16)
    x3 = (r1 - x2.astype(F32)).astype(BF16)
    return x1, x2, x3


def _dot_sel(x, sel, sel_first=False):
    parts = _split3(x)
    if sel_first:
        return _dot(sel, parts[0]) + _dot(sel, parts[1]) + _dot(sel, parts[2])
    return _dot(parts[0], sel) + _dot(parts[1], sel) + _dot(parts[2], sel)


def _sigmoid(x):
    return 1.0 / (1.0 + jnp.exp(-x))


def _silu(x):
    return x * _sigmoid(x)


def _softplus(x):
    return jnp.maximum(x, 0.0) + jnp.log(1.0 + jnp.exp(-jnp.abs(x)))


def _rms(x, g):
    ms = jnp.mean(x * x, axis=-1, keepdims=True)
    return x * lax.rsqrt(ms + EPS) * g


def _rmsnorm_kernel(x_ref, g_ref, o_ref):
    o_ref[...] = _rms(x_ref[...], g_ref[...]).astype(o_ref.dtype)


def rmsnorm_cast(x2, g):
    m, d = x2.shape
    tm = min(m, 512)
    return pl.pallas_call(
        _rmsnorm_kernel,
        grid=(m // tm,),
        in_specs=[pl.BlockSpec((tm, d), lambda i: (i, 0)), pl.BlockSpec((1, d), lambda i: (0, 0))],
        out_specs=pl.BlockSpec((tm, d), lambda i: (i, 0)),
        out_shape=jax.ShapeDtypeStruct((m, d), BF16),
        compiler_params=_cp("parallel"),
        name="rmsnorm_cast",
    )(x2, g.reshape(1, d))


def _mm_kernel(a_ref, w_ref, o_ref):
    o_ref[...] = _dot(a_ref[...], w_ref[...]).astype(o_ref.dtype)


def matmul(a, w, out_dtype=F32, tm=1024, tn=512, name="matmul"):
    m, k = a.shape
    n = w.shape[1]
    tm = min(m, tm)
    return pl.pallas_call(
        _mm_kernel,
        grid=(m // tm, n // tn),
        in_specs=[pl.BlockSpec((tm, k), lambda i, j: (i, 0)), pl.BlockSpec((k, tn), lambda i, j: (0, j))],
        out_specs=pl.BlockSpec((tm, tn), lambda i, j: (i, j)),
        out_shape=jax.ShapeDtypeStruct((m, n), out_dtype),
        compiler_params=_cp("parallel", "arbitrary"),
        name=name,
    )(a, w)


def _mm_res_norm_kernel(a_ref, w_ref, res_ref, g_ref, *out_refs):
    h = res_ref[...] + _dot(a_ref[...], w_ref[...])
    if len(out_refs) == 2:
        out_refs[0][...] = h
    out_refs[-1][...] = _rms(h, g_ref[...]).astype(out_refs[-1].dtype)


def mm_res_norm(a, w, res, g, emit_h, norm_dtype, name):
    m, kdim = a.shape
    n = w.shape[1]
    tm = min(m, 512)
    row_tile = lambda width: pl.BlockSpec((tm, width), lambda i: (i, 0))
    out_shape = [jax.ShapeDtypeStruct((m, n), norm_dtype)]
    out_specs = [row_tile(n)]
    if emit_h:
        out_shape = [jax.ShapeDtypeStruct((m, n), F32)] + out_shape
        out_specs = [row_tile(n)] + out_specs
    outs = pl.pallas_call(
        _mm_res_norm_kernel,
        grid=(m // tm,),
        in_specs=[row_tile(kdim), pl.BlockSpec((kdim, n), lambda i: (0, 0), pipeline_mode=pl.Buffered(1)),
                  row_tile(n), pl.BlockSpec((1, n), lambda i: (0, 0))],
        out_specs=out_specs,
        out_shape=out_shape,
        compiler_params=_cp("parallel"),
        name=name,
    )(a, w, res, g.reshape(1, n))
    return outs if emit_h else outs[0]


def _merge_kernel(u_ref, o_ref, wa_ref, wb_ref, ga_ref, gb_ref, out_ref):
    a = _dot(u_ref[...], wa_ref[...])
    b = _dot(o_ref[...], wb_ref[...])
    out_ref[...] = (_sigmoid(ga_ref[...]) * a + _sigmoid(gb_ref[...]) * b).astype(out_ref.dtype)


def merge_branches(u, o, wa, wb, proj):
    m, d = u.shape
    tm, tn = min(m, 1024), 512
    return pl.pallas_call(
        _merge_kernel,
        grid=(m // tm, d // tn),
        in_specs=[
            pl.BlockSpec((tm, d), lambda i, j: (i, 0)),
            pl.BlockSpec((tm, d), lambda i, j: (i, 0)),
            pl.BlockSpec((d, tn), lambda i, j: (0, j)),
            pl.BlockSpec((d, tn), lambda i, j: (0, j)),
            pl.BlockSpec((tm, tn), lambda i, j: (i, COL_GA // tn + j)),
            pl.BlockSpec((tm, tn), lambda i, j: (i, COL_GB // tn + j)),
        ],
        out_specs=pl.BlockSpec((tm, tn), lambda i, j: (i, j)),
        out_shape=jax.ShapeDtypeStruct((m, d), BF16),
        compiler_params=_cp("parallel", "arbitrary"),
        name="merge_branches",
    )(u, o, wa, wb, proj, proj)


CONV_PAD = 8


def _short_conv_kernel(*refs, taps, seq, nstreams, swiglu):
    ins = refs[: 4 * nstreams]
    out_ref = refs[4 * nstreams]
    ns_refs = refs[4 * nstreams + 1 :]
    vals = []
    for s in range(nstreams):
        u_ref, st_ref, w_ref, b_ref = ins[4 * s : 4 * s + 4]
        nseq, _, tc = st_ref.shape
        full = jnp.concatenate([st_ref[...], u_ref[...].reshape(nseq, seq, tc)], axis=1)
        acc = b_ref[...]
        for k in range(taps):
            acc = acc + full[:, k : k + seq] * w_ref[k : k + 1, :]
        vals.append(acc)
        ns_refs[s][...] = full[:, seq : seq + taps - 1]
    out = _silu(vals[1]) * vals[0] if swiglu else _silu(vals[0])
    out_ref[...] = out.reshape(out_ref.shape).astype(out_ref.dtype)


def short_conv(u2, seq, col_offs, width, states, ws, bs, taps, swiglu, out_dtype, name):
    ntok = u2.shape[0]
    nseq = ntok // seq
    tc = 512
    nstreams = len(col_offs)
    in_specs, args = [], []
    for s in range(nstreams):
        cb = col_offs[s] // tc
        in_specs += [
            pl.BlockSpec((ntok, tc), lambda c, cb=cb: (0, cb + c)),
            pl.BlockSpec((nseq, taps - 1, tc), lambda c: (0, 0, c)),
            pl.BlockSpec((taps, tc), lambda c: (0, c)),
            pl.BlockSpec((1, tc), lambda c: (0, c)),
        ]
        args += [u2, states[s], ws[s], bs[s].reshape(1, width)]
    out_specs = [pl.BlockSpec((ntok, tc), lambda c: (0, c))]
    out_shape = [jax.ShapeDtypeStruct((ntok, width), out_dtype)]
    for s in range(nstreams):
        out_specs.append(pl.BlockSpec((nseq, taps - 1, tc), lambda c: (0, 0, c)))
        out_shape.append(jax.ShapeDtypeStruct((nseq, taps - 1, width), F32))
    return pl.pallas_call(
        functools.partial(_short_conv_kernel, taps=taps, seq=seq, nstreams=nstreams, swiglu=swiglu),
        grid=(width // tc,),
        in_specs=in_specs,
        out_specs=out_specs,
        out_shape=out_shape,
        compiler_params=_cp("parallel"),
        name=name,
    )(*args)


def _ffn_up_kernel(hn_ref, wa_ref, wg_ref, cwa_ref, cwg_ref, cba_ref, cbg_ref, sta_ref, stg_ref,
                   act_ref, nsa_ref, nsg_ref, bufa, bufg, *, tm):
    m = pl.program_id(2)
    nm = pl.num_programs(2)
    lo = CONV_PAD - (FFN_CONV - 1)
    streams = ((wa_ref, cwa_ref, cba_ref, bufa), (wg_ref, cwg_ref, cbg_ref, bufg))

    @pl.when(m == 0)
    def _():
        bufa[lo:CONV_PAD, :] = sta_ref[0]
        bufg[lo:CONV_PAD, :] = stg_ref[0]

    hn = hn_ref[0]
    vals, tails = [], []
    for w_ref, cw_ref, cb_ref, buf in streams:
        buf[CONV_PAD : CONV_PAD + tm, :] = _dot(hn, w_ref[...])
        acc = cb_ref[...]
        for k in range(FFN_CONV):
            acc = acc + buf[lo + k : lo + k + tm, :] * cw_ref[k : k + 1, :]
        vals.append(acc)
        tail = buf[CONV_PAD + tm - (FFN_CONV - 1) : CONV_PAD + tm, :]
        buf[lo:CONV_PAD, :] = tail
        tails.append(tail)
    act_ref[0] = (_silu(vals[1]) * vals[0]).astype(act_ref.dtype)

    @pl.when(m == nm - 1)
    def _():
        nsa_ref[0] = tails[0]
        nsg_ref[0] = tails[1]


def ffn_up_conv_act(hn3, w_up, conv_w, conv_b, conv_state):
    nb, seq, d = hn3.shape
    ffn = w_up.shape[1] // 2
    tm, tn = min(seq, 1024), 512
    nn = ffn // tn
    half = lambda off: (lambda b, n, m: (0, off + n))
    st_spec = lambda off: pl.BlockSpec((1, FFN_CONV - 1, tn), lambda b, n, m: (b, 0, off + n))
    cb2 = conv_b.reshape(1, 2 * ffn)
    act, nsa, nsg = pl.pallas_call(
        functools.partial(_ffn_up_kernel, tm=tm),
        grid=(nb, nn, seq // tm),
        in_specs=[
            pl.BlockSpec((1, tm, d), lambda b, n, m: (b, m, 0)),
            pl.BlockSpec((d, tn), half(0)), pl.BlockSpec((d, tn), half(nn)),
            pl.BlockSpec((FFN_CONV, tn), half(0)), pl.BlockSpec((FFN_CONV, tn), half(nn)),
            pl.BlockSpec((1, tn), half(0)), pl.BlockSpec((1, tn), half(nn)),
            st_spec(0), st_spec(nn),
        ],
        out_specs=[
            pl.BlockSpec((1, tm, tn), lambda b, n, m: (b, m, n)),
            pl.BlockSpec((1, FFN_CONV - 1, tn), lambda b, n, m: (b, 0, n)),
            pl.BlockSpec((1, FFN_CONV - 1, tn), lambda b, n, m: (b, 0, n)),
        ],
        out_shape=[
            jax.ShapeDtypeStruct((nb, seq, ffn), BF16),
            jax.ShapeDtypeStruct((nb, FFN_CONV - 1, ffn), F32),
            jax.ShapeDtypeStruct((nb, FFN_CONV - 1, ffn), F32),
        ],
        scratch_shapes=[pltpu.VMEM((CONV_PAD + tm, tn), F32), pltpu.VMEM((CONV_PAD + tm, tn), F32)],
        compiler_params=_cp("parallel", "parallel", "arbitrary"),
        name="ffn_up_conv_act",
    )(hn3, w_up, w_up, conv_w, conv_w, cb2, cb2, conv_state, conv_state)
    return act, jnp.concatenate([nsa, nsg], axis=-1)


def _ssd_gate_norm(y, z, nrm):
    ug = y * _silu(z)
    outs = []
    for g in range(N_GROUPS):
        ugg = ug[:, g * GROUP_W : (g + 1) * GROUP_W]
        ms = jnp.mean(ugg * ugg, axis=-1, keepdims=True)
        outs.append(ugg * lax.rsqrt(ms + EPS))
    return jnp.concatenate(outs, axis=1) * nrm


def _gla_out_norm(o, r, nrm):
    outs = []
    for h in range(GLA_HEADS):
        oh = o[:, h * GLA_HEAD_V : (h + 1) * GLA_HEAD_V]
        rh = r[:, h * GLA_HEAD_V : (h + 1) * GLA_HEAD_V]
        outs.append(_rms(oh, nrm) * _silu(rh))
    return jnp.concatenate(outs, axis=1)


def _ssd_scan_kernel(xs_ref, b_ref, c_ref, dt_ref, z_ref, cst_ref, cw_ref, cbias_ref,
                     dtb_ref, alog_ref, dexp_ref, nrm_ref, e_ref,
                     u_ref, sout_ref, cso_ref, st_ref, cbuf, *, nc):
    c = pl.program_id(1)
    q = SSD_CHUNK
    lo = CONV_PAD - (SSD_CONV - 1)

    @pl.when(c == 0)
    def _():
        st_ref[...] = jnp.zeros_like(st_ref)
        cbuf[lo:CONV_PAD, :] = cst_ref[0]

    cbuf[CONV_PAD : CONV_PAD + q, 0:D_MODEL] = xs_ref[0]
    cbuf[CONV_PAD : CONV_PAD + q, D_MODEL : D_MODEL + BC_W] = b_ref[0]
    cbuf[CONV_PAD : CONV_PAD + q, D_MODEL + BC_W : CONV_DIM] = c_ref[0]
    acc = cbias_ref[...]
    for k in range(SSD_CONV):
        acc = acc + cbuf[lo + k : lo + k + q, :] * cw_ref[k : k + 1, :]
    xbc = _silu(acc)
    tail = cbuf[CONV_PAD + q - (SSD_CONV - 1) : CONV_PAD + q, :]
    cbuf[lo:CONV_PAD, :] = tail
    xs = xbc[:, 0:D_MODEL]
    bm = xbc[:, D_MODEL : D_MODEL + BC_W]
    cm = xbc[:, D_MODEL + BC_W : CONV_DIM]
    dt = _softplus(dt_ref[0] + dtb_ref[...])
    a = dt * (-jnp.exp(alog_ref[...]))
    row = lax.broadcasted_iota(jnp.int32, (q, q), 0)
    col = lax.broadcasted_iota(jnp.int32, (q, q), 1)
    tril = row >= col
    acum = _dot_sel(a, tril.astype(BF16), sel_first=True)
    acum_t = acum.T
    dt_t = dt.T
    last = acum[q - 1 : q, :]
    e_mat = e_ref[...]
    eexp = _dot_sel(jnp.exp(acum), e_mat)
    wexp = _dot_sel(jnp.exp(last - acum) * dt, e_mat)
    s_bf = st_ref[...].astype(BF16)
    cb16 = cm.astype(BF16)
    bb16 = bm.astype(BF16)
    x16 = xs.astype(BF16)
    xw16 = (xs * wexp).astype(BF16)
    lane_lo = lax.broadcasted_iota(jnp.int32, (q, LANE), 1) < SSD_HEAD_DIM
    ys = []
    for g in range(N_GROUPS):
        cg = cb16[:, g * D_STATE : (g + 1) * D_STATE]
        bg = bb16[:, g * D_STATE : (g + 1) * D_STATE]
        cb = _dot_nt(cg, bg)
        yoff = _dot(cg, s_bf[:, g * GROUP_W : (g + 1) * GROUP_W])
        pieces = []
        for pr in range(GROUP_W // LANE):
            h0 = g * (SSD_HEADS // N_GROUPS) + 2 * pr
            xp = x16[:, h0 * SSD_HEAD_DIM : h0 * SSD_HEAD_DIM + LANE]
            yh = []
            for h in (h0, h0 + 1):
                diff = acum[:, h : h + 1] - acum_t[h : h + 1, :]
                dec = jnp.exp(jnp.where(tril, diff, NEG_BIG))
                m = (cb * dec * dt_t[h : h + 1, :]).astype(BF16)
                yh.append(_dot(m, xp))
            pieces.append(jnp.where(lane_lo, yh[0], yh[1]))
        sl = slice(g * GROUP_W, (g + 1) * GROUP_W)
        ys.append(jnp.concatenate(pieces, axis=1) + yoff * eexp[:, sl])
        bg_t = bm[:, g * D_STATE : (g + 1) * D_STATE].T.astype(BF16)
        upd = _dot(bg_t, xw16[:, sl])
        st_ref[:, sl] = eexp[q - 1 : q, sl] * st_ref[:, sl] + upd
    y = jnp.concatenate(ys, axis=1) + dexp_ref[...] * xs
    u_ref[0] = _ssd_gate_norm(y, z_ref[0], nrm_ref[...]).astype(u_ref.dtype)

    @pl.when(c == nc - 1)
    def _():
        sout_ref[0] = st_ref[...].T
        cso_ref[0] = tail


def ssd_scan(proj3, conv_state, conv_w, conv_b, p):
    nb, seq, _ = proj3.shape
    nc = seq // SSD_CHUNK
    q = SSD_CHUNK
    vec = lambda n: pl.BlockSpec((1, n), lambda b, c: (0, 0))
    u, s_out, conv_new = pl.pallas_call(
        functools.partial(_ssd_scan_kernel, nc=nc),
        grid=(nb, nc),
        in_specs=[
            pl.BlockSpec((1, q, D_MODEL), lambda b, c: (b, c, COL_XBC // D_MODEL)),
            pl.BlockSpec((1, q, BC_W), lambda b, c: (b, c, (COL_XBC + D_MODEL) // BC_W)),
            pl.BlockSpec((1, q, BC_W), lambda b, c: (b, c, (COL_XBC + D_MODEL) // BC_W + 1)),
            pl.BlockSpec((1, q, LANE), lambda b, c: (b, c, COL_DT // LANE)),
            pl.BlockSpec((1, q, D_MODEL), lambda b, c: (b, c, COL_Z // D_MODEL)),
            pl.BlockSpec((1, SSD_CONV - 1, CONV_DIM), lambda b, c: (b, 0, 0)),
            pl.BlockSpec((SSD_CONV, CONV_DIM), lambda b, c: (0, 0)),
            vec(CONV_DIM),
            vec(LANE), vec(LANE), vec(D_MODEL), vec(D_MODEL),
            pl.BlockSpec((LANE, D_MODEL), lambda b, c: (0, 0)),
        ],
        out_specs=[
            pl.BlockSpec((1, q, D_MODEL), lambda b, c: (b, c, 0)),
            pl.BlockSpec((1, D_MODEL, D_STATE), lambda b, c: (b, 0, 0)),
            pl.BlockSpec((1, SSD_CONV - 1, CONV_DIM), lambda b, c: (b, 0, 0)),
        ],
        out_shape=[
            jax.ShapeDtypeStruct((nb, seq, D_MODEL), BF16),
            jax.ShapeDtypeStruct((nb, D_MODEL, D_STATE), F32),
            jax.ShapeDtypeStruct((nb, SSD_CONV - 1, CONV_DIM), F32),
        ],
        scratch_shapes=[pltpu.VMEM((D_STATE, D_MODEL), F32), pltpu.VMEM((CONV_PAD + q, CONV_DIM), F32)],
        compiler_params=_cp("parallel", "arbitrary"),
        name="ssd_scan",
    )(proj3, proj3, proj3, proj3, proj3, conv_state, conv_w, conv_b.reshape(1, CONV_DIM),
      p["dt_bias"], p["a_log"], p["d_exp"], p["ssd_norm"], p["e_head"])
    return u.reshape(nb * seq, D_MODEL), s_out.reshape(nb, SSD_HEADS, SSD_HEAD_DIM, D_STATE), conv_new


def _gla_gate_log(glr, wg, bg):
    x = _dot(glr.astype(BF16), wg) + bg
    return -_softplus(-x) / GATE_TAU


def _gla_scan_kernel(q_ref, k_ref, v_ref, r_ref, glr_ref, wg_ref, bg_ref, nrm_ref,
                     o_ref, sout_ref, st_ref, *, nc):
    c = pl.program_id(1)
    q = GLA_CHUNK

    @pl.when(c == 0)
    def _():
        st_ref[...] = jnp.zeros_like(st_ref)

    glog = _gla_gate_log(glr_ref[0], wg_ref[...], bg_ref[...])
    row = lax.broadcasted_iota(jnp.int32, (q, q), 0)
    col = lax.broadcasted_iota(jnp.int32, (q, q), 1)
    tril = row >= col
    bc = _dot_sel(glog, tril.astype(BF16), sel_first=True)
    last = bc[q - 1 : q, :]
    kk = k_ref[0]
    qe = q_ref[0] * (GLA_HEAD_K ** -0.5) * jnp.exp(bc)
    ke = kk * jnp.exp(-bc)
    kd = kk * jnp.exp(last - bc)
    elast = jnp.exp(last)
    v16 = v_ref[0].astype(BF16)
    zeros_v = jnp.zeros((q, GLA_HEAD_V), BF16)
    outs = []
    for h in range(GLA_HEADS):
        ks = slice(h * GLA_HEAD_K, (h + 1) * GLA_HEAD_K)
        vs = slice(h * GLA_HEAD_V, (h + 1) * GLA_HEAD_V)
        qh = qe[:, ks].astype(BF16)
        kh = ke[:, ks].astype(BF16)
        att = jnp.where(tril, _dot_nt(qh, kh), 0.0)
        s_h = st_ref[ks, :]
        outs.append(_dot(att.astype(BF16), v16[:, vs]) + _dot(qh, s_h.astype(BF16)))
        xt = jnp.concatenate([kd[:, ks], jnp.broadcast_to(elast[:, ks], (q, GLA_HEAD_K))], axis=0).T
        v2 = jnp.concatenate([v16[:, vs], zeros_v], axis=0)
        st_ref[ks, :] = xt[:, q : q + 1] * s_h + _dot(xt.astype(BF16), v2)
    o = jnp.concatenate(outs, axis=1)
    o_ref[0] = _gla_out_norm(o, r_ref[0], nrm_ref[...]).astype(o_ref.dtype)

    @pl.when(c == nc - 1)
    def _():
        sout_ref[0] = st_ref[...]


def gla_scan(proj3, p):
    nb, seq, _ = proj3.shape
    q = GLA_CHUNK
    nc = seq // q
    o, s_out = pl.pallas_call(
        functools.partial(_gla_scan_kernel, nc=nc),
        grid=(nb, nc),
        in_specs=[
            pl.BlockSpec((1, q, GLA_KEY_DIM), lambda b, c: (b, c, COL_Q // GLA_KEY_DIM)),
            pl.BlockSpec((1, q, GLA_KEY_DIM), lambda b, c: (b, c, COL_K // GLA_KEY_DIM)),
            pl.BlockSpec((1, q, D_MODEL), lambda b, c: (b, c, COL_V // D_MODEL)),
            pl.BlockSpec((1, q, D_MODEL), lambda b, c: (b, c, COL_R // D_MODEL)),
            pl.BlockSpec((1, q, LANE), lambda b, c: (b, c, COL_GLR // LANE)),
            pl.BlockSpec((LANE, GLA_KEY_DIM), lambda b, c: (0, 0)),
            pl.BlockSpec((1, GLA_KEY_DIM), lambda b, c: (0, 0)),
            pl.BlockSpec((1, GLA_HEAD_V), lambda b, c: (0, 0)),
        ],
        out_specs=[
            pl.BlockSpec((1, q, D_MODEL), lambda b, c: (b, c, 0)),
            pl.BlockSpec((1, GLA_KEY_DIM, GLA_HEAD_V), lambda b, c: (b, 0, 0)),
        ],
        out_shape=[
            jax.ShapeDtypeStruct((nb, seq, D_MODEL), BF16),
            jax.ShapeDtypeStruct((nb, GLA_KEY_DIM, GLA_HEAD_V), F32),
        ],
        scratch_shapes=[pltpu.VMEM((GLA_KEY_DIM, GLA_HEAD_V), F32)],
        compiler_params=_cp("parallel", "arbitrary"),
        name="gla_scan",
    )(proj3, proj3, proj3, proj3, proj3, p["w_gate"], p["b_gate"], p["gla_norm"])
    return o.reshape(nb * seq, D_MODEL), s_out.reshape(nb, GLA_HEADS, GLA_HEAD_K, GLA_HEAD_V)


TOK_BLOCK = 128


def _row_shift(x, d, tpos):
    return jnp.where(tpos >= d, pltpu.roll(x, d, 0), 0.0)


def _seq_cumsum_and_last(a, seq, tpos):
    nrows = a.shape[0]
    acum = a
    for d in range(1, seq):
        acum = acum + _row_shift(a, d, tpos)
    last = jnp.where(tpos == seq - 1, acum, 0.0)
    for d in range(1, seq):
        last = last + jnp.where(tpos == seq - 1 - d, pltpu.roll(acum, nrows - d, 0), 0.0)
    return acum, last


def _ssd_step_pre_kernel(xs_ref, b_ref, c_ref, dt_ref, dtb_ref, alog_ref, dexp_ref, e_ref, gh_ref,
                         ypart_ref, eexp_ref, xwt_ref, el_ref, *, seq):
    nrows = xs_ref.shape[0]
    xs = xs_ref[...]
    bm = b_ref[...]
    cm = c_ref[...]
    dt = _softplus(dt_ref[...] + dtb_ref[...])
    a = dt * (-jnp.exp(alog_ref[...]))
    pos = lambda w: lax.broadcasted_iota(jnp.int32, (nrows, w), 0) % seq
    t_h, t_c, t_x = pos(LANE), pos(BC_W), pos(D_MODEL)
    acum, last = _seq_cumsum_and_last(a, seq, t_h)
    e_mat = e_ref[...]
    eexp_ref[...] = _dot_sel(jnp.exp(acum), e_mat)
    wexp = _dot_sel(jnp.exp(last - acum) * dt, e_mat)
    xwt_ref[...] = (xs * wexp).T.astype(xwt_ref.dtype)
    el_ref[...] = jnp.exp(last)
    y = dexp_ref[...] * xs
    for d in range(seq):
        if d == 0:
            cbh = _dot_sel(cm * bm, gh_ref[...])
            coef = dt
            xd = xs
        else:
            cbh = _dot_sel(cm * _row_shift(bm, d, t_c), gh_ref[...])
            coef = jnp.where(t_h >= d, jnp.exp(acum - pltpu.roll(acum, d, 0)) * pltpu.roll(dt, d, 0), 0.0)
            xd = _row_shift(xs, d, t_x)
        y = y + _dot_sel(cbh * coef, e_mat) * xd
    ypart_ref[...] = y


def _ssd_step_state_kernel(st_ref, c_ref, b_ref, xwt_ref, el_ref, ypart_ref, eexp_ref, z_ref, nrm_ref,
                           u_ref, so_ref, *, sb, seq):
    i = pl.program_id(0)
    rows = sb * seq
    steps_per_block = TOK_BLOCK // rows
    base = (i % steps_per_block) * rows
    c16 = c_ref[...].astype(BF16)
    btok = b_ref[...]
    tok = lax.broadcasted_iota(jnp.int32, (TOK_BLOCK, LANE), 0)
    rsel = lax.broadcasted_iota(jnp.int32, (rows, GROUP_W), 0)
    heads_per_group = SSD_HEADS // N_GROUPS
    yoff = [jnp.zeros((rows, GROUP_W), F32) for _ in range(N_GROUPS)]
    for s in range(sb):
        lo = base + seq * s
        own = (tok >= lo) & (tok < lo + seq)
        mine = (rsel >= seq * s) & (rsel < seq * (s + 1))
        for g in range(N_GROUPS):
            sl = slice(g * GROUP_W, (g + 1) * GROUP_W)
            yo = _dot_nt(c16[:, g * D_STATE : (g + 1) * D_STATE], st_ref[s, sl, :].astype(BF16))
            yoff[g] = jnp.where(mine, yo, yoff[g])
            bsel = jnp.where(own, btok[:, g * D_STATE : (g + 1) * D_STATE], 0.0).astype(BF16)
            upd = _dot(xwt_ref[sl, :], bsel)
            for r in range(heads_per_group):
                h = g * heads_per_group + r
                hs = slice(h * SSD_HEAD_DIM, (h + 1) * SSD_HEAD_DIM)
                so_ref[s, hs, :] = el_ref[seq * s, h] * st_ref[s, hs, :] + upd[r * SSD_HEAD_DIM : (r + 1) * SSD_HEAD_DIM]
    y = ypart_ref[...] + jnp.concatenate(yoff, axis=1) * eexp_ref[...]
    u_ref[...] = _ssd_gate_norm(y, z_ref[...], nrm_ref[...]).astype(u_ref.dtype)


def ssd_step(xc2, proj2, state, p, seq):
    ntok = xc2.shape[0]
    nseq = ntok // seq
    full = lambda shape: pl.BlockSpec(shape, lambda i: (0,) * len(shape))
    ypart, eexp, xwt, elast = pl.pallas_call(
        functools.partial(_ssd_step_pre_kernel, seq=seq),
        grid=(1,),
        in_specs=[
            pl.BlockSpec((ntok, D_MODEL), lambda i: (0, 0)),
            pl.BlockSpec((ntok, BC_W), lambda i: (0, D_MODEL // BC_W)),
            pl.BlockSpec((ntok, BC_W), lambda i: (0, D_MODEL // BC_W + 1)),
            pl.BlockSpec((ntok, LANE), lambda i: (0, COL_DT // LANE)),
            full((1, LANE)), full((1, LANE)), full((1, D_MODEL)),
            full((LANE, D_MODEL)), full((BC_W, LANE)),
        ],
        out_specs=[full((ntok, D_MODEL)), full((ntok, D_MODEL)), full((D_MODEL, ntok)), full((ntok, LANE))],
        out_shape=[
            jax.ShapeDtypeStruct((ntok, D_MODEL), F32),
            jax.ShapeDtypeStruct((ntok, D_MODEL), F32),
            jax.ShapeDtypeStruct((D_MODEL, ntok), BF16),
            jax.ShapeDtypeStruct((ntok, LANE), F32),
        ],
        compiler_params=_cp("arbitrary"),
        name="ssd_step_pre",
    )(xc2, xc2, xc2, proj2, p["dt_bias"], p["a_log"], p["d_exp"], p["e_head"], p["g_head"])

    sb = 4
    rows = sb * seq
    spb = TOK_BLOCK // rows
    st3 = state.reshape(nseq, D_MODEL, D_STATE)
    u, s_new = pl.pallas_call(
        functools.partial(_ssd_step_state_kernel, sb=sb, seq=seq),
        grid=(nseq // sb,),
        in_specs=[
            pl.BlockSpec((sb, D_MODEL, D_STATE), lambda i: (i, 0, 0)),
            pl.BlockSpec((rows, BC_W), lambda i: (i, D_MODEL // BC_W + 1)),
            pl.BlockSpec((TOK_BLOCK, BC_W), lambda i: (i // spb, D_MODEL // BC_W)),
            pl.BlockSpec((D_MODEL, TOK_BLOCK), lambda i: (0, i // spb)),
            pl.BlockSpec((rows, LANE), lambda i: (i, 0), memory_space=pltpu.SMEM),
            pl.BlockSpec((rows, D_MODEL), lambda i: (i, 0)),
            pl.BlockSpec((rows, D_MODEL), lambda i: (i, 0)),
            pl.BlockSpec((rows, D_MODEL), lambda i: (i, COL_Z // D_MODEL)),
            pl.BlockSpec((1, D_MODEL), lambda i: (0, 0)),
        ],
        out_specs=[
            pl.BlockSpec((rows, D_MODEL), lambda i: (i, 0)),
            pl.BlockSpec((sb, D_MODEL, D_STATE), lambda i: (i, 0, 0)),
        ],
        out_shape=[
            jax.ShapeDtypeStruct((ntok, D_MODEL), BF16),
            jax.ShapeDtypeStruct((nseq, D_MODEL, D_STATE), F32),
        ],
        compiler_params=_cp("parallel"),
        name="ssd_step_state",
    )(st3, xc2, xc2, xwt, elast, ypart, eexp, proj2, p["ssd_norm"])
    return u, s_new.reshape(nseq, SSD_HEADS, SSD_HEAD_DIM, D_STATE)


def _gla_step_pre_kernel(q_ref, k_ref, v_ref, glr_ref, wg_ref, bg_ref, gv_ref,
                         oin_ref, qe_ref, kdt_ref, elt_ref, *, seq):
    nrows = q_ref.shape[0]
    glog = _gla_gate_log(glr_ref[...], wg_ref[...], bg_ref[...])
    pos = lambda w: lax.broadcasted_iota(jnp.int32, (nrows, w), 0) % seq
    t_k, t_v = pos(GLA_KEY_DIM), pos(D_MODEL)
    bc, last = _seq_cumsum_and_last(glog, seq, t_k)
    kk = k_ref[...]
    qe = q_ref[...] * (GLA_HEAD_K ** -0.5) * jnp.exp(bc)
    ke = kk * jnp.exp(-bc)
    qe_ref[...] = qe
    kdt_ref[...] = (kk * jnp.exp(last - bc)).T
    elt_ref[...] = jnp.exp(last).T
    v = v_ref[...]
    gv = gv_ref[...]
    o = jnp.zeros((nrows, D_MODEL), F32)
    for d in range(seq):
        ked = ke if d == 0 else _row_shift(ke, d, t_k)
        vd = v if d == 0 else _row_shift(v, d, t_v)
        att = _dot((qe * ked).astype(BF16), gv)
        o = o + att * vd
    oin_ref[...] = o


def _gla_step_state_kernel(st_ref, qe_ref, v_ref, kdt_ref, elt_ref, oin_ref, r_ref, nrm_ref,
                           o_ref, so_ref, *, sb, seq):
    i = pl.program_id(0)
    rows = sb * seq
    spb = TOK_BLOCK // rows
    base = (i % spb) * rows
    qe16 = qe_ref[...].astype(BF16)
    vtok = v_ref[...]
    tokv = lax.broadcasted_iota(jnp.int32, (TOK_BLOCK, GLA_HEAD_V), 0)
    tok = lax.broadcasted_iota(jnp.int32, (TOK_BLOCK, LANE), 0)
    rsel = lax.broadcasted_iota(jnp.int32, (rows, GLA_HEAD_V), 0)
    ooff = [jnp.zeros((rows, GLA_HEAD_V), F32) for _ in range(GLA_HEADS)]
    for s in range(sb):
        lo = base + seq * s
        own = (tokv >= lo) & (tokv < lo + seq)
        first = jnp.where(tok == lo, 1.0, 0.0).astype(BF16)
        mine = (rsel >= seq * s) & (rsel < seq * (s + 1))
        for h in range(GLA_HEADS):
            ks = slice(h * GLA_HEAD_K, (h + 1) * GLA_HEAD_K)
            vs = slice(h * GLA_HEAD_V, (h + 1) * GLA_HEAD_V)
            s_h = st_ref[s, ks, :]
            oo = _dot(qe16[:, ks], s_h.astype(BF16))
            ooff[h] = jnp.where(mine, oo, ooff[h])
            vsel = jnp.where(own, vtok[:, vs], 0.0).astype(BF16)
            upd = _dot(kdt_ref[ks, :].astype(BF16), vsel)
            ecol = _dot_sel(elt_ref[ks, :], first)
            so_ref[s, ks, :] = jnp.concatenate([ecol] * (GLA_HEAD_V // LANE), axis=1) * s_h + upd
    o = oin_ref[...] + jnp.concatenate(ooff, axis=1)
    o_ref[0] = _gla_out_norm(o, r_ref[...], nrm_ref[...]).astype(o_ref.dtype)


def gla_step(proj2, state, p, seq):
    ntok = proj2.shape[0]
    nseq = ntok // seq
    full = lambda shape: pl.BlockSpec(shape, lambda i: (0,) * len(shape))
    oin, qe, kdt, elt = pl.pallas_call(
        functools.partial(_gla_step_pre_kernel, seq=seq),
        grid=(1,),
        in_specs=[
            pl.BlockSpec((ntok, GLA_KEY_DIM), lambda i: (0, COL_Q // GLA_KEY_DIM)),
            pl.BlockSpec((ntok, GLA_KEY_DIM), lambda i: (0, COL_K // GLA_KEY_DIM)),
            pl.BlockSpec((ntok, D_MODEL), lambda i: (0, COL_V // D_MODEL)),
            pl.BlockSpec((ntok, LANE), lambda i: (0, COL_GLR // LANE)),
            full((LANE, GLA_KEY_DIM)), full((1, GLA_KEY_DIM)), full((GLA_KEY_DIM, D_MODEL)),
        ],
        out_specs=[full((ntok, D_MODEL)), full((ntok, GLA_KEY_DIM)), full((GLA_KEY_DIM, ntok)),
                   full((GLA_KEY_DIM, ntok))],
        out_shape=[
            jax.ShapeDtypeStruct((ntok, D_MODEL), F32),
            jax.ShapeDtypeStruct((ntok, GLA_KEY_DIM), F32),
            jax.ShapeDtypeStruct((GLA_KEY_DIM, ntok), F32),
            jax.ShapeDtypeStruct((GLA_KEY_DIM, ntok), F32),
        ],
        compiler_params=_cp("arbitrary"),
        name="gla_step_pre",
    )(proj2, proj2, proj2, proj2, p["w_gate"], p["b_gate"], p["g_val"])

    sb = 4
    rows = sb * seq
    spb = TOK_BLOCK // rows
    st3 = state.reshape(nseq, GLA_KEY_DIM, GLA_HEAD_V)
    o, s_new = pl.pallas_call(
        functools.partial(_gla_step_state_kernel, sb=sb, seq=seq),
        grid=(nseq // sb,),
        in_specs=[
            pl.BlockSpec((sb, GLA_KEY_DIM, GLA_HEAD_V), lambda i: (i, 0, 0)),
            pl.BlockSpec((rows, GLA_KEY_DIM), lambda i: (i, 0)),
            pl.BlockSpec((TOK_BLOCK, D_MODEL), lambda i: (i // spb, COL_V // D_MODEL)),
            pl.BlockSpec((GLA_KEY_DIM, TOK_BLOCK), lambda i: (0, i // spb)),
            pl.BlockSpec((GLA_KEY_DIM, TOK_BLOCK), lambda i: (0, i // spb)),
            pl.BlockSpec((rows, D_MODEL), lambda i: (i, 0)),
            pl.BlockSpec((rows, D_MODEL), lambda i: (i, COL_R // D_MODEL)),
            pl.BlockSpec((1, GLA_HEAD_V), lambda i: (0, 0)),
        ],
        out_specs=[
            pl.BlockSpec((1, rows, D_MODEL), lambda i: (i, 0, 0)),
            pl.BlockSpec((sb, GLA_KEY_DIM, GLA_HEAD_V), lambda i: (i, 0, 0)),
        ],
        out_shape=[
            jax.ShapeDtypeStruct((nseq // sb, rows, D_MODEL), BF16),
            jax.ShapeDtypeStruct((nseq, GLA_KEY_DIM, GLA_HEAD_V), F32),
        ],
        compiler_params=_cp("parallel"),
        name="gla_step_state",
    )(st3, qe, proj2, kdt, elt, oin, proj2, p["gla_norm"])
    return o.reshape(ntok, D_MODEL), s_new.reshape(nseq, GLA_HEADS, GLA_HEAD_K, GLA_HEAD_V)


def _softmax_rows(sc):
    e = jnp.exp(sc - jnp.max(sc, axis=-1, keepdims=True))
    return e / jnp.sum(e, axis=-1, keepdims=True)


def _xattn_kernel(q_ref, k_ref, v_ref, o_ref):
    q16 = q_ref[0]
    outs = []
    for h in range(CROSS_HEADS):
        hs = slice(h * CROSS_HEAD_DIM, (h + 1) * CROSS_HEAD_DIM)
        sc = _dot_nt(q16[:, hs], k_ref[0, :, hs].astype(BF16)) * (CROSS_HEAD_DIM ** -0.5)
        outs.append(_dot(_softmax_rows(sc).astype(BF16), v_ref[0, :, hs].astype(BF16)))
    o_ref[0] = jnp.concatenate(outs, axis=1).astype(o_ref.dtype)


def _xattn_step_kernel(q_ref, k_ref, v_ref, o_ref, *, nseq, tl):
    rows = nseq * tl
    n_mem = k_ref.shape[1]
    q = q_ref[0]
    qs = jnp.concatenate([q[:, h * CROSS_HEAD_DIM : (h + 1) * CROSS_HEAD_DIM] for h in range(CROSS_HEADS)],
                         axis=0).astype(BF16)
    shape = (CROSS_HEADS * rows, n_mem * CROSS_HEADS)
    col_head = lax.broadcasted_iota(jnp.int32, shape, 1) % CROSS_HEADS
    row_head = lax.broadcasted_iota(jnp.int32, shape, 0) // rows
    same_head = col_head == row_head
    rsel = lax.broadcasted_iota(jnp.int32, (CROSS_HEADS * rows, CROSS_HEAD_DIM), 0) % rows
    out = jnp.zeros((CROSS_HEADS * rows, CROSS_HEAD_DIM), F32)
    for s in range(nseq):
        kall = k_ref[s].reshape(n_mem * CROSS_HEADS, CROSS_HEAD_DIM).astype(BF16)
        vall = v_ref[s].reshape(n_mem * CROSS_HEADS, CROSS_HEAD_DIM).astype(BF16)
        sc = jnp.where(same_head, _dot_nt(qs, kall) * (CROSS_HEAD_DIM ** -0.5), NEG_BIG)
        oh = _dot(_softmax_rows(sc).astype(BF16), vall)
        out = jnp.where((rsel >= tl * s) & (rsel < tl * (s + 1)), oh, out)
    o_ref[0] = jnp.concatenate([out[h * rows : (h + 1) * rows] for h in range(CROSS_HEADS)],
                               axis=1).astype(o_ref.dtype)


def cross_attend(q2, mem_k, mem_v, nb, seq):
    n_mem = mem_k.shape[1]
    if mem_k.ndim == 3:
        nseq, tl = 1, min(seq, 512)
        body = _xattn_kernel
    else:
        nseq, tl = 8 // seq, seq
        body = functools.partial(_xattn_step_kernel, nseq=nseq, tl=tl)
    rows = nseq * tl
    lt = seq // tl
    nblk = nb * seq // rows
    q3 = q2.reshape(nblk, rows, D_MODEL)
    if mem_k.ndim == 4:
        kv_spec = pl.BlockSpec((nseq, n_mem, CROSS_HEADS, CROSS_HEAD_DIM), lambda i: (i // lt, 0, 0, 0))
    else:
        kv_spec = pl.BlockSpec((nseq, n_mem, D_MODEL), lambda i: (i // lt, 0, 0))
    o = pl.pallas_call(
        body,
        grid=(nblk,),
        in_specs=[pl.BlockSpec((1, rows, D_MODEL), lambda i: (i, 0, 0)), kv_spec, kv_spec],
        out_specs=pl.BlockSpec((1, rows, D_MODEL), lambda i: (i, 0, 0)),
        out_shape=jax.ShapeDtypeStruct((nblk, rows, D_MODEL), BF16),
        compiler_params=_cp("parallel"),
        name="cross_attend",
    )(q3, mem_k, mem_v)
    return o.reshape(nb * seq, D_MODEL)


PACK_TILE = 512


def _cast_cols_kernel(w0_ref, w1_ref, o_ref, *, shift, ncols):
    j = pl.program_id(0)
    both = jnp.concatenate([w0_ref[...], w1_ref[...]], axis=1)
    x = both[:, shift : shift + PACK_TILE]
    col = lax.broadcasted_iota(jnp.int32, x.shape, 1) + j * PACK_TILE
    o_ref[...] = jnp.where(col < ncols, x, 0.0).astype(o_ref.dtype)


def cast_cols(w, lo, hi, n_out, name):
    kdim, ncol = w.shape
    nblk = -(-ncol // PACK_TILE)
    b0 = lo // PACK_TILE
    src = lambda d: pl.BlockSpec((kdim, PACK_TILE), lambda j: (0, jnp.minimum(b0 + j + d, nblk - 1)))
    return pl.pallas_call(
        functools.partial(_cast_cols_kernel, shift=lo % PACK_TILE, ncols=hi - lo),
        grid=(n_out // PACK_TILE,),
        in_specs=[src(0), src(1)],
        out_specs=pl.BlockSpec((kdim, PACK_TILE), lambda j: (0, j)),
        out_shape=jax.ShapeDtypeStruct((kdim, n_out), BF16),
        compiler_params=_cp("parallel"),
        name=name,
    )(w, w)


def _split_w_in(w_in):
    ssd_end = D_MODEL + CONV_DIM + SSD_HEADS
    gla_end = ssd_end + 2 * GLA_KEY_DIM + 2 * D_MODEL + GATE_RANK
    return (cast_cols(w_in, 0, ssd_end, N_SSD_PROJ, "pack_ssd_in"),
            cast_cols(w_in, ssd_end, gla_end, N_GLA_PROJ, "pack_gla_in"),
            cast_cols(w_in, gla_end, w_in.shape[1], 2 * D_MODEL, "pack_gate_in"))


def _params(ssd_dt_bias, ssd_A_log, ssd_D, ssd_norm, w_gla_gate, b_gla_gate, gla_norm):
    padv = lambda a: jnp.pad(a.astype(F32), (0, LANE - a.shape[0])).reshape(1, LANE)
    head_of_chan = jnp.arange(D_MODEL, dtype=jnp.int32) // SSD_HEAD_DIM
    e_head = (jnp.arange(LANE, dtype=jnp.int32)[:, None] == head_of_chan[None, :]).astype(BF16)
    group_of_bc = jnp.arange(BC_W, dtype=jnp.int32) // D_STATE
    lane_h = jnp.arange(LANE, dtype=jnp.int32)
    g_head = ((lane_h[None, :] // (SSD_HEADS // N_GROUPS) == group_of_bc[:, None])
              & (lane_h[None, :] < SSD_HEADS)).astype(BF16)
    khead = jnp.arange(GLA_KEY_DIM, dtype=jnp.int32) // GLA_HEAD_K
    vhead = jnp.arange(D_MODEL, dtype=jnp.int32) // GLA_HEAD_V
    g_val = (khead[:, None] == vhead[None, :]).astype(BF16)
    return dict(
        dt_bias=padv(ssd_dt_bias), a_log=padv(ssd_A_log),
        d_exp=jnp.repeat(ssd_D.astype(F32), SSD_HEAD_DIM).reshape(1, D_MODEL),
        ssd_norm=ssd_norm.astype(F32).reshape(1, D_MODEL),
        e_head=e_head, g_head=g_head, g_val=g_val,
        w_gate=jnp.pad(w_gla_gate, ((0, LANE - GATE_RANK), (0, 0))).astype(BF16),
        b_gate=b_gla_gate.astype(F32).reshape(1, GLA_KEY_DIM),
        gla_norm=gla_norm.astype(F32).reshape(1, GLA_HEAD_V),
    )


def _layer(x3, mem_k, mem_v, ssd_conv, ssd_state, gla_state, ffn_conv, w, p, long_seq):
    nb, seq, d = x3.shape
    ntok = nb * seq
    ffn = w["w_down"].shape[0]
    x2 = x3.reshape(ntok, d)
    xn = rmsnorm_cast(x2, w["norm_mix"])
    w_ssd_in, w_gla_in, w_gate_in = w["w_in"]
    proj_ssd = matmul(xn, w_ssd_in, tm=2048, name="in_proj_ssd")
    proj_gla = matmul(xn, w_gla_in, tm=2048, name="in_proj_gla")
    gates = matmul(xn, w_gate_in, tm=2048, name="in_proj_gates")
    if long_seq:
        u, ssd_new, ssd_conv_new = ssd_scan(proj_ssd.reshape(nb, seq, N_SSD_PROJ), ssd_conv,
                                            w["ssd_conv_w"], w["ssd_conv_b"], p)
        o, gla_new = gla_scan(proj_gla.reshape(nb, seq, N_GLA_PROJ), p)
    else:
        xc2, ssd_conv_new = short_conv(proj_ssd, seq, [COL_XBC], CONV_DIM, [ssd_conv], [w["ssd_conv_w"]],
                                       [w["ssd_conv_b"]], SSD_CONV, False, F32, "ssd_conv")
        u, ssd_new = ssd_step(xc2, proj_ssd, ssd_state, p, seq)
        o, gla_new = gla_step(proj_gla, gla_state, p, seq)
    merged = merge_branches(u, o, w["w_ssd_out"], w["w_gla_out"], gates)
    h, hn = mm_res_norm(merged, w["w_mix_out"], x2, w["norm_cross"], True, BF16, "mix_out")
    qc = matmul(hn, w["w_cq"], out_dtype=BF16 if long_seq else F32, name="cross_q")
    att = cross_attend(qc, mem_k, mem_v, nb, seq)
    h2, hn2 = mm_res_norm(att, w["w_co"], h, w["norm_ffn"], True, BF16, "cross_out")
    cw, cbias = w["ffn_conv_w"], w["ffn_conv_b"]
    if long_seq:
        act, ffn_conv_new = ffn_up_conv_act(hn2.reshape(nb, seq, d), w["w_up"], cw, cbias, ffn_conv)
    else:
        up = matmul(hn2, w["w_up"], name="ffn_up")
        act, fa, fg = short_conv(up, seq, [0, ffn], ffn, [ffn_conv[:, :, :ffn], ffn_conv[:, :, ffn:]],
                                 [cw[:, :ffn], cw[:, ffn:]], [cbias[:ffn], cbias[ffn:]],
                                 FFN_CONV, True, BF16, "ffn_conv")
        ffn_conv_new = jnp.concatenate([fa, fg], axis=-1)
    y = mm_res_norm(act.reshape(ntok, ffn), w["w_down"], h2, w["norm_final"], False, F32, "ffn_down")
    return y.reshape(nb, seq, d), ssd_conv_new, ssd_new, gla_new, ffn_conv_new


def kernel(x_prompt, x_sample, cache_mem_k, cache_mem_v, state_ssd_conv, state_ssd, state_gla, state_ffn_conv, mem_prompt, norm_mix, w_in, ssd_conv_w, ssd_conv_b, ssd_dt_bias, ssd_A_log, ssd_D, ssd_norm, w_ssd_out, w_gla_gate, b_gla_gate, gla_norm, w_gla_out, w_mix_out, norm_cross, norm_mem, w_cq, w_ck, w_cv, w_co, norm_ffn, w_up, ffn_conv_w, ffn_conv_b, w_down, norm_final):
    nb, seq, d = x_prompt.shape
    n_mem = mem_prompt.shape[1]
    ffn2 = w_up.shape[1]
    w = dict(
        norm_mix=norm_mix, norm_cross=norm_cross, norm_ffn=norm_ffn, norm_final=norm_final,
        w_in=_split_w_in(w_in), ssd_conv_w=ssd_conv_w, ssd_conv_b=ssd_conv_b,
        w_ssd_out=w_ssd_out.astype(BF16), w_gla_out=w_gla_out.astype(BF16), w_mix_out=w_mix_out.astype(BF16),
        w_cq=w_cq.astype(BF16), w_co=w_co.astype(BF16), w_up=w_up.astype(BF16), w_down=w_down.astype(BF16),
        ffn_conv_w=ffn_conv_w, ffn_conv_b=ffn_conv_b,
    )
    p = _params(ssd_dt_bias, ssd_A_log, ssd_D, ssd_norm, w_gla_gate, b_gla_gate, gla_norm)

    mn = rmsnorm_cast(mem_prompt.reshape(nb * n_mem, d), norm_mem)
    p_mem_k = matmul(mn, w_ck.astype(BF16), name="mem_k").reshape(nb, n_mem, d)
    p_mem_v = matmul(mn, w_cv.astype(BF16), name="mem_v").reshape(nb, n_mem, d)
    zeros_ssd_conv = jnp.zeros((nb, SSD_CONV - 1, CONV_DIM), F32)
    zeros_ffn_conv = jnp.zeros((nb, FFN_CONV - 1, ffn2), F32)
    y_prompt, p_ssd_conv, p_ssd, p_gla, p_ffn_conv = _layer(
        x_prompt, p_mem_k, p_mem_v, zeros_ssd_conv, None, None, zeros_ffn_conv, w, p, True)

    ns = x_sample.shape[0]
    y_sample, s_ssd_conv, s_ssd, s_gla, s_ffn_conv = _layer(
        x_sample, cache_mem_k, cache_mem_v, state_ssd_conv, state_ssd, state_gla, state_ffn_conv, w, p, False)

    head_shape = (n_mem, CROSS_HEADS, CROSS_HEAD_DIM)
    return (y_prompt, y_sample, p_ssd_conv, p_ssd, p_gla, p_ffn_conv,
            p_mem_k.reshape((nb,) + head_shape), p_mem_v.reshape((nb,) + head_shape),
            s_ssd_conv, s_ssd, s_gla, s_ffn_conv)
```

```python
import functools

import jax
import jax.numpy as jnp
from jax import lax
from jax.experimental import pallas as pl
from jax.experimental.pallas import tpu as pltpu

F32 = jnp.float32
BF16 = jnp.bfloat16
EPS = 1e-6
NEG_BIG = -1e30

D_MODEL = 2048
SSD_HEAD_DIM = 64
SSD_HEADS = 32
D_STATE = 128
N_GROUPS = 4
GROUP_W = D_MODEL // N_GROUPS
BC_W = N_GROUPS * D_STATE
CONV_DIM = D_MODEL + 2 * BC_W
SSD_CONV = 4
SSD_CHUNK = 128
GLA_HEADS = 4
GLA_KEY_DIM = 1024
GLA_HEAD_K = 256
GLA_HEAD_V = 512
GATE_RANK = 16
GATE_TAU = 16.0
GLA_CHUNK = 64
CROSS_HEADS = 4
CROSS_HEAD_DIM = 512
FFN_CONV = 3
LANE = 128

COL_Z, COL_XBC, COL_DT, N_SSD_PROJ = 0, 2048, 5120, 5632
COL_Q, COL_K, COL_V, COL_R, COL_GLR, N_GLA_PROJ = 0, 1024, 2048, 4096, 6144, 6656
COL_GA, COL_GB = 0, 2048

VMEM_LIMIT = 56 * 1024 * 1024


def _cp(*sem):
    return pltpu.CompilerParams(dimension_semantics=sem, vmem_limit_bytes=VMEM_LIMIT)


def _dot(a, b, prec=None):
    return jnp.dot(a, b, preferred_element_type=F32, precision=prec)


def _dot_nt(a, b):
    return lax.dot_general(a, b, (((1,), (1,)), ((), ())), preferred_element_type=F32)


def _split3(x):
    x1 = x.astype(BF16)
    r1 = x - x1.astype(F32)
    x2 = r1.astype(BF16)
    x3 = (r1 - x2.astype(F32)).astype(BF16)
    return x1, x2, x3


def _dot_sel(x, sel, sel_first=False):
    parts = _split3(x)
    if sel_first:
        return _dot(sel, parts[0]) + _dot(sel, parts[1]) + _dot(sel, parts[2])
    return _dot(parts[0], sel) + _dot(parts[1], sel) + _dot(parts[2], sel)


def _sigmoid(x):
    return 1.0 / (1.0 + jnp.exp(-x))


def _silu(x):
    return x * _sigmoid(x)


def _softplus(x):
    return jnp.maximum(x, 0.0) + jnp.log(1.0 + jnp.exp(-jnp.abs(x)))


def _rms(x, g):
    ms = jnp.mean(x * x, axis=-1, keepdims=True)
    return x * lax.rsqrt(ms + EPS) * g


def _rmsnorm_kernel(x_ref, g_ref, o_ref):
    o_ref[...] = _rms(x_ref[...], g_ref[...]).astype(o_ref.dtype)


def rmsnorm_cast(x2, g):
    m, d = x2.shape
    tm = min(m, 512)
    return pl.pallas_call(
        _rmsnorm_kernel,
        grid=(m // tm,),
        in_specs=[pl.BlockSpec((tm, d), lambda i: (i, 0)), pl.BlockSpec((1, d), lambda i: (0, 0))],
        out_specs=pl.BlockSpec((tm, d), lambda i: (i, 0)),
        out_shape=jax.ShapeDtypeStruct((m, d), BF16),
        compiler_params=_cp("parallel"),
        name="rmsnorm_cast",
    )(x2, g.reshape(1, d))


def _mm_kernel(a_ref, w_ref, o_ref, *, w_rows_are_outputs):
    dot = _dot_nt if w_rows_are_outputs else _dot
    o_ref[...] = dot(a_ref[...], w_ref[...]).astype(o_ref.dtype)


def matmul(a, w, out_dtype=F32, tm=1024, tn=512, w_rows_are_outputs=False, name="matmul"):
    m, k = a.shape
    n = w.shape[0] if w_rows_are_outputs else w.shape[1]
    tm = min(m, tm)
    if w_rows_are_outputs:
        w_spec = pl.BlockSpec((tn, k), lambda i, j: (j, 0))
    else:
        w_spec = pl.BlockSpec((k, tn), lambda i, j: (0, j))
    return pl.pallas_call(
        functools.partial(_mm_kernel, w_rows_are_outputs=w_rows_are_outputs),
        grid=(m // tm, n // tn),
        in_specs=[pl.BlockSpec((tm, k), lambda i, j: (i, 0)), w_spec],
        out_specs=pl.BlockSpec((tm, tn), lambda i, j: (i, j)),
        out_shape=jax.ShapeDtypeStruct((m, n), out_dtype),
        compiler_params=_cp("parallel", "arbitrary"),
        name=name,
    )(a, w)


def _mm_res_norm_kernel(a_ref, w_ref, res_ref, g_ref, *out_refs):
    h = res_ref[...] + _dot(a_ref[...], w_ref[...])
    if len(out_refs) == 2:
        out_refs[0][...] = h
    out_refs[-1][...] = _rms(h, g_ref[...]).astype(out_refs[-1].dtype)


def mm_res_norm(a, w, res, g, emit_h, norm_dtype, name):
    m, kdim = a.shape
    n = w.shape[1]
    tm = min(m, 512)
    row_tile = lambda width: pl.BlockSpec((tm, width), lambda i: (i, 0))
    out_shape = [jax.ShapeDtypeStruct((m, n), norm_dtype)]
    out_specs = [row_tile(n)]
    if emit_h:
        out_shape = [jax.ShapeDtypeStruct((m, n), F32)] + out_shape
        out_specs = [row_tile(n)] + out_specs
    outs = pl.pallas_call(
        _mm_res_norm_kernel,
        grid=(m // tm,),
        in_specs=[row_tile(kdim), pl.BlockSpec((kdim, n), lambda i: (0, 0), pipeline_mode=pl.Buffered(1)),
                  row_tile(n), pl.BlockSpec((1, n), lambda i: (0, 0))],
        out_specs=out_specs,
        out_shape=out_shape,
        compiler_params=_cp("parallel"),
        name=name,
    )(a, w, res, g.reshape(1, n))
    return outs if emit_h else outs[0]


def _merge_kernel(u_ref, o_ref, wa_ref, wb_ref, ga_ref, gb_ref, out_ref):
    a = _dot(u_ref[...], wa_ref[...])
    b = _dot(o_ref[...], wb_ref[...])
    out_ref[...] = (_sigmoid(ga_ref[...]) * a + _sigmoid(gb_ref[...]) * b).astype(out_ref.dtype)


def merge_branches(u, o, wa, wb, proj):
    m, d = u.shape
    tm, tn = min(m, 1024), 512
    return pl.pallas_call(
        _merge_kernel,
        grid=(m // tm, d // tn),
        in_specs=[
            pl.BlockSpec((tm, d), lambda i, j: (i, 0)),
            pl.BlockSpec((tm, d), lambda i, j: (i, 0)),
            pl.BlockSpec((d, tn), lambda i, j: (0, j)),
            pl.BlockSpec((d, tn), lambda i, j: (0, j)),
            pl.BlockSpec((tm, tn), lambda i, j: (i, COL_GA // tn + j)),
            pl.BlockSpec((tm, tn), lambda i, j: (i, COL_GB // tn + j)),
        ],
        out_specs=pl.BlockSpec((tm, tn), lambda i, j: (i, j)),
        out_shape=jax.ShapeDtypeStruct((m, d), BF16),
        compiler_params=_cp("parallel", "arbitrary"),
        name="merge_branches",
    )(u, o, wa, wb, proj, proj)


CONV_PAD = 8


def _short_conv_kernel(*refs, taps, seq, nstreams, swiglu):
    ins = refs[: 4 * nstreams]
    out_ref = refs[4 * nstreams]
    ns_refs = refs[4 * nstreams + 1 :]
    vals = []
    for s in range(nstreams):
        u_ref, st_ref, w_ref, b_ref = ins[4 * s : 4 * s + 4]
        nseq, _, tc = st_ref.shape
        full = jnp.concatenate([st_ref[...], u_ref[...].reshape(nseq, seq, tc)], axis=1)
        acc = b_ref[...]
        for k in range(taps):
            acc = acc + full[:, k : k + seq] * w_ref[k : k + 1, :]
        vals.append(acc)
        ns_refs[s][...] = full[:, seq : seq + taps - 1]
    out = _silu(vals[1]) * vals[0] if swiglu else _silu(vals[0])
    out_ref[...] = out.reshape(out_ref.shape).astype(out_ref.dtype)


def short_conv(u2, seq, col_offs, width, states, ws, bs, taps, swiglu, out_dtype, name):
    ntok = u2.shape[0]
    nseq = ntok // seq
    tc = 512
    nstreams = len(col_offs)
    in_specs, args = [], []
    for s in range(nstreams):
        cb = col_offs[s] // tc
        in_specs += [
            pl.BlockSpec((ntok, tc), lambda c, cb=cb: (0, cb + c)),
            pl.BlockSpec((nseq, taps - 1, tc), lambda c: (0, 0, c)),
            pl.BlockSpec((taps, tc), lambda c: (0, c)),
            pl.BlockSpec((1, tc), lambda c: (0, c)),
        ]
        args += [u2, states[s], ws[s], bs[s].reshape(1, width)]
    out_specs = [pl.BlockSpec((ntok, tc), lambda c: (0, c))]
    out_shape = [jax.ShapeDtypeStruct((ntok, width), out_dtype)]
    for s in range(nstreams):
        out_specs.append(pl.BlockSpec((nseq, taps - 1, tc), lambda c: (0, 0, c)))
        out_shape.append(jax.ShapeDtypeStruct((nseq, taps - 1, width), F32))
    return pl.pallas_call(
        functools.partial(_short_conv_kernel, taps=taps, seq=seq, nstreams=nstreams, swiglu=swiglu),
        grid=(width // tc,),
        in_specs=in_specs,
        out_specs=out_specs,
        out_shape=out_shape,
        compiler_params=_cp("parallel"),
        name=name,
    )(*args)


def _shift_rows(u, d, prev):
    x = pltpu.roll(u, d, 0)
    r = lax.broadcasted_iota(jnp.int32, prev.shape, 0)
    head = jnp.where(r < d, pltpu.roll(prev, d, 0), x[:CONV_PAD])
    return jnp.concatenate([head, x[CONV_PAD:]], axis=0)


def _causal_taps(u, prev, cw, bias):
    taps = cw.shape[0]
    acc = bias
    for k in range(taps):
        d = taps - 1 - k
        acc = acc + (_shift_rows(u, d, prev) if d else u) * cw[k : k + 1, :]
    return acc


def _ffn_up_kernel(hn_ref, wa_ref, wg_ref, cwa_ref, cwg_ref, cba_ref, cbg_ref, sta_ref, stg_ref,
                   act_ref, nsa_ref, nsg_ref, preva, prevg, *, tm):
    m = pl.program_id(2)
    nm = pl.num_programs(2)
    lo = CONV_PAD - (FFN_CONV - 1)

    @pl.when(m == 0)
    def _():
        for prev, st_ref in ((preva, sta_ref), (prevg, stg_ref)):
            prev[0:lo, :] = jnp.zeros((lo, prev.shape[1]), F32)
            prev[lo:CONV_PAD, :] = st_ref[0]

    hn = hn_ref[0]
    vals = []
    for w_ref, cw_ref, cb_ref, prev in ((wa_ref, cwa_ref, cba_ref, preva), (wg_ref, cwg_ref, cbg_ref, prevg)):
        u = _dot(hn, w_ref[...])
        vals.append(_causal_taps(u, prev[...], cw_ref[...], cb_ref[...]))
        prev[...] = u[tm - CONV_PAD : tm]
    act_ref[0] = (_silu(vals[1]) * vals[0]).astype(act_ref.dtype)

    @pl.when(m == nm - 1)
    def _():
        nsa_ref[0] = preva[lo:CONV_PAD, :]
        nsg_ref[0] = prevg[lo:CONV_PAD, :]


def ffn_up_conv_act(hn3, w_up, conv_w, conv_b, conv_state):
    nb, seq, d = hn3.shape
    ffn = w_up.shape[1] // 2
    tm, tn = min(seq, 1024), 512
    nn = ffn // tn
    half = lambda off: (lambda b, n, m: (0, off + n))
    st_spec = lambda off: pl.BlockSpec((1, FFN_CONV - 1, tn), lambda b, n, m: (b, 0, off + n))
    cb2 = conv_b.reshape(1, 2 * ffn)
    act, nsa, nsg = pl.pallas_call(
        functools.partial(_ffn_up_kernel, tm=tm),
        grid=(nb, nn, seq // tm),
        in_specs=[
            pl.BlockSpec((1, tm, d), lambda b, n, m: (b, m, 0)),
            pl.BlockSpec((d, tn), half(0)), pl.BlockSpec((d, tn), half(nn)),
            pl.BlockSpec((FFN_CONV, tn), half(0)), pl.BlockSpec((FFN_CONV, tn), half(nn)),
            pl.BlockSpec((1, tn), half(0)), pl.BlockSpec((1, tn), half(nn)),
            st_spec(0), st_spec(nn),
        ],
        out_specs=[
            pl.BlockSpec((1, tm, tn), lambda b, n, m: (b, m, n)),
            pl.BlockSpec((1, FFN_CONV - 1, tn), lambda b, n, m: (b, 0, n)),
            pl.BlockSpec((1, FFN_CONV - 1, tn), lambda b, n, m: (b, 0, n)),
        ],
        out_shape=[
            jax.ShapeDtypeStruct((nb, seq, ffn), BF16),
            jax.ShapeDtypeStruct((nb, FFN_CONV - 1, ffn), F32),
            jax.ShapeDtypeStruct((nb, FFN_CONV - 1, ffn), F32),
        ],
        scratch_shapes=[pltpu.VMEM((CONV_PAD, tn), F32), pltpu.VMEM((CONV_PAD, tn), F32)],
        compiler_params=_cp("parallel", "parallel", "arbitrary"),
        name="ffn_up_conv_act",
    )(hn3, w_up, w_up, conv_w, conv_w, cb2, cb2, conv_state, conv_state)
    return act, jnp.concatenate([nsa, nsg], axis=-1)


def _ssd_gate_norm(y, z, nrm):
    ug = y * _silu(z)
    outs = []
    for g in range(N_GROUPS):
        ugg = ug[:, g * GROUP_W : (g + 1) * GROUP_W]
        ms = jnp.mean(ugg * ugg, axis=-1, keepdims=True)
        outs.append(ugg * lax.rsqrt(ms + EPS))
    return jnp.concatenate(outs, axis=1) * nrm


def _gla_out_norm(o, r, nrm):
    outs = []
    for h in range(GLA_HEADS):
        oh = o[:, h * GLA_HEAD_V : (h + 1) * GLA_HEAD_V]
        rh = r[:, h * GLA_HEAD_V : (h + 1) * GLA_HEAD_V]
        outs.append(_rms(oh, nrm) * _silu(rh))
    return jnp.concatenate(outs, axis=1)


def _ssd_scan_kernel(xs_ref, b_ref, c_ref, dt_ref, z_ref, cst_ref, cw_ref, cbias_ref,
                     dtb_ref, alog_ref, dexp_ref, nrm_ref, e_ref,
                     u_ref, sout_ref, cso_ref, st_ref, cprev, *, nc):
    c = pl.program_id(1)
    q = SSD_CHUNK
    lo = CONV_PAD - (SSD_CONV - 1)

    @pl.when(c == 0)
    def _():
        st_ref[...] = jnp.zeros_like(st_ref)
        cprev[0:lo, :] = jnp.zeros((lo, CONV_DIM), F32)
        cprev[lo:CONV_PAD, :] = cst_ref[0]

    conv = []
    for raw_ref, c0 in ((xs_ref, 0), (b_ref, D_MODEL), (c_ref, D_MODEL + BC_W)):
        raw = raw_ref[0]
        cols = slice(c0, c0 + raw.shape[1])
        conv.append(_silu(_causal_taps(raw, cprev[:, cols], cw_ref[:, cols], cbias_ref[:, cols])))
        cprev[:, cols] = raw[q - CONV_PAD : q]
    xs, bm, cm = conv
    dt = _softplus(dt_ref[0] + dtb_ref[...])
    a = dt * (-jnp.exp(alog_ref[...]))
    row = lax.broadcasted_iota(jnp.int32, (q, q), 0)
    col = lax.broadcasted_iota(jnp.int32, (q, q), 1)
    tril = row >= col
    acum = _dot_sel(a, tril.astype(BF16), sel_first=True)
    acum_t = acum.T
    dt_t = dt.T
    last = acum[q - 1 : q, :]
    e_mat = e_ref[...]
    eexp = _dot_sel(jnp.exp(acum), e_mat)
    wexp = _dot_sel(jnp.exp(last - acum) * dt, e_mat)
    s_bf = st_ref[...].astype(BF16)
    cb16 = cm.astype(BF16)
    bb16 = bm.astype(BF16)
    x16 = xs.astype(BF16)
    xw16 = (xs * wexp).astype(BF16)
    lane_lo = lax.broadcasted_iota(jnp.int32, (q, LANE), 1) < SSD_HEAD_DIM
    ys = []
    for g in range(N_GROUPS):
        cg = cb16[:, g * D_STATE : (g + 1) * D_STATE]
        bg = bb16[:, g * D_STATE : (g + 1) * D_STATE]
        cb = _dot_nt(cg, bg)
        yoff = _dot(cg, s_bf[:, g * GROUP_W : (g + 1) * GROUP_W])
        pieces = []
        for pr in range(GROUP_W // LANE):
            h0 = g * (SSD_HEADS // N_GROUPS) + 2 * pr
            xp = x16[:, h0 * SSD_HEAD_DIM : h0 * SSD_HEAD_DIM + LANE]
            yh = []
            for h in (h0, h0 + 1):
                diff = acum[:, h : h + 1] - acum_t[h : h + 1, :]
                dec = jnp.exp(jnp.where(tril, diff, NEG_BIG))
                m = (cb * dec * dt_t[h : h + 1, :]).astype(BF16)
                yh.append(_dot(m, xp))
            pieces.append(jnp.where(lane_lo, yh[0], yh[1]))
        sl = slice(g * GROUP_W, (g + 1) * GROUP_W)
        ys.append(jnp.concatenate(pieces, axis=1) + yoff * eexp[:, sl])
        bg_t = bm[:, g * D_STATE : (g + 1) * D_STATE].T.astype(BF16)
        upd = _dot(bg_t, xw16[:, sl])
        st_ref[:, sl] = eexp[q - 1 : q, sl] * st_ref[:, sl] + upd
    y = jnp.concatenate(ys, axis=1) + dexp_ref[...] * xs
    u_ref[0] = _ssd_gate_norm(y, z_ref[0], nrm_ref[...]).astype(u_ref.dtype)

    @pl.when(c == nc - 1)
    def _():
        sout_ref[0] = st_ref[...].T
        cso_ref[0] = cprev[lo:CONV_PAD, :]


def ssd_scan(proj3, conv_state, conv_w, conv_b, p):
    nb, seq, _ = proj3.shape
    nc = seq // SSD_CHUNK
    q = SSD_CHUNK
    vec = lambda n: pl.BlockSpec((1, n), lambda b, c: (0, 0))
    u, s_out, conv_new = pl.pallas_call(
        functools.partial(_ssd_scan_kernel, nc=nc),
        grid=(nb, nc),
        in_specs=[
            pl.BlockSpec((1, q, D_MODEL), lambda b, c: (b, c, COL_XBC // D_MODEL)),
            pl.BlockSpec((1, q, BC_W), lambda b, c: (b, c, (COL_XBC + D_MODEL) // BC_W)),
            pl.BlockSpec((1, q, BC_W), lambda b, c: (b, c, (COL_XBC + D_MODEL) // BC_W + 1)),
            pl.BlockSpec((1, q, LANE), lambda b, c: (b, c, COL_DT // LANE)),
            pl.BlockSpec((1, q, D_MODEL), lambda b, c: (b, c, COL_Z // D_MODEL)),
            pl.BlockSpec((1, SSD_CONV - 1, CONV_DIM), lambda b, c: (b, 0, 0)),
            pl.BlockSpec((SSD_CONV, CONV_DIM), lambda b, c: (0, 0)),
            vec(CONV_DIM),
            vec(LANE), vec(LANE), vec(D_MODEL), vec(D_MODEL),
            pl.BlockSpec((LANE, D_MODEL), lambda b, c: (0, 0)),
        ],
        out_specs=[
            pl.BlockSpec((1, q, D_MODEL), lambda b, c: (b, c, 0)),
            pl.BlockSpec((1, D_MODEL, D_STATE), lambda b, c: (b, 0, 0)),
            pl.BlockSpec((1, SSD_CONV - 1, CONV_DIM), lambda b, c: (b, 0, 0)),
        ],
        out_shape=[
            jax.ShapeDtypeStruct((nb, seq, D_MODEL), BF16),
            jax.ShapeDtypeStruct((nb, D_MODEL, D_STATE), F32),
            jax.ShapeDtypeStruct((nb, SSD_CONV - 1, CONV_DIM), F32),
        ],
        scratch_shapes=[pltpu.VMEM((D_STATE, D_MODEL), F32), pltpu.VMEM((CONV_PAD, CONV_DIM), F32)],
        compiler_params=_cp("parallel", "arbitrary"),
        name="ssd_scan",
    )(proj3, proj3, proj3, proj3, proj3, conv_state, conv_w, conv_b.reshape(1, CONV_DIM),
      p["dt_bias"], p["a_log"], p["d_exp"], p["ssd_norm"], p["e_head"])
    return u.reshape(nb * seq, D_MODEL), s_out.reshape(nb, SSD_HEADS, SSD_HEAD_DIM, D_STATE), conv_new


def _gla_gate_log(glr, wg, bg):
    x = _dot(glr.astype(BF16), wg) + bg
    return -_softplus(-x) / GATE_TAU


def _gla_scan_kernel(q_ref, k_ref, v_ref, r_ref, glr_ref, wg_ref, bg_ref, nrm_ref,
                     o_ref, sout_ref, st_ref, *, nc):
    c = pl.program_id(1)
    q = GLA_CHUNK

    @pl.when(c == 0)
    def _():
        st_ref[...] = jnp.zeros_like(st_ref)

    glog = _gla_gate_log(glr_ref[0], wg_ref[...], bg_ref[...])
    row = lax.broadcasted_iota(jnp.int32, (q, q), 0)
    col = lax.broadcasted_iota(jnp.int32, (q, q), 1)
    tril = row >= col
    bc = _dot_sel(glog, tril.astype(BF16), sel_first=True)
    last = bc[q - 1 : q, :]
    kk = k_ref[0]
    qe = q_ref[0] * (GLA_HEAD_K ** -0.5) * jnp.exp(bc)
    ke = kk * jnp.exp(-bc)
    kd = kk * jnp.exp(last - bc)
    elast = jnp.exp(last)
    v16 = v_ref[0].astype(BF16)
    zeros_v = jnp.zeros((q, GLA_HEAD_V), BF16)
    outs = []
    for h in range(GLA_HEADS):
        ks = slice(h * GLA_HEAD_K, (h + 1) * GLA_HEAD_K)
        vs = slice(h * GLA_HEAD_V, (h + 1) * GLA_HEAD_V)
        qh = qe[:, ks].astype(BF16)
        kh = ke[:, ks].astype(BF16)
        att = jnp.where(tril, _dot_nt(qh, kh), 0.0)
        s_h = st_ref[ks, :]
        outs.append(_dot(att.astype(BF16), v16[:, vs]) + _dot(qh, s_h.astype(BF16)))
        xt = jnp.concatenate([kd[:, ks], jnp.broadcast_to(elast[:, ks], (q, GLA_HEAD_K))], axis=0).T
        v2 = jnp.concatenate([v16[:, vs], zeros_v], axis=0)
        st_ref[ks, :] = xt[:, q : q + 1] * s_h + _dot(xt.astype(BF16), v2)
    o = jnp.concatenate(outs, axis=1)
    o_ref[0] = _gla_out_norm(o, r_ref[0], nrm_ref[...]).astype(o_ref.dtype)

    @pl.when(c == nc - 1)
    def _():
        sout_ref[0] = st_ref[...]


def gla_scan(proj3, p):
    nb, seq, _ = proj3.shape
    q = GLA_CHUNK
    nc = seq // q
    o, s_out = pl.pallas_call(
        functools.partial(_gla_scan_kernel, nc=nc),
        grid=(nb, nc),
        in_specs=[
            pl.BlockSpec((1, q, GLA_KEY_DIM), lambda b, c: (b, c, COL_Q // GLA_KEY_DIM)),
            pl.BlockSpec((1, q, GLA_KEY_DIM), lambda b, c: (b, c, COL_K // GLA_KEY_DIM)),
            pl.BlockSpec((1, q, D_MODEL), lambda b, c: (b, c, COL_V // D_MODEL)),
            pl.BlockSpec((1, q, D_MODEL), lambda b, c: (b, c, COL_R // D_MODEL)),
            pl.BlockSpec((1, q, LANE), lambda b, c: (b, c, COL_GLR // LANE)),
            pl.BlockSpec((LANE, GLA_KEY_DIM), lambda b, c: (0, 0)),
            pl.BlockSpec((1, GLA_KEY_DIM), lambda b, c: (0, 0)),
            pl.BlockSpec((1, GLA_HEAD_V), lambda b, c: (0, 0)),
        ],
        out_specs=[
            pl.BlockSpec((1, q, D_MODEL), lambda b, c: (b, c, 0)),
            pl.BlockSpec((1, GLA_KEY_DIM, GLA_HEAD_V), lambda b, c: (b, 0, 0)),
        ],
        out_shape=[
            jax.ShapeDtypeStruct((nb, seq, D_MODEL), BF16),
            jax.ShapeDtypeStruct((nb, GLA_KEY_DIM, GLA_HEAD_V), F32),
        ],
        scratch_shapes=[pltpu.VMEM((GLA_KEY_DIM, GLA_HEAD_V), F32)],
        compiler_params=_cp("parallel", "arbitrary"),
        name="gla_scan",
    )(proj3, proj3, proj3, proj3, proj3, p["w_gate"], p["b_gate"], p["gla_norm"])
    return o.reshape(nb * seq, D_MODEL), s_out.reshape(nb, GLA_HEADS, GLA_HEAD_K, GLA_HEAD_V)


TOK_BLOCK = 128


def _row_shift(x, d, tpos):
    return jnp.where(tpos >= d, pltpu.roll(x, d, 0), 0.0)


def _seq_cumsum_and_last(a, seq, tpos):
    nrows = a.shape[0]
    acum = a
    for d in range(1, seq):
        acum = acum + _row_shift(a, d, tpos)
    last = jnp.where(tpos == seq - 1, acum, 0.0)
    for d in range(1, seq):
        last = last + jnp.where(tpos == seq - 1 - d, pltpu.roll(acum, nrows - d, 0), 0.0)
    return acum, last


def _ssd_step_pre_kernel(xs_ref, b_ref, c_ref, dt_ref, dtb_ref, alog_ref, dexp_ref, e_ref, gh_ref,
                         ypart_ref, eexp_ref, xwt_ref, el_ref, *, seq):
    nrows = xs_ref.shape[0]
    xs = xs_ref[...]
    bm = b_ref[...]
    cm = c_ref[...]
    dt = _softplus(dt_ref[...] + dtb_ref[...])
    a = dt * (-jnp.exp(alog_ref[...]))
    pos = lambda w: lax.broadcasted_iota(jnp.int32, (nrows, w), 0) % seq
    t_h, t_c, t_x = pos(LANE), pos(BC_W), pos(D_MODEL)
    acum, last = _seq_cumsum_and_last(a, seq, t_h)
    e_mat = e_ref[...]
    eexp_ref[...] = _dot_sel(jnp.exp(acum), e_mat)
    wexp = _dot_sel(jnp.exp(last - acum) * dt, e_mat)
    xwt_ref[...] = (xs * wexp).T.astype(xwt_ref.dtype)
    el_ref[...] = jnp.exp(last)
    y = dexp_ref[...] * xs
    for d in range(seq):
        if d == 0:
            cbh = _dot_sel(cm * bm, gh_ref[...])
            coef = dt
            xd = xs
        else:
            cbh = _dot_sel(cm * _row_shift(bm, d, t_c), gh_ref[...])
            coef = jnp.where(t_h >= d, jnp.exp(acum - pltpu.roll(acum, d, 0)) * pltpu.roll(dt, d, 0), 0.0)
            xd = _row_shift(xs, d, t_x)
        y = y + _dot_sel(cbh * coef, e_mat) * xd
    ypart_ref[...] = y


def _ssd_step_state_kernel(st_ref, c_ref, b_ref, xwt_ref, el_ref, ypart_ref, eexp_ref, z_ref, nrm_ref,
                           u_ref, so_ref, *, sb, seq):
    i = pl.program_id(0)
    rows = sb * seq
    steps_per_block = TOK_BLOCK // rows
    base = (i % steps_per_block) * rows
    c16 = c_ref[...].astype(BF16)
    btok = b_ref[...]
    tok = lax.broadcasted_iota(jnp.int32, (TOK_BLOCK, LANE), 0)
    rsel = lax.broadcasted_iota(jnp.int32, (rows, GROUP_W), 0)
    heads_per_group = SSD_HEADS // N_GROUPS
    yoff = [jnp.zeros((rows, GROUP_W), F32) for _ in range(N_GROUPS)]
    for s in range(sb):
        lo = base + seq * s
        own = (tok >= lo) & (tok < lo + seq)
        mine = (rsel >= seq * s) & (rsel < seq * (s + 1))
        for g in range(N_GROUPS):
            sl = slice(g * GROUP_W, (g + 1) * GROUP_W)
            yo = _dot_nt(c16[:, g * D_STATE : (g + 1) * D_STATE], st_ref[s, sl, :].astype(BF16))
            yoff[g] = jnp.where(mine, yo, yoff[g])
            bsel = jnp.where(own, btok[:, g * D_STATE : (g + 1) * D_STATE], 0.0).astype(BF16)
            upd = _dot(xwt_ref[sl, :], bsel)
            for r in range(heads_per_group):
                h = g * heads_per_group + r
                hs = slice(h * SSD_HEAD_DIM, (h + 1) * SSD_HEAD_DIM)
                so_ref[s, hs, :] = el_ref[seq * s, h] * st_ref[s, hs, :] + upd[r * SSD_HEAD_DIM : (r + 1) * SSD_HEAD_DIM]
    y = ypart_ref[...] + jnp.concatenate(yoff, axis=1) * eexp_ref[...]
    u_ref[...] = _ssd_gate_norm(y, z_ref[...], nrm_ref[...]).astype(u_ref.dtype)


def ssd_step(xc2, proj2, state, p, seq):
    ntok = xc2.shape[0]
    nseq = ntok // seq
    full = lambda shape: pl.BlockSpec(shape, lambda i: (0,) * len(shape))
    ypart, eexp, xwt, elast = pl.pallas_call(
        functools.partial(_ssd_step_pre_kernel, seq=seq),
        grid=(1,),
        in_specs=[
            pl.BlockSpec((ntok, D_MODEL), lambda i: (0, 0)),
            pl.BlockSpec((ntok, BC_W), lambda i: (0, D_MODEL // BC_W)),
            pl.BlockSpec((ntok, BC_W), lambda i: (0, D_MODEL // BC_W + 1)),
            pl.BlockSpec((ntok, LANE), lambda i: (0, COL_DT // LANE)),
            full((1, LANE)), full((1, LANE)), full((1, D_MODEL)),
            full((LANE, D_MODEL)), full((BC_W, LANE)),
        ],
        out_specs=[full((ntok, D_MODEL)), full((ntok, D_MODEL)), full((D_MODEL, ntok)), full((ntok, LANE))],
        out_shape=[
            jax.ShapeDtypeStruct((ntok, D_MODEL), F32),
            jax.ShapeDtypeStruct((ntok, D_MODEL), F32),
            jax.ShapeDtypeStruct((D_MODEL, ntok), BF16),
            jax.ShapeDtypeStruct((ntok, LANE), F32),
        ],
        compiler_params=_cp("arbitrary"),
        name="ssd_step_pre",
    )(xc2, xc2, xc2, proj2, p["dt_bias"], p["a_log"], p["d_exp"], p["e_head"], p["g_head"])

    sb = 4
    rows = sb * seq
    spb = TOK_BLOCK // rows
    st3 = state.reshape(nseq, D_MODEL, D_STATE)
    u, s_new = pl.pallas_call(
        functools.partial(_ssd_step_state_kernel, sb=sb, seq=seq),
        grid=(nseq // sb,),
        in_specs=[
            pl.BlockSpec((sb, D_MODEL, D_STATE), lambda i: (i, 0, 0)),
            pl.BlockSpec((rows, BC_W), lambda i: (i, D_MODEL // BC_W + 1)),
            pl.BlockSpec((TOK_BLOCK, BC_W), lambda i: (i // spb, D_MODEL // BC_W)),
            pl.BlockSpec((D_MODEL, TOK_BLOCK), lambda i: (0, i // spb)),
            pl.BlockSpec((rows, LANE), lambda i: (i, 0), memory_space=pltpu.SMEM),
            pl.BlockSpec((rows, D_MODEL), lambda i: (i, 0)),
            pl.BlockSpec((rows, D_MODEL), lambda i: (i, 0)),
            pl.BlockSpec((rows, D_MODEL), lambda i: (i, COL_Z // D_MODEL)),
            pl.BlockSpec((1, D_MODEL), lambda i: (0, 0)),
        ],
        out_specs=[
            pl.BlockSpec((rows, D_MODEL), lambda i: (i, 0)),
            pl.BlockSpec((sb, D_MODEL, D_STATE), lambda i: (i, 0, 0)),
        ],
        out_shape=[
            jax.ShapeDtypeStruct((ntok, D_MODEL), BF16),
            jax.ShapeDtypeStruct((nseq, D_MODEL, D_STATE), F32),
        ],
        compiler_params=_cp("parallel"),
        name="ssd_step_state",
    )(st3, xc2, xc2, xwt, elast, ypart, eexp, proj2, p["ssd_norm"])
    return u, s_new.reshape(nseq, SSD_HEADS, SSD_HEAD_DIM, D_STATE)


def _gla_step_pre_kernel(q_ref, k_ref, v_ref, glr_ref, wg_ref, bg_ref, gv_ref,
                         oin_ref, qe_ref, kdt_ref, elt_ref, *, seq):
    nrows = q_ref.shape[0]
    glog = _gla_gate_log(glr_ref[...], wg_ref[...], bg_ref[...])
    pos = lambda w: lax.broadcasted_iota(jnp.int32, (nrows, w), 0) % seq
    t_k, t_v = pos(GLA_KEY_DIM), pos(D_MODEL)
    bc, last = _seq_cumsum_and_last(glog, seq, t_k)
    kk = k_ref[...]
    qe = q_ref[...] * (GLA_HEAD_K ** -0.5) * jnp.exp(bc)
    ke = kk * jnp.exp(-bc)
    qe_ref[...] = qe
    kdt_ref[...] = (kk * jnp.exp(last - bc)).T
    elt_ref[...] = jnp.exp(last).T
    v = v_ref[...]
    gv = gv_ref[...]
    o = jnp.zeros((nrows, D_MODEL), F32)
    for d in range(seq):
        ked = ke if d == 0 else _row_shift(ke, d, t_k)
        vd = v if d == 0 else _row_shift(v, d, t_v)
        att = _dot((qe * ked).astype(BF16), gv)
        o = o + att * vd
    oin_ref[...] = o


def _gla_step_state_kernel(st_ref, qe_ref, v_ref, kdt_ref, elt_ref, oin_ref, r_ref, nrm_ref,
                           o_ref, so_ref, *, sb, seq):
    i = pl.program_id(0)
    rows = sb * seq
    spb = TOK_BLOCK // rows
    base = (i % spb) * rows
    qe16 = qe_ref[...].astype(BF16)
    vtok = v_ref[...]
    tokv = lax.broadcasted_iota(jnp.int32, (TOK_BLOCK, GLA_HEAD_V), 0)
    tok = lax.broadcasted_iota(jnp.int32, (TOK_BLOCK, LANE), 0)
    rsel = lax.broadcasted_iota(jnp.int32, (rows, GLA_HEAD_V), 0)
    ooff = [jnp.zeros((rows, GLA_HEAD_V), F32) for _ in range(GLA_HEADS)]
    for s in range(sb):
        lo = base + seq * s
        own = (tokv >= lo) & (tokv < lo + seq)
        first = jnp.where(tok == lo, 1.0, 0.0).astype(BF16)
        mine = (rsel >= seq * s) & (rsel < seq * (s + 1))
        for h in range(GLA_HEADS):
            ks = slice(h * GLA_HEAD_K, (h + 1) * GLA_HEAD_K)
            vs = slice(h * GLA_HEAD_V, (h + 1) * GLA_HEAD_V)
            s_h = st_ref[s, ks, :]
            oo = _dot(qe16[:, ks], s_h.astype(BF16))
            ooff[h] = jnp.where(mine, oo, ooff[h])
            vsel = jnp.where(own, vtok[:, vs], 0.0).astype(BF16)
            upd = _dot(kdt_ref[ks, :].astype(BF16), vsel)
            ecol = _dot_sel(elt_ref[ks, :], first)
            so_ref[s, ks, :] = jnp.concatenate([ecol] * (GLA_HEAD_V // LANE), axis=1) * s_h + upd
    o = oin_ref[...] + jnp.concatenate(ooff, axis=1)
    o_ref[0] = _gla_out_norm(o, r_ref[...], nrm_ref[...]).astype(o_ref.dtype)


def gla_step(proj2, state, p, seq):
    ntok = proj2.shape[0]
    nseq = ntok // seq
    full = lambda shape: pl.BlockSpec(shape, lambda i: (0,) * len(shape))
    oin, qe, kdt, elt = pl.pallas_call(
        functools.partial(_gla_step_pre_kernel, seq=seq),
        grid=(1,),
        in_specs=[
            pl.BlockSpec((ntok, GLA_KEY_DIM), lambda i: (0, COL_Q // GLA_KEY_DIM)),
            pl.BlockSpec((ntok, GLA_KEY_DIM), lambda i: (0, COL_K // GLA_KEY_DIM)),
            pl.BlockSpec((ntok, D_MODEL), lambda i: (0, COL_V // D_MODEL)),
            pl.BlockSpec((ntok, LANE), lambda i: (0, COL_GLR // LANE)),
            full((LANE, GLA_KEY_DIM)), full((1, GLA_KEY_DIM)), full((GLA_KEY_DIM, D_MODEL)),
        ],
        out_specs=[full((ntok, D_MODEL)), full((ntok, GLA_KEY_DIM)), full((GLA_KEY_DIM, ntok)),
                   full((GLA_KEY_DIM, ntok))],
        out_shape=[
            jax.ShapeDtypeStruct((ntok, D_MODEL), F32),
            jax.ShapeDtypeStruct((ntok, GLA_KEY_DIM), F32),
            jax.ShapeDtypeStruct((GLA_KEY_DIM, ntok), F32),
            jax.ShapeDtypeStruct((GLA_KEY_DIM, ntok), F32),
        ],
        compiler_params=_cp("arbitrary"),
        name="gla_step_pre",
    )(proj2, proj2, proj2, proj2, p["w_gate"], p["b_gate"], p["g_val"])

    sb = 4
    rows = sb * seq
    spb = TOK_BLOCK // rows
    st3 = state.reshape(nseq, GLA_KEY_DIM, GLA_HEAD_V)
    o, s_new = pl.pallas_call(
        functools.partial(_gla_step_state_kernel, sb=sb, seq=seq),
        grid=(nseq // sb,),
        in_specs=[
            pl.BlockSpec((sb, GLA_KEY_DIM, GLA_HEAD_V), lambda i: (i, 0, 0)),
            pl.BlockSpec((rows, GLA_KEY_DIM), lambda i: (i, 0)),
            pl.BlockSpec((TOK_BLOCK, D_MODEL), lambda i: (i // spb, COL_V // D_MODEL)),
            pl.BlockSpec((GLA_KEY_DIM, TOK_BLOCK), lambda i: (0, i // spb)),
            pl.BlockSpec((GLA_KEY_DIM, TOK_BLOCK), lambda i: (0, i // spb)),
            pl.BlockSpec((rows, D_MODEL), lambda i: (i, 0)),
            pl.BlockSpec((rows, D_MODEL), lambda i: (i, COL_R // D_MODEL)),
            pl.BlockSpec((1, GLA_HEAD_V), lambda i: (0, 0)),
        ],
        out_specs=[
            pl.BlockSpec((1, rows, D_MODEL), lambda i: (i, 0, 0)),
            pl.BlockSpec((sb, GLA_KEY_DIM, GLA_HEAD_V), lambda i: (i, 0, 0)),
        ],
        out_shape=[
            jax.ShapeDtypeStruct((nseq // sb, rows, D_MODEL), BF16),
            jax.ShapeDtypeStruct((nseq, GLA_KEY_DIM, GLA_HEAD_V), F32),
        ],
        compiler_params=_cp("parallel"),
        name="gla_step_state",
    )(st3, qe, proj2, kdt, elt, oin, proj2, p["gla_norm"])
    return o.reshape(ntok, D_MODEL), s_new.reshape(nseq, GLA_HEADS, GLA_HEAD_K, GLA_HEAD_V)


def _softmax_rows(sc):
    e = jnp.exp(sc - jnp.max(sc, axis=-1, keepdims=True))
    return e / jnp.sum(e, axis=-1, keepdims=True)


def _xattn_kernel(q_ref, k_ref, v_ref, o_ref):
    q16 = q_ref[0]
    outs = []
    for h in range(CROSS_HEADS):
        hs = slice(h * CROSS_HEAD_DIM, (h + 1) * CROSS_HEAD_DIM)
        sc = _dot_nt(q16[:, hs], k_ref[0, :, hs].astype(BF16)) * (CROSS_HEAD_DIM ** -0.5)
        outs.append(_dot(_softmax_rows(sc).astype(BF16), v_ref[0, :, hs].astype(BF16)))
    o_ref[0] = jnp.concatenate(outs, axis=1).astype(o_ref.dtype)


def _xattn_step_kernel(q_ref, k_ref, v_ref, o_ref, *, nseq, tl):
    rows = nseq * tl
    n_mem = k_ref.shape[1]
    q = q_ref[0]
    qs = jnp.concatenate([q[:, h * CROSS_HEAD_DIM : (h + 1) * CROSS_HEAD_DIM] for h in range(CROSS_HEADS)],
                         axis=0).astype(BF16)
    shape = (CROSS_HEADS * rows, n_mem * CROSS_HEADS)
    col_head = lax.broadcasted_iota(jnp.int32, shape, 1) % CROSS_HEADS
    row_head = lax.broadcasted_iota(jnp.int32, shape, 0) // rows
    same_head = col_head == row_head
    rsel = lax.broadcasted_iota(jnp.int32, (CROSS_HEADS * rows, CROSS_HEAD_DIM), 0) % rows
    out = jnp.zeros((CROSS_HEADS * rows, CROSS_HEAD_DIM), F32)
    for s in range(nseq):
        kall = k_ref[s].reshape(n_mem * CROSS_HEADS, CROSS_HEAD_DIM).astype(BF16)
        vall = v_ref[s].reshape(n_mem * CROSS_HEADS, CROSS_HEAD_DIM).astype(BF16)
        sc = jnp.where(same_head, _dot_nt(qs, kall) * (CROSS_HEAD_DIM ** -0.5), NEG_BIG)
        oh = _dot(_softmax_rows(sc).astype(BF16), vall)
        out = jnp.where((rsel >= tl * s) & (rsel < tl * (s + 1)), oh, out)
    o_ref[0] = jnp.concatenate([out[h * rows : (h + 1) * rows] for h in range(CROSS_HEADS)],
                               axis=1).astype(o_ref.dtype)


def cross_attend(q2, mem_k, mem_v, nb, seq):
    n_mem = mem_k.shape[1]
    if mem_k.ndim == 3:
        nseq, tl = 1, min(seq, 512)
        body = _xattn_kernel
    else:
        nseq, tl = 8 // seq, seq
        body = functools.partial(_xattn_step_kernel, nseq=nseq, tl=tl)
    rows = nseq * tl
    lt = seq // tl
    nblk = nb * seq // rows
    q3 = q2.reshape(nblk, rows, D_MODEL)
    if mem_k.ndim == 4:
        kv_spec = pl.BlockSpec((nseq, n_mem, CROSS_HEADS, CROSS_HEAD_DIM), lambda i: (i // lt, 0, 0, 0))
    else:
        kv_spec = pl.BlockSpec((nseq, n_mem, D_MODEL), lambda i: (i // lt, 0, 0))
    o = pl.pallas_call(
        body,
        grid=(nblk,),
        in_specs=[pl.BlockSpec((1, rows, D_MODEL), lambda i: (i, 0, 0)), kv_spec, kv_spec],
        out_specs=pl.BlockSpec((1, rows, D_MODEL), lambda i: (i, 0, 0)),
        out_shape=jax.ShapeDtypeStruct((nblk, rows, D_MODEL), BF16),
        compiler_params=_cp("parallel"),
        name="cross_attend",
    )(q3, mem_k, mem_v)
    return o.reshape(nb * seq, D_MODEL)


PACK_TILE = 512


SUBLANE = 8


def _cast_rows_kernel(w_ref, o_ref, *, nrows):
    j = pl.program_id(0)
    x = w_ref[...]
    row = lax.broadcasted_iota(jnp.int32, x.shape, 0) + j * PACK_TILE
    o_ref[...] = jnp.where(row < nrows, x, 0.0).astype(o_ref.dtype)


def cast_rows(wt, lo, hi, n_out, name):
    kdim = wt.shape[1]
    assert lo % SUBLANE == 0 and lo + n_out <= wt.shape[0]
    return pl.pallas_call(
        functools.partial(_cast_rows_kernel, nrows=hi - lo),
        grid=(n_out // PACK_TILE,),
        in_specs=[pl.BlockSpec((pl.Element(PACK_TILE), pl.Element(kdim)),
                               lambda j: (pl.multiple_of(lo + j * PACK_TILE, SUBLANE), 0))],
        out_specs=pl.BlockSpec((PACK_TILE, kdim), lambda j: (j, 0)),
        out_shape=jax.ShapeDtypeStruct((n_out, kdim), BF16),
        compiler_params=_cp("parallel"),
        name=name,
    )(wt)


def _split_w_in(w_in):
    wt = w_in.T
    ssd_end = D_MODEL + CONV_DIM + SSD_HEADS
    gla_end = ssd_end + 2 * GLA_KEY_DIM + 2 * D_MODEL + GATE_RANK
    return (cast_rows(wt, 0, ssd_end, N_SSD_PROJ, "pack_ssd_in"),
            cast_rows(wt, ssd_end, gla_end, N_GLA_PROJ, "pack_gla_in"),
            cast_rows(wt, gla_end, wt.shape[0], 2 * D_MODEL, "pack_gate_in"))


def _params(ssd_dt_bias, ssd_A_log, ssd_D, ssd_norm, w_gla_gate, b_gla_gate, gla_norm):
    padv = lambda a: jnp.pad(a.astype(F32), (0, LANE - a.shape[0])).reshape(1, LANE)
    head_of_chan = jnp.arange(D_MODEL, dtype=jnp.int32) // SSD_HEAD_DIM
    e_head = (jnp.arange(LANE, dtype=jnp.int32)[:, None] == head_of_chan[None, :]).astype(BF16)
    group_of_bc = jnp.arange(BC_W, dtype=jnp.int32) // D_STATE
    lane_h = jnp.arange(LANE, dtype=jnp.int32)
    g_head = ((lane_h[None, :] // (SSD_HEADS // N_GROUPS) == group_of_bc[:, None])
              & (lane_h[None, :] < SSD_HEADS)).astype(BF16)
    khead = jnp.arange(GLA_KEY_DIM, dtype=jnp.int32) // GLA_HEAD_K
    vhead = jnp.arange(D_MODEL, dtype=jnp.int32) // GLA_HEAD_V
    g_val = (khead[:, None] == vhead[None, :]).astype(BF16)
    return dict(
        dt_bias=padv(ssd_dt_bias), a_log=padv(ssd_A_log),
        d_exp=jnp.repeat(ssd_D.astype(F32), SSD_HEAD_DIM).reshape(1, D_MODEL),
        ssd_norm=ssd_norm.astype(F32).reshape(1, D_MODEL),
        e_head=e_head, g_head=g_head, g_val=g_val,
        w_gate=jnp.pad(w_gla_gate, ((0, LANE - GATE_RANK), (0, 0))).astype(BF16),
        b_gate=b_gla_gate.astype(F32).reshape(1, GLA_KEY_DIM),
        gla_norm=gla_norm.astype(F32).reshape(1, GLA_HEAD_V),
    )


def _layer(x3, mem_k, mem_v, ssd_conv, ssd_state, gla_state, ffn_conv, w, p, long_seq):
    nb, seq, d = x3.shape
    ntok = nb * seq
    ffn = w["w_down"].shape[0]
    x2 = x3.reshape(ntok, d)
    xn = rmsnorm_cast(x2, w["norm_mix"])
    w_ssd_in, w_gla_in, w_gate_in = w["w_in"]
    in_proj = functools.partial(matmul, xn, tm=2048, w_rows_are_outputs=True)
    proj_ssd = in_proj(w_ssd_in, name="in_proj_ssd")
    proj_gla = in_proj(w_gla_in, name="in_proj_gla")
    gates = in_proj(w_gate_in, name="in_proj_gates")
    if long_seq:
        u, ssd_new, ssd_conv_new = ssd_scan(proj_ssd.reshape(nb, seq, N_SSD_PROJ), ssd_conv,
                                            w["ssd_conv_w"], w["ssd_conv_b"], p)
        o, gla_new = gla_scan(proj_gla.reshape(nb, seq, N_GLA_PROJ), p)
    else:
        xc2, ssd_conv_new = short_conv(proj_ssd, seq, [COL_XBC], CONV_DIM, [ssd_conv], [w["ssd_conv_w"]],
                                       [w["ssd_conv_b"]], SSD_CONV, False, F32, "ssd_conv")
        u, ssd_new = ssd_step(xc2, proj_ssd, ssd_state, p, seq)
        o, gla_new = gla_step(proj_gla, gla_state, p, seq)
    merged = merge_branches(u, o, w["w_ssd_out"], w["w_gla_out"], gates)
    h, hn = mm_res_norm(merged, w["w_mix_out"], x2, w["norm_cross"], True, BF16, "mix_out")
    qc = matmul(hn, w["w_cq"], out_dtype=BF16 if long_seq else F32, name="cross_q")
    att = cross_attend(qc, mem_k, mem_v, nb, seq)
    h2, hn2 = mm_res_norm(att, w["w_co"], h, w["norm_ffn"], True, BF16, "cross_out")
    cw, cbias = w["ffn_conv_w"], w["ffn_conv_b"]
    if long_seq:
        act, ffn_conv_new = ffn_up_conv_act(hn2.reshape(nb, seq, d), w["w_up"], cw, cbias, ffn_conv)
    else:
        up = matmul(hn2, w["w_up"], name="ffn_up")
        act, fa, fg = short_conv(up, seq, [0, ffn], ffn, [ffn_conv[:, :, :ffn], ffn_conv[:, :, ffn:]],
                                 [cw[:, :ffn], cw[:, ffn:]], [cbias[:ffn], cbias[ffn:]],
                                 FFN_CONV, True, BF16, "ffn_conv")
        ffn_conv_new = jnp.concatenate([fa, fg], axis=-1)
    y = mm_res_norm(act.reshape(ntok, ffn), w["w_down"], h2, w["norm_final"], False, F32, "ffn_down")
    return y.reshape(nb, seq, d), ssd_conv_new, ssd_new, gla_new, ffn_conv_new


def kernel(x_prompt, x_sample, cache_mem_k, cache_mem_v, state_ssd_conv, state_ssd, state_gla, state_ffn_conv, mem_prompt, norm_mix, w_in, ssd_conv_w, ssd_conv_b, ssd_dt_bias, ssd_A_log, ssd_D, ssd_norm, w_ssd_out, w_gla_gate, b_gla_gate, gla_norm, w_gla_out, w_mix_out, norm_cross, norm_mem, w_cq, w_ck, w_cv, w_co, norm_ffn, w_up, ffn_conv_w, ffn_conv_b, w_down, norm_final):
    nb, seq, d = x_prompt.shape
    n_mem = mem_prompt.shape[1]
    ffn2 = w_up.shape[1]
    w = dict(
        norm_mix=norm_mix, norm_cross=norm_cross, norm_ffn=norm_ffn, norm_final=norm_final,
        w_in=_split_w_in(w_in), ssd_conv_w=ssd_conv_w, ssd_conv_b=ssd_conv_b,
        w_ssd_out=w_ssd_out.astype(BF16), w_gla_out=w_gla_out.astype(BF16), w_mix_out=w_mix_out.astype(BF16),
        w_cq=w_cq.astype(BF16), w_co=w_co.astype(BF16), w_up=w_up.astype(BF16), w_down=w_down.astype(BF16),
        ffn_conv_w=ffn_conv_w, ffn_conv_b=ffn_conv_b,
    )
    p = _params(ssd_dt_bias, ssd_A_log, ssd_D, ssd_norm, w_gla_gate, b_gla_gate, gla_norm)

    mn = rmsnorm_cast(mem_prompt.reshape(nb * n_mem, d), norm_mem)
    p_mem_k = matmul(mn, w_ck.astype(BF16), name="mem_k").reshape(nb, n_mem, d)
    p_mem_v = matmul(mn, w_cv.astype(BF16), name="mem_v").reshape(nb, n_mem, d)
    zeros_ssd_conv = jnp.zeros((nb, SSD_CONV - 1, CONV_DIM), F32)
    zeros_ffn_conv = jnp.zeros((nb, FFN_CONV - 1, ffn2), F32)
    y_prompt, p_ssd_conv, p_ssd, p_gla, p_ffn_conv = _layer(
        x_prompt, p_mem_k, p_mem_v, zeros_ssd_conv, None, None, zeros_ffn_conv, w, p, True)

    ns = x_sample.shape[0]
    y_sample, s_ssd_conv, s_ssd, s_gla, s_ffn_conv = _layer(
        x_sample, cache_mem_k, cache_mem_v, state_ssd_conv, state_ssd, state_gla, state_ffn_conv, w, p, False)

    head_shape = (n_mem, CROSS_HEADS, CROSS_HEAD_DIM)
    return (y_prompt, y_sample, p_ssd_conv, p_ssd, p_gla, p_ffn_conv,
            p_mem_k.reshape((nb,) + head_shape), p_mem_v.reshape((nb,) + head_shape),
            s_ssd_conv, s_ssd, s_gla, s_ffn_conv)
```

```python
import functools

import jax
import jax.numpy as jnp
from jax import lax
from jax.experimental import pallas as pl
from jax.experimental.pallas import tpu as pltpu

F32 = jnp.float32
BF16 = jnp.bfloat16
EPS = 1e-6
NEG_BIG = -1e30

D_MODEL = 2048
SSD_HEAD_DIM = 64
SSD_HEADS = 32
D_STATE = 128
N_GROUPS = 4
GROUP_W = D_MODEL // N_GROUPS
BC_W = N_GROUPS * D_STATE
CONV_DIM = D_MODEL + 2 * BC_W
SSD_CONV = 4
SSD_CHUNK = 128
GLA_HEADS = 4
GLA_KEY_DIM = 1024
GLA_HEAD_K = 256
GLA_HEAD_V = 512
GATE_RANK = 16
GATE_TAU = 16.0
GLA_CHUNK = 64
CROSS_HEADS = 4
CROSS_HEAD_DIM = 512
FFN_CONV = 3
LANE = 128

COL_Z, COL_XBC, COL_DT, N_SSD_PROJ = 0, 2048, 5120, 5632
COL_Q, COL_K, COL_V, COL_R, COL_GLR, N_GLA_PROJ = 0, 1024, 2048, 4096, 6144, 6656
COL_GA, COL_GB = 0, 2048

VMEM_LIMIT = 56 * 1024 * 1024


def _cp(*sem):
    return pltpu.CompilerParams(dimension_semantics=sem, vmem_limit_bytes=VMEM_LIMIT)


def _dot(a, b, prec=None):
    return jnp.dot(a, b, preferred_element_type=F32, precision=prec)


def _dot_nt(a, b):
    return lax.dot_general(a, b, (((1,), (1,)), ((), ())), preferred_element_type=F32)


def _split3(x):
    x1 = x.astype(BF16)
    r1 = x - x1.astype(F32)
    x2 = r1.astype(BF16)
    x3 = (r1 - x2.astype(F32)).astype(BF16)
    return x1, x2, x3


def _dot_sel(x, sel, sel_first=False):
    parts = _split3(x)
    if sel_first:
        return _dot(sel, parts[0]) + _dot(sel, parts[1]) + _dot(sel, parts[2])
    return _dot(parts[0], sel) + _dot(parts[1], sel) + _dot(parts[2], sel)


def _sigmoid(x):
    return 1.0 / (1.0 + jnp.exp(-x))


def _silu(x):
    return x * _sigmoid(x)


def _softplus(x):
    return jnp.maximum(x, 0.0) + jnp.log(1.0 + jnp.exp(-jnp.abs(x)))


def _rms(x, g):
    ms = jnp.mean(x * x, axis=-1, keepdims=True)
    return x * lax.rsqrt(ms + EPS) * g


def _rmsnorm_kernel(x_ref, g_ref, o_ref):
    o_ref[...] = _rms(x_ref[...], g_ref[...]).astype(o_ref.dtype)


def rmsnorm_cast(x2, g):
    m, d = x2.shape
    tm = min(m, 512)
    return pl.pallas_call(
        _rmsnorm_kernel,
        grid=(m // tm,),
        in_specs=[pl.BlockSpec((tm, d), lambda i: (i, 0)), pl.BlockSpec((1, d), lambda i: (0, 0))],
        out_specs=pl.BlockSpec((tm, d), lambda i: (i, 0)),
        out_shape=jax.ShapeDtypeStruct((m, d), BF16),
        compiler_params=_cp("parallel"),
        name="rmsnorm_cast",
    )(x2, g.reshape(1, d))


def _mm_kernel(a_ref, w_ref, o_ref, *, w_rows_are_outputs):
    dot = _dot_nt if w_rows_are_outputs else _dot
    o_ref[...] = dot(a_ref[...], w_ref[...]).astype(o_ref.dtype)


def matmul(a, w, out_dtype=F32, tm=1024, tn=512, w_rows_are_outputs=False, name="matmul"):
    m, k = a.shape
    n = w.shape[0] if w_rows_are_outputs else w.shape[1]
    tm = min(m, tm)
    if w_rows_are_outputs:
        w_spec = pl.BlockSpec((tn, k), lambda i, j: (j, 0))
    else:
        w_spec = pl.BlockSpec((k, tn), lambda i, j: (0, j))
    return pl.pallas_call(
        functools.partial(_mm_kernel, w_rows_are_outputs=w_rows_are_outputs),
        grid=(m // tm, n // tn),
        in_specs=[pl.BlockSpec((tm, k), lambda i, j: (i, 0)), w_spec],
        out_specs=pl.BlockSpec((tm, tn), lambda i, j: (i, j)),
        out_shape=jax.ShapeDtypeStruct((m, n), out_dtype),
        compiler_params=_cp("parallel", "arbitrary"),
        name=name,
    )(a, w)


def _mm_res_norm_kernel(a_ref, w_ref, res_ref, g_ref, *out_refs):
    h = res_ref[...] + _dot(a_ref[...], w_ref[...])
    if len(out_refs) == 2:
        out_refs[0][...] = h
    out_refs[-1][...] = _rms(h, g_ref[...]).astype(out_refs[-1].dtype)


def mm_res_norm(a, w, res, g, emit_h, norm_dtype, name):
    m, kdim = a.shape
    n = w.shape[1]
    tm = min(m, 512)
    row_tile = lambda width: pl.BlockSpec((tm, width), lambda i: (i, 0))
    out_shape = [jax.ShapeDtypeStruct((m, n), norm_dtype)]
    out_specs = [row_tile(n)]
    if emit_h:
        out_shape = [jax.ShapeDtypeStruct((m, n), F32)] + out_shape
        out_specs = [row_tile(n)] + out_specs
    outs = pl.pallas_call(
        _mm_res_norm_kernel,
        grid=(m // tm,),
        in_specs=[row_tile(kdim), pl.BlockSpec((kdim, n), lambda i: (0, 0), pipeline_mode=pl.Buffered(1)),
                  row_tile(n), pl.BlockSpec((1, n), lambda i: (0, 0))],
        out_specs=out_specs,
        out_shape=out_shape,
        compiler_params=_cp("parallel"),
        name=name,
    )(a, w, res, g.reshape(1, n))
    return outs if emit_h else outs[0]


def _merge_kernel(u_ref, o_ref, wa_ref, wb_ref, ga_ref, gb_ref, out_ref):
    a = _dot(u_ref[...], wa_ref[...])
    b = _dot(o_ref[...], wb_ref[...])
    out_ref[...] = (_sigmoid(ga_ref[...]) * a + _sigmoid(gb_ref[...]) * b).astype(out_ref.dtype)


def merge_branches(u, o, wa, wb, proj):
    m, d = u.shape
    tm, tn = min(m, 1024), 512
    return pl.pallas_call(
        _merge_kernel,
        grid=(m // tm, d // tn),
        in_specs=[
            pl.BlockSpec((tm, d), lambda i, j: (i, 0)),
            pl.BlockSpec((tm, d), lambda i, j: (i, 0)),
            pl.BlockSpec((d, tn), lambda i, j: (0, j)),
            pl.BlockSpec((d, tn), lambda i, j: (0, j)),
            pl.BlockSpec((tm, tn), lambda i, j: (i, COL_GA // tn + j)),
            pl.BlockSpec((tm, tn), lambda i, j: (i, COL_GB // tn + j)),
        ],
        out_specs=pl.BlockSpec((tm, tn), lambda i, j: (i, j)),
        out_shape=jax.ShapeDtypeStruct((m, d), BF16),
        compiler_params=_cp("parallel", "arbitrary"),
        name="merge_branches",
    )(u, o, wa, wb, proj, proj)


CONV_PAD = 8


def _short_conv_kernel(*refs, taps, seq, nstreams, swiglu):
    ins = refs[: 4 * nstreams]
    out_ref = refs[4 * nstreams]
    ns_refs = refs[4 * nstreams + 1 :]
    vals = []
    for s in range(nstreams):
        u_ref, st_ref, w_ref, b_ref = ins[4 * s : 4 * s + 4]
        nseq, _, tc = st_ref.shape
        full = jnp.concatenate([st_ref[...], u_ref[...].reshape(nseq, seq, tc)], axis=1)
        acc = b_ref[...]
        for k in range(taps):
            acc = acc + full[:, k : k + seq] * w_ref[k : k + 1, :]
        vals.append(acc)
        ns_refs[s][...] = full[:, seq : seq + taps - 1]
    out = _silu(vals[1]) * vals[0] if swiglu else _silu(vals[0])
    out_ref[...] = out.reshape(out_ref.shape).astype(out_ref.dtype)


def short_conv(u2, seq, col_offs, width, states, ws, bs, taps, swiglu, out_dtype, name):
    ntok = u2.shape[0]
    nseq = ntok // seq
    tc = 512
    nstreams = len(col_offs)
    in_specs, args = [], []
    for s in range(nstreams):
        cb = col_offs[s] // tc
        in_specs += [
            pl.BlockSpec((ntok, tc), lambda c, cb=cb: (0, cb + c)),
            pl.BlockSpec((nseq, taps - 1, tc), lambda c: (0, 0, c)),
            pl.BlockSpec((taps, tc), lambda c: (0, c)),
            pl.BlockSpec((1, tc), lambda c: (0, c)),
        ]
        args += [u2, states[s], ws[s], bs[s].reshape(1, width)]
    out_specs = [pl.BlockSpec((ntok, tc), lambda c: (0, c))]
    out_shape = [jax.ShapeDtypeStruct((ntok, width), out_dtype)]
    for s in range(nstreams):
        out_specs.append(pl.BlockSpec((nseq, taps - 1, tc), lambda c: (0, 0, c)))
        out_shape.append(jax.ShapeDtypeStruct((nseq, taps - 1, width), F32))
    return pl.pallas_call(
        functools.partial(_short_conv_kernel, taps=taps, seq=seq, nstreams=nstreams, swiglu=swiglu),
        grid=(width // tc,),
        in_specs=in_specs,
        out_specs=out_specs,
        out_shape=out_shape,
        compiler_params=_cp("parallel"),
        name=name,
    )(*args)


def _shift_rows(u, d, prev):
    x = pltpu.roll(u, d, 0)
    r = lax.broadcasted_iota(jnp.int32, prev.shape, 0)
    head = jnp.where(r < d, pltpu.roll(prev, d, 0), x[:CONV_PAD])
    return jnp.concatenate([head, x[CONV_PAD:]], axis=0)


def _causal_taps(u, prev, cw, bias):
    taps = cw.shape[0]
    acc = bias
    for k in range(taps):
        d = taps - 1 - k
        acc = acc + (_shift_rows(u, d, prev) if d else u) * cw[k : k + 1, :]
    return acc


def _ffn_up_kernel(hn_ref, wa_ref, wg_ref, cwa_ref, cwg_ref, cba_ref, cbg_ref, sta_ref, stg_ref,
                   act_ref, nsa_ref, nsg_ref, preva, prevg, *, tm):
    m = pl.program_id(2)
    nm = pl.num_programs(2)
    lo = CONV_PAD - (FFN_CONV - 1)

    @pl.when(m == 0)
    def _():
        for prev, st_ref in ((preva, sta_ref), (prevg, stg_ref)):
            prev[0:lo, :] = jnp.zeros((lo, prev.shape[1]), F32)
            prev[lo:CONV_PAD, :] = st_ref[0]

    hn = hn_ref[0]
    vals = []
    for w_ref, cw_ref, cb_ref, prev in ((wa_ref, cwa_ref, cba_ref, preva), (wg_ref, cwg_ref, cbg_ref, prevg)):
        u = _dot(hn, w_ref[...])
        vals.append(_causal_taps(u, prev[...], cw_ref[...], cb_ref[...]))
        prev[...] = u[tm - CONV_PAD : tm]
    act_ref[0] = (_silu(vals[1]) * vals[0]).astype(act_ref.dtype)

    @pl.when(m == nm - 1)
    def _():
        nsa_ref[0] = preva[lo:CONV_PAD, :]
        nsg_ref[0] = prevg[lo:CONV_PAD, :]


def ffn_up_conv_act(hn3, w_up, conv_w, conv_b, conv_state):
    nb, seq, d = hn3.shape
    ffn = w_up.shape[1] // 2
    tm, tn = min(seq, 1024), 512
    nn = ffn // tn
    half = lambda off: (lambda b, n, m: (0, off + n))
    st_spec = lambda off: pl.BlockSpec((1, FFN_CONV - 1, tn), lambda b, n, m: (b, 0, off + n))
    cb2 = conv_b.reshape(1, 2 * ffn)
    act, nsa, nsg = pl.pallas_call(
        functools.partial(_ffn_up_kernel, tm=tm),
        grid=(nb, nn, seq // tm),
        in_specs=[
            pl.BlockSpec((1, tm, d), lambda b, n, m: (b, m, 0)),
            pl.BlockSpec((d, tn), half(0)), pl.BlockSpec((d, tn), half(nn)),
            pl.BlockSpec((FFN_CONV, tn), half(0)), pl.BlockSpec((FFN_CONV, tn), half(nn)),
            pl.BlockSpec((1, tn), half(0)), pl.BlockSpec((1, tn), half(nn)),
            st_spec(0), st_spec(nn),
        ],
        out_specs=[
            pl.BlockSpec((1, tm, tn), lambda b, n, m: (b, m, n)),
            pl.BlockSpec((1, FFN_CONV - 1, tn), lambda b, n, m: (b, 0, n)),
            pl.BlockSpec((1, FFN_CONV - 1, tn), lambda b, n, m: (b, 0, n)),
        ],
        out_shape=[
            jax.ShapeDtypeStruct((nb, seq, ffn), BF16),
            jax.ShapeDtypeStruct((nb, FFN_CONV - 1, ffn), F32),
            jax.ShapeDtypeStruct((nb, FFN_CONV - 1, ffn), F32),
        ],
        scratch_shapes=[pltpu.VMEM((CONV_PAD, tn), F32), pltpu.VMEM((CONV_PAD, tn), F32)],
        compiler_params=_cp("parallel", "parallel", "arbitrary"),
        name="ffn_up_conv_act",
    )(hn3, w_up, w_up, conv_w, conv_w, cb2, cb2, conv_state, conv_state)
    return act, jnp.concatenate([nsa, nsg], axis=-1)


def _ssd_gate_norm(y, z, nrm):
    ug = y * _silu(z)
    outs = []
    for g in range(N_GROUPS):
        ugg = ug[:, g * GROUP_W : (g + 1) * GROUP_W]
        ms = jnp.mean(ugg * ugg, axis=-1, keepdims=True)
        outs.append(ugg * lax.rsqrt(ms + EPS))
    return jnp.concatenate(outs, axis=1) * nrm


def _gla_out_norm(o, r, nrm):
    outs = []
    for h in range(GLA_HEADS):
        oh = o[:, h * GLA_HEAD_V : (h + 1) * GLA_HEAD_V]
        rh = r[:, h * GLA_HEAD_V : (h + 1) * GLA_HEAD_V]
        outs.append(_rms(oh, nrm) * _silu(rh))
    return jnp.concatenate(outs, axis=1)


SCAN_ROWS_PER_STEP = 256


def _ssd_chunk(xs_ref, b_ref, c_ref, dt_ref, z_ref, cw_ref, cbias_ref, dtb_ref, alog_ref, dexp_ref, nrm_ref, e_ref,
               u_ref, st_ref, cprev):
    q = SSD_CHUNK
    conv = []
    for raw_ref, c0 in ((xs_ref, 0), (b_ref, D_MODEL), (c_ref, D_MODEL + BC_W)):
        raw = raw_ref[...]
        cols = slice(c0, c0 + raw.shape[1])
        conv.append(_silu(_causal_taps(raw, cprev[:, cols], cw_ref[:, cols], cbias_ref[:, cols])))
        cprev[:, cols] = raw[q - CONV_PAD : q]
    xs, bm, cm = conv
    dt = _softplus(dt_ref[...] + dtb_ref[...])
    a = dt * (-jnp.exp(alog_ref[...]))
    row = lax.broadcasted_iota(jnp.int32, (q, q), 0)
    col = lax.broadcasted_iota(jnp.int32, (q, q), 1)
    tril = row >= col
    acum = _dot_sel(a, tril.astype(BF16), sel_first=True)
    acum_t = acum.T
    dt_t = dt.T
    last = acum[q - 1 : q, :]
    e_mat = e_ref[...]
    eexp = _dot_sel(jnp.exp(acum), e_mat)
    wexp = _dot_sel(jnp.exp(last - acum) * dt, e_mat)
    s_bf = st_ref[...].astype(BF16)
    cb16 = cm.astype(BF16)
    bb16 = bm.astype(BF16)
    x16 = xs.astype(BF16)
    xw16 = (xs * wexp).astype(BF16)
    lane_lo = lax.broadcasted_iota(jnp.int32, (q, LANE), 1) < SSD_HEAD_DIM
    ys = []
    for g in range(N_GROUPS):
        cg = cb16[:, g * D_STATE : (g + 1) * D_STATE]
        bg = bb16[:, g * D_STATE : (g + 1) * D_STATE]
        cb = _dot_nt(cg, bg)
        yoff = _dot(cg, s_bf[:, g * GROUP_W : (g + 1) * GROUP_W])
        pieces = []
        for pr in range(GROUP_W // LANE):
            h0 = g * (SSD_HEADS // N_GROUPS) + 2 * pr
            xp = x16[:, h0 * SSD_HEAD_DIM : h0 * SSD_HEAD_DIM + LANE]
            yh = []
            for h in (h0, h0 + 1):
                diff = acum[:, h : h + 1] - acum_t[h : h + 1, :]
                dec = jnp.exp(jnp.where(tril, diff, NEG_BIG))
                m = (cb * dec * dt_t[h : h + 1, :]).astype(BF16)
                yh.append(_dot(m, xp))
            pieces.append(jnp.where(lane_lo, yh[0], yh[1]))
        sl = slice(g * GROUP_W, (g + 1) * GROUP_W)
        ys.append(jnp.concatenate(pieces, axis=1) + yoff * eexp[:, sl])
        bg_t = bm[:, g * D_STATE : (g + 1) * D_STATE].T.astype(BF16)
        upd = _dot(bg_t, xw16[:, sl])
        st_ref[:, sl] = eexp[q - 1 : q, sl] * st_ref[:, sl] + upd
    y = jnp.concatenate(ys, axis=1) + dexp_ref[...] * xs
    u_ref[...] = _ssd_gate_norm(y, z_ref[...], nrm_ref[...]).astype(u_ref.dtype)


def _ssd_scan_kernel(xs_ref, b_ref, c_ref, dt_ref, z_ref, cst_ref, cw_ref, cbias_ref,
                     dtb_ref, alog_ref, dexp_ref, nrm_ref, e_ref,
                     u_ref, sout_ref, cso_ref, st_ref, cprev, *, nsteps, nsub):
    step = pl.program_id(1)
    lo = CONV_PAD - (SSD_CONV - 1)

    @pl.when(step == 0)
    def _():
        st_ref[...] = jnp.zeros_like(st_ref)
        cprev[0:lo, :] = jnp.zeros((lo, CONV_DIM), F32)
        cprev[lo:CONV_PAD, :] = cst_ref[0]

    for sub in range(nsub):
        rows = lambda ref: ref.at[0, pl.ds(sub * SSD_CHUNK, SSD_CHUNK)]
        _ssd_chunk(rows(xs_ref), rows(b_ref), rows(c_ref), rows(dt_ref), rows(z_ref), cw_ref, cbias_ref,
                   dtb_ref, alog_ref, dexp_ref, nrm_ref, e_ref, rows(u_ref), st_ref, cprev)

    @pl.when(step == nsteps - 1)
    def _():
        sout_ref[0] = st_ref[...].T
        cso_ref[0] = cprev[lo:CONV_PAD, :]


def ssd_scan(proj3, conv_state, conv_w, conv_b, p):
    nb, seq, _ = proj3.shape
    nsub = SCAN_ROWS_PER_STEP // SSD_CHUNK if seq % SCAN_ROWS_PER_STEP == 0 else 1
    q = nsub * SSD_CHUNK
    nsteps = seq // q
    vec = lambda n: pl.BlockSpec((1, n), lambda b, c: (0, 0))
    u, s_out, conv_new = pl.pallas_call(
        functools.partial(_ssd_scan_kernel, nsteps=nsteps, nsub=nsub),
        grid=(nb, nsteps),
        in_specs=[
            pl.BlockSpec((1, q, D_MODEL), lambda b, c: (b, c, COL_XBC // D_MODEL)),
            pl.BlockSpec((1, q, BC_W), lambda b, c: (b, c, (COL_XBC + D_MODEL) // BC_W)),
            pl.BlockSpec((1, q, BC_W), lambda b, c: (b, c, (COL_XBC + D_MODEL) // BC_W + 1)),
            pl.BlockSpec((1, q, LANE), lambda b, c: (b, c, COL_DT // LANE)),
            pl.BlockSpec((1, q, D_MODEL), lambda b, c: (b, c, COL_Z // D_MODEL)),
            pl.BlockSpec((1, SSD_CONV - 1, CONV_DIM), lambda b, c: (b, 0, 0)),
            pl.BlockSpec((SSD_CONV, CONV_DIM), lambda b, c: (0, 0)),
            vec(CONV_DIM),
            vec(LANE), vec(LANE), vec(D_MODEL), vec(D_MODEL),
            pl.BlockSpec((LANE, D_MODEL), lambda b, c: (0, 0)),
        ],
        out_specs=[
            pl.BlockSpec((1, q, D_MODEL), lambda b, c: (b, c, 0)),
            pl.BlockSpec((1, D_MODEL, D_STATE), lambda b, c: (b, 0, 0)),
            pl.BlockSpec((1, SSD_CONV - 1, CONV_DIM), lambda b, c: (b, 0, 0)),
        ],
        out_shape=[
            jax.ShapeDtypeStruct((nb, seq, D_MODEL), BF16),
            jax.ShapeDtypeStruct((nb, D_MODEL, D_STATE), F32),
            jax.ShapeDtypeStruct((nb, SSD_CONV - 1, CONV_DIM), F32),
        ],
        scratch_shapes=[pltpu.VMEM((D_STATE, D_MODEL), F32), pltpu.VMEM((CONV_PAD, CONV_DIM), F32)],
        compiler_params=_cp("parallel", "arbitrary"),
        name="ssd_scan",
    )(proj3, proj3, proj3, proj3, proj3, conv_state, conv_w, conv_b.reshape(1, CONV_DIM),
      p["dt_bias"], p["a_log"], p["d_exp"], p["ssd_norm"], p["e_head"])
    return u.reshape(nb * seq, D_MODEL), s_out.reshape(nb, SSD_HEADS, SSD_HEAD_DIM, D_STATE), conv_new


def _gla_gate_log(glr, wg, bg):
    x = _dot(glr.astype(BF16), wg) + bg
    return -_softplus(-x) / GATE_TAU


def _gla_chunk(q_ref, k_ref, v_ref, r_ref, glr_ref, wg_ref, bg_ref, nrm_ref, o_ref, st_ref):
    q = GLA_CHUNK
    glog = _gla_gate_log(glr_ref[...], wg_ref[...], bg_ref[...])
    row = lax.broadcasted_iota(jnp.int32, (q, q), 0)
    col = lax.broadcasted_iota(jnp.int32, (q, q), 1)
    tril = row >= col
    bc = _dot_sel(glog, tril.astype(BF16), sel_first=True)
    last = bc[q - 1 : q, :]
    kk = k_ref[...]
    qe = q_ref[...] * (GLA_HEAD_K ** -0.5) * jnp.exp(bc)
    ke = kk * jnp.exp(-bc)
    kd = kk * jnp.exp(last - bc)
    elast = jnp.exp(last)
    v16 = v_ref[...].astype(BF16)
    zeros_v = jnp.zeros((q, GLA_HEAD_V), BF16)
    outs = []
    for h in range(GLA_HEADS):
        ks = slice(h * GLA_HEAD_K, (h + 1) * GLA_HEAD_K)
        vs = slice(h * GLA_HEAD_V, (h + 1) * GLA_HEAD_V)
        qh = qe[:, ks].astype(BF16)
        kh = ke[:, ks].astype(BF16)
        att = jnp.where(tril, _dot_nt(qh, kh), 0.0)
        s_h = st_ref[ks, :]
        outs.append(_dot(att.astype(BF16), v16[:, vs]) + _dot(qh, s_h.astype(BF16)))
        xt = jnp.concatenate([kd[:, ks], jnp.broadcast_to(elast[:, ks], (q, GLA_HEAD_K))], axis=0).T
        v2 = jnp.concatenate([v16[:, vs], zeros_v], axis=0)
        st_ref[ks, :] = xt[:, q : q + 1] * s_h + _dot(xt.astype(BF16), v2)
    o = jnp.concatenate(outs, axis=1)
    o_ref[...] = _gla_out_norm(o, r_ref[...], nrm_ref[...]).astype(o_ref.dtype)


def _gla_scan_kernel(q_ref, k_ref, v_ref, r_ref, glr_ref, wg_ref, bg_ref, nrm_ref,
                     o_ref, sout_ref, st_ref, *, nsteps, nsub):
    step = pl.program_id(1)

    @pl.when(step == 0)
    def _():
        st_ref[...] = jnp.zeros_like(st_ref)

    for sub in range(nsub):
        rows = lambda ref: ref.at[0, pl.ds(sub * GLA_CHUNK, GLA_CHUNK)]
        _gla_chunk(rows(q_ref), rows(k_ref), rows(v_ref), rows(r_ref), rows(glr_ref), wg_ref, bg_ref, nrm_ref,
                   rows(o_ref), st_ref)

    @pl.when(step == nsteps - 1)
    def _():
        sout_ref[0] = st_ref[...]


def gla_scan(proj3, p):
    nb, seq, _ = proj3.shape
    nsub = SCAN_ROWS_PER_STEP // GLA_CHUNK if seq % SCAN_ROWS_PER_STEP == 0 else 1
    q = nsub * GLA_CHUNK
    nsteps = seq // q
    o, s_out = pl.pallas_call(
        functools.partial(_gla_scan_kernel, nsteps=nsteps, nsub=nsub),
        grid=(nb, nsteps),
        in_specs=[
            pl.BlockSpec((1, q, GLA_KEY_DIM), lambda b, c: (b, c, COL_Q // GLA_KEY_DIM)),
            pl.BlockSpec((1, q, GLA_KEY_DIM), lambda b, c: (b, c, COL_K // GLA_KEY_DIM)),
            pl.BlockSpec((1, q, D_MODEL), lambda b, c: (b, c, COL_V // D_MODEL)),
            pl.BlockSpec((1, q, D_MODEL), lambda b, c: (b, c, COL_R // D_MODEL)),
            pl.BlockSpec((1, q, LANE), lambda b, c: (b, c, COL_GLR // LANE)),
            pl.BlockSpec((LANE, GLA_KEY_DIM), lambda b, c: (0, 0)),
            pl.BlockSpec((1, GLA_KEY_DIM), lambda b, c: (0, 0)),
            pl.BlockSpec((1, GLA_HEAD_V), lambda b, c: (0, 0)),
        ],
        out_specs=[
            pl.BlockSpec((1, q, D_MODEL), lambda b, c: (b, c, 0)),
            pl.BlockSpec((1, GLA_KEY_DIM, GLA_HEAD_V), lambda b, c: (b, 0, 0)),
        ],
        out_shape=[
            jax.ShapeDtypeStruct((nb, seq, D_MODEL), BF16),
            jax.ShapeDtypeStruct((nb, GLA_KEY_DIM, GLA_HEAD_V), F32),
        ],
        scratch_shapes=[pltpu.VMEM((GLA_KEY_DIM, GLA_HEAD_V), F32)],
        compiler_params=_cp("parallel", "arbitrary"),
        name="gla_scan",
    )(proj3, proj3, proj3, proj3, proj3, p["w_gate"], p["b_gate"], p["gla_norm"])
    return o.reshape(nb * seq, D_MODEL), s_out.reshape(nb, GLA_HEADS, GLA_HEAD_K, GLA_HEAD_V)


TOK_BLOCK = 128


def _row_shift(x, d, tpos):
    return jnp.where(tpos >= d, pltpu.roll(x, d, 0), 0.0)


def _seq_cumsum_and_last(a, seq, tpos):
    nrows = a.shape[0]
    acum = a
    for d in range(1, seq):
        acum = acum + _row_shift(a, d, tpos)
    last = jnp.where(tpos == seq - 1, acum, 0.0)
    for d in range(1, seq):
        last = last + jnp.where(tpos == seq - 1 - d, pltpu.roll(acum, nrows - d, 0), 0.0)
    return acum, last


def _ssd_step_pre_kernel(xs_ref, b_ref, c_ref, dt_ref, dtb_ref, alog_ref, dexp_ref, e_ref, gh_ref,
                         ypart_ref, eexp_ref, xwt_ref, el_ref, *, seq):
    nrows = xs_ref.shape[0]
    xs = xs_ref[...]
    bm = b_ref[...]
    cm = c_ref[...]
    dt = _softplus(dt_ref[...] + dtb_ref[...])
    a = dt * (-jnp.exp(alog_ref[...]))
    pos = lambda w: lax.broadcasted_iota(jnp.int32, (nrows, w), 0) % seq
    t_h, t_c, t_x = pos(LANE), pos(BC_W), pos(D_MODEL)
    acum, last = _seq_cumsum_and_last(a, seq, t_h)
    e_mat = e_ref[...]
    eexp_ref[...] = _dot_sel(jnp.exp(acum), e_mat)
    wexp = _dot_sel(jnp.exp(last - acum) * dt, e_mat)
    xwt_ref[...] = (xs * wexp).T.astype(xwt_ref.dtype)
    el_ref[...] = jnp.exp(last)
    y = dexp_ref[...] * xs
    for d in range(seq):
        if d == 0:
            cbh = _dot_sel(cm * bm, gh_ref[...])
            coef = dt
            xd = xs
        else:
            cbh = _dot_sel(cm * _row_shift(bm, d, t_c), gh_ref[...])
            coef = jnp.where(t_h >= d, jnp.exp(acum - pltpu.roll(acum, d, 0)) * pltpu.roll(dt, d, 0), 0.0)
            xd = _row_shift(xs, d, t_x)
        y = y + _dot_sel(cbh * coef, e_mat) * xd
    ypart_ref[...] = y


def _ssd_step_state_kernel(st_ref, c_ref, b_ref, xwt_ref, el_ref, ypart_ref, eexp_ref, z_ref, nrm_ref,
                           u_ref, so_ref, *, sb, seq):
    i = pl.program_id(0)
    rows = sb * seq
    steps_per_block = TOK_BLOCK // rows
    base = (i % steps_per_block) * rows
    c16 = c_ref[...].astype(BF16)
    btok = b_ref[...]
    tok = lax.broadcasted_iota(jnp.int32, (TOK_BLOCK, LANE), 0)
    rsel = lax.broadcasted_iota(jnp.int32, (rows, GROUP_W), 0)
    heads_per_group = SSD_HEADS // N_GROUPS
    yoff = [jnp.zeros((rows, GROUP_W), F32) for _ in range(N_GROUPS)]
    for s in range(sb):
        lo = base + seq * s
        own = (tok >= lo) & (tok < lo + seq)
        mine = (rsel >= seq * s) & (rsel < seq * (s + 1))
        for g in range(N_GROUPS):
            sl = slice(g * GROUP_W, (g + 1) * GROUP_W)
            yo = _dot_nt(c16[:, g * D_STATE : (g + 1) * D_STATE], st_ref[s, sl, :].astype(BF16))
            yoff[g] = jnp.where(mine, yo, yoff[g])
            bsel = jnp.where(own, btok[:, g * D_STATE : (g + 1) * D_STATE], 0.0).astype(BF16)
            upd = _dot(xwt_ref[sl, :], bsel)
            for r in range(heads_per_group):
                h = g * heads_per_group + r
                hs = slice(h * SSD_HEAD_DIM, (h + 1) * SSD_HEAD_DIM)
                so_ref[s, hs, :] = el_ref[seq * s, h] * st_ref[s, hs, :] + upd[r * SSD_HEAD_DIM : (r + 1) * SSD_HEAD_DIM]
    y = ypart_ref[...] + jnp.concatenate(yoff, axis=1) * eexp_ref[...]
    u_ref[...] = _ssd_gate_norm(y, z_ref[...], nrm_ref[...]).astype(u_ref.dtype)


def ssd_step(xc2, proj2, state, p, seq):
    ntok = xc2.shape[0]
    nseq = ntok // seq
    full = lambda shape: pl.BlockSpec(shape, lambda i: (0,) * len(shape))
    ypart, eexp, xwt, elast = pl.pallas_call(
        functools.partial(_ssd_step_pre_kernel, seq=seq),
        grid=(1,),
        in_specs=[
            pl.BlockSpec((ntok, D_MODEL), lambda i: (0, 0)),
            pl.BlockSpec((ntok, BC_W), lambda i: (0, D_MODEL // BC_W)),
            pl.BlockSpec((ntok, BC_W), lambda i: (0, D_MODEL // BC_W + 1)),
            pl.BlockSpec((ntok, LANE), lambda i: (0, COL_DT // LANE)),
            full((1, LANE)), full((1, LANE)), full((1, D_MODEL)),
            full((LANE, D_MODEL)), full((BC_W, LANE)),
        ],
        out_specs=[full((ntok, D_MODEL)), full((ntok, D_MODEL)), full((D_MODEL, ntok)), full((ntok, LANE))],
        out_shape=[
            jax.ShapeDtypeStruct((ntok, D_MODEL), F32),
            jax.ShapeDtypeStruct((ntok, D_MODEL), F32),
            jax.ShapeDtypeStruct((D_MODEL, ntok), BF16),
            jax.ShapeDtypeStruct((ntok, LANE), F32),
        ],
        compiler_params=_cp("arbitrary"),
        name="ssd_step_pre",
    )(xc2, xc2, xc2, proj2, p["dt_bias"], p["a_log"], p["d_exp"], p["e_head"], p["g_head"])

    sb = 4
    rows = sb * seq
    spb = TOK_BLOCK // rows
    st3 = state.reshape(nseq, D_MODEL, D_STATE)
    u, s_new = pl.pallas_call(
        functools.partial(_ssd_step_state_kernel, sb=sb, seq=seq),
        grid=(nseq // sb,),
        in_specs=[
            pl.BlockSpec((sb, D_MODEL, D_STATE), lambda i: (i, 0, 0)),
            pl.BlockSpec((rows, BC_W), lambda i: (i, D_MODEL // BC_W + 1)),
            pl.BlockSpec((TOK_BLOCK, BC_W), lambda i: (i // spb, D_MODEL // BC_W)),
            pl.BlockSpec((D_MODEL, TOK_BLOCK), lambda i: (0, i // spb)),
            pl.BlockSpec((rows, LANE), lambda i: (i, 0), memory_space=pltpu.SMEM),
            pl.BlockSpec((rows, D_MODEL), lambda i: (i, 0)),
            pl.BlockSpec((rows, D_MODEL), lambda i: (i, 0)),
            pl.BlockSpec((rows, D_MODEL), lambda i: (i, COL_Z // D_MODEL)),
            pl.BlockSpec((1, D_MODEL), lambda i: (0, 0)),
        ],
        out_specs=[
            pl.BlockSpec((rows, D_MODEL), lambda i: (i, 0)),
            pl.BlockSpec((sb, D_MODEL, D_STATE), lambda i: (i, 0, 0)),
        ],
        out_shape=[
            jax.ShapeDtypeStruct((ntok, D_MODEL), BF16),
            jax.ShapeDtypeStruct((nseq, D_MODEL, D_STATE), F32),
        ],
        compiler_params=_cp("parallel"),
        name="ssd_step_state",
    )(st3, xc2, xc2, xwt, elast, ypart, eexp, proj2, p["ssd_norm"])
    return u, s_new.reshape(nseq, SSD_HEADS, SSD_HEAD_DIM, D_STATE)


def _gla_step_pre_kernel(q_ref, k_ref, v_ref, glr_ref, wg_ref, bg_ref, gv_ref,
                         oin_ref, qe_ref, kdt_ref, elt_ref, *, seq):
    nrows = q_ref.shape[0]
    glog = _gla_gate_log(glr_ref[...], wg_ref[...], bg_ref[...])
    pos = lambda w: lax.broadcasted_iota(jnp.int32, (nrows, w), 0) % seq
    t_k, t_v = pos(GLA_KEY_DIM), pos(D_MODEL)
    bc, last = _seq_cumsum_and_last(glog, seq, t_k)
    kk = k_ref[...]
    qe = q_ref[...] * (GLA_HEAD_K ** -0.5) * jnp.exp(bc)
    ke = kk * jnp.exp(-bc)
    qe_ref[...] = qe
    kdt_ref[...] = (kk * jnp.exp(last - bc)).T
    elt_ref[...] = jnp.exp(last).T
    v = v_ref[...]
    gv = gv_ref[...]
    o = jnp.zeros((nrows, D_MODEL), F32)
    for d in range(seq):
        ked = ke if d == 0 else _row_shift(ke, d, t_k)
        vd = v if d == 0 else _row_shift(v, d, t_v)
        att = _dot((qe * ked).astype(BF16), gv)
        o = o + att * vd
    oin_ref[...] = o


def _gla_step_state_kernel(st_ref, qe_ref, v_ref, kdt_ref, elt_ref, oin_ref, r_ref, nrm_ref,
                           o_ref, so_ref, *, sb, seq):
    i = pl.program_id(0)
    rows = sb * seq
    spb = TOK_BLOCK // rows
    base = (i % spb) * rows
    qe16 = qe_ref[...].astype(BF16)
    vtok = v_ref[...]
    tokv = lax.broadcasted_iota(jnp.int32, (TOK_BLOCK, GLA_HEAD_V), 0)
    tok = lax.broadcasted_iota(jnp.int32, (TOK_BLOCK, LANE), 0)
    rsel = lax.broadcasted_iota(jnp.int32, (rows, GLA_HEAD_V), 0)
    ooff = [jnp.zeros((rows, GLA_HEAD_V), F32) for _ in range(GLA_HEADS)]
    for s in range(sb):
        lo = base + seq * s
        own = (tokv >= lo) & (tokv < lo + seq)
        first = jnp.where(tok == lo, 1.0, 0.0).astype(BF16)
        mine = (rsel >= seq * s) & (rsel < seq * (s + 1))
        for h in range(GLA_HEADS):
            ks = slice(h * GLA_HEAD_K, (h + 1) * GLA_HEAD_K)
            vs = slice(h * GLA_HEAD_V, (h + 1) * GLA_HEAD_V)
            s_h = st_ref[s, ks, :]
            oo = _dot(qe16[:, ks], s_h.astype(BF16))
            ooff[h] = jnp.where(mine, oo, ooff[h])
            vsel = jnp.where(own, vtok[:, vs], 0.0).astype(BF16)
            upd = _dot(kdt_ref[ks, :].astype(BF16), vsel)
            ecol = _dot_sel(elt_ref[ks, :], first)
            so_ref[s, ks, :] = jnp.concatenate([ecol] * (GLA_HEAD_V // LANE), axis=1) * s_h + upd
    o = oin_ref[...] + jnp.concatenate(ooff, axis=1)
    o_ref[0] = _gla_out_norm(o, r_ref[...], nrm_ref[...]).astype(o_ref.dtype)


def gla_step(proj2, state, p, seq):
    ntok = proj2.shape[0]
    nseq = ntok // seq
    full = lambda shape: pl.BlockSpec(shape, lambda i: (0,) * len(shape))
    oin, qe, kdt, elt = pl.pallas_call(
        functools.partial(_gla_step_pre_kernel, seq=seq),
        grid=(1,),
        in_specs=[
            pl.BlockSpec((ntok, GLA_KEY_DIM), lambda i: (0, COL_Q // GLA_KEY_DIM)),
            pl.BlockSpec((ntok, GLA_KEY_DIM), lambda i: (0, COL_K // GLA_KEY_DIM)),
            pl.BlockSpec((ntok, D_MODEL), lambda i: (0, COL_V // D_MODEL)),
            pl.BlockSpec((ntok, LANE), lambda i: (0, COL_GLR // LANE)),
            full((LANE, GLA_KEY_DIM)), full((1, GLA_KEY_DIM)), full((GLA_KEY_DIM, D_MODEL)),
        ],
        out_specs=[full((ntok, D_MODEL)), full((ntok, GLA_KEY_DIM)), full((GLA_KEY_DIM, ntok)),
                   full((GLA_KEY_DIM, ntok))],
        out_shape=[
            jax.ShapeDtypeStruct((ntok, D_MODEL), F32),
            jax.ShapeDtypeStruct((ntok, GLA_KEY_DIM), F32),
            jax.ShapeDtypeStruct((GLA_KEY_DIM, ntok), F32),
            jax.ShapeDtypeStruct((GLA_KEY_DIM, ntok), F32),
        ],
        compiler_params=_cp("arbitrary"),
        name="gla_step_pre",
    )(proj2, proj2, proj2, proj2, p["w_gate"], p["b_gate"], p["g_val"])

    sb = 4
    rows = sb * seq
    spb = TOK_BLOCK // rows
    st3 = state.reshape(nseq, GLA_KEY_DIM, GLA_HEAD_V)
    o, s_new = pl.pallas_call(
        functools.partial(_gla_step_state_kernel, sb=sb, seq=seq),
        grid=(nseq // sb,),
        in_specs=[
            pl.BlockSpec((sb, GLA_KEY_DIM, GLA_HEAD_V), lambda i: (i, 0, 0)),
            pl.BlockSpec((rows, GLA_KEY_DIM), lambda i: (i, 0)),
            pl.BlockSpec((TOK_BLOCK, D_MODEL), lambda i: (i // spb, COL_V // D_MODEL)),
            pl.BlockSpec((GLA_KEY_DIM, TOK_BLOCK), lambda i: (0, i // spb)),
            pl.BlockSpec((GLA_KEY_DIM, TOK_BLOCK), lambda i: (0, i // spb)),
            pl.BlockSpec((rows, D_MODEL), lambda i: (i, 0)),
            pl.BlockSpec((rows, D_MODEL), lambda i: (i, COL_R // D_MODEL)),
            pl.BlockSpec((1, GLA_HEAD_V), lambda i: (0, 0)),
        ],
        out_specs=[
            pl.BlockSpec((1, rows, D_MODEL), lambda i: (i, 0, 0)),
            pl.BlockSpec((sb, GLA_KEY_DIM, GLA_HEAD_V), lambda i: (i, 0, 0)),
        ],
        out_shape=[
            jax.ShapeDtypeStruct((nseq // sb, rows, D_MODEL), BF16),
            jax.ShapeDtypeStruct((nseq, GLA_KEY_DIM, GLA_HEAD_V), F32),
        ],
        compiler_params=_cp("parallel"),
        name="gla_step_state",
    )(st3, qe, proj2, kdt, elt, oin, proj2, p["gla_norm"])
    return o.reshape(ntok, D_MODEL), s_new.reshape(nseq, GLA_HEADS, GLA_HEAD_K, GLA_HEAD_V)


def _softmax_rows(sc):
    e = jnp.exp(sc - jnp.max(sc, axis=-1, keepdims=True))
    return e / jnp.sum(e, axis=-1, keepdims=True)


def _cross_block_kernel(hn_ref, h_ref, k_ref, v_ref, wq_ref, wo_ref, g_ref, h2_ref, hn2_ref):
    q16 = _dot(hn_ref[...], wq_ref[...]).astype(BF16)
    outs = []
    for h in range(CROSS_HEADS):
        hs = slice(h * CROSS_HEAD_DIM, (h + 1) * CROSS_HEAD_DIM)
        sc = _dot_nt(q16[:, hs], k_ref[0, :, hs].astype(BF16)) * (CROSS_HEAD_DIM ** -0.5)
        outs.append(_dot(_softmax_rows(sc).astype(BF16), v_ref[0, :, hs].astype(BF16)))
    att = jnp.concatenate(outs, axis=1).astype(BF16)
    h2 = h_ref[...] + _dot(att, wo_ref[...])
    h2_ref[...] = h2
    hn2_ref[...] = _rms(h2, g_ref[...]).astype(hn2_ref.dtype)


def cross_block(hn, h, mem_k, mem_v, w_cq, w_co, g, seq):
    ntok, d = hn.shape
    n_mem = mem_k.shape[1]
    tl = min(seq, 512)
    lt = seq // tl
    row_tile = pl.BlockSpec((tl, d), lambda i: (i, 0))
    resident = lambda shape: pl.BlockSpec(shape, lambda i: (0,) * len(shape), pipeline_mode=pl.Buffered(1))
    kv_spec = pl.BlockSpec((1, n_mem, d), lambda i: (i // lt, 0, 0))
    return pl.pallas_call(
        _cross_block_kernel,
        grid=(ntok // tl,),
        in_specs=[row_tile, row_tile, kv_spec, kv_spec, resident((d, d)), resident((d, d)),
                  pl.BlockSpec((1, d), lambda i: (0, 0))],
        out_specs=[row_tile, row_tile],
        out_shape=[jax.ShapeDtypeStruct((ntok, d), F32), jax.ShapeDtypeStruct((ntok, d), BF16)],
        compiler_params=_cp("parallel"),
        name="cross_block",
    )(hn, h, mem_k, mem_v, w_cq, w_co, g.reshape(1, d))


def _xattn_step_kernel(q_ref, k_ref, v_ref, o_ref, *, nseq, tl):
    rows = nseq * tl
    n_mem = k_ref.shape[1]
    q = q_ref[0]
    qs = jnp.concatenate([q[:, h * CROSS_HEAD_DIM : (h + 1) * CROSS_HEAD_DIM] for h in range(CROSS_HEADS)],
                         axis=0).astype(BF16)
    shape = (CROSS_HEADS * rows, n_mem * CROSS_HEADS)
    col_head = lax.broadcasted_iota(jnp.int32, shape, 1) % CROSS_HEADS
    row_head = lax.broadcasted_iota(jnp.int32, shape, 0) // rows
    same_head = col_head == row_head
    rsel = lax.broadcasted_iota(jnp.int32, (CROSS_HEADS * rows, CROSS_HEAD_DIM), 0) % rows
    out = jnp.zeros((CROSS_HEADS * rows, CROSS_HEAD_DIM), F32)
    for s in range(nseq):
        kall = k_ref[s].reshape(n_mem * CROSS_HEADS, CROSS_HEAD_DIM).astype(BF16)
        vall = v_ref[s].reshape(n_mem * CROSS_HEADS, CROSS_HEAD_DIM).astype(BF16)
        sc = jnp.where(same_head, _dot_nt(qs, kall) * (CROSS_HEAD_DIM ** -0.5), NEG_BIG)
        oh = _dot(_softmax_rows(sc).astype(BF16), vall)
        out = jnp.where((rsel >= tl * s) & (rsel < tl * (s + 1)), oh, out)
    o_ref[0] = jnp.concatenate([out[h * rows : (h + 1) * rows] for h in range(CROSS_HEADS)],
                               axis=1).astype(o_ref.dtype)


def cross_attend(q2, mem_k, mem_v, nb, seq):
    n_mem = mem_k.shape[1]
    nseq, tl = 16 // seq, seq
    rows = nseq * tl
    nblk = nb * seq // rows
    q3 = q2.reshape(nblk, rows, D_MODEL)
    kv_spec = pl.BlockSpec((nseq, n_mem, CROSS_HEADS, CROSS_HEAD_DIM), lambda i: (i, 0, 0, 0))
    o = pl.pallas_call(
        functools.partial(_xattn_step_kernel, nseq=nseq, tl=tl),
        grid=(nblk,),
        in_specs=[pl.BlockSpec((1, rows, D_MODEL), lambda i: (i, 0, 0)), kv_spec, kv_spec],
        out_specs=pl.BlockSpec((1, rows, D_MODEL), lambda i: (i, 0, 0)),
        out_shape=jax.ShapeDtypeStruct((nblk, rows, D_MODEL), BF16),
        compiler_params=_cp("parallel"),
        name="cross_attend",
    )(q3, mem_k, mem_v)
    return o.reshape(nb * seq, D_MODEL)


PACK_TILE = 512


SUBLANE = 8


def _cast_rows_kernel(w_ref, o_ref, *, nrows):
    j = pl.program_id(0)
    x = w_ref[...]
    row = lax.broadcasted_iota(jnp.int32, x.shape, 0) + j * PACK_TILE
    o_ref[...] = jnp.where(row < nrows, x, 0.0).astype(o_ref.dtype)


def cast_rows(wt, lo, hi, n_out, name):
    kdim = wt.shape[1]
    assert lo % SUBLANE == 0 and lo + n_out <= wt.shape[0]
    return pl.pallas_call(
        functools.partial(_cast_rows_kernel, nrows=hi - lo),
        grid=(n_out // PACK_TILE,),
        in_specs=[pl.BlockSpec((pl.Element(PACK_TILE), pl.Element(kdim)),
                               lambda j: (pl.multiple_of(lo + j * PACK_TILE, SUBLANE), 0))],
        out_specs=pl.BlockSpec((PACK_TILE, kdim), lambda j: (j, 0)),
        out_shape=jax.ShapeDtypeStruct((n_out, kdim), BF16),
        compiler_params=_cp("parallel"),
        name=name,
    )(wt)


def _split_w_in(w_in):
    wt = w_in.T
    ssd_end = D_MODEL + CONV_DIM + SSD_HEADS
    gla_end = ssd_end + 2 * GLA_KEY_DIM + 2 * D_MODEL + GATE_RANK
    return (cast_rows(wt, 0, ssd_end, N_SSD_PROJ, "pack_ssd_in"),
            cast_rows(wt, ssd_end, gla_end, N_GLA_PROJ, "pack_gla_in"),
            cast_rows(wt, gla_end, wt.shape[0], 2 * D_MODEL, "pack_gate_in"))


def _params(ssd_dt_bias, ssd_A_log, ssd_D, ssd_norm, w_gla_gate, b_gla_gate, gla_norm):
    padv = lambda a: jnp.pad(a.astype(F32), (0, LANE - a.shape[0])).reshape(1, LANE)
    head_of_chan = jnp.arange(D_MODEL, dtype=jnp.int32) // SSD_HEAD_DIM
    e_head = (jnp.arange(LANE, dtype=jnp.int32)[:, None] == head_of_chan[None, :]).astype(BF16)
    group_of_bc = jnp.arange(BC_W, dtype=jnp.int32) // D_STATE
    lane_h = jnp.arange(LANE, dtype=jnp.int32)
    g_head = ((lane_h[None, :] // (SSD_HEADS // N_GROUPS) == group_of_bc[:, None])
              & (lane_h[None, :] < SSD_HEADS)).astype(BF16)
    khead = jnp.arange(GLA_KEY_DIM, dtype=jnp.int32) // GLA_HEAD_K
    vhead = jnp.arange(D_MODEL, dtype=jnp.int32) // GLA_HEAD_V
    g_val = (khead[:, None] == vhead[None, :]).astype(BF16)
    return dict(
        dt_bias=padv(ssd_dt_bias), a_log=padv(ssd_A_log),
        d_exp=jnp.repeat(ssd_D.astype(F32), SSD_HEAD_DIM).reshape(1, D_MODEL),
        ssd_norm=ssd_norm.astype(F32).reshape(1, D_MODEL),
        e_head=e_head, g_head=g_head, g_val=g_val,
        w_gate=jnp.pad(w_gla_gate, ((0, LANE - GATE_RANK), (0, 0))).astype(BF16),
        b_gate=b_gla_gate.astype(F32).reshape(1, GLA_KEY_DIM),
        gla_norm=gla_norm.astype(F32).reshape(1, GLA_HEAD_V),
    )


def _layer(x3, mem_k, mem_v, ssd_conv, ssd_state, gla_state, ffn_conv, w, p, long_seq):
    nb, seq, d = x3.shape
    ntok = nb * seq
    ffn = w["w_down"].shape[0]
    x2 = x3.reshape(ntok, d)
    xn = rmsnorm_cast(x2, w["norm_mix"])
    w_ssd_in, w_gla_in, w_gate_in = w["w_in"]
    in_proj = functools.partial(matmul, xn, tm=2048, w_rows_are_outputs=True)
    proj_ssd = in_proj(w_ssd_in, name="in_proj_ssd")
    proj_gla = in_proj(w_gla_in, name="in_proj_gla")
    gates = in_proj(w_gate_in, name="in_proj_gates")
    if long_seq:
        u, ssd_new, ssd_conv_new = ssd_scan(proj_ssd.reshape(nb, seq, N_SSD_PROJ), ssd_conv,
                                            w["ssd_conv_w"], w["ssd_conv_b"], p)
        o, gla_new = gla_scan(proj_gla.reshape(nb, seq, N_GLA_PROJ), p)
    else:
        xc2, ssd_conv_new = short_conv(proj_ssd, seq, [COL_XBC], CONV_DIM, [ssd_conv], [w["ssd_conv_w"]],
                                       [w["ssd_conv_b"]], SSD_CONV, False, F32, "ssd_conv")
        u, ssd_new = ssd_step(xc2, proj_ssd, ssd_state, p, seq)
        o, gla_new = gla_step(proj_gla, gla_state, p, seq)
    merged = merge_branches(u, o, w["w_ssd_out"], w["w_gla_out"], gates)
    h, hn = mm_res_norm(merged, w["w_mix_out"], x2, w["norm_cross"], True, BF16, "mix_out")
    if long_seq:
        h2, hn2 = cross_block(hn, h, mem_k, mem_v, w["w_cq"], w["w_co"], w["norm_ffn"], seq)
    else:
        qc = matmul(hn, w["w_cq"], name="cross_q")
        att = cross_attend(qc, mem_k, mem_v, nb, seq)
        h2, hn2 = mm_res_norm(att, w["w_co"], h, w["norm_ffn"], True, BF16, "cross_out")
    cw, cbias = w["ffn_conv_w"], w["ffn_conv_b"]
    if long_seq:
        act, ffn_conv_new = ffn_up_conv_act(hn2.reshape(nb, seq, d), w["w_up"], cw, cbias, ffn_conv)
    else:
        up = matmul(hn2, w["w_up"], name="ffn_up")
        act, fa, fg = short_conv(up, seq, [0, ffn], ffn, [ffn_conv[:, :, :ffn], ffn_conv[:, :, ffn:]],
                                 [cw[:, :ffn], cw[:, ffn:]], [cbias[:ffn], cbias[ffn:]],
                                 FFN_CONV, True, BF16, "ffn_conv")
        ffn_conv_new = jnp.concatenate([fa, fg], axis=-1)
    y = mm_res_norm(act.reshape(ntok, ffn), w["w_down"], h2, w["norm_final"], False, F32, "ffn_down")
    return y.reshape(nb, seq, d), ssd_conv_new, ssd_new, gla_new, ffn_conv_new


def kernel(x_prompt, x_sample, cache_mem_k, cache_mem_v, state_ssd_conv, state_ssd, state_gla, state_ffn_conv, mem_prompt, norm_mix, w_in, ssd_conv_w, ssd_conv_b, ssd_dt_bias, ssd_A_log, ssd_D, ssd_norm, w_ssd_out, w_gla_gate, b_gla_gate, gla_norm, w_gla_out, w_mix_out, norm_cross, norm_mem, w_cq, w_ck, w_cv, w_co, norm_ffn, w_up, ffn_conv_w, ffn_conv_b, w_down, norm_final):
    nb, seq, d = x_prompt.shape
    n_mem = mem_prompt.shape[1]
    ffn2 = w_up.shape[1]
    w = dict(
        norm_mix=norm_mix, norm_cross=norm_cross, norm_ffn=norm_ffn, norm_final=norm_final,
        w_in=_split_w_in(w_in), ssd_conv_w=ssd_conv_w, ssd_conv_b=ssd_conv_b,
        w_ssd_out=w_ssd_out.astype(BF16), w_gla_out=w_gla_out.astype(BF16), w_mix_out=w_mix_out.astype(BF16),
        w_cq=w_cq.astype(BF16), w_co=w_co.astype(BF16), w_up=w_up.astype(BF16), w_down=w_down.astype(BF16),
        ffn_conv_w=ffn_conv_w, ffn_conv_b=ffn_conv_b,
    )
    p = _params(ssd_dt_bias, ssd_A_log, ssd_D, ssd_norm, w_gla_gate, b_gla_gate, gla_norm)

    mn = rmsnorm_cast(mem_prompt.reshape(nb * n_mem, d), norm_mem)
    p_mem_k = matmul(mn, w_ck.astype(BF16), name="mem_k").reshape(nb, n_mem, d)
    p_mem_v = matmul(mn, w_cv.astype(BF16), name="mem_v").reshape(nb, n_mem, d)
    zeros_ssd_conv = jnp.zeros((nb, SSD_CONV - 1, CONV_DIM), F32)
    zeros_ffn_conv = jnp.zeros((nb, FFN_CONV - 1, ffn2), F32)
    y_prompt, p_ssd_conv, p_ssd, p_gla, p_ffn_conv = _layer(
        x_prompt, p_mem_k, p_mem_v, zeros_ssd_conv, None, None, zeros_ffn_conv, w, p, True)

    ns = x_sample.shape[0]
    y_sample, s_ssd_conv, s_ssd, s_gla, s_ffn_conv = _layer(
        x_sample, cache_mem_k, cache_mem_v, state_ssd_conv, state_ssd, state_gla, state_ffn_conv, w, p, False)

    head_shape = (n_mem, CROSS_HEADS, CROSS_HEAD_DIM)
    return (y_prompt, y_sample, p_ssd_conv, p_ssd, p_gla, p_ffn_conv,
            p_mem_k.reshape((nb,) + head_shape), p_mem_v.reshape((nb,) + head_shape),
            s_ssd_conv, s_ssd, s_gla, s_ffn_conv)
```

```python
import functools
from typing import NamedTuple

import jax
import jax.numpy as jnp
from jax import lax
from jax.experimental import pallas as pl
from jax.experimental.pallas import tpu as pltpu

F32 = jnp.float32
BF16 = jnp.bfloat16
EPS = 1e-6
NEG_BIG = -1e30

D_MODEL = 2048
SSD_HEAD_DIM = 64
SSD_HEADS = 32
D_STATE = 128
N_GROUPS = 4
GROUP_W = D_MODEL // N_GROUPS
BC_W = N_GROUPS * D_STATE
CONV_DIM = D_MODEL + 2 * BC_W
SSD_CONV = 4
SSD_CHUNK = 128
GLA_HEADS = 4
GLA_KEY_DIM = 1024
GLA_HEAD_K = 256
GLA_HEAD_V = 512
GATE_RANK = 16
GATE_TAU = 16.0
GLA_CHUNK = 64
CROSS_HEADS = 4
CROSS_HEAD_DIM = 512
FFN_CONV = 3
LANE = 128

COL_Z, COL_XBC, COL_DT, N_SSD_PROJ = 0, 2048, 5120, 5632
COL_Q, COL_K, COL_V, COL_R, COL_GLR, N_GLA_PROJ = 0, 1024, 2048, 4096, 6144, 6656
COL_GA, COL_GB = 0, 2048

VMEM_LIMIT = 56 * 1024 * 1024


def _cp(*sem):
    return pltpu.CompilerParams(dimension_semantics=sem, vmem_limit_bytes=VMEM_LIMIT)


def _dot(a, b, prec=None):
    return jnp.dot(a, b, preferred_element_type=F32, precision=prec)


def _dot_nt(a, b):
    return lax.dot_general(a, b, (((1,), (1,)), ((), ())), preferred_element_type=F32)


def _split3(x):
    x1 = x.astype(BF16)
    r1 = x - x1.astype(F32)
    x2 = r1.astype(BF16)
    x3 = (r1 - x2.astype(F32)).astype(BF16)
    return x1, x2, x3


def _dot_sel(x, sel, sel_first=False):
    parts = _split3(x)
    if sel_first:
        return _dot(sel, parts[0]) + _dot(sel, parts[1]) + _dot(sel, parts[2])
    return _dot(parts[0], sel) + _dot(parts[1], sel) + _dot(parts[2], sel)


def _sigmoid(x):
    return 1.0 / (1.0 + jnp.exp(-x))


def _silu(x):
    return x * _sigmoid(x)


def _softplus(x):
    return jnp.maximum(x, 0.0) + jnp.log(1.0 + jnp.exp(-jnp.abs(x)))


def _rms(x, g):
    ms = jnp.mean(x * x, axis=-1, keepdims=True)
    return x * lax.rsqrt(ms + EPS) * g


def _rmsnorm_kernel(x_ref, g_ref, o_ref):
    o_ref[...] = _rms(x_ref[...], g_ref[...]).astype(o_ref.dtype)


def rmsnorm_cast(x2, g):
    m, d = x2.shape
    tm = min(m, 512)
    return pl.pallas_call(
        _rmsnorm_kernel,
        grid=(m // tm,),
        in_specs=[pl.BlockSpec((tm, d), lambda i: (i, 0)), pl.BlockSpec((1, d), lambda i: (0, 0))],
        out_specs=pl.BlockSpec((tm, d), lambda i: (i, 0)),
        out_shape=jax.ShapeDtypeStruct((m, d), BF16),
        compiler_params=_cp("parallel"),
        name="rmsnorm_cast",
    )(x2, g.reshape(1, d))


def _mm_kernel(a_ref, w_ref, o_ref, *, w_rows_are_outputs):
    dot = _dot_nt if w_rows_are_outputs else _dot
    o_ref[...] = dot(a_ref[...], w_ref[...]).astype(o_ref.dtype)


def matmul(a, w, out_dtype=F32, tm=1024, tn=512, w_rows_are_outputs=False, name="matmul"):
    m, k = a.shape
    n = w.shape[0] if w_rows_are_outputs else w.shape[1]
    tm = min(m, tm)
    if w_rows_are_outputs:
        w_spec = pl.BlockSpec((tn, k), lambda i, j: (j, 0))
    else:
        w_spec = pl.BlockSpec((k, tn), lambda i, j: (0, j))
    return pl.pallas_call(
        functools.partial(_mm_kernel, w_rows_are_outputs=w_rows_are_outputs),
        grid=(m // tm, n // tn),
        in_specs=[pl.BlockSpec((tm, k), lambda i, j: (i, 0)), w_spec],
        out_specs=pl.BlockSpec((tm, tn), lambda i, j: (i, j)),
        out_shape=jax.ShapeDtypeStruct((m, n), out_dtype),
        compiler_params=_cp("parallel", "arbitrary"),
        name=name,
    )(a, w)


class HostedMatmul(NamedTuple):
    a: jax.Array
    w: jax.Array
    tm: int = 1024
    tn: int = 512


def call_with_hosted_matmul(body, nsteps, in_specs, out_specs, out_shape, args, name, hosted=None):
    if hosted is None:
        outs = pl.pallas_call(body, grid=(nsteps,), in_specs=in_specs, out_specs=out_specs, out_shape=out_shape,
                              compiler_params=_cp("arbitrary"), name=name)(*args)
        return list(outs), None
    m, k = hosted.a.shape
    n = hosted.w.shape[0]
    tm, tn = min(m, hosted.tm), hosted.tn
    nj = n // tn
    hsteps = (m // tm) * nj
    total = max(nsteps, hsteps)
    hc = lambda i: jnp.minimum(i, hsteps - 1)
    h_in = [pl.BlockSpec((tm, k), lambda i: (hc(i) // nj, 0)), pl.BlockSpec((tn, k), lambda i: (hc(i) % nj, 0))]
    h_out = pl.BlockSpec((tm, tn), lambda i: (hc(i) // nj, hc(i) % nj))
    n_in, n_out = len(in_specs), len(out_specs)

    def fused(*refs):
        g_in = refs[:n_in]
        ha_ref, hw_ref = refs[n_in : n_in + 2]
        g_out = refs[n_in + 2 : n_in + 2 + n_out]
        ho_ref = refs[n_in + 2 + n_out]
        step = pl.program_id(0)

        def guest():
            body(*g_in, *g_out)

        def host():
            ho_ref[...] = _dot_nt(ha_ref[...], hw_ref[...])

        guest() if nsteps == total else pl.when(step < nsteps)(guest)
        host() if hsteps == total else pl.when(step < hsteps)(host)

    outs = pl.pallas_call(
        fused,
        grid=(total,),
        in_specs=list(in_specs) + h_in,
        out_specs=list(out_specs) + [h_out],
        out_shape=list(out_shape) + [jax.ShapeDtypeStruct((m, n), F32)],
        compiler_params=_cp("arbitrary"),
        name=name,
    )(*args, hosted.a, hosted.w)
    return list(outs[:-1]), outs[-1]


def _mm_res_norm_kernel(a_ref, w_ref, res_ref, g_ref, *out_refs):
    h = res_ref[...] + _dot(a_ref[...], w_ref[...])
    if len(out_refs) == 2:
        out_refs[0][...] = h
    out_refs[-1][...] = _rms(h, g_ref[...]).astype(out_refs[-1].dtype)


def mm_res_norm(a, w, res, g, emit_h, norm_dtype, name):
    m, kdim = a.shape
    n = w.shape[1]
    tm = min(m, 512)
    row_tile = lambda width: pl.BlockSpec((tm, width), lambda i: (i, 0))
    out_shape = [jax.ShapeDtypeStruct((m, n), norm_dtype)]
    out_specs = [row_tile(n)]
    if emit_h:
        out_shape = [jax.ShapeDtypeStruct((m, n), F32)] + out_shape
        out_specs = [row_tile(n)] + out_specs
    outs = pl.pallas_call(
        _mm_res_norm_kernel,
        grid=(m // tm,),
        in_specs=[row_tile(kdim), pl.BlockSpec((kdim, n), lambda i: (0, 0), pipeline_mode=pl.Buffered(1)),
                  row_tile(n), pl.BlockSpec((1, n), lambda i: (0, 0))],
        out_specs=out_specs,
        out_shape=out_shape,
        compiler_params=_cp("parallel"),
        name=name,
    )(a, w, res, g.reshape(1, n))
    return outs if emit_h else outs[0]


def _merge_kernel(u_ref, o_ref, wa_ref, wb_ref, ga_ref, gb_ref, out_ref):
    a = _dot(u_ref[...], wa_ref[...])
    b = _dot(o_ref[...], wb_ref[...])
    out_ref[...] = (_sigmoid(ga_ref[...]) * a + _sigmoid(gb_ref[...]) * b).astype(out_ref.dtype)


def merge_branches(u, o, wa, wb, proj):
    m, d = u.shape
    tm, tn = min(m, 1024), 512
    return pl.pallas_call(
        _merge_kernel,
        grid=(m // tm, d // tn),
        in_specs=[
            pl.BlockSpec((tm, d), lambda i, j: (i, 0)),
            pl.BlockSpec((tm, d), lambda i, j: (i, 0)),
            pl.BlockSpec((d, tn), lambda i, j: (0, j)),
            pl.BlockSpec((d, tn), lambda i, j: (0, j)),
            pl.BlockSpec((tm, tn), lambda i, j: (i, COL_GA // tn + j)),
            pl.BlockSpec((tm, tn), lambda i, j: (i, COL_GB // tn + j)),
        ],
        out_specs=pl.BlockSpec((tm, tn), lambda i, j: (i, j)),
        out_shape=jax.ShapeDtypeStruct((m, d), BF16),
        compiler_params=_cp("parallel", "arbitrary"),
        name="merge_branches",
    )(u, o, wa, wb, proj, proj)


CONV_PAD = 8


def _short_conv_kernel(*refs, taps, seq, nstreams, swiglu):
    ins = refs[: 4 * nstreams]
    out_ref = refs[4 * nstreams]
    ns_refs = refs[4 * nstreams + 1 :]
    vals = []
    for s in range(nstreams):
        u_ref, st_ref, w_ref, b_ref = ins[4 * s : 4 * s + 4]
        nseq, _, tc = st_ref.shape
        full = jnp.concatenate([st_ref[...], u_ref[...].reshape(nseq, seq, tc)], axis=1)
        acc = b_ref[...]
        for k in range(taps):
            acc = acc + full[:, k : k + seq] * w_ref[k : k + 1, :]
        vals.append(acc)
        ns_refs[s][...] = full[:, seq : seq + taps - 1]
    out = _silu(vals[1]) * vals[0] if swiglu else _silu(vals[0])
    out_ref[...] = out.reshape(out_ref.shape).astype(out_ref.dtype)


def short_conv(u2, seq, col_offs, width, states, ws, bs, taps, swiglu, out_dtype, name):
    ntok = u2.shape[0]
    nseq = ntok // seq
    tc = 512
    nstreams = len(col_offs)
    in_specs, args = [], []
    for s in range(nstreams):
        cb = col_offs[s] // tc
        in_specs += [
            pl.BlockSpec((ntok, tc), lambda c, cb=cb: (0, cb + c)),
            pl.BlockSpec((nseq, taps - 1, tc), lambda c: (0, 0, c)),
            pl.BlockSpec((taps, tc), lambda c: (0, c)),
            pl.BlockSpec((1, tc), lambda c: (0, c)),
        ]
        args += [u2, states[s], ws[s], bs[s].reshape(1, width)]
    out_specs = [pl.BlockSpec((ntok, tc), lambda c: (0, c))]
    out_shape = [jax.ShapeDtypeStruct((ntok, width), out_dtype)]
    for s in range(nstreams):
        out_specs.append(pl.BlockSpec((nseq, taps - 1, tc), lambda c: (0, 0, c)))
        out_shape.append(jax.ShapeDtypeStruct((nseq, taps - 1, width), F32))
    return pl.pallas_call(
        functools.partial(_short_conv_kernel, taps=taps, seq=seq, nstreams=nstreams, swiglu=swiglu),
        grid=(width // tc,),
        in_specs=in_specs,
        out_specs=out_specs,
        out_shape=out_shape,
        compiler_params=_cp("parallel"),
        name=name,
    )(*args)


def _shift_rows(u, d, prev):
    x = pltpu.roll(u, d, 0)
    r = lax.broadcasted_iota(jnp.int32, prev.shape, 0)
    head = jnp.where(r < d, pltpu.roll(prev, d, 0), x[:CONV_PAD])
    return jnp.concatenate([head, x[CONV_PAD:]], axis=0)


def _causal_taps(u, prev, cw, bias):
    taps = cw.shape[0]
    acc = bias
    for k in range(taps):
        d = taps - 1 - k
        acc = acc + (_shift_rows(u, d, prev) if d else u) * cw[k : k + 1, :]
    return acc


def _ffn_up_kernel(hn_ref, wa_ref, wg_ref, cwa_ref, cwg_ref, cba_ref, cbg_ref, sta_ref, stg_ref,
                   act_ref, nsa_ref, nsg_ref, preva, prevg, *, tm):
    m = pl.program_id(2)
    nm = pl.num_programs(2)
    lo = CONV_PAD - (FFN_CONV - 1)

    @pl.when(m == 0)
    def _():
        for prev, st_ref in ((preva, sta_ref), (prevg, stg_ref)):
            prev[0:lo, :] = jnp.zeros((lo, prev.shape[1]), F32)
            prev[lo:CONV_PAD, :] = st_ref[0]

    hn = hn_ref[0]
    vals = []
    for w_ref, cw_ref, cb_ref, prev in ((wa_ref, cwa_ref, cba_ref, preva), (wg_ref, cwg_ref, cbg_ref, prevg)):
        u = _dot(hn, w_ref[...])
        vals.append(_causal_taps(u, prev[...], cw_ref[...], cb_ref[...]))
        prev[...] = u[tm - CONV_PAD : tm]
    act_ref[0] = (_silu(vals[1]) * vals[0]).astype(act_ref.dtype)

    @pl.when(m == nm - 1)
    def _():
        nsa_ref[0] = preva[lo:CONV_PAD, :]
        nsg_ref[0] = prevg[lo:CONV_PAD, :]


def ffn_up_conv_act(hn3, w_up, conv_w, conv_b, conv_state):
    nb, seq, d = hn3.shape
    ffn = w_up.shape[1] // 2
    tm, tn = min(seq, 1024), 512
    nn = ffn // tn
    half = lambda off: (lambda b, n, m: (0, off + n))
    st_spec = lambda off: pl.BlockSpec((1, FFN_CONV - 1, tn), lambda b, n, m: (b, 0, off + n))
    cb2 = conv_b.reshape(1, 2 * ffn)
    act, nsa, nsg = pl.pallas_call(
        functools.partial(_ffn_up_kernel, tm=tm),
        grid=(nb, nn, seq // tm),
        in_specs=[
            pl.BlockSpec((1, tm, d), lambda b, n, m: (b, m, 0)),
            pl.BlockSpec((d, tn), half(0)), pl.BlockSpec((d, tn), half(nn)),
            pl.BlockSpec((FFN_CONV, tn), half(0)), pl.BlockSpec((FFN_CONV, tn), half(nn)),
            pl.BlockSpec((1, tn), half(0)), pl.BlockSpec((1, tn), half(nn)),
            st_spec(0), st_spec(nn),
        ],
        out_specs=[
            pl.BlockSpec((1, tm, tn), lambda b, n, m: (b, m, n)),
            pl.BlockSpec((1, FFN_CONV - 1, tn), lambda b, n, m: (b, 0, n)),
            pl.BlockSpec((1, FFN_CONV - 1, tn), lambda b, n, m: (b, 0, n)),
        ],
        out_shape=[
            jax.ShapeDtypeStruct((nb, seq, ffn), BF16),
            jax.ShapeDtypeStruct((nb, FFN_CONV - 1, ffn), F32),
            jax.ShapeDtypeStruct((nb, FFN_CONV - 1, ffn), F32),
        ],
        scratch_shapes=[pltpu.VMEM((CONV_PAD, tn), F32), pltpu.VMEM((CONV_PAD, tn), F32)],
        compiler_params=_cp("parallel", "parallel", "arbitrary"),
        name="ffn_up_conv_act",
    )(hn3, w_up, w_up, conv_w, conv_w, cb2, cb2, conv_state, conv_state)
    return act, jnp.concatenate([nsa, nsg], axis=-1)


def _ssd_gate_norm(y, z, nrm):
    ug = y * _silu(z)
    outs = []
    for g in range(N_GROUPS):
        ugg = ug[:, g * GROUP_W : (g + 1) * GROUP_W]
        ms = jnp.mean(ugg * ugg, axis=-1, keepdims=True)
        outs.append(ugg * lax.rsqrt(ms + EPS))
    return jnp.concatenate(outs, axis=1) * nrm


def _gla_out_norm(o, r, nrm):
    outs = []
    for h in range(GLA_HEADS):
        oh = o[:, h * GLA_HEAD_V : (h + 1) * GLA_HEAD_V]
        rh = r[:, h * GLA_HEAD_V : (h + 1) * GLA_HEAD_V]
        outs.append(_rms(oh, nrm) * _silu(rh))
    return jnp.concatenate(outs, axis=1)


SCAN_ROWS_PER_STEP = 256


def _ssd_chunk(xs_ref, b_ref, c_ref, dt_ref, z_ref, cw_ref, cbias_ref, dtb_ref, alog_ref, dexp_ref, nrm_ref, e_ref,
               u_ref, st_ref, cprev):
    q = SSD_CHUNK
    conv = []
    for raw_ref, c0 in ((xs_ref, 0), (b_ref, D_MODEL), (c_ref, D_MODEL + BC_W)):
        raw = raw_ref[...]
        cols = slice(c0, c0 + raw.shape[1])
        conv.append(_silu(_causal_taps(raw, cprev[:, cols], cw_ref[:, cols], cbias_ref[:, cols])))
        cprev[:, cols] = raw[q - CONV_PAD : q]
    xs, bm, cm = conv
    dt = _softplus(dt_ref[...] + dtb_ref[...])
    a = dt * (-jnp.exp(alog_ref[...]))
    row = lax.broadcasted_iota(jnp.int32, (q, q), 0)
    col = lax.broadcasted_iota(jnp.int32, (q, q), 1)
    tril = row >= col
    acum = _dot_sel(a, tril.astype(BF16), sel_first=True)
    acum_t = acum.T
    dt_t = dt.T
    last = acum[q - 1 : q, :]
    e_mat = e_ref[...]
    eexp = _dot_sel(jnp.exp(acum), e_mat)
    wexp = _dot_sel(jnp.exp(last - acum) * dt, e_mat)
    s_bf = st_ref[...].astype(BF16)
    cb16 = cm.astype(BF16)
    bb16 = bm.astype(BF16)
    x16 = xs.astype(BF16)
    xw16 = (xs * wexp).astype(BF16)
    lane_lo = lax.broadcasted_iota(jnp.int32, (q, LANE), 1) < SSD_HEAD_DIM
    ys = []
    for g in range(N_GROUPS):
        cg = cb16[:, g * D_STATE : (g + 1) * D_STATE]
        bg = bb16[:, g * D_STATE : (g + 1) * D_STATE]
        cb = _dot_nt(cg, bg)
        yoff = _dot(cg, s_bf[:, g * GROUP_W : (g + 1) * GROUP_W])
        pieces = []
        for pr in range(GROUP_W // LANE):
            h0 = g * (SSD_HEADS // N_GROUPS) + 2 * pr
            xp = x16[:, h0 * SSD_HEAD_DIM : h0 * SSD_HEAD_DIM + LANE]
            yh = []
            for h in (h0, h0 + 1):
                diff = acum[:, h : h + 1] - acum_t[h : h + 1, :]
                dec = jnp.exp(jnp.where(tril, diff, NEG_BIG))
                m = (cb * dec * dt_t[h : h + 1, :]).astype(BF16)
                yh.append(_dot(m, xp))
            pieces.append(jnp.where(lane_lo, yh[0], yh[1]))
        sl = slice(g * GROUP_W, (g + 1) * GROUP_W)
        ys.append(jnp.concatenate(pieces, axis=1) + yoff * eexp[:, sl])
        bg_t = bm[:, g * D_STATE : (g + 1) * D_STATE].T.astype(BF16)
        upd = _dot(bg_t, xw16[:, sl])
        st_ref[:, sl] = eexp[q - 1 : q, sl] * st_ref[:, sl] + upd
    y = jnp.concatenate(ys, axis=1) + dexp_ref[...] * xs
    u_ref[...] = _ssd_gate_norm(y, z_ref[...], nrm_ref[...]).astype(u_ref.dtype)


def _ssd_scan_kernel(xs_ref, b_ref, c_ref, dt_ref, z_ref, cst_ref, cw_ref, cbias_ref,
                     dtb_ref, alog_ref, dexp_ref, nrm_ref, e_ref,
                     u_ref, sout_ref, cso_ref, st_ref, cprev, *, nsteps, nsub):
    step = pl.program_id(1)
    lo = CONV_PAD - (SSD_CONV - 1)

    @pl.when(step == 0)
    def _():
        st_ref[...] = jnp.zeros_like(st_ref)
        cprev[0:lo, :] = jnp.zeros((lo, CONV_DIM), F32)
        cprev[lo:CONV_PAD, :] = cst_ref[0]

    for sub in range(nsub):
        rows = lambda ref: ref.at[0, pl.ds(sub * SSD_CHUNK, SSD_CHUNK)]
        _ssd_chunk(rows(xs_ref), rows(b_ref), rows(c_ref), rows(dt_ref), rows(z_ref), cw_ref, cbias_ref,
                   dtb_ref, alog_ref, dexp_ref, nrm_ref, e_ref, rows(u_ref), st_ref, cprev)

    @pl.when(step == nsteps - 1)
    def _():
        sout_ref[0] = st_ref[...].T
        cso_ref[0] = cprev[lo:CONV_PAD, :]


def ssd_scan(proj3, conv_state, conv_w, conv_b, p):
    nb, seq, _ = proj3.shape
    nsub = SCAN_ROWS_PER_STEP // SSD_CHUNK if seq % SCAN_ROWS_PER_STEP == 0 else 1
    q = nsub * SSD_CHUNK
    nsteps = seq // q
    vec = lambda n: pl.BlockSpec((1, n), lambda b, c: (0, 0))
    u, s_out, conv_new = pl.pallas_call(
        functools.partial(_ssd_scan_kernel, nsteps=nsteps, nsub=nsub),
        grid=(nb, nsteps),
        in_specs=[
            pl.BlockSpec((1, q, D_MODEL), lambda b, c: (b, c, COL_XBC // D_MODEL)),
            pl.BlockSpec((1, q, BC_W), lambda b, c: (b, c, (COL_XBC + D_MODEL) // BC_W)),
            pl.BlockSpec((1, q, BC_W), lambda b, c: (b, c, (COL_XBC + D_MODEL) // BC_W + 1)),
            pl.BlockSpec((1, q, LANE), lambda b, c: (b, c, COL_DT // LANE)),
            pl.BlockSpec((1, q, D_MODEL), lambda b, c: (b, c, COL_Z // D_MODEL)),
            pl.BlockSpec((1, SSD_CONV - 1, CONV_DIM), lambda b, c: (b, 0, 0)),
            pl.BlockSpec((SSD_CONV, CONV_DIM), lambda b, c: (0, 0)),
            vec(CONV_DIM),
            vec(LANE), vec(LANE), vec(D_MODEL), vec(D_MODEL),
            pl.BlockSpec((LANE, D_MODEL), lambda b, c: (0, 0)),
        ],
        out_specs=[
            pl.BlockSpec((1, q, D_MODEL), lambda b, c: (b, c, 0)),
            pl.BlockSpec((1, D_MODEL, D_STATE), lambda b, c: (b, 0, 0)),
            pl.BlockSpec((1, SSD_CONV - 1, CONV_DIM), lambda b, c: (b, 0, 0)),
        ],
        out_shape=[
            jax.ShapeDtypeStruct((nb, seq, D_MODEL), BF16),
            jax.ShapeDtypeStruct((nb, D_MODEL, D_STATE), F32),
            jax.ShapeDtypeStruct((nb, SSD_CONV - 1, CONV_DIM), F32),
        ],
        scratch_shapes=[pltpu.VMEM((D_STATE, D_MODEL), F32), pltpu.VMEM((CONV_PAD, CONV_DIM), F32)],
        compiler_params=_cp("parallel", "arbitrary"),
        name="ssd_scan",
    )(proj3, proj3, proj3, proj3, proj3, conv_state, conv_w, conv_b.reshape(1, CONV_DIM),
      p["dt_bias"], p["a_log"], p["d_exp"], p["ssd_norm"], p["e_head"])
    return u.reshape(nb * seq, D_MODEL), s_out.reshape(nb, SSD_HEADS, SSD_HEAD_DIM, D_STATE), conv_new


def _gla_gate_log(glr, wg, bg):
    x = _dot(glr.astype(BF16), wg) + bg
    return -_softplus(-x) / GATE_TAU


def _gla_chunk(q_ref, k_ref, v_ref, r_ref, glr_ref, wg_ref, bg_ref, nrm_ref, o_ref, st_ref):
    q = GLA_CHUNK
    glog = _gla_gate_log(glr_ref[...], wg_ref[...], bg_ref[...])
    row = lax.broadcasted_iota(jnp.int32, (q, q), 0)
    col = lax.broadcasted_iota(jnp.int32, (q, q), 1)
    tril = row >= col
    bc = _dot_sel(glog, tril.astype(BF16), sel_first=True)
    last = bc[q - 1 : q, :]
    kk = k_ref[...]
    qe = q_ref[...] * (GLA_HEAD_K ** -0.5) * jnp.exp(bc)
    ke = kk * jnp.exp(-bc)
    kd = kk * jnp.exp(last - bc)
    elast = jnp.exp(last)
    v16 = v_ref[...].astype(BF16)
    zeros_v = jnp.zeros((q, GLA_HEAD_V), BF16)
    outs = []
    for h in range(GLA_HEADS):
        ks = slice(h * GLA_HEAD_K, (h + 1) * GLA_HEAD_K)
        vs = slice(h * GLA_HEAD_V, (h + 1) * GLA_HEAD_V)
        qh = qe[:, ks].astype(BF16)
        kh = ke[:, ks].astype(BF16)
        att = jnp.where(tril, _dot_nt(qh, kh), 0.0)
        s_h = st_ref[ks, :]
        outs.append(_dot(att.astype(BF16), v16[:, vs]) + _dot(qh, s_h.astype(BF16)))
        xt = jnp.concatenate([kd[:, ks], jnp.broadcast_to(elast[:, ks], (q, GLA_HEAD_K))], axis=0).T
        v2 = jnp.concatenate([v16[:, vs], zeros_v], axis=0)
        st_ref[ks, :] = xt[:, q : q + 1] * s_h + _dot(xt.astype(BF16), v2)
    o = jnp.concatenate(outs, axis=1)
    o_ref[...] = _gla_out_norm(o, r_ref[...], nrm_ref[...]).astype(o_ref.dtype)


def _gla_scan_kernel(q_ref, k_ref, v_ref, r_ref, glr_ref, wg_ref, bg_ref, nrm_ref,
                     o_ref, sout_ref, st_ref, *, nsteps, nsub):
    step = pl.program_id(1)

    @pl.when(step == 0)
    def _():
        st_ref[...] = jnp.zeros_like(st_ref)

    for sub in range(nsub):
        rows = lambda ref: ref.at[0, pl.ds(sub * GLA_CHUNK, GLA_CHUNK)]
        _gla_chunk(rows(q_ref), rows(k_ref), rows(v_ref), rows(r_ref), rows(glr_ref), wg_ref, bg_ref, nrm_ref,
                   rows(o_ref), st_ref)

    @pl.when(step == nsteps - 1)
    def _():
        sout_ref[0] = st_ref[...]


def gla_scan(proj3, p):
    nb, seq, _ = proj3.shape
    nsub = SCAN_ROWS_PER_STEP // GLA_CHUNK if seq % SCAN_ROWS_PER_STEP == 0 else 1
    q = nsub * GLA_CHUNK
    nsteps = seq // q
    o, s_out = pl.pallas_call(
        functools.partial(_gla_scan_kernel, nsteps=nsteps, nsub=nsub),
        grid=(nb, nsteps),
        in_specs=[
            pl.BlockSpec((1, q, GLA_KEY_DIM), lambda b, c: (b, c, COL_Q // GLA_KEY_DIM)),
            pl.BlockSpec((1, q, GLA_KEY_DIM), lambda b, c: (b, c, COL_K // GLA_KEY_DIM)),
            pl.BlockSpec((1, q, D_MODEL), lambda b, c: (b, c, COL_V // D_MODEL)),
            pl.BlockSpec((1, q, D_MODEL), lambda b, c: (b, c, COL_R // D_MODEL)),
            pl.BlockSpec((1, q, LANE), lambda b, c: (b, c, COL_GLR // LANE)),
            pl.BlockSpec((LANE, GLA_KEY_DIM), lambda b, c: (0, 0)),
            pl.BlockSpec((1, GLA_KEY_DIM), lambda b, c: (0, 0)),
            pl.BlockSpec((1, GLA_HEAD_V), lambda b, c: (0, 0)),
        ],
        out_specs=[
            pl.BlockSpec((1, q, D_MODEL), lambda b, c: (b, c, 0)),
            pl.BlockSpec((1, GLA_KEY_DIM, GLA_HEAD_V), lambda b, c: (b, 0, 0)),
        ],
        out_shape=[
            jax.ShapeDtypeStruct((nb, seq, D_MODEL), BF16),
            jax.ShapeDtypeStruct((nb, GLA_KEY_DIM, GLA_HEAD_V), F32),
        ],
        scratch_shapes=[pltpu.VMEM((GLA_KEY_DIM, GLA_HEAD_V), F32)],
        compiler_params=_cp("parallel", "arbitrary"),
        name="gla_scan",
    )(proj3, proj3, proj3, proj3, proj3, p["w_gate"], p["b_gate"], p["gla_norm"])
    return o.reshape(nb * seq, D_MODEL), s_out.reshape(nb, GLA_HEADS, GLA_HEAD_K, GLA_HEAD_V)


TOK_BLOCK = 128


def _row_shift(x, d, tpos):
    return jnp.where(tpos >= d, pltpu.roll(x, d, 0), 0.0)


def _seq_cumsum_and_last(a, seq, tpos):
    nrows = a.shape[0]
    acum = a
    for d in range(1, seq):
        acum = acum + _row_shift(a, d, tpos)
    last = jnp.where(tpos == seq - 1, acum, 0.0)
    for d in range(1, seq):
        last = last + jnp.where(tpos == seq - 1 - d, pltpu.roll(acum, nrows - d, 0), 0.0)
    return acum, last


def _ssd_step_pre_kernel(xs_ref, b_ref, c_ref, dt_ref, dtb_ref, alog_ref, dexp_ref, e_ref, gh_ref,
                         ypart_ref, eexp_ref, xwt_ref, el_ref, *, seq):
    nrows = xs_ref.shape[0]
    xs = xs_ref[...]
    bm = b_ref[...]
    cm = c_ref[...]
    dt = _softplus(dt_ref[...] + dtb_ref[...])
    a = dt * (-jnp.exp(alog_ref[...]))
    pos = lambda w: lax.broadcasted_iota(jnp.int32, (nrows, w), 0) % seq
    t_h, t_c, t_x = pos(LANE), pos(BC_W), pos(D_MODEL)
    acum, last = _seq_cumsum_and_last(a, seq, t_h)
    e_mat = e_ref[...]
    eexp_ref[...] = _dot_sel(jnp.exp(acum), e_mat)
    wexp = _dot_sel(jnp.exp(last - acum) * dt, e_mat)
    xwt_ref[...] = (xs * wexp).T.astype(xwt_ref.dtype)
    el_ref[...] = jnp.exp(last)
    y = dexp_ref[...] * xs
    for d in range(seq):
        if d == 0:
            cbh = _dot_sel(cm * bm, gh_ref[...])
            coef = dt
            xd = xs
        else:
            cbh = _dot_sel(cm * _row_shift(bm, d, t_c), gh_ref[...])
            coef = jnp.where(t_h >= d, jnp.exp(acum - pltpu.roll(acum, d, 0)) * pltpu.roll(dt, d, 0), 0.0)
            xd = _row_shift(xs, d, t_x)
        y = y + _dot_sel(cbh * coef, e_mat) * xd
    ypart_ref[...] = y


def _ssd_step_state_kernel(st_ref, c_ref, b_ref, xwt_ref, el_ref, ypart_ref, eexp_ref, z_ref, nrm_ref,
                           u_ref, so_ref, *, sb, seq):
    i = pl.program_id(0)
    rows = sb * seq
    steps_per_block = TOK_BLOCK // rows
    base = (i % steps_per_block) * rows
    c16 = c_ref[...].astype(BF16)
    btok = b_ref[...]
    tok = lax.broadcasted_iota(jnp.int32, (TOK_BLOCK, LANE), 0)
    rsel = lax.broadcasted_iota(jnp.int32, (rows, GROUP_W), 0)
    heads_per_group = SSD_HEADS // N_GROUPS
    yoff = [jnp.zeros((rows, GROUP_W), F32) for _ in range(N_GROUPS)]
    for s in range(sb):
        lo = base + seq * s
        own = (tok >= lo) & (tok < lo + seq)
        mine = (rsel >= seq * s) & (rsel < seq * (s + 1))
        for g in range(N_GROUPS):
            sl = slice(g * GROUP_W, (g + 1) * GROUP_W)
            yo = _dot_nt(c16[:, g * D_STATE : (g + 1) * D_STATE], st_ref[s, sl, :].astype(BF16))
            yoff[g] = jnp.where(mine, yo, yoff[g])
            bsel = jnp.where(own, btok[:, g * D_STATE : (g + 1) * D_STATE], 0.0).astype(BF16)
            upd = _dot(xwt_ref[sl, :], bsel)
            for r in range(heads_per_group):
                h = g * heads_per_group + r
                hs = slice(h * SSD_HEAD_DIM, (h + 1) * SSD_HEAD_DIM)
                so_ref[s, hs, :] = el_ref[seq * s, h] * st_ref[s, hs, :] + upd[r * SSD_HEAD_DIM : (r + 1) * SSD_HEAD_DIM]
    y = ypart_ref[...] + jnp.concatenate(yoff, axis=1) * eexp_ref[...]
    u_ref[...] = _ssd_gate_norm(y, z_ref[...], nrm_ref[...]).astype(u_ref.dtype)


def ssd_step(xc2, proj2, state, p, seq, hosted=None):
    ntok = xc2.shape[0]
    nseq = ntok // seq
    full = lambda shape: pl.BlockSpec(shape, lambda i: (0,) * len(shape))
    ypart, eexp, xwt, elast = pl.pallas_call(
        functools.partial(_ssd_step_pre_kernel, seq=seq),
        grid=(1,),
        in_specs=[
            pl.BlockSpec((ntok, D_MODEL), lambda i: (0, 0)),
            pl.BlockSpec((ntok, BC_W), lambda i: (0, D_MODEL // BC_W)),
            pl.BlockSpec((ntok, BC_W), lambda i: (0, D_MODEL // BC_W + 1)),
            pl.BlockSpec((ntok, LANE), lambda i: (0, COL_DT // LANE)),
            full((1, LANE)), full((1, LANE)), full((1, D_MODEL)),
            full((LANE, D_MODEL)), full((BC_W, LANE)),
        ],
        out_specs=[full((ntok, D_MODEL)), full((ntok, D_MODEL)), full((D_MODEL, ntok)), full((ntok, LANE))],
        out_shape=[
            jax.ShapeDtypeStruct((ntok, D_MODEL), F32),
            jax.ShapeDtypeStruct((ntok, D_MODEL), F32),
            jax.ShapeDtypeStruct((D_MODEL, ntok), BF16),
            jax.ShapeDtypeStruct((ntok, LANE), F32),
        ],
        compiler_params=_cp("arbitrary"),
        name="ssd_step_pre",
    )(xc2, xc2, xc2, proj2, p["dt_bias"], p["a_log"], p["d_exp"], p["e_head"], p["g_head"])

    sb = 4
    rows = sb * seq
    spb = TOK_BLOCK // rows
    st3 = state.reshape(nseq, D_MODEL, D_STATE)
    nsteps = nseq // sb
    c = lambda i: jnp.minimum(i, nsteps - 1)
    (u, s_new), hosted_out = call_with_hosted_matmul(
        functools.partial(_ssd_step_state_kernel, sb=sb, seq=seq),
        nsteps,
        in_specs=[
            pl.BlockSpec((sb, D_MODEL, D_STATE), lambda i: (c(i), 0, 0)),
            pl.BlockSpec((rows, BC_W), lambda i: (c(i), D_MODEL // BC_W + 1)),
            pl.BlockSpec((TOK_BLOCK, BC_W), lambda i: (c(i) // spb, D_MODEL // BC_W)),
            pl.BlockSpec((D_MODEL, TOK_BLOCK), lambda i: (0, c(i) // spb)),
            pl.BlockSpec((rows, LANE), lambda i: (c(i), 0), memory_space=pltpu.SMEM),
            pl.BlockSpec((rows, D_MODEL), lambda i: (c(i), 0)),
            pl.BlockSpec((rows, D_MODEL), lambda i: (c(i), 0)),
            pl.BlockSpec((rows, D_MODEL), lambda i: (c(i), COL_Z // D_MODEL)),
            pl.BlockSpec((1, D_MODEL), lambda i: (0, 0)),
        ],
        out_specs=[
            pl.BlockSpec((rows, D_MODEL), lambda i: (c(i), 0)),
            pl.BlockSpec((sb, D_MODEL, D_STATE), lambda i: (c(i), 0, 0)),
        ],
        out_shape=[
            jax.ShapeDtypeStruct((ntok, D_MODEL), BF16),
            jax.ShapeDtypeStruct((nseq, D_MODEL, D_STATE), F32),
        ],
        args=(st3, xc2, xc2, xwt, elast, ypart, eexp, proj2, p["ssd_norm"]),
        name="ssd_step_state",
        hosted=hosted,
    )
    return u, s_new.reshape(nseq, SSD_HEADS, SSD_HEAD_DIM, D_STATE), hosted_out


def _gla_step_pre_kernel(q_ref, k_ref, v_ref, glr_ref, wg_ref, bg_ref, gv_ref,
                         oin_ref, qe_ref, kdt_ref, elt_ref, *, seq):
    nrows = q_ref.shape[0]
    glog = _gla_gate_log(glr_ref[...], wg_ref[...], bg_ref[...])
    pos = lambda w: lax.broadcasted_iota(jnp.int32, (nrows, w), 0) % seq
    t_k, t_v = pos(GLA_KEY_DIM), pos(D_MODEL)
    bc, last = _seq_cumsum_and_last(glog, seq, t_k)
    kk = k_ref[...]
    qe = q_ref[...] * (GLA_HEAD_K ** -0.5) * jnp.exp(bc)
    ke = kk * jnp.exp(-bc)
    qe_ref[...] = qe
    kdt_ref[...] = (kk * jnp.exp(last - bc)).T
    elt_ref[...] = jnp.exp(last).T
    v = v_ref[...]
    gv = gv_ref[...]
    o = jnp.zeros((nrows, D_MODEL), F32)
    for d in range(seq):
        ked = ke if d == 0 else _row_shift(ke, d, t_k)
        vd = v if d == 0 else _row_shift(v, d, t_v)
        att = _dot((qe * ked).astype(BF16), gv)
        o = o + att * vd
    oin_ref[...] = o


def _gla_step_state_kernel(st_ref, qe_ref, v_ref, kdt_ref, elt_ref, oin_ref, r_ref, nrm_ref,
                           o_ref, so_ref, *, sb, seq):
    i = pl.program_id(0)
    rows = sb * seq
    spb = TOK_BLOCK // rows
    base = (i % spb) * rows
    qe16 = qe_ref[...].astype(BF16)
    vtok = v_ref[...]
    tokv = lax.broadcasted_iota(jnp.int32, (TOK_BLOCK, GLA_HEAD_V), 0)
    tok = lax.broadcasted_iota(jnp.int32, (TOK_BLOCK, LANE), 0)
    rsel = lax.broadcasted_iota(jnp.int32, (rows, GLA_HEAD_V), 0)
    ooff = [jnp.zeros((rows, GLA_HEAD_V), F32) for _ in range(GLA_HEADS)]
    for s in range(sb):
        lo = base + seq * s
        own = (tokv >= lo) & (tokv < lo + seq)
        first = jnp.where(tok == lo, 1.0, 0.0).astype(BF16)
        mine = (rsel >= seq * s) & (rsel < seq * (s + 1))
        for h in range(GLA_HEADS):
            ks = slice(h * GLA_HEAD_K, (h + 1) * GLA_HEAD_K)
            vs = slice(h * GLA_HEAD_V, (h + 1) * GLA_HEAD_V)
            s_h = st_ref[s, ks, :]
            oo = _dot(qe16[:, ks], s_h.astype(BF16))
            ooff[h] = jnp.where(mine, oo, ooff[h])
            vsel = jnp.where(own, vtok[:, vs], 0.0).astype(BF16)
            upd = _dot(kdt_ref[ks, :].astype(BF16), vsel)
            ecol = _dot_sel(elt_ref[ks, :], first)
            so_ref[s, ks, :] = jnp.concatenate([ecol] * (GLA_HEAD_V // LANE), axis=1) * s_h + upd
    o = oin_ref[...] + jnp.concatenate(ooff, axis=1)
    o_ref[0] = _gla_out_norm(o, r_ref[...], nrm_ref[...]).astype(o_ref.dtype)


def gla_step(proj2, state, p, seq, hosted=None):
    ntok = proj2.shape[0]
    nseq = ntok // seq
    full = lambda shape: pl.BlockSpec(shape, lambda i: (0,) * len(shape))
    oin, qe, kdt, elt = pl.pallas_call(
        functools.partial(_gla_step_pre_kernel, seq=seq),
        grid=(1,),
        in_specs=[
            pl.BlockSpec((ntok, GLA_KEY_DIM), lambda i: (0, COL_Q // GLA_KEY_DIM)),
            pl.BlockSpec((ntok, GLA_KEY_DIM), lambda i: (0, COL_K // GLA_KEY_DIM)),
            pl.BlockSpec((ntok, D_MODEL), lambda i: (0, COL_V // D_MODEL)),
            pl.BlockSpec((ntok, LANE), lambda i: (0, COL_GLR // LANE)),
            full((LANE, GLA_KEY_DIM)), full((1, GLA_KEY_DIM)), full((GLA_KEY_DIM, D_MODEL)),
        ],
        out_specs=[full((ntok, D_MODEL)), full((ntok, GLA_KEY_DIM)), full((GLA_KEY_DIM, ntok)),
                   full((GLA_KEY_DIM, ntok))],
        out_shape=[
            jax.ShapeDtypeStruct((ntok, D_MODEL), F32),
            jax.ShapeDtypeStruct((ntok, GLA_KEY_DIM), F32),
            jax.ShapeDtypeStruct((GLA_KEY_DIM, ntok), F32),
            jax.ShapeDtypeStruct((GLA_KEY_DIM, ntok), F32),
        ],
        compiler_params=_cp("arbitrary"),
        name="gla_step_pre",
    )(proj2, proj2, proj2, proj2, p["w_gate"], p["b_gate"], p["g_val"])

    sb = 2 if hosted is not None else 4
    rows = sb * seq
    spb = TOK_BLOCK // rows
    st3 = state.reshape(nseq, GLA_KEY_DIM, GLA_HEAD_V)
    nsteps = nseq // sb
    c = lambda i: jnp.minimum(i, nsteps - 1)
    (o, s_new), hosted_out = call_with_hosted_matmul(
        functools.partial(_gla_step_state_kernel, sb=sb, seq=seq),
        nsteps,
        in_specs=[
            pl.BlockSpec((sb, GLA_KEY_DIM, GLA_HEAD_V), lambda i: (c(i), 0, 0)),
            pl.BlockSpec((rows, GLA_KEY_DIM), lambda i: (c(i), 0)),
            pl.BlockSpec((TOK_BLOCK, D_MODEL), lambda i: (c(i) // spb, COL_V // D_MODEL)),
            pl.BlockSpec((GLA_KEY_DIM, TOK_BLOCK), lambda i: (0, c(i) // spb)),
            pl.BlockSpec((GLA_KEY_DIM, TOK_BLOCK), lambda i: (0, c(i) // spb)),
            pl.BlockSpec((rows, D_MODEL), lambda i: (c(i), 0)),
            pl.BlockSpec((rows, D_MODEL), lambda i: (c(i), COL_R // D_MODEL)),
            pl.BlockSpec((1, GLA_HEAD_V), lambda i: (0, 0)),
        ],
        out_specs=[
            pl.BlockSpec((1, rows, D_MODEL), lambda i: (c(i), 0, 0)),
            pl.BlockSpec((sb, GLA_KEY_DIM, GLA_HEAD_V), lambda i: (c(i), 0, 0)),
        ],
        out_shape=[
            jax.ShapeDtypeStruct((nseq // sb, rows, D_MODEL), BF16),
            jax.ShapeDtypeStruct((nseq, GLA_KEY_DIM, GLA_HEAD_V), F32),
        ],
        args=(st3, qe, proj2, kdt, elt, oin, proj2, p["gla_norm"]),
        name="gla_step_state",
        hosted=hosted,
    )
    return o.reshape(ntok, D_MODEL), s_new.reshape(nseq, GLA_HEADS, GLA_HEAD_K, GLA_HEAD_V), hosted_out


def _softmax_rows(sc):
    e = jnp.exp(sc - jnp.max(sc, axis=-1, keepdims=True))
    return e / jnp.sum(e, axis=-1, keepdims=True)


def _cross_block_kernel(hn_ref, h_ref, k_ref, v_ref, wq_ref, wo_ref, g_ref, h2_ref, hn2_ref):
    q16 = _dot(hn_ref[...], wq_ref[...]).astype(BF16)
    outs = []
    for h in range(CROSS_HEADS):
        hs = slice(h * CROSS_HEAD_DIM, (h + 1) * CROSS_HEAD_DIM)
        sc = _dot_nt(q16[:, hs], k_ref[0, :, hs].astype(BF16)) * (CROSS_HEAD_DIM ** -0.5)
        outs.append(_dot(_softmax_rows(sc).astype(BF16), v_ref[0, :, hs].astype(BF16)))
    att = jnp.concatenate(outs, axis=1).astype(BF16)
    h2 = h_ref[...] + _dot(att, wo_ref[...])
    h2_ref[...] = h2
    hn2_ref[...] = _rms(h2, g_ref[...]).astype(hn2_ref.dtype)


def cross_block(hn, h, mem_k, mem_v, w_cq, w_co, g, seq):
    ntok, d = hn.shape
    n_mem = mem_k.shape[1]
    tl = min(seq, 512)
    lt = seq // tl
    row_tile = pl.BlockSpec((tl, d), lambda i: (i, 0))
    resident = lambda shape: pl.BlockSpec(shape, lambda i: (0,) * len(shape), pipeline_mode=pl.Buffered(1))
    kv_spec = pl.BlockSpec((1, n_mem, d), lambda i: (i // lt, 0, 0))
    return pl.pallas_call(
        _cross_block_kernel,
        grid=(ntok // tl,),
        in_specs=[row_tile, row_tile, kv_spec, kv_spec, resident((d, d)), resident((d, d)),
                  pl.BlockSpec((1, d), lambda i: (0, 0))],
        out_specs=[row_tile, row_tile],
        out_shape=[jax.ShapeDtypeStruct((ntok, d), F32), jax.ShapeDtypeStruct((ntok, d), BF16)],
        compiler_params=_cp("parallel"),
        name="cross_block",
    )(hn, h, mem_k, mem_v, w_cq, w_co, g.reshape(1, d))


def _xattn_step_kernel(q_ref, k_ref, v_ref, o_ref, *, nseq, tl):
    rows = nseq * tl
    n_mem = k_ref.shape[1]
    q = q_ref[0]
    qs = jnp.concatenate([q[:, h * CROSS_HEAD_DIM : (h + 1) * CROSS_HEAD_DIM] for h in range(CROSS_HEADS)],
                         axis=0).astype(BF16)
    shape = (CROSS_HEADS * rows, n_mem * CROSS_HEADS)
    col_head = lax.broadcasted_iota(jnp.int32, shape, 1) % CROSS_HEADS
    row_head = lax.broadcasted_iota(jnp.int32, shape, 0) // rows
    same_head = col_head == row_head
    rsel = lax.broadcasted_iota(jnp.int32, (CROSS_HEADS * rows, CROSS_HEAD_DIM), 0) % rows
    out = jnp.zeros((CROSS_HEADS * rows, CROSS_HEAD_DIM), F32)
    for s in range(nseq):
        kall = k_ref[s].reshape(n_mem * CROSS_HEADS, CROSS_HEAD_DIM).astype(BF16)
        vall = v_ref[s].reshape(n_mem * CROSS_HEADS, CROSS_HEAD_DIM).astype(BF16)
        sc = jnp.where(same_head, _dot_nt(qs, kall) * (CROSS_HEAD_DIM ** -0.5), NEG_BIG)
        oh = _dot(_softmax_rows(sc).astype(BF16), vall)
        out = jnp.where((rsel >= tl * s) & (rsel < tl * (s + 1)), oh, out)
    o_ref[0] = jnp.concatenate([out[h * rows : (h + 1) * rows] for h in range(CROSS_HEADS)],
                               axis=1).astype(o_ref.dtype)


def cross_attend(q2, mem_k, mem_v, nb, seq, hosted=None):
    n_mem = mem_k.shape[1]
    nseq, tl = (8 if hosted is not None else 16) // seq, seq
    rows = nseq * tl
    nblk = nb * seq // rows
    q3 = q2.reshape(nblk, rows, D_MODEL)
    c = lambda i: jnp.minimum(i, nblk - 1)
    kv_spec = pl.BlockSpec((nseq, n_mem, CROSS_HEADS, CROSS_HEAD_DIM), lambda i: (c(i), 0, 0, 0))
    (o,), hosted_out = call_with_hosted_matmul(
        functools.partial(_xattn_step_kernel, nseq=nseq, tl=tl),
        nblk,
        in_specs=[pl.BlockSpec((1, rows, D_MODEL), lambda i: (c(i), 0, 0)), kv_spec, kv_spec],
        out_specs=[pl.BlockSpec((1, rows, D_MODEL), lambda i: (c(i), 0, 0))],
        out_shape=[jax.ShapeDtypeStruct((nblk, rows, D_MODEL), BF16)],
        args=(q3, mem_k, mem_v),
        name="cross_attend",
        hosted=hosted,
    )
    return o.reshape(nb * seq, D_MODEL), hosted_out


PACK_TILE = 512


SUBLANE = 8


def _cast_rows_kernel(w_ref, o_ref, *, nrows):
    j = pl.program_id(0)
    x = w_ref[...]
    row = lax.broadcasted_iota(jnp.int32, x.shape, 0) + j * PACK_TILE
    o_ref[...] = jnp.where(row < nrows, x, 0.0).astype(o_ref.dtype)


def cast_rows(wt, lo, hi, n_out, name):
    kdim = wt.shape[1]
    assert lo % SUBLANE == 0 and lo + n_out <= wt.shape[0]
    return pl.pallas_call(
        functools.partial(_cast_rows_kernel, nrows=hi - lo),
        grid=(n_out // PACK_TILE,),
        in_specs=[pl.BlockSpec((pl.Element(PACK_TILE), pl.Element(kdim)),
                               lambda j: (pl.multiple_of(lo + j * PACK_TILE, SUBLANE), 0))],
        out_specs=pl.BlockSpec((PACK_TILE, kdim), lambda j: (j, 0)),
        out_shape=jax.ShapeDtypeStruct((n_out, kdim), BF16),
        compiler_params=_cp("parallel"),
        name=name,
    )(wt)


def _split_w_in(w_in):
    wt = w_in.T
    ssd_end = D_MODEL + CONV_DIM + SSD_HEADS
    gla_end = ssd_end + 2 * GLA_KEY_DIM + 2 * D_MODEL + GATE_RANK
    return (cast_rows(wt, 0, ssd_end, N_SSD_PROJ, "pack_ssd_in"),
            cast_rows(wt, ssd_end, gla_end, N_GLA_PROJ, "pack_gla_in"),
            cast_rows(wt, gla_end, wt.shape[0], 2 * D_MODEL, "pack_gate_in"))


def _params(ssd_dt_bias, ssd_A_log, ssd_D, ssd_norm, w_gla_gate, b_gla_gate, gla_norm):
    padv = lambda a: jnp.pad(a.astype(F32), (0, LANE - a.shape[0])).reshape(1, LANE)
    head_of_chan = jnp.arange(D_MODEL, dtype=jnp.int32) // SSD_HEAD_DIM
    e_head = (jnp.arange(LANE, dtype=jnp.int32)[:, None] == head_of_chan[None, :]).astype(BF16)
    group_of_bc = jnp.arange(BC_W, dtype=jnp.int32) // D_STATE
    lane_h = jnp.arange(LANE, dtype=jnp.int32)
    g_head = ((lane_h[None, :] // (SSD_HEADS // N_GROUPS) == group_of_bc[:, None])
              & (lane_h[None, :] < SSD_HEADS)).astype(BF16)
    khead = jnp.arange(GLA_KEY_DIM, dtype=jnp.int32) // GLA_HEAD_K
    vhead = jnp.arange(D_MODEL, dtype=jnp.int32) // GLA_HEAD_V
    g_val = (khead[:, None] == vhead[None, :]).astype(BF16)
    return dict(
        dt_bias=padv(ssd_dt_bias), a_log=padv(ssd_A_log),
        d_exp=jnp.repeat(ssd_D.astype(F32), SSD_HEAD_DIM).reshape(1, D_MODEL),
        ssd_norm=ssd_norm.astype(F32).reshape(1, D_MODEL),
        e_head=e_head, g_head=g_head, g_val=g_val,
        w_gate=jnp.pad(w_gla_gate, ((0, LANE - GATE_RANK), (0, 0))).astype(BF16),
        b_gate=b_gla_gate.astype(F32).reshape(1, GLA_KEY_DIM),
        gla_norm=gla_norm.astype(F32).reshape(1, GLA_HEAD_V),
    )


def _layer(x3, mem_k, mem_v, ssd_conv, ssd_state, gla_state, ffn_conv, w, p, long_seq, projections=None,
           other_xn=None):
    nb, seq, d = x3.shape
    ntok = nb * seq
    ffn = w["w_down"].shape[0]
    x2 = x3.reshape(ntok, d)
    w_ssd_in, w_gla_in, w_gate_in = w["w_in"]
    if projections is None:
        xn = rmsnorm_cast(x2, w["norm_mix"])
        in_proj = functools.partial(matmul, xn, tm=2048, w_rows_are_outputs=True)
        proj_ssd = in_proj(w_ssd_in, name="in_proj_ssd")
        proj_gla = in_proj(w_gla_in, name="in_proj_gla")
        gates = in_proj(w_gate_in, name="in_proj_gates")
    else:
        proj_ssd, proj_gla, gates = projections
    hosted = (lambda wt: HostedMatmul(other_xn, wt)) if other_xn is not None else (lambda wt: None)
    other_proj = None
    if long_seq:
        u, ssd_new, ssd_conv_new = ssd_scan(proj_ssd.reshape(nb, seq, N_SSD_PROJ), ssd_conv,
                                            w["ssd_conv_w"], w["ssd_conv_b"], p)
        o, gla_new = gla_scan(proj_gla.reshape(nb, seq, N_GLA_PROJ), p)
    else:
        xc2, ssd_conv_new = short_conv(proj_ssd, seq, [COL_XBC], CONV_DIM, [ssd_conv], [w["ssd_conv_w"]],
                                       [w["ssd_conv_b"]], SSD_CONV, False, F32, "ssd_conv")
        u, ssd_new, other_ssd = ssd_step(xc2, proj_ssd, ssd_state, p, seq, hosted(w_ssd_in))
        o, gla_new, other_gates = gla_step(proj_gla, gla_state, p, seq, hosted(w_gate_in))
    merged = merge_branches(u, o, w["w_ssd_out"], w["w_gla_out"], gates)
    h, hn = mm_res_norm(merged, w["w_mix_out"], x2, w["norm_cross"], True, BF16, "mix_out")
    if long_seq:
        h2, hn2 = cross_block(hn, h, mem_k, mem_v, w["w_cq"], w["w_co"], w["norm_ffn"], seq)
    else:
        qc = matmul(hn, w["w_cq"], name="cross_q")
        att, other_gla = cross_attend(qc, mem_k, mem_v, nb, seq, hosted(w_gla_in))
        other_proj = (other_ssd, other_gla, other_gates)
        h2, hn2 = mm_res_norm(att, w["w_co"], h, w["norm_ffn"], True, BF16, "cross_out")
    cw, cbias = w["ffn_conv_w"], w["ffn_conv_b"]
    if long_seq:
        act, ffn_conv_new = ffn_up_conv_act(hn2.reshape(nb, seq, d), w["w_up"], cw, cbias, ffn_conv)
    else:
        up = matmul(hn2, w["w_up"], name="ffn_up")
        act, fa, fg = short_conv(up, seq, [0, ffn], ffn, [ffn_conv[:, :, :ffn], ffn_conv[:, :, ffn:]],
                                 [cw[:, :ffn], cw[:, ffn:]], [cbias[:ffn], cbias[ffn:]],
                                 FFN_CONV, True, BF16, "ffn_conv")
        ffn_conv_new = jnp.concatenate([fa, fg], axis=-1)
    y = mm_res_norm(act.reshape(ntok, ffn), w["w_down"], h2, w["norm_final"], False, F32, "ffn_down")
    return y.reshape(nb, seq, d), ssd_conv_new, ssd_new, gla_new, ffn_conv_new, other_proj


def kernel(x_prompt, x_sample, cache_mem_k, cache_mem_v, state_ssd_conv, state_ssd, state_gla, state_ffn_conv, mem_prompt, norm_mix, w_in, ssd_conv_w, ssd_conv_b, ssd_dt_bias, ssd_A_log, ssd_D, ssd_norm, w_ssd_out, w_gla_gate, b_gla_gate, gla_norm, w_gla_out, w_mix_out, norm_cross, norm_mem, w_cq, w_ck, w_cv, w_co, norm_ffn, w_up, ffn_conv_w, ffn_conv_b, w_down, norm_final):
    nb, seq, d = x_prompt.shape
    n_mem = mem_prompt.shape[1]
    ffn2 = w_up.shape[1]
    w = dict(
        norm_mix=norm_mix, norm_cross=norm_cross, norm_ffn=norm_ffn, norm_final=norm_final,
        w_in=_split_w_in(w_in), ssd_conv_w=ssd_conv_w, ssd_conv_b=ssd_conv_b,
        w_ssd_out=w_ssd_out.astype(BF16), w_gla_out=w_gla_out.astype(BF16), w_mix_out=w_mix_out.astype(BF16),
        w_cq=w_cq.astype(BF16), w_co=w_co.astype(BF16), w_up=w_up.astype(BF16), w_down=w_down.astype(BF16),
        ffn_conv_w=ffn_conv_w, ffn_conv_b=ffn_conv_b,
    )
    p = _params(ssd_dt_bias, ssd_A_log, ssd_D, ssd_norm, w_gla_gate, b_gla_gate, gla_norm)

    mn = rmsnorm_cast(mem_prompt.reshape(nb * n_mem, d), norm_mem)
    p_mem_k = matmul(mn, w_ck.astype(BF16), name="mem_k").reshape(nb, n_mem, d)
    p_mem_v = matmul(mn, w_cv.astype(BF16), name="mem_v").reshape(nb, n_mem, d)
    zeros_ssd_conv = jnp.zeros((nb, SSD_CONV - 1, CONV_DIM), F32)
    zeros_ffn_conv = jnp.zeros((nb, FFN_CONV - 1, ffn2), F32)
    xn_prompt = rmsnorm_cast(x_prompt.reshape(nb * seq, d), norm_mix)

    y_sample, s_ssd_conv, s_ssd, s_gla, s_ffn_conv, prompt_proj = _layer(
        x_sample, cache_mem_k, cache_mem_v, state_ssd_conv, state_ssd, state_gla, state_ffn_conv, w, p, False,
        other_xn=xn_prompt)

    y_prompt, p_ssd_conv, p_ssd, p_gla, p_ffn_conv, _ = _layer(
        x_prompt, p_mem_k, p_mem_v, zeros_ssd_conv, None, None, zeros_ffn_conv, w, p, True,
        projections=prompt_proj)

    head_shape = (n_mem, CROSS_HEADS, CROSS_HEAD_DIM)
    return (y_prompt, y_sample, p_ssd_conv, p_ssd, p_gla, p_ffn_conv,
            p_mem_k.reshape((nb,) + head_shape), p_mem_v.reshape((nb,) + head_shape),
            s_ssd_conv, s_ssd, s_gla, s_ffn_conv)
```

```python
import dataclasses
import functools
from typing import NamedTuple

import jax
import jax.numpy as jnp
from jax import lax
from jax.experimental import pallas as pl
from jax.experimental.pallas import tpu as pltpu

F32 = jnp.float32
BF16 = jnp.bfloat16
EPS = 1e-6
NEG_BIG = -1e30

D_MODEL = 2048
SSD_HEAD_DIM = 64
SSD_HEADS = 32
D_STATE = 128
N_GROUPS = 4
GROUP_W = D_MODEL // N_GROUPS
BC_W = N_GROUPS * D_STATE
CONV_DIM = D_MODEL + 2 * BC_W
SSD_CONV = 4
SSD_CHUNK = 128
GLA_HEADS = 4
GLA_KEY_DIM = 1024
GLA_HEAD_K = 256
GLA_HEAD_V = 512
GATE_RANK = 16
GATE_TAU = 16.0
GLA_CHUNK = 64
CROSS_HEADS = 4
CROSS_HEAD_DIM = 512
FFN_CONV = 3
LANE = 128

COL_Z, COL_XBC, COL_DT, N_SSD_PROJ = 0, 2048, 5120, 5632
COL_Q, COL_K, COL_V, COL_R, COL_GLR, N_GLA_PROJ = 0, 1024, 2048, 4096, 6144, 6656
COL_GA, COL_GB = 0, 2048

VMEM_LIMIT = 56 * 1024 * 1024


def _cp(*sem):
    return pltpu.CompilerParams(dimension_semantics=sem, vmem_limit_bytes=VMEM_LIMIT)


def _dot(a, b, prec=None):
    return jnp.dot(a, b, preferred_element_type=F32, precision=prec)


def _dot_nt(a, b):
    return lax.dot_general(a, b, (((1,), (1,)), ((), ())), preferred_element_type=F32)


def _split3(x):
    x1 = x.astype(BF16)
    r1 = x - x1.astype(F32)
    x2 = r1.astype(BF16)
    x3 = (r1 - x2.astype(F32)).astype(BF16)
    return x1, x2, x3


def _dot_sel(x, sel, sel_first=False):
    parts = _split3(x)
    if sel_first:
        return _dot(sel, parts[0]) + _dot(sel, parts[1]) + _dot(sel, parts[2])
    return _dot(parts[0], sel) + _dot(parts[1], sel) + _dot(parts[2], sel)


def _sigmoid(x):
    return 1.0 / (1.0 + jnp.exp(-x))


def _silu(x):
    return x * _sigmoid(x)


def _softplus(x):
    return jnp.maximum(x, 0.0) + jnp.log(1.0 + jnp.exp(-jnp.abs(x)))


def _rms(x, g):
    ms = jnp.mean(x * x, axis=-1, keepdims=True)
    return x * lax.rsqrt(ms + EPS) * g


def _rmsnorm_kernel(x_ref, g_ref, o_ref):
    o_ref[...] = _rms(x_ref[...], g_ref[...]).astype(o_ref.dtype)


def rmsnorm_cast(x2, g):
    m, d = x2.shape
    tm = min(m, 512)
    return pl.pallas_call(
        _rmsnorm_kernel,
        grid=(m // tm,),
        in_specs=[pl.BlockSpec((tm, d), lambda i: (i, 0)), pl.BlockSpec((1, d), lambda i: (0, 0))],
        out_specs=pl.BlockSpec((tm, d), lambda i: (i, 0)),
        out_shape=jax.ShapeDtypeStruct((m, d), BF16),
        compiler_params=_cp("parallel"),
        name="rmsnorm_cast",
    )(x2, g.reshape(1, d))


def _mm_kernel(a_ref, w_ref, o_ref, *, w_rows_are_outputs):
    dot = _dot_nt if w_rows_are_outputs else _dot
    o_ref[...] = dot(a_ref[...], w_ref[...]).astype(o_ref.dtype)


def matmul(a, w, out_dtype=F32, tm=1024, tn=512, w_rows_are_outputs=False, name="matmul"):
    m, k = a.shape
    n = w.shape[0] if w_rows_are_outputs else w.shape[1]
    tm = min(m, tm)
    if w_rows_are_outputs:
        w_spec = pl.BlockSpec((tn, k), lambda i, j: (j, 0))
    else:
        w_spec = pl.BlockSpec((k, tn), lambda i, j: (0, j))
    return pl.pallas_call(
        functools.partial(_mm_kernel, w_rows_are_outputs=w_rows_are_outputs),
        grid=(m // tm, n // tn),
        in_specs=[pl.BlockSpec((tm, k), lambda i, j: (i, 0)), w_spec],
        out_specs=pl.BlockSpec((tm, tn), lambda i, j: (i, j)),
        out_shape=jax.ShapeDtypeStruct((m, n), out_dtype),
        compiler_params=_cp("parallel", "arbitrary"),
        name=name,
    )(a, w)


class HostedMatmul(NamedTuple):
    a: jax.Array
    w: jax.Array
    tm: int = 1024
    tn: int = 512


def call_with_hosted_matmul(body, nsteps, in_specs, out_specs, out_shape, args, name, hosted=None):
    if hosted is None:
        outs = pl.pallas_call(body, grid=(nsteps,), in_specs=in_specs, out_specs=out_specs, out_shape=out_shape,
                              compiler_params=_cp("arbitrary"), name=name)(*args)
        return list(outs), None
    m, k = hosted.a.shape
    n = hosted.w.shape[0]
    tm, tn = min(m, hosted.tm), hosted.tn
    nj = n // tn
    hsteps = (m // tm) * nj
    total = max(nsteps, hsteps)
    hc = lambda i: jnp.minimum(i, hsteps - 1)
    h_in = [pl.BlockSpec((tm, k), lambda i: (hc(i) // nj, 0)), pl.BlockSpec((tn, k), lambda i: (hc(i) % nj, 0))]
    h_out = pl.BlockSpec((tm, tn), lambda i: (hc(i) // nj, hc(i) % nj))
    n_in, n_out = len(in_specs), len(out_specs)

    def fused(*refs):
        g_in = refs[:n_in]
        ha_ref, hw_ref = refs[n_in : n_in + 2]
        g_out = refs[n_in + 2 : n_in + 2 + n_out]
        ho_ref = refs[n_in + 2 + n_out]
        step = pl.program_id(0)

        def guest():
            body(*g_in, *g_out)

        def host():
            ho_ref[...] = _dot_nt(ha_ref[...], hw_ref[...])

        guest() if nsteps == total else pl.when(step < nsteps)(guest)
        host() if hsteps == total else pl.when(step < hsteps)(host)

    outs = pl.pallas_call(
        fused,
        grid=(total,),
        in_specs=list(in_specs) + h_in,
        out_specs=list(out_specs) + [h_out],
        out_shape=list(out_shape) + [jax.ShapeDtypeStruct((m, n), F32)],
        compiler_params=_cp("arbitrary"),
        name=name,
    )(*args, hosted.a, hosted.w)
    return list(outs[:-1]), outs[-1]


def _mm_res_norm_kernel(a_ref, w_ref, res_ref, g_ref, *out_refs):
    h = res_ref[...] + _dot(a_ref[...], w_ref[...])
    if len(out_refs) == 2:
        out_refs[0][...] = h
    out_refs[-1][...] = _rms(h, g_ref[...]).astype(out_refs[-1].dtype)


def mm_res_norm(a, w, res, g, emit_h, norm_dtype, name):
    m, kdim = a.shape
    n = w.shape[1]
    tm = min(m, 512)
    row_tile = lambda width: pl.BlockSpec((tm, width), lambda i: (i, 0))
    out_shape = [jax.ShapeDtypeStruct((m, n), norm_dtype)]
    out_specs = [row_tile(n)]
    if emit_h:
        out_shape = [jax.ShapeDtypeStruct((m, n), F32)] + out_shape
        out_specs = [row_tile(n)] + out_specs
    outs = pl.pallas_call(
        _mm_res_norm_kernel,
        grid=(m // tm,),
        in_specs=[row_tile(kdim), pl.BlockSpec((kdim, n), lambda i: (0, 0), pipeline_mode=pl.Buffered(1)),
                  row_tile(n), pl.BlockSpec((1, n), lambda i: (0, 0))],
        out_specs=out_specs,
        out_shape=out_shape,
        compiler_params=_cp("parallel"),
        name=name,
    )(a, w, res, g.reshape(1, n))
    return outs if emit_h else outs[0]


def _merge_kernel(u_ref, o_ref, wa_ref, wb_ref, ga_ref, gb_ref, out_ref):
    a = _dot(u_ref[...], wa_ref[...])
    b = _dot(o_ref[...], wb_ref[...])
    out_ref[...] = (_sigmoid(ga_ref[...]) * a + _sigmoid(gb_ref[...]) * b).astype(out_ref.dtype)


def merge_branches(u, o, wa, wb, proj):
    m, d = u.shape
    tm, tn = min(m, 1024), 512
    return pl.pallas_call(
        _merge_kernel,
        grid=(m // tm, d // tn),
        in_specs=[
            pl.BlockSpec((tm, d), lambda i, j: (i, 0)),
            pl.BlockSpec((tm, d), lambda i, j: (i, 0)),
            pl.BlockSpec((d, tn), lambda i, j: (0, j)),
            pl.BlockSpec((d, tn), lambda i, j: (0, j)),
            pl.BlockSpec((tm, tn), lambda i, j: (i, COL_GA // tn + j)),
            pl.BlockSpec((tm, tn), lambda i, j: (i, COL_GB // tn + j)),
        ],
        out_specs=pl.BlockSpec((tm, tn), lambda i, j: (i, j)),
        out_shape=jax.ShapeDtypeStruct((m, d), BF16),
        compiler_params=_cp("parallel", "arbitrary"),
        name="merge_branches",
    )(u, o, wa, wb, proj, proj)


CONV_PAD = 8


def _short_conv_kernel(*refs, taps, seq, nstreams, swiglu):
    ins = refs[: 4 * nstreams]
    out_ref = refs[4 * nstreams]
    ns_refs = refs[4 * nstreams + 1 :]
    vals = []
    for s in range(nstreams):
        u_ref, st_ref, w_ref, b_ref = ins[4 * s : 4 * s + 4]
        nseq, _, tc = st_ref.shape
        full = jnp.concatenate([st_ref[...], u_ref[...].reshape(nseq, seq, tc)], axis=1)
        acc = b_ref[...]
        for k in range(taps):
            acc = acc + full[:, k : k + seq] * w_ref[k : k + 1, :]
        vals.append(acc)
        ns_refs[s][...] = full[:, seq : seq + taps - 1]
    out = _silu(vals[1]) * vals[0] if swiglu else _silu(vals[0])
    out_ref[...] = out.reshape(out_ref.shape).astype(out_ref.dtype)


def short_conv(u2, seq, col_offs, width, states, ws, bs, taps, swiglu, out_dtype, name):
    ntok = u2.shape[0]
    nseq = ntok // seq
    tc = 512
    nstreams = len(col_offs)
    in_specs, args = [], []
    for s in range(nstreams):
        cb = col_offs[s] // tc
        in_specs += [
            pl.BlockSpec((ntok, tc), lambda c, cb=cb: (0, cb + c)),
            pl.BlockSpec((nseq, taps - 1, tc), lambda c: (0, 0, c)),
            pl.BlockSpec((taps, tc), lambda c: (0, c)),
            pl.BlockSpec((1, tc), lambda c: (0, c)),
        ]
        args += [u2, states[s], ws[s], bs[s].reshape(1, width)]
    out_specs = [pl.BlockSpec((ntok, tc), lambda c: (0, c))]
    out_shape = [jax.ShapeDtypeStruct((ntok, width), out_dtype)]
    for s in range(nstreams):
        out_specs.append(pl.BlockSpec((nseq, taps - 1, tc), lambda c: (0, 0, c)))
        out_shape.append(jax.ShapeDtypeStruct((nseq, taps - 1, width), F32))
    return pl.pallas_call(
        functools.partial(_short_conv_kernel, taps=taps, seq=seq, nstreams=nstreams, swiglu=swiglu),
        grid=(width // tc,),
        in_specs=in_specs,
        out_specs=out_specs,
        out_shape=out_shape,
        compiler_params=_cp("parallel"),
        name=name,
    )(*args)


def _shift_rows(u, d, prev):
    x = pltpu.roll(u, d, 0)
    r = lax.broadcasted_iota(jnp.int32, prev.shape, 0)
    head = jnp.where(r < d, pltpu.roll(prev, d, 0), x[:CONV_PAD])
    return jnp.concatenate([head, x[CONV_PAD:]], axis=0)


def _causal_taps(u, prev, cw, bias):
    taps = cw.shape[0]
    acc = bias
    for k in range(taps):
        d = taps - 1 - k
        acc = acc + (_shift_rows(u, d, prev) if d else u) * cw[k : k + 1, :]
    return acc


class RiderKernel(NamedTuple):
    body: object
    nsteps: int
    in_specs: list
    out_specs: list
    out_shape: list
    args: tuple


def _ffn_up_kernel(*refs, tm, rider_body, n_rider_in, n_rider_out, rider_steps):
    n_in = 9
    hn_ref, wa_ref, wg_ref, cwa_ref, cwg_ref, cba_ref, cbg_ref, sta_ref, stg_ref = refs[:n_in]
    rider_in = refs[n_in : n_in + n_rider_in]
    act_ref, nsa_ref, nsg_ref = refs[n_in + n_rider_in : n_in + n_rider_in + 3]
    rider_out = refs[n_in + n_rider_in + 3 : n_in + n_rider_in + 3 + n_rider_out]
    preva, prevg = refs[-2:]
    m = pl.program_id(2)
    nm = pl.num_programs(2)
    lo = CONV_PAD - (FFN_CONV - 1)

    @pl.when(m == 0)
    def _():
        for prev, st_ref in ((preva, sta_ref), (prevg, stg_ref)):
            prev[0:lo, :] = jnp.zeros((lo, prev.shape[1]), F32)
            prev[lo:CONV_PAD, :] = st_ref[0]

    hn = hn_ref[0]
    vals = []
    for w_ref, cw_ref, cb_ref, prev in ((wa_ref, cwa_ref, cba_ref, preva), (wg_ref, cwg_ref, cbg_ref, prevg)):
        u = _dot(hn, w_ref[...])
        vals.append(_causal_taps(u, prev[...], cw_ref[...], cb_ref[...]))
        prev[...] = u[tm - CONV_PAD : tm]
    act_ref[0] = (_silu(vals[1]) * vals[0]).astype(act_ref.dtype)

    @pl.when(m == nm - 1)
    def _():
        nsa_ref[0] = preva[lo:CONV_PAD, :]
        nsg_ref[0] = prevg[lo:CONV_PAD, :]

    if rider_body is not None:
        step = (pl.program_id(0) * pl.num_programs(1) + pl.program_id(1)) * nm + m

        @pl.when(step < rider_steps)
        def _():
            rider_body(*rider_in, *rider_out)


def ffn_up_conv_act(hn3, w_up, conv_w, conv_b, conv_state, rider=None):
    nb, seq, d = hn3.shape
    ffn = w_up.shape[1] // 2
    tm, tn = min(seq, 1024), 512
    nn, nm = ffn // tn, seq // tm
    half = lambda off: (lambda b, n, m: (0, off + n))
    st_spec = lambda off: pl.BlockSpec((1, FFN_CONV - 1, tn), lambda b, n, m: (b, 0, off + n))
    cb2 = conv_b.reshape(1, 2 * ffn)
    in_specs = [
        pl.BlockSpec((1, tm, d), lambda b, n, m: (b, m, 0)),
        pl.BlockSpec((d, tn), half(0)), pl.BlockSpec((d, tn), half(nn)),
        pl.BlockSpec((FFN_CONV, tn), half(0)), pl.BlockSpec((FFN_CONV, tn), half(nn)),
        pl.BlockSpec((1, tn), half(0)), pl.BlockSpec((1, tn), half(nn)),
        st_spec(0), st_spec(nn),
    ]
    out_specs = [
        pl.BlockSpec((1, tm, tn), lambda b, n, m: (b, m, n)),
        pl.BlockSpec((1, FFN_CONV - 1, tn), lambda b, n, m: (b, 0, n)),
        pl.BlockSpec((1, FFN_CONV - 1, tn), lambda b, n, m: (b, 0, n)),
    ]
    out_shape = [
        jax.ShapeDtypeStruct((nb, seq, ffn), BF16),
        jax.ShapeDtypeStruct((nb, FFN_CONV - 1, ffn), F32),
        jax.ShapeDtypeStruct((nb, FFN_CONV - 1, ffn), F32),
    ]
    args = [hn3, w_up, w_up, conv_w, conv_w, cb2, cb2, conv_state, conv_state]
    rider_kw = dict(rider_body=None, n_rider_in=0, n_rider_out=0, rider_steps=0)
    if rider is not None and rider.nsteps > nb * nn * nm:
        rider_outs, _ = call_with_hosted_matmul(rider.body, rider.nsteps, rider.in_specs, rider.out_specs,
                                                rider.out_shape, rider.args, "rider")
        act, conv_new, _ = ffn_up_conv_act(hn3, w_up, conv_w, conv_b, conv_state)
        return act, conv_new, rider_outs
    if rider is not None:
        on_grid = lambda spec: dataclasses.replace(
            spec, index_map=lambda b, n, m, im=spec.index_map: im((b * nn + n) * nm + m))
        in_specs += [on_grid(s) for s in rider.in_specs]
        out_specs += [on_grid(s) for s in rider.out_specs]
        out_shape += list(rider.out_shape)
        args += list(rider.args)
        rider_kw = dict(rider_body=rider.body, n_rider_in=len(rider.in_specs), n_rider_out=len(rider.out_specs),
                        rider_steps=rider.nsteps)
    outs = pl.pallas_call(
        functools.partial(_ffn_up_kernel, tm=tm, **rider_kw),
        grid=(nb, nn, nm),
        in_specs=in_specs,
        out_specs=out_specs,
        out_shape=out_shape,
        scratch_shapes=[pltpu.VMEM((CONV_PAD, tn), F32), pltpu.VMEM((CONV_PAD, tn), F32)],
        compiler_params=_cp("parallel" if rider is None else "arbitrary", "parallel" if rider is None else "arbitrary",
                            "arbitrary"),
        name="ffn_up_conv_act",
    )(*args)
    act, nsa, nsg = outs[:3]
    return act, jnp.concatenate([nsa, nsg], axis=-1), list(outs[3:])


def _ssd_gate_norm(y, z, nrm):
    ug = y * _silu(z)
    outs = []
    for g in range(N_GROUPS):
        ugg = ug[:, g * GROUP_W : (g + 1) * GROUP_W]
        ms = jnp.mean(ugg * ugg, axis=-1, keepdims=True)
        outs.append(ugg * lax.rsqrt(ms + EPS))
    return jnp.concatenate(outs, axis=1) * nrm


def _gla_out_norm(o, r, nrm):
    outs = []
    for h in range(GLA_HEADS):
        oh = o[:, h * GLA_HEAD_V : (h + 1) * GLA_HEAD_V]
        rh = r[:, h * GLA_HEAD_V : (h + 1) * GLA_HEAD_V]
        outs.append(_rms(oh, nrm) * _silu(rh))
    return jnp.concatenate(outs, axis=1)


SCAN_ROWS_PER_STEP = 256


def _ssd_chunk(xs_ref, b_ref, c_ref, dt_ref, z_ref, cw_ref, cbias_ref, dtb_ref, alog_ref, dexp_ref, nrm_ref, e_ref,
               u_ref, st_ref, cprev):
    q = SSD_CHUNK
    conv = []
    for raw_ref, c0 in ((xs_ref, 0), (b_ref, D_MODEL), (c_ref, D_MODEL + BC_W)):
        raw = raw_ref[...]
        cols = slice(c0, c0 + raw.shape[1])
        conv.append(_silu(_causal_taps(raw, cprev[:, cols], cw_ref[:, cols], cbias_ref[:, cols])))
        cprev[:, cols] = raw[q - CONV_PAD : q]
    xs, bm, cm = conv
    dt = _softplus(dt_ref[...] + dtb_ref[...])
    a = dt * (-jnp.exp(alog_ref[...]))
    row = lax.broadcasted_iota(jnp.int32, (q, q), 0)
    col = lax.broadcasted_iota(jnp.int32, (q, q), 1)
    tril = row >= col
    acum = _dot_sel(a, tril.astype(BF16), sel_first=True)
    acum_t = acum.T
    dt_t = dt.T
    last = acum[q - 1 : q, :]
    e_mat = e_ref[...]
    eexp = _dot_sel(jnp.exp(acum), e_mat)
    wexp = _dot_sel(jnp.exp(last - acum) * dt, e_mat)
    s_bf = st_ref[...].astype(BF16)
    cb16 = cm.astype(BF16)
    bb16 = bm.astype(BF16)
    x16 = xs.astype(BF16)
    xw16 = (xs * wexp).astype(BF16)
    lane_lo = lax.broadcasted_iota(jnp.int32, (q, LANE), 1) < SSD_HEAD_DIM
    ys = []
    for g in range(N_GROUPS):
        cg = cb16[:, g * D_STATE : (g + 1) * D_STATE]
        bg = bb16[:, g * D_STATE : (g + 1) * D_STATE]
        cb = _dot_nt(cg, bg)
        yoff = _dot(cg, s_bf[:, g * GROUP_W : (g + 1) * GROUP_W])
        pieces = []
        for pr in range(GROUP_W // LANE):
            h0 = g * (SSD_HEADS // N_GROUPS) + 2 * pr
            xp = x16[:, h0 * SSD_HEAD_DIM : h0 * SSD_HEAD_DIM + LANE]
            yh = []
            for h in (h0, h0 + 1):
                diff = acum[:, h : h + 1] - acum_t[h : h + 1, :]
                dec = jnp.exp(jnp.where(tril, diff, NEG_BIG))
                m = (cb * dec * dt_t[h : h + 1, :]).astype(BF16)
                yh.append(_dot(m, xp))
            pieces.append(jnp.where(lane_lo, yh[0], yh[1]))
        sl = slice(g * GROUP_W, (g + 1) * GROUP_W)
        ys.append(jnp.concatenate(pieces, axis=1) + yoff * eexp[:, sl])
        bg_t = bm[:, g * D_STATE : (g + 1) * D_STATE].T.astype(BF16)
        upd = _dot(bg_t, xw16[:, sl])
        st_ref[:, sl] = eexp[q - 1 : q, sl] * st_ref[:, sl] + upd
    y = jnp.concatenate(ys, axis=1) + dexp_ref[...] * xs
    u_ref[...] = _ssd_gate_norm(y, z_ref[...], nrm_ref[...]).astype(u_ref.dtype)


def _ssd_scan_kernel(xs_ref, b_ref, c_ref, dt_ref, z_ref, cst_ref, cw_ref, cbias_ref,
                     dtb_ref, alog_ref, dexp_ref, nrm_ref, e_ref,
                     u_ref, sout_ref, cso_ref, st_ref, cprev, *, nsteps, nsub):
    step = pl.program_id(1)
    lo = CONV_PAD - (SSD_CONV - 1)

    @pl.when(step == 0)
    def _():
        st_ref[...] = jnp.zeros_like(st_ref)
        cprev[0:lo, :] = jnp.zeros((lo, CONV_DIM), F32)
        cprev[lo:CONV_PAD, :] = cst_ref[0]

    for sub in range(nsub):
        rows = lambda ref: ref.at[0, pl.ds(sub * SSD_CHUNK, SSD_CHUNK)]
        _ssd_chunk(rows(xs_ref), rows(b_ref), rows(c_ref), rows(dt_ref), rows(z_ref), cw_ref, cbias_ref,
                   dtb_ref, alog_ref, dexp_ref, nrm_ref, e_ref, rows(u_ref), st_ref, cprev)

    @pl.when(step == nsteps - 1)
    def _():
        sout_ref[0] = st_ref[...].T
        cso_ref[0] = cprev[lo:CONV_PAD, :]


def ssd_scan(proj3, conv_state, conv_w, conv_b, p):
    nb, seq, _ = proj3.shape
    nsub = SCAN_ROWS_PER_STEP // SSD_CHUNK if seq % SCAN_ROWS_PER_STEP == 0 else 1
    q = nsub * SSD_CHUNK
    nsteps = seq // q
    vec = lambda n: pl.BlockSpec((1, n), lambda b, c: (0, 0))
    u, s_out, conv_new = pl.pallas_call(
        functools.partial(_ssd_scan_kernel, nsteps=nsteps, nsub=nsub),
        grid=(nb, nsteps),
        in_specs=[
            pl.BlockSpec((1, q, D_MODEL), lambda b, c: (b, c, COL_XBC // D_MODEL)),
            pl.BlockSpec((1, q, BC_W), lambda b, c: (b, c, (COL_XBC + D_MODEL) // BC_W)),
            pl.BlockSpec((1, q, BC_W), lambda b, c: (b, c, (COL_XBC + D_MODEL) // BC_W + 1)),
            pl.BlockSpec((1, q, LANE), lambda b, c: (b, c, COL_DT // LANE)),
            pl.BlockSpec((1, q, D_MODEL), lambda b, c: (b, c, COL_Z // D_MODEL)),
            pl.BlockSpec((1, SSD_CONV - 1, CONV_DIM), lambda b, c: (b, 0, 0)),
            pl.BlockSpec((SSD_CONV, CONV_DIM), lambda b, c: (0, 0)),
            vec(CONV_DIM),
            vec(LANE), vec(LANE), vec(D_MODEL), vec(D_MODEL),
            pl.BlockSpec((LANE, D_MODEL), lambda b, c: (0, 0)),
        ],
        out_specs=[
            pl.BlockSpec((1, q, D_MODEL), lambda b, c: (b, c, 0)),
            pl.BlockSpec((1, D_MODEL, D_STATE), lambda b, c: (b, 0, 0)),
            pl.BlockSpec((1, SSD_CONV - 1, CONV_DIM), lambda b, c: (b, 0, 0)),
        ],
        out_shape=[
            jax.ShapeDtypeStruct((nb, seq, D_MODEL), BF16),
            jax.ShapeDtypeStruct((nb, D_MODEL, D_STATE), F32),
            jax.ShapeDtypeStruct((nb, SSD_CONV - 1, CONV_DIM), F32),
        ],
        scratch_shapes=[pltpu.VMEM((D_STATE, D_MODEL), F32), pltpu.VMEM((CONV_PAD, CONV_DIM), F32)],
        compiler_params=_cp("parallel", "arbitrary"),
        name="ssd_scan",
    )(proj3, proj3, proj3, proj3, proj3, conv_state, conv_w, conv_b.reshape(1, CONV_DIM),
      p["dt_bias"], p["a_log"], p["d_exp"], p["ssd_norm"], p["e_head"])
    return u.reshape(nb * seq, D_MODEL), s_out.reshape(nb, SSD_HEADS, SSD_HEAD_DIM, D_STATE), conv_new


def _gla_gate_log(glr, wg, bg):
    x = _dot(glr.astype(BF16), wg) + bg
    return -_softplus(-x) / GATE_TAU


def _gla_chunk(q_ref, k_ref, v_ref, r_ref, glr_ref, wg_ref, bg_ref, nrm_ref, o_ref, st_ref):
    q = GLA_CHUNK
    glog = _gla_gate_log(glr_ref[...], wg_ref[...], bg_ref[...])
    row = lax.broadcasted_iota(jnp.int32, (q, q), 0)
    col = lax.broadcasted_iota(jnp.int32, (q, q), 1)
    tril = row >= col
    bc = _dot_sel(glog, tril.astype(BF16), sel_first=True)
    last = bc[q - 1 : q, :]
    kk = k_ref[...]
    qe = q_ref[...] * (GLA_HEAD_K ** -0.5) * jnp.exp(bc)
    ke = kk * jnp.exp(-bc)
    kd = kk * jnp.exp(last - bc)
    elast = jnp.exp(last)
    v16 = v_ref[...].astype(BF16)
    zeros_v = jnp.zeros((q, GLA_HEAD_V), BF16)
    outs = []
    for h in range(GLA_HEADS):
        ks = slice(h * GLA_HEAD_K, (h + 1) * GLA_HEAD_K)
        vs = slice(h * GLA_HEAD_V, (h + 1) * GLA_HEAD_V)
        qh = qe[:, ks].astype(BF16)
        kh = ke[:, ks].astype(BF16)
        att = jnp.where(tril, _dot_nt(qh, kh), 0.0)
        s_h = st_ref[ks, :]
        outs.append(_dot(att.astype(BF16), v16[:, vs]) + _dot(qh, s_h.astype(BF16)))
        xt = jnp.concatenate([kd[:, ks], jnp.broadcast_to(elast[:, ks], (q, GLA_HEAD_K))], axis=0).T
        v2 = jnp.concatenate([v16[:, vs], zeros_v], axis=0)
        st_ref[ks, :] = xt[:, q : q + 1] * s_h + _dot(xt.astype(BF16), v2)
    o = jnp.concatenate(outs, axis=1)
    o_ref[...] = _gla_out_norm(o, r_ref[...], nrm_ref[...]).astype(o_ref.dtype)


def _gla_scan_kernel(q_ref, k_ref, v_ref, r_ref, glr_ref, wg_ref, bg_ref, nrm_ref,
                     o_ref, sout_ref, st_ref, *, nsteps, nsub):
    step = pl.program_id(1)

    @pl.when(step == 0)
    def _():
        st_ref[...] = jnp.zeros_like(st_ref)

    for sub in range(nsub):
        rows = lambda ref: ref.at[0, pl.ds(sub * GLA_CHUNK, GLA_CHUNK)]
        _gla_chunk(rows(q_ref), rows(k_ref), rows(v_ref), rows(r_ref), rows(glr_ref), wg_ref, bg_ref, nrm_ref,
                   rows(o_ref), st_ref)

    @pl.when(step == nsteps - 1)
    def _():
        sout_ref[0] = st_ref[...]


def gla_scan(proj3, p):
    nb, seq, _ = proj3.shape
    nsub = SCAN_ROWS_PER_STEP // GLA_CHUNK if seq % SCAN_ROWS_PER_STEP == 0 else 1
    q = nsub * GLA_CHUNK
    nsteps = seq // q
    o, s_out = pl.pallas_call(
        functools.partial(_gla_scan_kernel, nsteps=nsteps, nsub=nsub),
        grid=(nb, nsteps),
        in_specs=[
            pl.BlockSpec((1, q, GLA_KEY_DIM), lambda b, c: (b, c, COL_Q // GLA_KEY_DIM)),
            pl.BlockSpec((1, q, GLA_KEY_DIM), lambda b, c: (b, c, COL_K // GLA_KEY_DIM)),
            pl.BlockSpec((1, q, D_MODEL), lambda b, c: (b, c, COL_V // D_MODEL)),
            pl.BlockSpec((1, q, D_MODEL), lambda b, c: (b, c, COL_R // D_MODEL)),
            pl.BlockSpec((1, q, LANE), lambda b, c: (b, c, COL_GLR // LANE)),
            pl.BlockSpec((LANE, GLA_KEY_DIM), lambda b, c: (0, 0)),
            pl.BlockSpec((1, GLA_KEY_DIM), lambda b, c: (0, 0)),
            pl.BlockSpec((1, GLA_HEAD_V), lambda b, c: (0, 0)),
        ],
        out_specs=[
            pl.BlockSpec((1, q, D_MODEL), lambda b, c: (b, c, 0)),
            pl.BlockSpec((1, GLA_KEY_DIM, GLA_HEAD_V), lambda b, c: (b, 0, 0)),
        ],
        out_shape=[
            jax.ShapeDtypeStruct((nb, seq, D_MODEL), BF16),
            jax.ShapeDtypeStruct((nb, GLA_KEY_DIM, GLA_HEAD_V), F32),
        ],
        scratch_shapes=[pltpu.VMEM((GLA_KEY_DIM, GLA_HEAD_V), F32)],
        compiler_params=_cp("parallel", "arbitrary"),
        name="gla_scan",
    )(proj3, proj3, proj3, proj3, proj3, p["w_gate"], p["b_gate"], p["gla_norm"])
    return o.reshape(nb * seq, D_MODEL), s_out.reshape(nb, GLA_HEADS, GLA_HEAD_K, GLA_HEAD_V)


TOK_BLOCK = 128


def _row_shift(x, d, tpos):
    return jnp.where(tpos >= d, pltpu.roll(x, d, 0), 0.0)


def _seq_cumsum_and_last(a, seq, tpos):
    nrows = a.shape[0]
    acum = a
    for d in range(1, seq):
        acum = acum + _row_shift(a, d, tpos)
    last = jnp.where(tpos == seq - 1, acum, 0.0)
    for d in range(1, seq):
        last = last + jnp.where(tpos == seq - 1 - d, pltpu.roll(acum, nrows - d, 0), 0.0)
    return acum, last


def _ssd_step_pre_kernel(xs_ref, b_ref, c_ref, dt_ref, dtb_ref, alog_ref, dexp_ref, e_ref, gh_ref,
                         ypart_ref, eexp_ref, xwt_ref, el_ref, *, seq):
    nrows = xs_ref.shape[0]
    xs = xs_ref[...]
    bm = b_ref[...]
    cm = c_ref[...]
    dt = _softplus(dt_ref[...] + dtb_ref[...])
    a = dt * (-jnp.exp(alog_ref[...]))
    pos = lambda w: lax.broadcasted_iota(jnp.int32, (nrows, w), 0) % seq
    t_h, t_c, t_x = pos(LANE), pos(BC_W), pos(D_MODEL)
    acum, last = _seq_cumsum_and_last(a, seq, t_h)
    e_mat = e_ref[...]
    eexp_ref[...] = _dot_sel(jnp.exp(acum), e_mat)
    wexp = _dot_sel(jnp.exp(last - acum) * dt, e_mat)
    xwt_ref[...] = (xs * wexp).T.astype(xwt_ref.dtype)
    el_ref[...] = jnp.exp(last)
    y = dexp_ref[...] * xs
    for d in range(seq):
        if d == 0:
            cbh = _dot_sel(cm * bm, gh_ref[...])
            coef = dt
            xd = xs
        else:
            cbh = _dot_sel(cm * _row_shift(bm, d, t_c), gh_ref[...])
            coef = jnp.where(t_h >= d, jnp.exp(acum - pltpu.roll(acum, d, 0)) * pltpu.roll(dt, d, 0), 0.0)
            xd = _row_shift(xs, d, t_x)
        y = y + _dot_sel(cbh * coef, e_mat) * xd
    ypart_ref[...] = y


def _ssd_step_state_kernel(st_ref, c_ref, b_ref, xwt_ref, el_ref, ypart_ref, eexp_ref, z_ref, nrm_ref,
                           u_ref, so_ref, *, sb, seq):
    i = pl.program_id(0)
    rows = sb * seq
    steps_per_block = TOK_BLOCK // rows
    base = (i % steps_per_block) * rows
    c16 = c_ref[...].astype(BF16)
    btok = b_ref[...]
    tok = lax.broadcasted_iota(jnp.int32, (TOK_BLOCK, LANE), 0)
    rsel = lax.broadcasted_iota(jnp.int32, (rows, GROUP_W), 0)
    heads_per_group = SSD_HEADS // N_GROUPS
    yoff = [jnp.zeros((rows, GROUP_W), F32) for _ in range(N_GROUPS)]
    for s in range(sb):
        lo = base + seq * s
        own = (tok >= lo) & (tok < lo + seq)
        mine = (rsel >= seq * s) & (rsel < seq * (s + 1))
        for g in range(N_GROUPS):
            sl = slice(g * GROUP_W, (g + 1) * GROUP_W)
            yo = _dot_nt(c16[:, g * D_STATE : (g + 1) * D_STATE], st_ref[s, sl, :].astype(BF16))
            yoff[g] = jnp.where(mine, yo, yoff[g])
            bsel = jnp.where(own, btok[:, g * D_STATE : (g + 1) * D_STATE], 0.0).astype(BF16)
            upd = _dot(xwt_ref[sl, :], bsel)
            for r in range(heads_per_group):
                h = g * heads_per_group + r
                hs = slice(h * SSD_HEAD_DIM, (h + 1) * SSD_HEAD_DIM)
                so_ref[s, hs, :] = el_ref[seq * s, h] * st_ref[s, hs, :] + upd[r * SSD_HEAD_DIM : (r + 1) * SSD_HEAD_DIM]
    y = ypart_ref[...] + jnp.concatenate(yoff, axis=1) * eexp_ref[...]
    u_ref[...] = _ssd_gate_norm(y, z_ref[...], nrm_ref[...]).astype(u_ref.dtype)


def ssd_step(xc2, proj2, state, p, seq, hosted=None):
    ntok = xc2.shape[0]
    nseq = ntok // seq
    full = lambda shape: pl.BlockSpec(shape, lambda i: (0,) * len(shape))
    ypart, eexp, xwt, elast = pl.pallas_call(
        functools.partial(_ssd_step_pre_kernel, seq=seq),
        grid=(1,),
        in_specs=[
            pl.BlockSpec((ntok, D_MODEL), lambda i: (0, 0)),
            pl.BlockSpec((ntok, BC_W), lambda i: (0, D_MODEL // BC_W)),
            pl.BlockSpec((ntok, BC_W), lambda i: (0, D_MODEL // BC_W + 1)),
            pl.BlockSpec((ntok, LANE), lambda i: (0, COL_DT // LANE)),
            full((1, LANE)), full((1, LANE)), full((1, D_MODEL)),
            full((LANE, D_MODEL)), full((BC_W, LANE)),
        ],
        out_specs=[full((ntok, D_MODEL)), full((ntok, D_MODEL)), full((D_MODEL, ntok)), full((ntok, LANE))],
        out_shape=[
            jax.ShapeDtypeStruct((ntok, D_MODEL), F32),
            jax.ShapeDtypeStruct((ntok, D_MODEL), F32),
            jax.ShapeDtypeStruct((D_MODEL, ntok), BF16),
            jax.ShapeDtypeStruct((ntok, LANE), F32),
        ],
        compiler_params=_cp("arbitrary"),
        name="ssd_step_pre",
    )(xc2, xc2, xc2, proj2, p["dt_bias"], p["a_log"], p["d_exp"], p["e_head"], p["g_head"])

    sb = 4
    rows = sb * seq
    spb = TOK_BLOCK // rows
    st3 = state.reshape(nseq, D_MODEL, D_STATE)
    nsteps = nseq // sb
    c = lambda i: jnp.minimum(i, nsteps - 1)
    (u, s_new), hosted_out = call_with_hosted_matmul(
        functools.partial(_ssd_step_state_kernel, sb=sb, seq=seq),
        nsteps,
        in_specs=[
            pl.BlockSpec((sb, D_MODEL, D_STATE), lambda i: (c(i), 0, 0)),
            pl.BlockSpec((rows, BC_W), lambda i: (c(i), D_MODEL // BC_W + 1)),
            pl.BlockSpec((TOK_BLOCK, BC_W), lambda i: (c(i) // spb, D_MODEL // BC_W)),
            pl.BlockSpec((D_MODEL, TOK_BLOCK), lambda i: (0, c(i) // spb)),
            pl.BlockSpec((rows, LANE), lambda i: (c(i), 0), memory_space=pltpu.SMEM),
            pl.BlockSpec((rows, D_MODEL), lambda i: (c(i), 0)),
            pl.BlockSpec((rows, D_MODEL), lambda i: (c(i), 0)),
            pl.BlockSpec((rows, D_MODEL), lambda i: (c(i), COL_Z // D_MODEL)),
            pl.BlockSpec((1, D_MODEL), lambda i: (0, 0)),
        ],
        out_specs=[
            pl.BlockSpec((rows, D_MODEL), lambda i: (c(i), 0)),
            pl.BlockSpec((sb, D_MODEL, D_STATE), lambda i: (c(i), 0, 0)),
        ],
        out_shape=[
            jax.ShapeDtypeStruct((ntok, D_MODEL), BF16),
            jax.ShapeDtypeStruct((nseq, D_MODEL, D_STATE), F32),
        ],
        args=(st3, xc2, xc2, xwt, elast, ypart, eexp, proj2, p["ssd_norm"]),
        name="ssd_step_state",
        hosted=hosted,
    )
    return u, s_new.reshape(nseq, SSD_HEADS, SSD_HEAD_DIM, D_STATE), hosted_out


def _gla_step_pre_kernel(q_ref, k_ref, v_ref, glr_ref, wg_ref, bg_ref, gv_ref,
                         oin_ref, qe_ref, kdt_ref, elt_ref, *, seq):
    nrows = q_ref.shape[0]
    glog = _gla_gate_log(glr_ref[...], wg_ref[...], bg_ref[...])
    pos = lambda w: lax.broadcasted_iota(jnp.int32, (nrows, w), 0) % seq
    t_k, t_v = pos(GLA_KEY_DIM), pos(D_MODEL)
    bc, last = _seq_cumsum_and_last(glog, seq, t_k)
    kk = k_ref[...]
    qe = q_ref[...] * (GLA_HEAD_K ** -0.5) * jnp.exp(bc)
    ke = kk * jnp.exp(-bc)
    qe_ref[...] = qe
    kdt_ref[...] = (kk * jnp.exp(last - bc)).T
    elt_ref[...] = jnp.exp(last).T
    v = v_ref[...]
    gv = gv_ref[...]
    o = jnp.zeros((nrows, D_MODEL), F32)
    for d in range(seq):
        ked = ke if d == 0 else _row_shift(ke, d, t_k)
        vd = v if d == 0 else _row_shift(v, d, t_v)
        att = _dot((qe * ked).astype(BF16), gv)
        o = o + att * vd
    oin_ref[...] = o


def _gla_step_state_kernel(st_ref, qe_ref, v_ref, kdt_ref, elt_ref, oin_ref, r_ref, nrm_ref,
                           o_ref, so_ref, *, sb, seq):
    i = pl.program_id(0)
    rows = sb * seq
    spb = TOK_BLOCK // rows
    base = (i % spb) * rows
    qe16 = qe_ref[...].astype(BF16)
    vtok = v_ref[...]
    tokv = lax.broadcasted_iota(jnp.int32, (TOK_BLOCK, GLA_HEAD_V), 0)
    tok = lax.broadcasted_iota(jnp.int32, (TOK_BLOCK, LANE), 0)
    rsel = lax.broadcasted_iota(jnp.int32, (rows, GLA_HEAD_V), 0)
    ooff = [jnp.zeros((rows, GLA_HEAD_V), F32) for _ in range(GLA_HEADS)]
    for s in range(sb):
        lo = base + seq * s
        own = (tokv >= lo) & (tokv < lo + seq)
        first = jnp.where(tok == lo, 1.0, 0.0).astype(BF16)
        mine = (rsel >= seq * s) & (rsel < seq * (s + 1))
        for h in range(GLA_HEADS):
            ks = slice(h * GLA_HEAD_K, (h + 1) * GLA_HEAD_K)
            vs = slice(h * GLA_HEAD_V, (h + 1) * GLA_HEAD_V)
            s_h = st_ref[s, ks, :]
            oo = _dot(qe16[:, ks], s_h.astype(BF16))
            ooff[h] = jnp.where(mine, oo, ooff[h])
            vsel = jnp.where(own, vtok[:, vs], 0.0).astype(BF16)
            upd = _dot(kdt_ref[ks, :].astype(BF16), vsel)
            ecol = _dot_sel(elt_ref[ks, :], first)
            so_ref[s, ks, :] = jnp.concatenate([ecol] * (GLA_HEAD_V // LANE), axis=1) * s_h + upd
    o = oin_ref[...] + jnp.concatenate(ooff, axis=1)
    o_ref[0] = _gla_out_norm(o, r_ref[...], nrm_ref[...]).astype(o_ref.dtype)


def gla_step(proj2, state, p, seq, hosted=None):
    ntok = proj2.shape[0]
    nseq = ntok // seq
    full = lambda shape: pl.BlockSpec(shape, lambda i: (0,) * len(shape))
    oin, qe, kdt, elt = pl.pallas_call(
        functools.partial(_gla_step_pre_kernel, seq=seq),
        grid=(1,),
        in_specs=[
            pl.BlockSpec((ntok, GLA_KEY_DIM), lambda i: (0, COL_Q // GLA_KEY_DIM)),
            pl.BlockSpec((ntok, GLA_KEY_DIM), lambda i: (0, COL_K // GLA_KEY_DIM)),
            pl.BlockSpec((ntok, D_MODEL), lambda i: (0, COL_V // D_MODEL)),
            pl.BlockSpec((ntok, LANE), lambda i: (0, COL_GLR // LANE)),
            full((LANE, GLA_KEY_DIM)), full((1, GLA_KEY_DIM)), full((GLA_KEY_DIM, D_MODEL)),
        ],
        out_specs=[full((ntok, D_MODEL)), full((ntok, GLA_KEY_DIM)), full((GLA_KEY_DIM, ntok)),
                   full((GLA_KEY_DIM, ntok))],
        out_shape=[
            jax.ShapeDtypeStruct((ntok, D_MODEL), F32),
            jax.ShapeDtypeStruct((ntok, GLA_KEY_DIM), F32),
            jax.ShapeDtypeStruct((GLA_KEY_DIM, ntok), F32),
            jax.ShapeDtypeStruct((GLA_KEY_DIM, ntok), F32),
        ],
        compiler_params=_cp("arbitrary"),
        name="gla_step_pre",
    )(proj2, proj2, proj2, proj2, p["w_gate"], p["b_gate"], p["g_val"])

    sb = 2 if hosted is not None else 4
    rows = sb * seq
    spb = TOK_BLOCK // rows
    st3 = state.reshape(nseq, GLA_KEY_DIM, GLA_HEAD_V)
    nsteps = nseq // sb
    c = lambda i: jnp.minimum(i, nsteps - 1)
    (o, s_new), hosted_out = call_with_hosted_matmul(
        functools.partial(_gla_step_state_kernel, sb=sb, seq=seq),
        nsteps,
        in_specs=[
            pl.BlockSpec((sb, GLA_KEY_DIM, GLA_HEAD_V), lambda i: (c(i), 0, 0)),
            pl.BlockSpec((rows, GLA_KEY_DIM), lambda i: (c(i), 0)),
            pl.BlockSpec((TOK_BLOCK, D_MODEL), lambda i: (c(i) // spb, COL_V // D_MODEL)),
            pl.BlockSpec((GLA_KEY_DIM, TOK_BLOCK), lambda i: (0, c(i) // spb)),
            pl.BlockSpec((GLA_KEY_DIM, TOK_BLOCK), lambda i: (0, c(i) // spb)),
            pl.BlockSpec((rows, D_MODEL), lambda i: (c(i), 0)),
            pl.BlockSpec((rows, D_MODEL), lambda i: (c(i), COL_R // D_MODEL)),
            pl.BlockSpec((1, GLA_HEAD_V), lambda i: (0, 0)),
        ],
        out_specs=[
            pl.BlockSpec((1, rows, D_MODEL), lambda i: (c(i), 0, 0)),
            pl.BlockSpec((sb, GLA_KEY_DIM, GLA_HEAD_V), lambda i: (c(i), 0, 0)),
        ],
        out_shape=[
            jax.ShapeDtypeStruct((nseq // sb, rows, D_MODEL), BF16),
            jax.ShapeDtypeStruct((nseq, GLA_KEY_DIM, GLA_HEAD_V), F32),
        ],
        args=(st3, qe, proj2, kdt, elt, oin, proj2, p["gla_norm"]),
        name="gla_step_state",
        hosted=hosted,
    )
    return o.reshape(ntok, D_MODEL), s_new.reshape(nseq, GLA_HEADS, GLA_HEAD_K, GLA_HEAD_V), hosted_out


def _softmax_rows(sc):
    e = jnp.exp(sc - jnp.max(sc, axis=-1, keepdims=True))
    return e / jnp.sum(e, axis=-1, keepdims=True)


def _cross_block_kernel(hn_ref, h_ref, k_ref, v_ref, wq_ref, wo_ref, g_ref, h2_ref, hn2_ref):
    q16 = _dot(hn_ref[...], wq_ref[...]).astype(BF16)
    outs = []
    for h in range(CROSS_HEADS):
        hs = slice(h * CROSS_HEAD_DIM, (h + 1) * CROSS_HEAD_DIM)
        sc = _dot_nt(q16[:, hs], k_ref[0, :, hs].astype(BF16)) * (CROSS_HEAD_DIM ** -0.5)
        outs.append(_dot(_softmax_rows(sc).astype(BF16), v_ref[0, :, hs].astype(BF16)))
    att = jnp.concatenate(outs, axis=1).astype(BF16)
    h2 = h_ref[...] + _dot(att, wo_ref[...])
    h2_ref[...] = h2
    hn2_ref[...] = _rms(h2, g_ref[...]).astype(hn2_ref.dtype)


def cross_block(hn, h, mem_k, mem_v, w_cq, w_co, g, seq):
    ntok, d = hn.shape
    n_mem = mem_k.shape[1]
    tl = min(seq, 512)
    lt = seq // tl
    row_tile = pl.BlockSpec((tl, d), lambda i: (i, 0))
    resident = lambda shape: pl.BlockSpec(shape, lambda i: (0,) * len(shape), pipeline_mode=pl.Buffered(1))
    kv_spec = pl.BlockSpec((1, n_mem, d), lambda i: (i // lt, 0, 0))
    return pl.pallas_call(
        _cross_block_kernel,
        grid=(ntok // tl,),
        in_specs=[row_tile, row_tile, kv_spec, kv_spec, resident((d, d)), resident((d, d)),
                  pl.BlockSpec((1, d), lambda i: (0, 0))],
        out_specs=[row_tile, row_tile],
        out_shape=[jax.ShapeDtypeStruct((ntok, d), F32), jax.ShapeDtypeStruct((ntok, d), BF16)],
        compiler_params=_cp("parallel"),
        name="cross_block",
    )(hn, h, mem_k, mem_v, w_cq, w_co, g.reshape(1, d))


def _xattn_step_kernel(q_ref, k_ref, v_ref, o_ref, *, nseq, tl):
    rows = nseq * tl
    n_mem = k_ref.shape[1]
    q = q_ref[0]
    qs = jnp.concatenate([q[:, h * CROSS_HEAD_DIM : (h + 1) * CROSS_HEAD_DIM] for h in range(CROSS_HEADS)],
                         axis=0).astype(BF16)
    shape = (CROSS_HEADS * rows, n_mem * CROSS_HEADS)
    col_head = lax.broadcasted_iota(jnp.int32, shape, 1) % CROSS_HEADS
    row_head = lax.broadcasted_iota(jnp.int32, shape, 0) // rows
    same_head = col_head == row_head
    rsel = lax.broadcasted_iota(jnp.int32, (CROSS_HEADS * rows, CROSS_HEAD_DIM), 0) % rows
    out = jnp.zeros((CROSS_HEADS * rows, CROSS_HEAD_DIM), F32)
    for s in range(nseq):
        kall = k_ref[s].reshape(n_mem * CROSS_HEADS, CROSS_HEAD_DIM).astype(BF16)
        vall = v_ref[s].reshape(n_mem * CROSS_HEADS, CROSS_HEAD_DIM).astype(BF16)
        sc = jnp.where(same_head, _dot_nt(qs, kall) * (CROSS_HEAD_DIM ** -0.5), NEG_BIG)
        oh = _dot(_softmax_rows(sc).astype(BF16), vall)
        out = jnp.where((rsel >= tl * s) & (rsel < tl * (s + 1)), oh, out)
    o_ref[0] = jnp.concatenate([out[h * rows : (h + 1) * rows] for h in range(CROSS_HEADS)],
                               axis=1).astype(o_ref.dtype)


def cross_attend_rider(q2, mem_k, mem_v, nb, seq):
    n_mem = mem_k.shape[1]
    nseq, tl = 8 // seq, seq
    rows = nseq * tl
    nblk = nb * seq // rows
    c = lambda i: jnp.minimum(i, nblk - 1)
    kv_spec = pl.BlockSpec((nseq, n_mem, CROSS_HEADS, CROSS_HEAD_DIM), lambda i: (c(i), 0, 0, 0))
    return RiderKernel(
        body=functools.partial(_xattn_step_kernel, nseq=nseq, tl=tl),
        nsteps=nblk,
        in_specs=[pl.BlockSpec((1, rows, D_MODEL), lambda i: (c(i), 0, 0)), kv_spec, kv_spec],
        out_specs=[pl.BlockSpec((1, rows, D_MODEL), lambda i: (c(i), 0, 0))],
        out_shape=[jax.ShapeDtypeStruct((nblk, rows, D_MODEL), BF16)],
        args=(q2.reshape(nblk, rows, D_MODEL), mem_k, mem_v),
    )


PACK_TILE = 512


SUBLANE = 8


def _cast_rows_kernel(w_ref, o_ref, *, nrows):
    j = pl.program_id(0)
    x = w_ref[...]
    row = lax.broadcasted_iota(jnp.int32, x.shape, 0) + j * PACK_TILE
    o_ref[...] = jnp.where(row < nrows, x, 0.0).astype(o_ref.dtype)


def cast_rows(wt, lo, hi, n_out, name):
    kdim = wt.shape[1]
    assert lo % SUBLANE == 0 and lo + n_out <= wt.shape[0]
    return pl.pallas_call(
        functools.partial(_cast_rows_kernel, nrows=hi - lo),
        grid=(n_out // PACK_TILE,),
        in_specs=[pl.BlockSpec((pl.Element(PACK_TILE), pl.Element(kdim)),
                               lambda j: (pl.multiple_of(lo + j * PACK_TILE, SUBLANE), 0))],
        out_specs=pl.BlockSpec((PACK_TILE, kdim), lambda j: (j, 0)),
        out_shape=jax.ShapeDtypeStruct((n_out, kdim), BF16),
        compiler_params=_cp("parallel"),
        name=name,
    )(wt)


def _split_w_in(w_in):
    wt = w_in.T
    ssd_end = D_MODEL + CONV_DIM + SSD_HEADS
    gla_end = ssd_end + 2 * GLA_KEY_DIM + 2 * D_MODEL + GATE_RANK
    return (cast_rows(wt, 0, ssd_end, N_SSD_PROJ, "pack_ssd_in"),
            cast_rows(wt, ssd_end, gla_end, N_GLA_PROJ, "pack_gla_in"),
            cast_rows(wt, gla_end, wt.shape[0], 2 * D_MODEL, "pack_gate_in"))


def _params(ssd_dt_bias, ssd_A_log, ssd_D, ssd_norm, w_gla_gate, b_gla_gate, gla_norm):
    padv = lambda a: jnp.pad(a.astype(F32), (0, LANE - a.shape[0])).reshape(1, LANE)
    head_of_chan = jnp.arange(D_MODEL, dtype=jnp.int32) // SSD_HEAD_DIM
    e_head = (jnp.arange(LANE, dtype=jnp.int32)[:, None] == head_of_chan[None, :]).astype(BF16)
    group_of_bc = jnp.arange(BC_W, dtype=jnp.int32) // D_STATE
    lane_h = jnp.arange(LANE, dtype=jnp.int32)
    g_head = ((lane_h[None, :] // (SSD_HEADS // N_GROUPS) == group_of_bc[:, None])
              & (lane_h[None, :] < SSD_HEADS)).astype(BF16)
    khead = jnp.arange(GLA_KEY_DIM, dtype=jnp.int32) // GLA_HEAD_K
    vhead = jnp.arange(D_MODEL, dtype=jnp.int32) // GLA_HEAD_V
    g_val = (khead[:, None] == vhead[None, :]).astype(BF16)
    return dict(
        dt_bias=padv(ssd_dt_bias), a_log=padv(ssd_A_log),
        d_exp=jnp.repeat(ssd_D.astype(F32), SSD_HEAD_DIM).reshape(1, D_MODEL),
        ssd_norm=ssd_norm.astype(F32).reshape(1, D_MODEL),
        e_head=e_head, g_head=g_head, g_val=g_val,
        w_gate=jnp.pad(w_gla_gate, ((0, LANE - GATE_RANK), (0, 0))).astype(BF16),
        b_gate=b_gla_gate.astype(F32).reshape(1, GLA_KEY_DIM),
        gla_norm=gla_norm.astype(F32).reshape(1, GLA_HEAD_V),
    )


def _mixers(x3, xn, ssd_conv, ssd_state, gla_state, w, p, long_seq, proj_ssd=None, gates=None, other_xn=None):
    nb, seq, d = x3.shape
    ntok = nb * seq
    x2 = x3.reshape(ntok, d)
    w_ssd_in, w_gla_in, w_gate_in = w["w_in"]
    in_proj = functools.partial(matmul, xn, tm=2048, w_rows_are_outputs=True)
    proj_ssd = in_proj(w_ssd_in, name="in_proj_ssd") if proj_ssd is None else proj_ssd
    proj_gla = in_proj(w_gla_in, name="in_proj_gla")
    gates = in_proj(w_gate_in, name="in_proj_gates") if gates is None else gates
    hosted = (lambda wt: HostedMatmul(other_xn, wt)) if other_xn is not None else (lambda wt: None)
    other_ssd = other_gates = None
    if long_seq:
        u, ssd_new, ssd_conv_new = ssd_scan(proj_ssd.reshape(nb, seq, N_SSD_PROJ), ssd_conv,
                                            w["ssd_conv_w"], w["ssd_conv_b"], p)
        o, gla_new = gla_scan(proj_gla.reshape(nb, seq, N_GLA_PROJ), p)
    else:
        xc2, ssd_conv_new = short_conv(proj_ssd, seq, [COL_XBC], CONV_DIM, [ssd_conv], [w["ssd_conv_w"]],
                                       [w["ssd_conv_b"]], SSD_CONV, False, F32, "ssd_conv")
        u, ssd_new, other_ssd = ssd_step(xc2, proj_ssd, ssd_state, p, seq, hosted(w_ssd_in))
        o, gla_new, other_gates = gla_step(proj_gla, gla_state, p, seq, hosted(w_gate_in))
    merged = merge_branches(u, o, w["w_ssd_out"], w["w_gla_out"], gates)
    h, hn = mm_res_norm(merged, w["w_mix_out"], x2, w["norm_cross"], True, BF16, "mix_out")
    return h, hn, ssd_conv_new, ssd_new, gla_new, (other_ssd, other_gates)


def _conv_ffn(h2, hn2, ffn_conv, w, nb, seq, long_seq, rider=None):
    ntok, d = h2.shape
    ffn = w["w_down"].shape[0]
    cw, cbias = w["ffn_conv_w"], w["ffn_conv_b"]
    rider_out = None
    if long_seq:
        act, ffn_conv_new, rider_out = ffn_up_conv_act(hn2.reshape(nb, seq, d), w["w_up"], cw, cbias, ffn_conv, rider)
    else:
        up = matmul(hn2, w["w_up"], name="ffn_up")
        act, fa, fg = short_conv(up, seq, [0, ffn], ffn, [ffn_conv[:, :, :ffn], ffn_conv[:, :, ffn:]],
                                 [cw[:, :ffn], cw[:, ffn:]], [cbias[:ffn], cbias[ffn:]],
                                 FFN_CONV, True, BF16, "ffn_conv")
        ffn_conv_new = jnp.concatenate([fa, fg], axis=-1)
    y = mm_res_norm(act.reshape(ntok, ffn), w["w_down"], h2, w["norm_final"], False, F32, "ffn_down")
    return y.reshape(nb, seq, d), ffn_conv_new, rider_out


def kernel(x_prompt, x_sample, cache_mem_k, cache_mem_v, state_ssd_conv, state_ssd, state_gla, state_ffn_conv, mem_prompt, norm_mix, w_in, ssd_conv_w, ssd_conv_b, ssd_dt_bias, ssd_A_log, ssd_D, ssd_norm, w_ssd_out, w_gla_gate, b_gla_gate, gla_norm, w_gla_out, w_mix_out, norm_cross, norm_mem, w_cq, w_ck, w_cv, w_co, norm_ffn, w_up, ffn_conv_w, ffn_conv_b, w_down, norm_final):
    nb, seq, d = x_prompt.shape
    n_mem = mem_prompt.shape[1]
    ffn2 = w_up.shape[1]
    w = dict(
        norm_mix=norm_mix, norm_cross=norm_cross, norm_ffn=norm_ffn, norm_final=norm_final,
        w_in=_split_w_in(w_in), ssd_conv_w=ssd_conv_w, ssd_conv_b=ssd_conv_b,
        w_ssd_out=w_ssd_out.astype(BF16), w_gla_out=w_gla_out.astype(BF16), w_mix_out=w_mix_out.astype(BF16),
        w_cq=w_cq.astype(BF16), w_co=w_co.astype(BF16), w_up=w_up.astype(BF16), w_down=w_down.astype(BF16),
        ffn_conv_w=ffn_conv_w, ffn_conv_b=ffn_conv_b,
    )
    p = _params(ssd_dt_bias, ssd_A_log, ssd_D, ssd_norm, w_gla_gate, b_gla_gate, gla_norm)

    mn = rmsnorm_cast(mem_prompt.reshape(nb * n_mem, d), norm_mem)
    p_mem_k = matmul(mn, w_ck.astype(BF16), name="mem_k").reshape(nb, n_mem, d)
    p_mem_v = matmul(mn, w_cv.astype(BF16), name="mem_v").reshape(nb, n_mem, d)
    zeros_ssd_conv = jnp.zeros((nb, SSD_CONV - 1, CONV_DIM), F32)
    zeros_ffn_conv = jnp.zeros((nb, FFN_CONV - 1, ffn2), F32)
    ns, sseq, _ = x_sample.shape
    xn_prompt = rmsnorm_cast(x_prompt.reshape(nb * seq, d), norm_mix)
    xn_sample = rmsnorm_cast(x_sample.reshape(ns * sseq, d), norm_mix)

    h_s, hn_s, s_ssd_conv, s_ssd, s_gla, (proj_ssd_p, gates_p) = _mixers(
        x_sample, xn_sample, state_ssd_conv, state_ssd, state_gla, w, p, False, other_xn=xn_prompt)

    h_p, hn_p, p_ssd_conv, p_ssd, p_gla, _ = _mixers(
        x_prompt, xn_prompt, zeros_ssd_conv, None, None, w, p, True, proj_ssd=proj_ssd_p, gates=gates_p)
    h2_p, hn2_p = cross_block(hn_p, h_p, p_mem_k, p_mem_v, w["w_cq"], w["w_co"], norm_ffn, seq)

    attend_s = cross_attend_rider(matmul(hn_s, w["w_cq"], name="cross_q"), cache_mem_k, cache_mem_v, ns, sseq)
    y_prompt, p_ffn_conv, (att_s,) = _conv_ffn(h2_p, hn2_p, zeros_ffn_conv, w, nb, seq, True, rider=attend_s)

    h2_s, hn2_s = mm_res_norm(att_s.reshape(ns * sseq, d), w["w_co"], h_s, norm_ffn, True, BF16, "cross_out")
    y_sample, s_ffn_conv, _ = _conv_ffn(h2_s, hn2_s, state_ffn_conv, w, ns, sseq, False)

    head_shape = (n_mem, CROSS_HEADS, CROSS_HEAD_DIM)
    return (y_prompt, y_sample, p_ssd_conv, p_ssd, p_gla, p_ffn_conv,
            p_mem_k.reshape((nb,) + head_shape), p_mem_v.reshape((nb,) + head_shape),
            s_ssd_conv, s_ssd, s_gla, s_ffn_conv)
```

```python
import dataclasses
import functools
from typing import NamedTuple

import jax
import jax.numpy as jnp
from jax import lax
from jax.experimental import pallas as pl
from jax.experimental.pallas import tpu as pltpu

F32 = jnp.float32
BF16 = jnp.bfloat16
EPS = 1e-6
NEG_BIG = -1e30

D_MODEL = 2048
SSD_HEAD_DIM = 64
SSD_HEADS = 32
D_STATE = 128
N_GROUPS = 4
GROUP_W = D_MODEL // N_GROUPS
BC_W = N_GROUPS * D_STATE
CONV_DIM = D_MODEL + 2 * BC_W
SSD_CONV = 4
SSD_CHUNK = 128
GLA_HEADS = 4
GLA_KEY_DIM = 1024
GLA_HEAD_K = 256
GLA_HEAD_V = 512
GATE_RANK = 16
GATE_TAU = 16.0
GLA_CHUNK = 64
CROSS_HEADS = 4
CROSS_HEAD_DIM = 512
FFN_CONV = 3
LANE = 128

COL_Z, COL_XBC, COL_DT, N_SSD_PROJ = 0, 2048, 5120, 5632
COL_Q, COL_K, COL_V, COL_R, COL_GLR, N_GLA_PROJ = 0, 1024, 2048, 4096, 6144, 6656
COL_GA, COL_GB = 0, 2048

VMEM_LIMIT = 56 * 1024 * 1024


def _cp(*sem):
    return pltpu.CompilerParams(dimension_semantics=sem, vmem_limit_bytes=VMEM_LIMIT)


def _dot(a, b, prec=None):
    return jnp.dot(a, b, preferred_element_type=F32, precision=prec)


def _dot_nt(a, b):
    return lax.dot_general(a, b, (((1,), (1,)), ((), ())), preferred_element_type=F32)


def _split3(x):
    x1 = x.astype(BF16)
    r1 = x - x1.astype(F32)
    x2 = r1.astype(BF16)
    x3 = (r1 - x2.astype(F32)).astype(BF16)
    return x1, x2, x3


def _dot_sel(x, sel, sel_first=False):
    parts = _split3(x)
    if sel_first:
        return _dot(sel, parts[0]) + _dot(sel, parts[1]) + _dot(sel, parts[2])
    return _dot(parts[0], sel) + _dot(parts[1], sel) + _dot(parts[2], sel)


def _sigmoid(x):
    return 1.0 / (1.0 + jnp.exp(-x))


def _silu(x):
    return x * _sigmoid(x)


def _softplus(x):
    return jnp.maximum(x, 0.0) + jnp.log(1.0 + jnp.exp(-jnp.abs(x)))


def _rms(x, g):
    ms = jnp.mean(x * x, axis=-1, keepdims=True)
    return x * lax.rsqrt(ms + EPS) * g


def _rmsnorm_kernel(x_ref, g_ref, o_ref):
    o_ref[...] = _rms(x_ref[...], g_ref[...]).astype(o_ref.dtype)


def rmsnorm_cast(x2, g):
    m, d = x2.shape
    tm = min(m, 512)
    return pl.pallas_call(
        _rmsnorm_kernel,
        grid=(m // tm,),
        in_specs=[pl.BlockSpec((tm, d), lambda i: (i, 0)), pl.BlockSpec((1, d), lambda i: (0, 0))],
        out_specs=pl.BlockSpec((tm, d), lambda i: (i, 0)),
        out_shape=jax.ShapeDtypeStruct((m, d), BF16),
        compiler_params=_cp("parallel"),
        name="rmsnorm_cast",
    )(x2, g.reshape(1, d))


def _mm_kernel(a_ref, w_ref, o_ref, *, w_rows_are_outputs):
    dot = _dot_nt if w_rows_are_outputs else _dot
    o_ref[...] = dot(a_ref[...], w_ref[...]).astype(o_ref.dtype)


def matmul(a, w, out_dtype=F32, tm=1024, tn=512, w_rows_are_outputs=False, name="matmul"):
    m, k = a.shape
    n = w.shape[0] if w_rows_are_outputs else w.shape[1]
    tm = min(m, tm)
    if w_rows_are_outputs:
        w_spec = pl.BlockSpec((tn, k), lambda i, j: (j, 0))
    else:
        w_spec = pl.BlockSpec((k, tn), lambda i, j: (0, j))
    return pl.pallas_call(
        functools.partial(_mm_kernel, w_rows_are_outputs=w_rows_are_outputs),
        grid=(m // tm, n // tn),
        in_specs=[pl.BlockSpec((tm, k), lambda i, j: (i, 0)), w_spec],
        out_specs=pl.BlockSpec((tm, tn), lambda i, j: (i, j)),
        out_shape=jax.ShapeDtypeStruct((m, n), out_dtype),
        compiler_params=_cp("parallel", "arbitrary"),
        name=name,
    )(a, w)


class HostedMatmul(NamedTuple):
    a: jax.Array
    w: jax.Array
    tm: int = 2048
    tn: int = 512


def call_with_hosted_matmul(body, nsteps, in_specs, out_specs, out_shape, args, name, hosted=None):
    if hosted is None:
        outs = pl.pallas_call(body, grid=(nsteps,), in_specs=in_specs, out_specs=out_specs, out_shape=out_shape,
                              compiler_params=_cp("arbitrary"), name=name)(*args)
        return list(outs), None
    m, k = hosted.a.shape
    n = hosted.w.shape[0]
    tm, tn = min(m, hosted.tm), hosted.tn
    nj = n // tn
    hsteps = (m // tm) * nj
    total = max(nsteps, hsteps)
    hc = lambda i: jnp.minimum(i, hsteps - 1)
    h_in = [pl.BlockSpec((tm, k), lambda i: (hc(i) // nj, 0)), pl.BlockSpec((tn, k), lambda i: (hc(i) % nj, 0))]
    h_out = pl.BlockSpec((tm, tn), lambda i: (hc(i) // nj, hc(i) % nj))
    n_in, n_out = len(in_specs), len(out_specs)

    def fused(*refs):
        g_in = refs[:n_in]
        ha_ref, hw_ref = refs[n_in : n_in + 2]
        g_out = refs[n_in + 2 : n_in + 2 + n_out]
        ho_ref = refs[n_in + 2 + n_out]
        step = pl.program_id(0)

        def guest():
            body(*g_in, *g_out)

        def host():
            ho_ref[...] = _dot_nt(ha_ref[...], hw_ref[...])

        guest() if nsteps == total else pl.when(step < nsteps)(guest)
        host() if hsteps == total else pl.when(step < hsteps)(host)

    outs = pl.pallas_call(
        fused,
        grid=(total,),
        in_specs=list(in_specs) + h_in,
        out_specs=list(out_specs) + [h_out],
        out_shape=list(out_shape) + [jax.ShapeDtypeStruct((m, n), F32)],
        compiler_params=_cp("arbitrary"),
        name=name,
    )(*args, hosted.a, hosted.w)
    return list(outs[:-1]), outs[-1]


def _mm_res_norm_kernel(a_ref, w_ref, res_ref, g_ref, *out_refs):
    h = res_ref[...] + _dot(a_ref[...], w_ref[...])
    if len(out_refs) == 2:
        out_refs[0][...] = h
    out_refs[-1][...] = _rms(h, g_ref[...]).astype(out_refs[-1].dtype)


def mm_res_norm(a, w, res, g, emit_h, norm_dtype, name):
    m, kdim = a.shape
    n = w.shape[1]
    tm = min(m, 512)
    row_tile = lambda width: pl.BlockSpec((tm, width), lambda i: (i, 0))
    out_shape = [jax.ShapeDtypeStruct((m, n), norm_dtype)]
    out_specs = [row_tile(n)]
    if emit_h:
        out_shape = [jax.ShapeDtypeStruct((m, n), F32)] + out_shape
        out_specs = [row_tile(n)] + out_specs
    outs = pl.pallas_call(
        _mm_res_norm_kernel,
        grid=(m // tm,),
        in_specs=[row_tile(kdim), pl.BlockSpec((kdim, n), lambda i: (0, 0), pipeline_mode=pl.Buffered(1)),
                  row_tile(n), pl.BlockSpec((1, n), lambda i: (0, 0))],
        out_specs=out_specs,
        out_shape=out_shape,
        compiler_params=_cp("parallel"),
        name=name,
    )(a, w, res, g.reshape(1, n))
    return outs if emit_h else outs[0]


def _merge_kernel(u_ref, o_ref, wa_ref, wb_ref, ga_ref, gb_ref, out_ref):
    a = _dot(u_ref[...], wa_ref[...])
    b = _dot(o_ref[...], wb_ref[...])
    out_ref[...] = (_sigmoid(ga_ref[...]) * a + _sigmoid(gb_ref[...]) * b).astype(out_ref.dtype)


def merge_branches(u, o, wa, wb, proj):
    m, d = u.shape
    tm, tn = min(m, 1024), 512
    return pl.pallas_call(
        _merge_kernel,
        grid=(m // tm, d // tn),
        in_specs=[
            pl.BlockSpec((tm, d), lambda i, j: (i, 0)),
            pl.BlockSpec((tm, d), lambda i, j: (i, 0)),
            pl.BlockSpec((d, tn), lambda i, j: (0, j)),
            pl.BlockSpec((d, tn), lambda i, j: (0, j)),
            pl.BlockSpec((tm, tn), lambda i, j: (i, COL_GA // tn + j)),
            pl.BlockSpec((tm, tn), lambda i, j: (i, COL_GB // tn + j)),
        ],
        out_specs=pl.BlockSpec((tm, tn), lambda i, j: (i, j)),
        out_shape=jax.ShapeDtypeStruct((m, d), BF16),
        compiler_params=_cp("parallel", "arbitrary"),
        name="merge_branches",
    )(u, o, wa, wb, proj, proj)


CONV_PAD = 8


def _short_conv_kernel(*refs, taps, seq, nstreams, swiglu):
    ins = refs[: 4 * nstreams]
    out_ref = refs[4 * nstreams]
    ns_refs = refs[4 * nstreams + 1 :]
    vals = []
    for s in range(nstreams):
        u_ref, st_ref, w_ref, b_ref = ins[4 * s : 4 * s + 4]
        nseq, _, tc = st_ref.shape
        full = jnp.concatenate([st_ref[...], u_ref[...].reshape(nseq, seq, tc)], axis=1)
        acc = b_ref[...]
        for k in range(taps):
            acc = acc + full[:, k : k + seq] * w_ref[k : k + 1, :]
        vals.append(acc)
        ns_refs[s][...] = full[:, seq : seq + taps - 1]
    out = _silu(vals[1]) * vals[0] if swiglu else _silu(vals[0])
    out_ref[...] = out.reshape(out_ref.shape).astype(out_ref.dtype)


def short_conv(u2, seq, col_offs, width, states, ws, bs, taps, swiglu, out_dtype, name):
    ntok = u2.shape[0]
    nseq = ntok // seq
    tc = 512
    nstreams = len(col_offs)
    in_specs, args = [], []
    for s in range(nstreams):
        cb = col_offs[s] // tc
        in_specs += [
            pl.BlockSpec((ntok, tc), lambda c, cb=cb: (0, cb + c)),
            pl.BlockSpec((nseq, taps - 1, tc), lambda c: (0, 0, c)),
            pl.BlockSpec((taps, tc), lambda c: (0, c)),
            pl.BlockSpec((1, tc), lambda c: (0, c)),
        ]
        args += [u2, states[s], ws[s], bs[s].reshape(1, width)]
    out_specs = [pl.BlockSpec((ntok, tc), lambda c: (0, c))]
    out_shape = [jax.ShapeDtypeStruct((ntok, width), out_dtype)]
    for s in range(nstreams):
        out_specs.append(pl.BlockSpec((nseq, taps - 1, tc), lambda c: (0, 0, c)))
        out_shape.append(jax.ShapeDtypeStruct((nseq, taps - 1, width), F32))
    return pl.pallas_call(
        functools.partial(_short_conv_kernel, taps=taps, seq=seq, nstreams=nstreams, swiglu=swiglu),
        grid=(width // tc,),
        in_specs=in_specs,
        out_specs=out_specs,
        out_shape=out_shape,
        compiler_params=_cp("parallel"),
        name=name,
    )(*args)


def _shift_rows(u, d, prev):
    x = pltpu.roll(u, d, 0)
    r = lax.broadcasted_iota(jnp.int32, prev.shape, 0)
    head = jnp.where(r < d, pltpu.roll(prev, d, 0), x[:CONV_PAD])
    return jnp.concatenate([head, x[CONV_PAD:]], axis=0)


def _causal_taps(u, prev, cw, bias):
    taps = cw.shape[0]
    acc = bias
    for k in range(taps):
        d = taps - 1 - k
        acc = acc + (_shift_rows(u, d, prev) if d else u) * cw[k : k + 1, :]
    return acc


class RiderKernel(NamedTuple):
    body: object
    nsteps: int
    in_specs: list
    out_specs: list
    out_shape: list
    args: tuple


def _ffn_up_kernel(*refs, tm, rider_body, n_rider_in, n_rider_out, rider_steps):
    n_in = 9
    hn_ref, wa_ref, wg_ref, cwa_ref, cwg_ref, cba_ref, cbg_ref, sta_ref, stg_ref = refs[:n_in]
    rider_in = refs[n_in : n_in + n_rider_in]
    act_ref, nsa_ref, nsg_ref = refs[n_in + n_rider_in : n_in + n_rider_in + 3]
    rider_out = refs[n_in + n_rider_in + 3 : n_in + n_rider_in + 3 + n_rider_out]
    preva, prevg = refs[-2:]
    m = pl.program_id(2)
    nm = pl.num_programs(2)
    lo = CONV_PAD - (FFN_CONV - 1)

    @pl.when(m == 0)
    def _():
        for prev, st_ref in ((preva, sta_ref), (prevg, stg_ref)):
            prev[0:lo, :] = jnp.zeros((lo, prev.shape[1]), F32)
            prev[lo:CONV_PAD, :] = st_ref[0]

    hn = hn_ref[0]
    vals = []
    for w_ref, cw_ref, cb_ref, prev in ((wa_ref, cwa_ref, cba_ref, preva), (wg_ref, cwg_ref, cbg_ref, prevg)):
        u = _dot(hn, w_ref[...])
        vals.append(_causal_taps(u, prev[...], cw_ref[...], cb_ref[...]))
        prev[...] = u[tm - CONV_PAD : tm]
    act_ref[0] = (_silu(vals[1]) * vals[0]).astype(act_ref.dtype)

    @pl.when(m == nm - 1)
    def _():
        nsa_ref[0] = preva[lo:CONV_PAD, :]
        nsg_ref[0] = prevg[lo:CONV_PAD, :]

    if rider_body is not None:
        step = (pl.program_id(0) * pl.num_programs(1) + pl.program_id(1)) * nm + m

        @pl.when(step < rider_steps)
        def _():
            rider_body(*rider_in, *rider_out)


def ffn_up_conv_act(hn3, w_up, conv_w, conv_b, conv_state, rider=None):
    nb, seq, d = hn3.shape
    ffn = w_up.shape[1] // 2
    tm, tn = min(seq, 1024), 512
    nn, nm = ffn // tn, seq // tm
    half = lambda off: (lambda b, n, m: (0, off + n))
    st_spec = lambda off: pl.BlockSpec((1, FFN_CONV - 1, tn), lambda b, n, m: (b, 0, off + n))
    cb2 = conv_b.reshape(1, 2 * ffn)
    in_specs = [
        pl.BlockSpec((1, tm, d), lambda b, n, m: (b, m, 0)),
        pl.BlockSpec((d, tn), half(0)), pl.BlockSpec((d, tn), half(nn)),
        pl.BlockSpec((FFN_CONV, tn), half(0)), pl.BlockSpec((FFN_CONV, tn), half(nn)),
        pl.BlockSpec((1, tn), half(0)), pl.BlockSpec((1, tn), half(nn)),
        st_spec(0), st_spec(nn),
    ]
    out_specs = [
        pl.BlockSpec((1, tm, tn), lambda b, n, m: (b, m, n)),
        pl.BlockSpec((1, FFN_CONV - 1, tn), lambda b, n, m: (b, 0, n)),
        pl.BlockSpec((1, FFN_CONV - 1, tn), lambda b, n, m: (b, 0, n)),
    ]
    out_shape = [
        jax.ShapeDtypeStruct((nb, seq, ffn), BF16),
        jax.ShapeDtypeStruct((nb, FFN_CONV - 1, ffn), F32),
        jax.ShapeDtypeStruct((nb, FFN_CONV - 1, ffn), F32),
    ]
    args = [hn3, w_up, w_up, conv_w, conv_w, cb2, cb2, conv_state, conv_state]
    rider_kw = dict(rider_body=None, n_rider_in=0, n_rider_out=0, rider_steps=0)
    if rider is not None and rider.nsteps > nb * nn * nm:
        rider_outs, _ = call_with_hosted_matmul(rider.body, rider.nsteps, rider.in_specs, rider.out_specs,
                                                rider.out_shape, rider.args, "rider")
        act, conv_new, _ = ffn_up_conv_act(hn3, w_up, conv_w, conv_b, conv_state)
        return act, conv_new, rider_outs
    if rider is not None:
        on_grid = lambda spec: dataclasses.replace(
            spec, index_map=lambda b, n, m, im=spec.index_map: im((b * nn + n) * nm + m))
        in_specs += [on_grid(s) for s in rider.in_specs]
        out_specs += [on_grid(s) for s in rider.out_specs]
        out_shape += list(rider.out_shape)
        args += list(rider.args)
        rider_kw = dict(rider_body=rider.body, n_rider_in=len(rider.in_specs), n_rider_out=len(rider.out_specs),
                        rider_steps=rider.nsteps)
    outs = pl.pallas_call(
        functools.partial(_ffn_up_kernel, tm=tm, **rider_kw),
        grid=(nb, nn, nm),
        in_specs=in_specs,
        out_specs=out_specs,
        out_shape=out_shape,
        scratch_shapes=[pltpu.VMEM((CONV_PAD, tn), F32), pltpu.VMEM((CONV_PAD, tn), F32)],
        compiler_params=_cp("parallel" if rider is None else "arbitrary", "parallel" if rider is None else "arbitrary",
                            "arbitrary"),
        name="ffn_up_conv_act",
    )(*args)
    act, nsa, nsg = outs[:3]
    return act, jnp.concatenate([nsa, nsg], axis=-1), list(outs[3:])


def _ssd_gate_norm(y, z, nrm):
    ug = y * _silu(z)
    outs = []
    for g in range(N_GROUPS):
        ugg = ug[:, g * GROUP_W : (g + 1) * GROUP_W]
        ms = jnp.mean(ugg * ugg, axis=-1, keepdims=True)
        outs.append(ugg * lax.rsqrt(ms + EPS))
    return jnp.concatenate(outs, axis=1) * nrm


def _gla_out_norm(o, r, nrm):
    outs = []
    for h in range(GLA_HEADS):
        oh = o[:, h * GLA_HEAD_V : (h + 1) * GLA_HEAD_V]
        rh = r[:, h * GLA_HEAD_V : (h + 1) * GLA_HEAD_V]
        outs.append(_rms(oh, nrm) * _silu(rh))
    return jnp.concatenate(outs, axis=1)


SCAN_ROWS_PER_STEP = 256


def _ssd_chunk(xs_ref, b_ref, c_ref, dt_ref, z_ref, cw_ref, cbias_ref, dtb_ref, alog_ref, dexp_ref, nrm_ref, e_ref,
               u_ref, st_ref, cprev):
    q = SSD_CHUNK
    conv = []
    for raw_ref, c0 in ((xs_ref, 0), (b_ref, D_MODEL), (c_ref, D_MODEL + BC_W)):
        raw = raw_ref[...]
        cols = slice(c0, c0 + raw.shape[1])
        conv.append(_silu(_causal_taps(raw, cprev[:, cols], cw_ref[:, cols], cbias_ref[:, cols])))
        cprev[:, cols] = raw[q - CONV_PAD : q]
    xs, bm, cm = conv
    dt = _softplus(dt_ref[...] + dtb_ref[...])
    a = dt * (-jnp.exp(alog_ref[...]))
    row = lax.broadcasted_iota(jnp.int32, (q, q), 0)
    col = lax.broadcasted_iota(jnp.int32, (q, q), 1)
    tril = row >= col
    acum = _dot_sel(a, tril.astype(BF16), sel_first=True)
    acum_t = acum.T
    dt_t = dt.T
    last = acum[q - 1 : q, :]
    e_mat = e_ref[...]
    eexp = _dot_sel(jnp.exp(acum), e_mat)
    wexp = _dot_sel(jnp.exp(last - acum) * dt, e_mat)
    s_bf = st_ref[...].astype(BF16)
    cb16 = cm.astype(BF16)
    bb16 = bm.astype(BF16)
    x16 = xs.astype(BF16)
    xw16 = (xs * wexp).astype(BF16)
    lane_lo = lax.broadcasted_iota(jnp.int32, (q, LANE), 1) < SSD_HEAD_DIM
    ys = []
    for g in range(N_GROUPS):
        cg = cb16[:, g * D_STATE : (g + 1) * D_STATE]
        bg = bb16[:, g * D_STATE : (g + 1) * D_STATE]
        cb = _dot_nt(cg, bg)
        yoff = _dot(cg, s_bf[:, g * GROUP_W : (g + 1) * GROUP_W])
        pieces = []
        for pr in range(GROUP_W // LANE):
            h0 = g * (SSD_HEADS // N_GROUPS) + 2 * pr
            xp = x16[:, h0 * SSD_HEAD_DIM : h0 * SSD_HEAD_DIM + LANE]
            yh = []
            for h in (h0, h0 + 1):
                diff = acum[:, h : h + 1] - acum_t[h : h + 1, :]
                dec = jnp.exp(jnp.where(tril, diff, NEG_BIG))
                m = (cb * dec * dt_t[h : h + 1, :]).astype(BF16)
                yh.append(_dot(m, xp))
            pieces.append(jnp.where(lane_lo, yh[0], yh[1]))
        sl = slice(g * GROUP_W, (g + 1) * GROUP_W)
        ys.append(jnp.concatenate(pieces, axis=1) + yoff * eexp[:, sl])
        bg_t = bm[:, g * D_STATE : (g + 1) * D_STATE].T.astype(BF16)
        upd = _dot(bg_t, xw16[:, sl])
        st_ref[:, sl] = eexp[q - 1 : q, sl] * st_ref[:, sl] + upd
    y = jnp.concatenate(ys, axis=1) + dexp_ref[...] * xs
    u_ref[...] = _ssd_gate_norm(y, z_ref[...], nrm_ref[...]).astype(u_ref.dtype)


def _ssd_scan_kernel(xs_ref, b_ref, c_ref, dt_ref, z_ref, cst_ref, cw_ref, cbias_ref,
                     dtb_ref, alog_ref, dexp_ref, nrm_ref, e_ref,
                     u_ref, sout_ref, cso_ref, st_ref, cprev, *, nsteps, nsub):
    step = pl.program_id(1)
    lo = CONV_PAD - (SSD_CONV - 1)

    @pl.when(step == 0)
    def _():
        st_ref[...] = jnp.zeros_like(st_ref)
        cprev[0:lo, :] = jnp.zeros((lo, CONV_DIM), F32)
        cprev[lo:CONV_PAD, :] = cst_ref[0]

    for sub in range(nsub):
        rows = lambda ref: ref.at[0, pl.ds(sub * SSD_CHUNK, SSD_CHUNK)]
        _ssd_chunk(rows(xs_ref), rows(b_ref), rows(c_ref), rows(dt_ref), rows(z_ref), cw_ref, cbias_ref,
                   dtb_ref, alog_ref, dexp_ref, nrm_ref, e_ref, rows(u_ref), st_ref, cprev)

    @pl.when(step == nsteps - 1)
    def _():
        sout_ref[0] = st_ref[...].T
        cso_ref[0] = cprev[lo:CONV_PAD, :]


def ssd_scan(proj3, conv_state, conv_w, conv_b, p):
    nb, seq, _ = proj3.shape
    nsub = SCAN_ROWS_PER_STEP // SSD_CHUNK if seq % SCAN_ROWS_PER_STEP == 0 else 1
    q = nsub * SSD_CHUNK
    nsteps = seq // q
    vec = lambda n: pl.BlockSpec((1, n), lambda b, c: (0, 0))
    u, s_out, conv_new = pl.pallas_call(
        functools.partial(_ssd_scan_kernel, nsteps=nsteps, nsub=nsub),
        grid=(nb, nsteps),
        in_specs=[
            pl.BlockSpec((1, q, D_MODEL), lambda b, c: (b, c, COL_XBC // D_MODEL)),
            pl.BlockSpec((1, q, BC_W), lambda b, c: (b, c, (COL_XBC + D_MODEL) // BC_W)),
            pl.BlockSpec((1, q, BC_W), lambda b, c: (b, c, (COL_XBC + D_MODEL) // BC_W + 1)),
            pl.BlockSpec((1, q, LANE), lambda b, c: (b, c, COL_DT // LANE)),
            pl.BlockSpec((1, q, D_MODEL), lambda b, c: (b, c, COL_Z // D_MODEL)),
            pl.BlockSpec((1, SSD_CONV - 1, CONV_DIM), lambda b, c: (b, 0, 0)),
            pl.BlockSpec((SSD_CONV, CONV_DIM), lambda b, c: (0, 0)),
            vec(CONV_DIM),
            vec(LANE), vec(LANE), vec(D_MODEL), vec(D_MODEL),
            pl.BlockSpec((LANE, D_MODEL), lambda b, c: (0, 0)),
        ],
        out_specs=[
            pl.BlockSpec((1, q, D_MODEL), lambda b, c: (b, c, 0)),
            pl.BlockSpec((1, D_MODEL, D_STATE), lambda b, c: (b, 0, 0)),
            pl.BlockSpec((1, SSD_CONV - 1, CONV_DIM), lambda b, c: (b, 0, 0)),
        ],
        out_shape=[
            jax.ShapeDtypeStruct((nb, seq, D_MODEL), BF16),
            jax.ShapeDtypeStruct((nb, D_MODEL, D_STATE), F32),
            jax.ShapeDtypeStruct((nb, SSD_CONV - 1, CONV_DIM), F32),
        ],
        scratch_shapes=[pltpu.VMEM((D_STATE, D_MODEL), F32), pltpu.VMEM((CONV_PAD, CONV_DIM), F32)],
        compiler_params=_cp("parallel", "arbitrary"),
        name="ssd_scan",
    )(proj3, proj3, proj3, proj3, proj3, conv_state, conv_w, conv_b.reshape(1, CONV_DIM),
      p["dt_bias"], p["a_log"], p["d_exp"], p["ssd_norm"], p["e_head"])
    return u.reshape(nb * seq, D_MODEL), s_out.reshape(nb, SSD_HEADS, SSD_HEAD_DIM, D_STATE), conv_new


def _gla_gate_log(glr, wg, bg):
    x = _dot(glr.astype(BF16), wg) + bg
    return -_softplus(-x) / GATE_TAU


def _gla_chunk(q_ref, k_ref, v_ref, r_ref, glr_ref, wg_ref, bg_ref, nrm_ref, o_ref, st_ref):
    q = GLA_CHUNK
    glog = _gla_gate_log(glr_ref[...], wg_ref[...], bg_ref[...])
    row = lax.broadcasted_iota(jnp.int32, (q, q), 0)
    col = lax.broadcasted_iota(jnp.int32, (q, q), 1)
    tril = row >= col
    bc = _dot_sel(glog, tril.astype(BF16), sel_first=True)
    last = bc[q - 1 : q, :]
    kk = k_ref[...]
    qe = q_ref[...] * (GLA_HEAD_K ** -0.5) * jnp.exp(bc)
    ke = kk * jnp.exp(-bc)
    kd = kk * jnp.exp(last - bc)
    elast = jnp.exp(last)
    v16 = v_ref[...].astype(BF16)
    zeros_v = jnp.zeros((q, GLA_HEAD_V), BF16)
    outs = []
    for h in range(GLA_HEADS):
        ks = slice(h * GLA_HEAD_K, (h + 1) * GLA_HEAD_K)
        vs = slice(h * GLA_HEAD_V, (h + 1) * GLA_HEAD_V)
        qh = qe[:, ks].astype(BF16)
        kh = ke[:, ks].astype(BF16)
        att = jnp.where(tril, _dot_nt(qh, kh), 0.0)
        s_h = st_ref[ks, :]
        outs.append(_dot(att.astype(BF16), v16[:, vs]) + _dot(qh, s_h.astype(BF16)))
        xt = jnp.concatenate([kd[:, ks], jnp.broadcast_to(elast[:, ks], (q, GLA_HEAD_K))], axis=0).T
        v2 = jnp.concatenate([v16[:, vs], zeros_v], axis=0)
        st_ref[ks, :] = xt[:, q : q + 1] * s_h + _dot(xt.astype(BF16), v2)
    o = jnp.concatenate(outs, axis=1)
    o_ref[...] = _gla_out_norm(o, r_ref[...], nrm_ref[...]).astype(o_ref.dtype)


def _gla_scan_kernel(q_ref, k_ref, v_ref, r_ref, glr_ref, wg_ref, bg_ref, nrm_ref,
                     o_ref, sout_ref, st_ref, *, nsteps, nsub):
    step = pl.program_id(1)

    @pl.when(step == 0)
    def _():
        st_ref[...] = jnp.zeros_like(st_ref)

    for sub in range(nsub):
        rows = lambda ref: ref.at[0, pl.ds(sub * GLA_CHUNK, GLA_CHUNK)]
        _gla_chunk(rows(q_ref), rows(k_ref), rows(v_ref), rows(r_ref), rows(glr_ref), wg_ref, bg_ref, nrm_ref,
                   rows(o_ref), st_ref)

    @pl.when(step == nsteps - 1)
    def _():
        sout_ref[0] = st_ref[...]


def gla_scan(proj3, p):
    nb, seq, _ = proj3.shape
    nsub = SCAN_ROWS_PER_STEP // GLA_CHUNK if seq % SCAN_ROWS_PER_STEP == 0 else 1
    q = nsub * GLA_CHUNK
    nsteps = seq // q
    o, s_out = pl.pallas_call(
        functools.partial(_gla_scan_kernel, nsteps=nsteps, nsub=nsub),
        grid=(nb, nsteps),
        in_specs=[
            pl.BlockSpec((1, q, GLA_KEY_DIM), lambda b, c: (b, c, COL_Q // GLA_KEY_DIM)),
            pl.BlockSpec((1, q, GLA_KEY_DIM), lambda b, c: (b, c, COL_K // GLA_KEY_DIM)),
            pl.BlockSpec((1, q, D_MODEL), lambda b, c: (b, c, COL_V // D_MODEL)),
            pl.BlockSpec((1, q, D_MODEL), lambda b, c: (b, c, COL_R // D_MODEL)),
            pl.BlockSpec((1, q, LANE), lambda b, c: (b, c, COL_GLR // LANE)),
            pl.BlockSpec((LANE, GLA_KEY_DIM), lambda b, c: (0, 0)),
            pl.BlockSpec((1, GLA_KEY_DIM), lambda b, c: (0, 0)),
            pl.BlockSpec((1, GLA_HEAD_V), lambda b, c: (0, 0)),
        ],
        out_specs=[
            pl.BlockSpec((1, q, D_MODEL), lambda b, c: (b, c, 0)),
            pl.BlockSpec((1, GLA_KEY_DIM, GLA_HEAD_V), lambda b, c: (b, 0, 0)),
        ],
        out_shape=[
            jax.ShapeDtypeStruct((nb, seq, D_MODEL), BF16),
            jax.ShapeDtypeStruct((nb, GLA_KEY_DIM, GLA_HEAD_V), F32),
        ],
        scratch_shapes=[pltpu.VMEM((GLA_KEY_DIM, GLA_HEAD_V), F32)],
        compiler_params=_cp("parallel", "arbitrary"),
        name="gla_scan",
    )(proj3, proj3, proj3, proj3, proj3, p["w_gate"], p["b_gate"], p["gla_norm"])
    return o.reshape(nb * seq, D_MODEL), s_out.reshape(nb, GLA_HEADS, GLA_HEAD_K, GLA_HEAD_V)


TOK_BLOCK = 128


def _row_shift(x, d, tpos):
    return jnp.where(tpos >= d, pltpu.roll(x, d, 0), 0.0)


def _seq_cumsum_and_last(a, seq, tpos):
    nrows = a.shape[0]
    acum = a
    for d in range(1, seq):
        acum = acum + _row_shift(a, d, tpos)
    last = jnp.where(tpos == seq - 1, acum, 0.0)
    for d in range(1, seq):
        last = last + jnp.where(tpos == seq - 1 - d, pltpu.roll(acum, nrows - d, 0), 0.0)
    return acum, last


def _ssd_step_pre_kernel(xs_ref, b_ref, c_ref, dt_ref, dtb_ref, alog_ref, dexp_ref, e_ref, gh_ref,
                         ypart_ref, eexp_ref, xwt_ref, el_ref, *, seq):
    nrows = xs_ref.shape[0]
    xs = xs_ref[...]
    bm = b_ref[...]
    cm = c_ref[...]
    dt = _softplus(dt_ref[...] + dtb_ref[...])
    a = dt * (-jnp.exp(alog_ref[...]))
    pos = lambda w: lax.broadcasted_iota(jnp.int32, (nrows, w), 0) % seq
    t_h, t_c, t_x = pos(LANE), pos(BC_W), pos(D_MODEL)
    acum, last = _seq_cumsum_and_last(a, seq, t_h)
    e_mat = e_ref[...]
    eexp_ref[...] = _dot_sel(jnp.exp(acum), e_mat)
    wexp = _dot_sel(jnp.exp(last - acum) * dt, e_mat)
    xwt_ref[...] = (xs * wexp).T.astype(xwt_ref.dtype)
    el_ref[...] = jnp.exp(last)
    y = dexp_ref[...] * xs
    for d in range(seq):
        if d == 0:
            cbh = _dot_sel(cm * bm, gh_ref[...])
            coef = dt
            xd = xs
        else:
            cbh = _dot_sel(cm * _row_shift(bm, d, t_c), gh_ref[...])
            coef = jnp.where(t_h >= d, jnp.exp(acum - pltpu.roll(acum, d, 0)) * pltpu.roll(dt, d, 0), 0.0)
            xd = _row_shift(xs, d, t_x)
        y = y + _dot_sel(cbh * coef, e_mat) * xd
    ypart_ref[...] = y


def _ssd_step_state_kernel(st_ref, c_ref, b_ref, xwt_ref, el_ref, ypart_ref, eexp_ref, z_ref, nrm_ref,
                           u_ref, so_ref, *, sb, seq):
    i = pl.program_id(0)
    rows = sb * seq
    steps_per_block = TOK_BLOCK // rows
    base = (i % steps_per_block) * rows
    c16 = c_ref[...].astype(BF16)
    btok = b_ref[...]
    tok = lax.broadcasted_iota(jnp.int32, (TOK_BLOCK, LANE), 0)
    rsel = lax.broadcasted_iota(jnp.int32, (rows, GROUP_W), 0)
    heads_per_group = SSD_HEADS // N_GROUPS
    yoff = [jnp.zeros((rows, GROUP_W), F32) for _ in range(N_GROUPS)]
    for s in range(sb):
        lo = base + seq * s
        own = (tok >= lo) & (tok < lo + seq)
        mine = (rsel >= seq * s) & (rsel < seq * (s + 1))
        for g in range(N_GROUPS):
            sl = slice(g * GROUP_W, (g + 1) * GROUP_W)
            yo = _dot_nt(c16[:, g * D_STATE : (g + 1) * D_STATE], st_ref[s, sl, :].astype(BF16))
            yoff[g] = jnp.where(mine, yo, yoff[g])
            bsel = jnp.where(own, btok[:, g * D_STATE : (g + 1) * D_STATE], 0.0).astype(BF16)
            upd = _dot(xwt_ref[sl, :], bsel)
            for r in range(heads_per_group):
                h = g * heads_per_group + r
                hs = slice(h * SSD_HEAD_DIM, (h + 1) * SSD_HEAD_DIM)
                so_ref[s, hs, :] = el_ref[seq * s, h] * st_ref[s, hs, :] + upd[r * SSD_HEAD_DIM : (r + 1) * SSD_HEAD_DIM]
    y = ypart_ref[...] + jnp.concatenate(yoff, axis=1) * eexp_ref[...]
    u_ref[...] = _ssd_gate_norm(y, z_ref[...], nrm_ref[...]).astype(u_ref.dtype)


def ssd_step(xc2, proj2, state, p, seq, hosted=None):
    ntok = xc2.shape[0]
    nseq = ntok // seq
    full = lambda shape: pl.BlockSpec(shape, lambda i: (0,) * len(shape))
    ypart, eexp, xwt, elast = pl.pallas_call(
        functools.partial(_ssd_step_pre_kernel, seq=seq),
        grid=(1,),
        in_specs=[
            pl.BlockSpec((ntok, D_MODEL), lambda i: (0, 0)),
            pl.BlockSpec((ntok, BC_W), lambda i: (0, D_MODEL // BC_W)),
            pl.BlockSpec((ntok, BC_W), lambda i: (0, D_MODEL // BC_W + 1)),
            pl.BlockSpec((ntok, LANE), lambda i: (0, COL_DT // LANE)),
            full((1, LANE)), full((1, LANE)), full((1, D_MODEL)),
            full((LANE, D_MODEL)), full((BC_W, LANE)),
        ],
        out_specs=[full((ntok, D_MODEL)), full((ntok, D_MODEL)), full((D_MODEL, ntok)), full((ntok, LANE))],
        out_shape=[
            jax.ShapeDtypeStruct((ntok, D_MODEL), F32),
            jax.ShapeDtypeStruct((ntok, D_MODEL), F32),
            jax.ShapeDtypeStruct((D_MODEL, ntok), BF16),
            jax.ShapeDtypeStruct((ntok, LANE), F32),
        ],
        compiler_params=_cp("arbitrary"),
        name="ssd_step_pre",
    )(xc2, xc2, xc2, proj2, p["dt_bias"], p["a_log"], p["d_exp"], p["e_head"], p["g_head"])

    sb = 4
    rows = sb * seq
    spb = TOK_BLOCK // rows
    st3 = state.reshape(nseq, D_MODEL, D_STATE)
    nsteps = nseq // sb
    c = lambda i: jnp.minimum(i, nsteps - 1)
    (u, s_new), hosted_out = call_with_hosted_matmul(
        functools.partial(_ssd_step_state_kernel, sb=sb, seq=seq),
        nsteps,
        in_specs=[
            pl.BlockSpec((sb, D_MODEL, D_STATE), lambda i: (c(i), 0, 0)),
            pl.BlockSpec((rows, BC_W), lambda i: (c(i), D_MODEL // BC_W + 1)),
            pl.BlockSpec((TOK_BLOCK, BC_W), lambda i: (c(i) // spb, D_MODEL // BC_W)),
            pl.BlockSpec((D_MODEL, TOK_BLOCK), lambda i: (0, c(i) // spb)),
            pl.BlockSpec((rows, LANE), lambda i: (c(i), 0), memory_space=pltpu.SMEM),
            pl.BlockSpec((rows, D_MODEL), lambda i: (c(i), 0)),
            pl.BlockSpec((rows, D_MODEL), lambda i: (c(i), 0)),
            pl.BlockSpec((rows, D_MODEL), lambda i: (c(i), COL_Z // D_MODEL)),
            pl.BlockSpec((1, D_MODEL), lambda i: (0, 0)),
        ],
        out_specs=[
            pl.BlockSpec((rows, D_MODEL), lambda i: (c(i), 0)),
            pl.BlockSpec((sb, D_MODEL, D_STATE), lambda i: (c(i), 0, 0)),
        ],
        out_shape=[
            jax.ShapeDtypeStruct((ntok, D_MODEL), BF16),
            jax.ShapeDtypeStruct((nseq, D_MODEL, D_STATE), F32),
        ],
        args=(st3, xc2, xc2, xwt, elast, ypart, eexp, proj2, p["ssd_norm"]),
        name="ssd_step_state",
        hosted=hosted,
    )
    return u, s_new.reshape(nseq, SSD_HEADS, SSD_HEAD_DIM, D_STATE), hosted_out


def _gla_step_pre_kernel(q_ref, k_ref, v_ref, glr_ref, wg_ref, bg_ref, gv_ref,
                         oin_ref, qe_ref, kdt_ref, elt_ref, *, seq):
    nrows = q_ref.shape[0]
    glog = _gla_gate_log(glr_ref[...], wg_ref[...], bg_ref[...])
    pos = lambda w: lax.broadcasted_iota(jnp.int32, (nrows, w), 0) % seq
    t_k, t_v = pos(GLA_KEY_DIM), pos(D_MODEL)
    bc, last = _seq_cumsum_and_last(glog, seq, t_k)
    kk = k_ref[...]
    qe = q_ref[...] * (GLA_HEAD_K ** -0.5) * jnp.exp(bc)
    ke = kk * jnp.exp(-bc)
    qe_ref[...] = qe
    kdt_ref[...] = (kk * jnp.exp(last - bc)).T
    elt_ref[...] = jnp.exp(last).T
    v = v_ref[...]
    gv = gv_ref[...]
    o = jnp.zeros((nrows, D_MODEL), F32)
    for d in range(seq):
        ked = ke if d == 0 else _row_shift(ke, d, t_k)
        vd = v if d == 0 else _row_shift(v, d, t_v)
        att = _dot((qe * ked).astype(BF16), gv)
        o = o + att * vd
    oin_ref[...] = o


def _gla_step_state_kernel(st_ref, qe_ref, v_ref, kdt_ref, elt_ref, oin_ref, r_ref, nrm_ref,
                           o_ref, so_ref, *, sb, seq):
    i = pl.program_id(0)
    rows = sb * seq
    spb = TOK_BLOCK // rows
    base = (i % spb) * rows
    qe16 = qe_ref[...].astype(BF16)
    vtok = v_ref[...]
    tokv = lax.broadcasted_iota(jnp.int32, (TOK_BLOCK, GLA_HEAD_V), 0)
    tok = lax.broadcasted_iota(jnp.int32, (TOK_BLOCK, LANE), 0)
    rsel = lax.broadcasted_iota(jnp.int32, (rows, GLA_HEAD_V), 0)
    ooff = [jnp.zeros((rows, GLA_HEAD_V), F32) for _ in range(GLA_HEADS)]
    for s in range(sb):
        lo = base + seq * s
        own = (tokv >= lo) & (tokv < lo + seq)
        first = jnp.where(tok == lo, 1.0, 0.0).astype(BF16)
        mine = (rsel >= seq * s) & (rsel < seq * (s + 1))
        for h in range(GLA_HEADS):
            ks = slice(h * GLA_HEAD_K, (h + 1) * GLA_HEAD_K)
            vs = slice(h * GLA_HEAD_V, (h + 1) * GLA_HEAD_V)
            s_h = st_ref[s, ks, :]
            oo = _dot(qe16[:, ks], s_h.astype(BF16))
            ooff[h] = jnp.where(mine, oo, ooff[h])
            vsel = jnp.where(own, vtok[:, vs], 0.0).astype(BF16)
            upd = _dot(kdt_ref[ks, :].astype(BF16), vsel)
            ecol = _dot_sel(elt_ref[ks, :], first)
            so_ref[s, ks, :] = jnp.concatenate([ecol] * (GLA_HEAD_V // LANE), axis=1) * s_h + upd
    o = oin_ref[...] + jnp.concatenate(ooff, axis=1)
    o_ref[0] = _gla_out_norm(o, r_ref[...], nrm_ref[...]).astype(o_ref.dtype)


def gla_step(proj2, state, p, seq, hosted=None):
    ntok = proj2.shape[0]
    nseq = ntok // seq
    full = lambda shape: pl.BlockSpec(shape, lambda i: (0,) * len(shape))
    oin, qe, kdt, elt = pl.pallas_call(
        functools.partial(_gla_step_pre_kernel, seq=seq),
        grid=(1,),
        in_specs=[
            pl.BlockSpec((ntok, GLA_KEY_DIM), lambda i: (0, COL_Q // GLA_KEY_DIM)),
            pl.BlockSpec((ntok, GLA_KEY_DIM), lambda i: (0, COL_K // GLA_KEY_DIM)),
            pl.BlockSpec((ntok, D_MODEL), lambda i: (0, COL_V // D_MODEL)),
            pl.BlockSpec((ntok, LANE), lambda i: (0, COL_GLR // LANE)),
            full((LANE, GLA_KEY_DIM)), full((1, GLA_KEY_DIM)), full((GLA_KEY_DIM, D_MODEL)),
        ],
        out_specs=[full((ntok, D_MODEL)), full((ntok, GLA_KEY_DIM)), full((GLA_KEY_DIM, ntok)),
                   full((GLA_KEY_DIM, ntok))],
        out_shape=[
            jax.ShapeDtypeStruct((ntok, D_MODEL), F32),
            jax.ShapeDtypeStruct((ntok, GLA_KEY_DIM), F32),
            jax.ShapeDtypeStruct((GLA_KEY_DIM, ntok), F32),
            jax.ShapeDtypeStruct((GLA_KEY_DIM, ntok), F32),
        ],
        compiler_params=_cp("arbitrary"),
        name="gla_step_pre",
    )(proj2, proj2, proj2, proj2, p["w_gate"], p["b_gate"], p["g_val"])

    sb = 2 if hosted is not None else 4
    rows = sb * seq
    spb = TOK_BLOCK // rows
    st3 = state.reshape(nseq, GLA_KEY_DIM, GLA_HEAD_V)
    nsteps = nseq // sb
    c = lambda i: jnp.minimum(i, nsteps - 1)
    (o, s_new), hosted_out = call_with_hosted_matmul(
        functools.partial(_gla_step_state_kernel, sb=sb, seq=seq),
        nsteps,
        in_specs=[
            pl.BlockSpec((sb, GLA_KEY_DIM, GLA_HEAD_V), lambda i: (c(i), 0, 0)),
            pl.BlockSpec((rows, GLA_KEY_DIM), lambda i: (c(i), 0)),
            pl.BlockSpec((TOK_BLOCK, D_MODEL), lambda i: (c(i) // spb, COL_V // D_MODEL)),
            pl.BlockSpec((GLA_KEY_DIM, TOK_BLOCK), lambda i: (0, c(i) // spb)),
            pl.BlockSpec((GLA_KEY_DIM, TOK_BLOCK), lambda i: (0, c(i) // spb)),
            pl.BlockSpec((rows, D_MODEL), lambda i: (c(i), 0)),
            pl.BlockSpec((rows, D_MODEL), lambda i: (c(i), COL_R // D_MODEL)),
            pl.BlockSpec((1, GLA_HEAD_V), lambda i: (0, 0)),
        ],
        out_specs=[
            pl.BlockSpec((1, rows, D_MODEL), lambda i: (c(i), 0, 0)),
            pl.BlockSpec((sb, GLA_KEY_DIM, GLA_HEAD_V), lambda i: (c(i), 0, 0)),
        ],
        out_shape=[
            jax.ShapeDtypeStruct((nseq // sb, rows, D_MODEL), BF16),
            jax.ShapeDtypeStruct((nseq, GLA_KEY_DIM, GLA_HEAD_V), F32),
        ],
        args=(st3, qe, proj2, kdt, elt, oin, proj2, p["gla_norm"]),
        name="gla_step_state",
        hosted=hosted,
    )
    return o.reshape(ntok, D_MODEL), s_new.reshape(nseq, GLA_HEADS, GLA_HEAD_K, GLA_HEAD_V), hosted_out


def _softmax_rows(sc):
    e = jnp.exp(sc - jnp.max(sc, axis=-1, keepdims=True))
    return e / jnp.sum(e, axis=-1, keepdims=True)


def _cross_block_kernel(hn_ref, h_ref, k_ref, v_ref, wq_ref, wo_ref, g_ref, h2_ref, hn2_ref):
    q16 = _dot(hn_ref[...], wq_ref[...]).astype(BF16)
    outs = []
    for h in range(CROSS_HEADS):
        hs = slice(h * CROSS_HEAD_DIM, (h + 1) * CROSS_HEAD_DIM)
        sc = _dot_nt(q16[:, hs], k_ref[0, :, hs].astype(BF16)) * (CROSS_HEAD_DIM ** -0.5)
        outs.append(_dot(_softmax_rows(sc).astype(BF16), v_ref[0, :, hs].astype(BF16)))
    att = jnp.concatenate(outs, axis=1).astype(BF16)
    h2 = h_ref[...] + _dot(att, wo_ref[...])
    h2_ref[...] = h2
    hn2_ref[...] = _rms(h2, g_ref[...]).astype(hn2_ref.dtype)


def cross_block(hn, h, mem_k, mem_v, w_cq, w_co, g, seq):
    ntok, d = hn.shape
    n_mem = mem_k.shape[1]
    tl = min(seq, 512)
    lt = seq // tl
    row_tile = pl.BlockSpec((tl, d), lambda i: (i, 0))
    resident = lambda shape: pl.BlockSpec(shape, lambda i: (0,) * len(shape), pipeline_mode=pl.Buffered(1))
    kv_spec = pl.BlockSpec((1, n_mem, d), lambda i: (i // lt, 0, 0))
    return pl.pallas_call(
        _cross_block_kernel,
        grid=(ntok // tl,),
        in_specs=[row_tile, row_tile, kv_spec, kv_spec, resident((d, d)), resident((d, d)),
                  pl.BlockSpec((1, d), lambda i: (0, 0))],
        out_specs=[row_tile, row_tile],
        out_shape=[jax.ShapeDtypeStruct((ntok, d), F32), jax.ShapeDtypeStruct((ntok, d), BF16)],
        compiler_params=_cp("parallel"),
        name="cross_block",
    )(hn, h, mem_k, mem_v, w_cq, w_co, g.reshape(1, d))


def _xattn_step_kernel(q_ref, k_ref, v_ref, o_ref, *, nseq, tl):
    rows = nseq * tl
    n_mem = k_ref.shape[1]
    q = q_ref[0]
    qs = jnp.concatenate([q[:, h * CROSS_HEAD_DIM : (h + 1) * CROSS_HEAD_DIM] for h in range(CROSS_HEADS)],
                         axis=0).astype(BF16)
    shape = (CROSS_HEADS * rows, n_mem * CROSS_HEADS)
    col_head = lax.broadcasted_iota(jnp.int32, shape, 1) % CROSS_HEADS
    row_head = lax.broadcasted_iota(jnp.int32, shape, 0) // rows
    same_head = col_head == row_head
    rsel = lax.broadcasted_iota(jnp.int32, (CROSS_HEADS * rows, CROSS_HEAD_DIM), 0) % rows
    out = jnp.zeros((CROSS_HEADS * rows, CROSS_HEAD_DIM), F32)
    for s in range(nseq):
        kall = k_ref[s].reshape(n_mem * CROSS_HEADS, CROSS_HEAD_DIM).astype(BF16)
        vall = v_ref[s].reshape(n_mem * CROSS_HEADS, CROSS_HEAD_DIM).astype(BF16)
        sc = jnp.where(same_head, _dot_nt(qs, kall) * (CROSS_HEAD_DIM ** -0.5), NEG_BIG)
        oh = _dot(_softmax_rows(sc).astype(BF16), vall)
        out = jnp.where((rsel >= tl * s) & (rsel < tl * (s + 1)), oh, out)
    o_ref[0] = jnp.concatenate([out[h * rows : (h + 1) * rows] for h in range(CROSS_HEADS)],
                               axis=1).astype(o_ref.dtype)


def cross_attend_rider(q2, mem_k, mem_v, nb, seq):
    n_mem = mem_k.shape[1]
    nseq, tl = 8 // seq, seq
    rows = nseq * tl
    nblk = nb * seq // rows
    c = lambda i: jnp.minimum(i, nblk - 1)
    kv_spec = pl.BlockSpec((nseq, n_mem, CROSS_HEADS, CROSS_HEAD_DIM), lambda i: (c(i), 0, 0, 0))
    return RiderKernel(
        body=functools.partial(_xattn_step_kernel, nseq=nseq, tl=tl),
        nsteps=nblk,
        in_specs=[pl.BlockSpec((1, rows, D_MODEL), lambda i: (c(i), 0, 0)), kv_spec, kv_spec],
        out_specs=[pl.BlockSpec((1, rows, D_MODEL), lambda i: (c(i), 0, 0))],
        out_shape=[jax.ShapeDtypeStruct((nblk, rows, D_MODEL), BF16)],
        args=(q2.reshape(nblk, rows, D_MODEL), mem_k, mem_v),
    )


PACK_TILE = 512


SUBLANE = 8


def _cast_rows_kernel(w_ref, o_ref, *, nrows):
    j = pl.program_id(0)
    x = w_ref[...]
    row = lax.broadcasted_iota(jnp.int32, x.shape, 0) + j * PACK_TILE
    o_ref[...] = jnp.where(row < nrows, x, 0.0).astype(o_ref.dtype)


def cast_rows(wt, lo, hi, n_out, name):
    kdim = wt.shape[1]
    assert lo % SUBLANE == 0 and lo + n_out <= wt.shape[0]
    return pl.pallas_call(
        functools.partial(_cast_rows_kernel, nrows=hi - lo),
        grid=(n_out // PACK_TILE,),
        in_specs=[pl.BlockSpec((pl.Element(PACK_TILE), pl.Element(kdim)),
                               lambda j: (pl.multiple_of(lo + j * PACK_TILE, SUBLANE), 0))],
        out_specs=pl.BlockSpec((PACK_TILE, kdim), lambda j: (j, 0)),
        out_shape=jax.ShapeDtypeStruct((n_out, kdim), BF16),
        compiler_params=_cp("parallel"),
        name=name,
    )(wt)


def _split_w_in(w_in):
    wt = w_in.T
    ssd_end = D_MODEL + CONV_DIM + SSD_HEADS
    gla_end = ssd_end + 2 * GLA_KEY_DIM + 2 * D_MODEL + GATE_RANK
    return (cast_rows(wt, 0, ssd_end, N_SSD_PROJ, "pack_ssd_in"),
            cast_rows(wt, ssd_end, gla_end, N_GLA_PROJ, "pack_gla_in"),
            cast_rows(wt, gla_end, wt.shape[0], 2 * D_MODEL, "pack_gate_in"))


def _params(ssd_dt_bias, ssd_A_log, ssd_D, ssd_norm, w_gla_gate, b_gla_gate, gla_norm):
    padv = lambda a: jnp.pad(a.astype(F32), (0, LANE - a.shape[0])).reshape(1, LANE)
    head_of_chan = jnp.arange(D_MODEL, dtype=jnp.int32) // SSD_HEAD_DIM
    e_head = (jnp.arange(LANE, dtype=jnp.int32)[:, None] == head_of_chan[None, :]).astype(BF16)
    group_of_bc = jnp.arange(BC_W, dtype=jnp.int32) // D_STATE
    lane_h = jnp.arange(LANE, dtype=jnp.int32)
    g_head = ((lane_h[None, :] // (SSD_HEADS // N_GROUPS) == group_of_bc[:, None])
              & (lane_h[None, :] < SSD_HEADS)).astype(BF16)
    khead = jnp.arange(GLA_KEY_DIM, dtype=jnp.int32) // GLA_HEAD_K
    vhead = jnp.arange(D_MODEL, dtype=jnp.int32) // GLA_HEAD_V
    g_val = (khead[:, None] == vhead[None, :]).astype(BF16)
    return dict(
        dt_bias=padv(ssd_dt_bias), a_log=padv(ssd_A_log),
        d_exp=jnp.repeat(ssd_D.astype(F32), SSD_HEAD_DIM).reshape(1, D_MODEL),
        ssd_norm=ssd_norm.astype(F32).reshape(1, D_MODEL),
        e_head=e_head, g_head=g_head, g_val=g_val,
        w_gate=jnp.pad(w_gla_gate, ((0, LANE - GATE_RANK), (0, 0))).astype(BF16),
        b_gate=b_gla_gate.astype(F32).reshape(1, GLA_KEY_DIM),
        gla_norm=gla_norm.astype(F32).reshape(1, GLA_HEAD_V),
    )


def _mixers(x3, xn, ssd_conv, ssd_state, gla_state, w, p, long_seq, proj_ssd=None, proj_gla=None, other_xn=None):
    nb, seq, d = x3.shape
    ntok = nb * seq
    x2 = x3.reshape(ntok, d)
    w_ssd_in, w_gla_in, w_gate_in = w["w_in"]
    in_proj = functools.partial(matmul, xn, tm=2048, w_rows_are_outputs=True)
    proj_ssd = in_proj(w_ssd_in, name="in_proj_ssd") if proj_ssd is None else proj_ssd
    proj_gla = in_proj(w_gla_in, name="in_proj_gla") if proj_gla is None else proj_gla
    gates = in_proj(w_gate_in, name="in_proj_gates")
    hosted = (lambda wt: HostedMatmul(other_xn, wt)) if other_xn is not None else (lambda wt: None)
    other_ssd = other_gla = None
    if long_seq:
        u, ssd_new, ssd_conv_new = ssd_scan(proj_ssd.reshape(nb, seq, N_SSD_PROJ), ssd_conv,
                                            w["ssd_conv_w"], w["ssd_conv_b"], p)
        o, gla_new = gla_scan(proj_gla.reshape(nb, seq, N_GLA_PROJ), p)
    else:
        xc2, ssd_conv_new = short_conv(proj_ssd, seq, [COL_XBC], CONV_DIM, [ssd_conv], [w["ssd_conv_w"]],
                                       [w["ssd_conv_b"]], SSD_CONV, False, F32, "ssd_conv")
        u, ssd_new, other_ssd = ssd_step(xc2, proj_ssd, ssd_state, p, seq, hosted(w_ssd_in))
        o, gla_new, other_gla = gla_step(proj_gla, gla_state, p, seq, hosted(w_gla_in))
    merged = merge_branches(u, o, w["w_ssd_out"], w["w_gla_out"], gates)
    h, hn = mm_res_norm(merged, w["w_mix_out"], x2, w["norm_cross"], True, BF16, "mix_out")
    return h, hn, ssd_conv_new, ssd_new, gla_new, (other_ssd, other_gla)


def _conv_ffn(h2, hn2, ffn_conv, w, nb, seq, long_seq, rider=None):
    ntok, d = h2.shape
    ffn = w["w_down"].shape[0]
    cw, cbias = w["ffn_conv_w"], w["ffn_conv_b"]
    rider_out = None
    if long_seq:
        act, ffn_conv_new, rider_out = ffn_up_conv_act(hn2.reshape(nb, seq, d), w["w_up"], cw, cbias, ffn_conv, rider)
    else:
        up = matmul(hn2, w["w_up"], name="ffn_up")
        act, fa, fg = short_conv(up, seq, [0, ffn], ffn, [ffn_conv[:, :, :ffn], ffn_conv[:, :, ffn:]],
                                 [cw[:, :ffn], cw[:, ffn:]], [cbias[:ffn], cbias[ffn:]],
                                 FFN_CONV, True, BF16, "ffn_conv")
        ffn_conv_new = jnp.concatenate([fa, fg], axis=-1)
    y = mm_res_norm(act.reshape(ntok, ffn), w["w_down"], h2, w["norm_final"], False, F32, "ffn_down")
    return y.reshape(nb, seq, d), ffn_conv_new, rider_out


def kernel(x_prompt, x_sample, cache_mem_k, cache_mem_v, state_ssd_conv, state_ssd, state_gla, state_ffn_conv, mem_prompt, norm_mix, w_in, ssd_conv_w, ssd_conv_b, ssd_dt_bias, ssd_A_log, ssd_D, ssd_norm, w_ssd_out, w_gla_gate, b_gla_gate, gla_norm, w_gla_out, w_mix_out, norm_cross, norm_mem, w_cq, w_ck, w_cv, w_co, norm_ffn, w_up, ffn_conv_w, ffn_conv_b, w_down, norm_final):
    nb, seq, d = x_prompt.shape
    n_mem = mem_prompt.shape[1]
    ffn2 = w_up.shape[1]
    w = dict(
        norm_mix=norm_mix, norm_cross=norm_cross, norm_ffn=norm_ffn, norm_final=norm_final,
        w_in=_split_w_in(w_in), ssd_conv_w=ssd_conv_w, ssd_conv_b=ssd_conv_b,
        w_ssd_out=w_ssd_out.astype(BF16), w_gla_out=w_gla_out.astype(BF16), w_mix_out=w_mix_out.astype(BF16),
        w_cq=w_cq.astype(BF16), w_co=w_co.astype(BF16), w_up=w_up.astype(BF16), w_down=w_down.astype(BF16),
        ffn_conv_w=ffn_conv_w, ffn_conv_b=ffn_conv_b,
    )
    p = _params(ssd_dt_bias, ssd_A_log, ssd_D, ssd_norm, w_gla_gate, b_gla_gate, gla_norm)

    mn = rmsnorm_cast(mem_prompt.reshape(nb * n_mem, d), norm_mem)
    p_mem_k = matmul(mn, w_ck.astype(BF16), name="mem_k").reshape(nb, n_mem, d)
    p_mem_v = matmul(mn, w_cv.astype(BF16), name="mem_v").reshape(nb, n_mem, d)
    zeros_ssd_conv = jnp.zeros((nb, SSD_CONV - 1, CONV_DIM), F32)
    zeros_ffn_conv = jnp.zeros((nb, FFN_CONV - 1, ffn2), F32)
    ns, sseq, _ = x_sample.shape
    xn_prompt = rmsnorm_cast(x_prompt.reshape(nb * seq, d), norm_mix)
    xn_sample = rmsnorm_cast(x_sample.reshape(ns * sseq, d), norm_mix)

    h_s, hn_s, s_ssd_conv, s_ssd, s_gla, (proj_ssd_p, proj_gla_p) = _mixers(
        x_sample, xn_sample, state_ssd_conv, state_ssd, state_gla, w, p, False, other_xn=xn_prompt)

    h_p, hn_p, p_ssd_conv, p_ssd, p_gla, _ = _mixers(
        x_prompt, xn_prompt, zeros_ssd_conv, None, None, w, p, True, proj_ssd=proj_ssd_p, proj_gla=proj_gla_p)
    h2_p, hn2_p = cross_block(hn_p, h_p, p_mem_k, p_mem_v, w["w_cq"], w["w_co"], norm_ffn, seq)

    attend_s = cross_attend_rider(matmul(hn_s, w["w_cq"], name="cross_q"), cache_mem_k, cache_mem_v, ns, sseq)
    y_prompt, p_ffn_conv, (att_s,) = _conv_ffn(h2_p, hn2_p, zeros_ffn_conv, w, nb, seq, True, rider=attend_s)

    h2_s, hn2_s = mm_res_norm(att_s.reshape(ns * sseq, d), w["w_co"], h_s, norm_ffn, True, BF16, "cross_out")
    y_sample, s_ffn_conv, _ = _conv_ffn(h2_s, hn2_s, state_ffn_conv, w, ns, sseq, False)

    head_shape = (n_mem, CROSS_HEADS, CROSS_HEAD_DIM)
    return (y_prompt, y_sample, p_ssd_conv, p_ssd, p_gla, p_ffn_conv,
            p_mem_k.reshape((nb,) + head_shape), p_mem_v.reshape((nb,) + head_shape),
            s_ssd_conv, s_ssd, s_gla, s_ffn_conv)
```

```python
import dataclasses
import functools
from typing import NamedTuple

import jax
import jax.numpy as jnp
from jax import lax
from jax.experimental import pallas as pl
from jax.experimental.pallas import tpu as pltpu

F32 = jnp.float32
BF16 = jnp.bfloat16
EPS = 1e-6
NEG_BIG = -1e30

D_MODEL = 2048
SSD_HEAD_DIM = 64
SSD_HEADS = 32
D_STATE = 128
N_GROUPS = 4
GROUP_W = D_MODEL // N_GROUPS
BC_W = N_GROUPS * D_STATE
CONV_DIM = D_MODEL + 2 * BC_W
SSD_CONV = 4
SSD_CHUNK = 128
GLA_HEADS = 4
GLA_KEY_DIM = 1024
GLA_HEAD_K = 256
GLA_HEAD_V = 512
GATE_RANK = 16
GATE_TAU = 16.0
GLA_CHUNK = 64
CROSS_HEADS = 4
CROSS_HEAD_DIM = 512
FFN_CONV = 3
LANE = 128

COL_Z, COL_XBC, COL_DT, N_SSD_PROJ = 0, 2048, 5120, 5632
COL_Q, COL_K, COL_V, COL_R, COL_GLR, N_GLA_PROJ = 0, 1024, 2048, 4096, 6144, 6656
COL_GA, COL_GB = 0, 2048

VMEM_LIMIT = 56 * 1024 * 1024


def _cp(*sem):
    return pltpu.CompilerParams(dimension_semantics=sem, vmem_limit_bytes=VMEM_LIMIT)


def _dot(a, b, prec=None):
    return jnp.dot(a, b, preferred_element_type=F32, precision=prec)


def _dot_nt(a, b):
    return lax.dot_general(a, b, (((1,), (1,)), ((), ())), preferred_element_type=F32)


def _split3(x):
    x1 = x.astype(BF16)
    r1 = x - x1.astype(F32)
    x2 = r1.astype(BF16)
    x3 = (r1 - x2.astype(F32)).astype(BF16)
    return x1, x2, x3


def _dot_sel(x, sel, sel_first=False):
    parts = _split3(x)
    if sel_first:
        return _dot(sel, parts[0]) + _dot(sel, parts[1]) + _dot(sel, parts[2])
    return _dot(parts[0], sel) + _dot(parts[1], sel) + _dot(parts[2], sel)


def _sigmoid(x):
    return 1.0 / (1.0 + jnp.exp(-x))


def _silu(x):
    return x * _sigmoid(x)


def _softplus(x):
    return jnp.maximum(x, 0.0) + jnp.log(1.0 + jnp.exp(-jnp.abs(x)))


def _rms(x, g):
    ms = jnp.mean(x * x, axis=-1, keepdims=True)
    return x * lax.rsqrt(ms + EPS) * g


def _rmsnorm_kernel(x_ref, g_ref, o_ref):
    o_ref[...] = _rms(x_ref[...], g_ref[...]).astype(o_ref.dtype)


def rmsnorm_cast(x2, g):
    m, d = x2.shape
    tm = min(m, 512)
    return pl.pallas_call(
        _rmsnorm_kernel,
        grid=(m // tm,),
        in_specs=[pl.BlockSpec((tm, d), lambda i: (i, 0)), pl.BlockSpec((1, d), lambda i: (0, 0))],
        out_specs=pl.BlockSpec((tm, d), lambda i: (i, 0)),
        out_shape=jax.ShapeDtypeStruct((m, d), BF16),
        compiler_params=_cp("parallel"),
        name="rmsnorm_cast",
    )(x2, g.reshape(1, d))


def _mm_kernel(a_ref, w_ref, o_ref, *, w_rows_are_outputs):
    dot = _dot_nt if w_rows_are_outputs else _dot
    o_ref[...] = dot(a_ref[...], w_ref[...]).astype(o_ref.dtype)


def matmul(a, w, out_dtype=F32, tm=1024, tn=512, w_rows_are_outputs=False, name="matmul"):
    m, k = a.shape
    n = w.shape[0] if w_rows_are_outputs else w.shape[1]
    tm = min(m, tm)
    if w_rows_are_outputs:
        w_spec = pl.BlockSpec((tn, k), lambda i, j: (j, 0))
    else:
        w_spec = pl.BlockSpec((k, tn), lambda i, j: (0, j))
    return pl.pallas_call(
        functools.partial(_mm_kernel, w_rows_are_outputs=w_rows_are_outputs),
        grid=(m // tm, n // tn),
        in_specs=[pl.BlockSpec((tm, k), lambda i, j: (i, 0)), w_spec],
        out_specs=pl.BlockSpec((tm, tn), lambda i, j: (i, j)),
        out_shape=jax.ShapeDtypeStruct((m, n), out_dtype),
        compiler_params=_cp("parallel", "arbitrary"),
        name=name,
    )(a, w)


def _mem_kv_kernel(a_ref, wk_ref, wv_ref, k_ref, v_ref, k4_ref, v4_ref):
    a = a_ref[...]
    for w_ref, flat_ref, heads_ref in ((wk_ref, k_ref, k4_ref), (wv_ref, v_ref, v4_ref)):
        y = _dot(a, w_ref[...])
        flat_ref[...] = y
        heads_ref[...] = y.reshape(heads_ref.shape)


def mem_kv(mn, w_ck, w_cv):
    m, d = mn.shape
    tm = min(m, 256)
    flat = pl.BlockSpec((tm, d), lambda i: (i, 0))
    heads = pl.BlockSpec((tm, CROSS_HEADS, CROSS_HEAD_DIM), lambda i: (i, 0, 0))
    weight = pl.BlockSpec((d, d), lambda i: (0, 0), pipeline_mode=pl.Buffered(1))
    return pl.pallas_call(
        _mem_kv_kernel,
        grid=(m // tm,),
        in_specs=[flat, weight, weight],
        out_specs=[flat, flat, heads, heads],
        out_shape=[jax.ShapeDtypeStruct((m, d), F32)] * 2
        + [jax.ShapeDtypeStruct((m, CROSS_HEADS, CROSS_HEAD_DIM), F32)] * 2,
        compiler_params=_cp("parallel"),
        name="mem_kv",
    )(mn, w_ck, w_cv)


class HostedMatmul(NamedTuple):
    a: jax.Array
    w: jax.Array
    tm: int = 2048
    tn: int = 512


def call_with_hosted_matmul(body, nsteps, in_specs, out_specs, out_shape, args, name, hosted=None):
    if hosted is None:
        outs = pl.pallas_call(body, grid=(nsteps,), in_specs=in_specs, out_specs=out_specs, out_shape=out_shape,
                              compiler_params=_cp("arbitrary"), name=name)(*args)
        return list(outs), None
    m, k = hosted.a.shape
    n = hosted.w.shape[0]
    tm, tn = min(m, hosted.tm), hosted.tn
    nj = n // tn
    hsteps = (m // tm) * nj
    total = max(nsteps, hsteps)
    hc = lambda i: jnp.minimum(i, hsteps - 1)
    h_in = [pl.BlockSpec((tm, k), lambda i: (hc(i) // nj, 0)), pl.BlockSpec((tn, k), lambda i: (hc(i) % nj, 0))]
    h_out = pl.BlockSpec((tm, tn), lambda i: (hc(i) // nj, hc(i) % nj))
    n_in, n_out = len(in_specs), len(out_specs)

    def fused(*refs):
        g_in = refs[:n_in]
        ha_ref, hw_ref = refs[n_in : n_in + 2]
        g_out = refs[n_in + 2 : n_in + 2 + n_out]
        ho_ref = refs[n_in + 2 + n_out]
        step = pl.program_id(0)

        def guest():
            body(*g_in, *g_out)

        def host():
            ho_ref[...] = _dot_nt(ha_ref[...], hw_ref[...])

        guest() if nsteps == total else pl.when(step < nsteps)(guest)
        host() if hsteps == total else pl.when(step < hsteps)(host)

    outs = pl.pallas_call(
        fused,
        grid=(total,),
        in_specs=list(in_specs) + h_in,
        out_specs=list(out_specs) + [h_out],
        out_shape=list(out_shape) + [jax.ShapeDtypeStruct((m, n), F32)],
        compiler_params=_cp("arbitrary"),
        name=name,
    )(*args, hosted.a, hosted.w)
    return list(outs[:-1]), outs[-1]


def _mm_res_norm_kernel(a_ref, w_ref, res_ref, g_ref, *out_refs):
    h = res_ref[...] + _dot(a_ref[...], w_ref[...])
    if len(out_refs) == 2:
        out_refs[0][...] = h
    out_refs[-1][...] = _rms(h, g_ref[...]).astype(out_refs[-1].dtype)


def mm_res_norm(a, w, res, g, emit_h, norm_dtype, name):
    m, kdim = a.shape
    n = w.shape[1]
    tm = min(m, 512)
    row_tile = lambda width: pl.BlockSpec((tm, width), lambda i: (i, 0))
    out_shape = [jax.ShapeDtypeStruct((m, n), norm_dtype)]
    out_specs = [row_tile(n)]
    if emit_h:
        out_shape = [jax.ShapeDtypeStruct((m, n), F32)] + out_shape
        out_specs = [row_tile(n)] + out_specs
    outs = pl.pallas_call(
        _mm_res_norm_kernel,
        grid=(m // tm,),
        in_specs=[row_tile(kdim), pl.BlockSpec((kdim, n), lambda i: (0, 0), pipeline_mode=pl.Buffered(1)),
                  row_tile(n), pl.BlockSpec((1, n), lambda i: (0, 0))],
        out_specs=out_specs,
        out_shape=out_shape,
        compiler_params=_cp("parallel"),
        name=name,
    )(a, w, res, g.reshape(1, n))
    return outs if emit_h else outs[0]


def _merge_kernel(u_ref, o_ref, wa_ref, wb_ref, ga_ref, gb_ref, out_ref):
    a = _dot(u_ref[...], wa_ref[...])
    b = _dot(o_ref[...], wb_ref[...])
    out_ref[...] = (_sigmoid(ga_ref[...]) * a + _sigmoid(gb_ref[...]) * b).astype(out_ref.dtype)


def merge_branches(u, o, wa, wb, proj):
    m, d = u.shape
    tm, tn = min(m, 1024), 512
    return pl.pallas_call(
        _merge_kernel,
        grid=(m // tm, d // tn),
        in_specs=[
            pl.BlockSpec((tm, d), lambda i, j: (i, 0)),
            pl.BlockSpec((tm, d), lambda i, j: (i, 0)),
            pl.BlockSpec((d, tn), lambda i, j: (0, j)),
            pl.BlockSpec((d, tn), lambda i, j: (0, j)),
            pl.BlockSpec((tm, tn), lambda i, j: (i, COL_GA // tn + j)),
            pl.BlockSpec((tm, tn), lambda i, j: (i, COL_GB // tn + j)),
        ],
        out_specs=pl.BlockSpec((tm, tn), lambda i, j: (i, j)),
        out_shape=jax.ShapeDtypeStruct((m, d), BF16),
        compiler_params=_cp("parallel", "arbitrary"),
        name="merge_branches",
    )(u, o, wa, wb, proj, proj)


CONV_PAD = 8


def _short_conv_kernel(*refs, taps, seq, nstreams, swiglu):
    ins = refs[: 4 * nstreams]
    out_ref = refs[4 * nstreams]
    ns_refs = refs[4 * nstreams + 1 :]
    vals = []
    for s in range(nstreams):
        u_ref, st_ref, w_ref, b_ref = ins[4 * s : 4 * s + 4]
        nseq, _, tc = st_ref.shape
        full = jnp.concatenate([st_ref[...], u_ref[...].reshape(nseq, seq, tc)], axis=1)
        acc = b_ref[...]
        for k in range(taps):
            acc = acc + full[:, k : k + seq] * w_ref[k : k + 1, :]
        vals.append(acc)
        ns_refs[s][...] = full[:, seq : seq + taps - 1]
    out = _silu(vals[1]) * vals[0] if swiglu else _silu(vals[0])
    out_ref[...] = out.reshape(out_ref.shape).astype(out_ref.dtype)


def short_conv(u2, seq, col_offs, width, states, ws, bs, taps, swiglu, out_dtype, name):
    ntok = u2.shape[0]
    nseq = ntok // seq
    tc = 512
    nstreams = len(col_offs)
    in_specs, args = [], []
    for s in range(nstreams):
        cb = col_offs[s] // tc
        in_specs += [
            pl.BlockSpec((ntok, tc), lambda c, cb=cb: (0, cb + c)),
            pl.BlockSpec((nseq, taps - 1, tc), lambda c: (0, 0, c)),
            pl.BlockSpec((taps, tc), lambda c: (0, c)),
            pl.BlockSpec((1, tc), lambda c: (0, c)),
        ]
        args += [u2, states[s], ws[s], bs[s].reshape(1, width)]
    out_specs = [pl.BlockSpec((ntok, tc), lambda c: (0, c))]
    out_shape = [jax.ShapeDtypeStruct((ntok, width), out_dtype)]
    for s in range(nstreams):
        out_specs.append(pl.BlockSpec((nseq, taps - 1, tc), lambda c: (0, 0, c)))
        out_shape.append(jax.ShapeDtypeStruct((nseq, taps - 1, width), F32))
    return pl.pallas_call(
        functools.partial(_short_conv_kernel, taps=taps, seq=seq, nstreams=nstreams, swiglu=swiglu),
        grid=(width // tc,),
        in_specs=in_specs,
        out_specs=out_specs,
        out_shape=out_shape,
        compiler_params=_cp("parallel"),
        name=name,
    )(*args)


def _shift_rows(u, d, prev):
    x = pltpu.roll(u, d, 0)
    r = lax.broadcasted_iota(jnp.int32, prev.shape, 0)
    head = jnp.where(r < d, pltpu.roll(prev, d, 0), x[:CONV_PAD])
    return jnp.concatenate([head, x[CONV_PAD:]], axis=0)


def _causal_taps(u, prev, cw, bias):
    taps = cw.shape[0]
    acc = bias
    for k in range(taps):
        d = taps - 1 - k
        acc = acc + (_shift_rows(u, d, prev) if d else u) * cw[k : k + 1, :]
    return acc


class RiderKernel(NamedTuple):
    body: object
    nsteps: int
    in_specs: list
    out_specs: list
    out_shape: list
    args: tuple


def _ffn_up_kernel(*refs, tm, rider_body, n_rider_in, n_rider_out, rider_steps):
    n_in = 9
    hn_ref, wa_ref, wg_ref, cwa_ref, cwg_ref, cba_ref, cbg_ref, sta_ref, stg_ref = refs[:n_in]
    rider_in = refs[n_in : n_in + n_rider_in]
    act_ref, nsa_ref, nsg_ref = refs[n_in + n_rider_in : n_in + n_rider_in + 3]
    rider_out = refs[n_in + n_rider_in + 3 : n_in + n_rider_in + 3 + n_rider_out]
    preva, prevg = refs[-2:]
    m = pl.program_id(2)
    nm = pl.num_programs(2)
    lo = CONV_PAD - (FFN_CONV - 1)

    @pl.when(m == 0)
    def _():
        for prev, st_ref in ((preva, sta_ref), (prevg, stg_ref)):
            prev[0:lo, :] = jnp.zeros((lo, prev.shape[1]), F32)
            prev[lo:CONV_PAD, :] = st_ref[0]

    hn = hn_ref[0]
    vals = []
    for w_ref, cw_ref, cb_ref, prev in ((wa_ref, cwa_ref, cba_ref, preva), (wg_ref, cwg_ref, cbg_ref, prevg)):
        u = _dot(hn, w_ref[...])
        vals.append(_causal_taps(u, prev[...], cw_ref[...], cb_ref[...]))
        prev[...] = u[tm - CONV_PAD : tm]
    act_ref[0] = (_silu(vals[1]) * vals[0]).astype(act_ref.dtype)

    @pl.when(m == nm - 1)
    def _():
        nsa_ref[0] = preva[lo:CONV_PAD, :]
        nsg_ref[0] = prevg[lo:CONV_PAD, :]

    if rider_body is not None:
        step = (pl.program_id(0) * pl.num_programs(1) + pl.program_id(1)) * nm + m

        @pl.when(step < rider_steps)
        def _():
            rider_body(*rider_in, *rider_out)


def ffn_up_conv_act(hn3, w_up, conv_w, conv_b, conv_state, rider=None):
    nb, seq, d = hn3.shape
    ffn = w_up.shape[1] // 2
    tm, tn = min(seq, 1024), 512
    nn, nm = ffn // tn, seq // tm
    half = lambda off: (lambda b, n, m: (0, off + n))
    st_spec = lambda off: pl.BlockSpec((1, FFN_CONV - 1, tn), lambda b, n, m: (b, 0, off + n))
    cb2 = conv_b.reshape(1, 2 * ffn)
    in_specs = [
        pl.BlockSpec((1, tm, d), lambda b, n, m: (b, m, 0)),
        pl.BlockSpec((d, tn), half(0)), pl.BlockSpec((d, tn), half(nn)),
        pl.BlockSpec((FFN_CONV, tn), half(0)), pl.BlockSpec((FFN_CONV, tn), half(nn)),
        pl.BlockSpec((1, tn), half(0)), pl.BlockSpec((1, tn), half(nn)),
        st_spec(0), st_spec(nn),
    ]
    out_specs = [
        pl.BlockSpec((1, tm, tn), lambda b, n, m: (b, m, n)),
        pl.BlockSpec((1, FFN_CONV - 1, tn), lambda b, n, m: (b, 0, n)),
        pl.BlockSpec((1, FFN_CONV - 1, tn), lambda b, n, m: (b, 0, n)),
    ]
    out_shape = [
        jax.ShapeDtypeStruct((nb, seq, ffn), BF16),
        jax.ShapeDtypeStruct((nb, FFN_CONV - 1, ffn), F32),
        jax.ShapeDtypeStruct((nb, FFN_CONV - 1, ffn), F32),
    ]
    args = [hn3, w_up, w_up, conv_w, conv_w, cb2, cb2, conv_state, conv_state]
    rider_kw = dict(rider_body=None, n_rider_in=0, n_rider_out=0, rider_steps=0)
    if rider is not None and rider.nsteps > nb * nn * nm:
        rider_outs, _ = call_with_hosted_matmul(rider.body, rider.nsteps, rider.in_specs, rider.out_specs,
                                                rider.out_shape, rider.args, "rider")
        act, conv_new, _ = ffn_up_conv_act(hn3, w_up, conv_w, conv_b, conv_state)
        return act, conv_new, rider_outs
    if rider is not None:
        on_grid = lambda spec: dataclasses.replace(
            spec, index_map=lambda b, n, m, im=spec.index_map: im((b * nn + n) * nm + m))
        in_specs += [on_grid(s) for s in rider.in_specs]
        out_specs += [on_grid(s) for s in rider.out_specs]
        out_shape += list(rider.out_shape)
        args += list(rider.args)
        rider_kw = dict(rider_body=rider.body, n_rider_in=len(rider.in_specs), n_rider_out=len(rider.out_specs),
                        rider_steps=rider.nsteps)
    outs = pl.pallas_call(
        functools.partial(_ffn_up_kernel, tm=tm, **rider_kw),
        grid=(nb, nn, nm),
        in_specs=in_specs,
        out_specs=out_specs,
        out_shape=out_shape,
        scratch_shapes=[pltpu.VMEM((CONV_PAD, tn), F32), pltpu.VMEM((CONV_PAD, tn), F32)],
        compiler_params=_cp("parallel" if rider is None else "arbitrary", "parallel" if rider is None else "arbitrary",
                            "arbitrary"),
        name="ffn_up_conv_act",
    )(*args)
    act, nsa, nsg = outs[:3]
    return act, jnp.concatenate([nsa, nsg], axis=-1), list(outs[3:])


def _ssd_gate_norm(y, z, nrm):
    ug = y * _silu(z)
    outs = []
    for g in range(N_GROUPS):
        ugg = ug[:, g * GROUP_W : (g + 1) * GROUP_W]
        ms = jnp.mean(ugg * ugg, axis=-1, keepdims=True)
        outs.append(ugg * lax.rsqrt(ms + EPS))
    return jnp.concatenate(outs, axis=1) * nrm


def _gla_out_norm(o, r, nrm):
    outs = []
    for h in range(GLA_HEADS):
        oh = o[:, h * GLA_HEAD_V : (h + 1) * GLA_HEAD_V]
        rh = r[:, h * GLA_HEAD_V : (h + 1) * GLA_HEAD_V]
        outs.append(_rms(oh, nrm) * _silu(rh))
    return jnp.concatenate(outs, axis=1)


SCAN_ROWS_PER_STEP = 256


def _ssd_chunk(xs_ref, b_ref, c_ref, dt_ref, z_ref, cw_ref, cbias_ref, dtb_ref, alog_ref, dexp_ref, nrm_ref, e_ref,
               u_ref, st_ref, cprev):
    q = SSD_CHUNK
    conv = []
    for raw_ref, c0 in ((xs_ref, 0), (b_ref, D_MODEL), (c_ref, D_MODEL + BC_W)):
        raw = raw_ref[...]
        cols = slice(c0, c0 + raw.shape[1])
        conv.append(_silu(_causal_taps(raw, cprev[:, cols], cw_ref[:, cols], cbias_ref[:, cols])))
        cprev[:, cols] = raw[q - CONV_PAD : q]
    xs, bm, cm = conv
    dt = _softplus(dt_ref[...] + dtb_ref[...])
    a = dt * (-jnp.exp(alog_ref[...]))
    row = lax.broadcasted_iota(jnp.int32, (q, q), 0)
    col = lax.broadcasted_iota(jnp.int32, (q, q), 1)
    tril = row >= col
    acum = _dot_sel(a, tril.astype(BF16), sel_first=True)
    acum_t = acum.T
    dt_t = dt.T
    last = acum[q - 1 : q, :]
    e_mat = e_ref[...]
    eexp = _dot_sel(jnp.exp(acum), e_mat)
    wexp = _dot_sel(jnp.exp(last - acum) * dt, e_mat)
    s_bf = st_ref[...].astype(BF16)
    cb16 = cm.astype(BF16)
    bb16 = bm.astype(BF16)
    x16 = xs.astype(BF16)
    xw16 = (xs * wexp).astype(BF16)
    lane_lo = lax.broadcasted_iota(jnp.int32, (q, LANE), 1) < SSD_HEAD_DIM
    ys = []
    for g in range(N_GROUPS):
        cg = cb16[:, g * D_STATE : (g + 1) * D_STATE]
        bg = bb16[:, g * D_STATE : (g + 1) * D_STATE]
        cb = _dot_nt(cg, bg)
        yoff = _dot(cg, s_bf[:, g * GROUP_W : (g + 1) * GROUP_W])
        pieces = []
        for pr in range(GROUP_W // LANE):
            h0 = g * (SSD_HEADS // N_GROUPS) + 2 * pr
            xp = x16[:, h0 * SSD_HEAD_DIM : h0 * SSD_HEAD_DIM + LANE]
            yh = []
            for h in (h0, h0 + 1):
                diff = acum[:, h : h + 1] - acum_t[h : h + 1, :]
                dec = jnp.exp(jnp.where(tril, diff, NEG_BIG))
                m = (cb * dec * dt_t[h : h + 1, :]).astype(BF16)
                yh.append(_dot(m, xp))
            pieces.append(jnp.where(lane_lo, yh[0], yh[1]))
        sl = slice(g * GROUP_W, (g + 1) * GROUP_W)
        ys.append(jnp.concatenate(pieces, axis=1) + yoff * eexp[:, sl])
        bg_t = bm[:, g * D_STATE : (g + 1) * D_STATE].T.astype(BF16)
        upd = _dot(bg_t, xw16[:, sl])
        st_ref[:, sl] = eexp[q - 1 : q, sl] * st_ref[:, sl] + upd
    y = jnp.concatenate(ys, axis=1) + dexp_ref[...] * xs
    u_ref[...] = _ssd_gate_norm(y, z_ref[...], nrm_ref[...]).astype(u_ref.dtype)


def _ssd_scan_kernel(xs_ref, b_ref, c_ref, dt_ref, z_ref, cst_ref, cw_ref, cbias_ref,
                     dtb_ref, alog_ref, dexp_ref, nrm_ref, e_ref,
                     u_ref, sout_ref, cso_ref, st_ref, cprev, *, nsteps, nsub):
    step = pl.program_id(1)
    lo = CONV_PAD - (SSD_CONV - 1)

    @pl.when(step == 0)
    def _():
        st_ref[...] = jnp.zeros_like(st_ref)
        cprev[0:lo, :] = jnp.zeros((lo, CONV_DIM), F32)
        cprev[lo:CONV_PAD, :] = cst_ref[0]

    for sub in range(nsub):
        rows = lambda ref: ref.at[0, pl.ds(sub * SSD_CHUNK, SSD_CHUNK)]
        _ssd_chunk(rows(xs_ref), rows(b_ref), rows(c_ref), rows(dt_ref), rows(z_ref), cw_ref, cbias_ref,
                   dtb_ref, alog_ref, dexp_ref, nrm_ref, e_ref, rows(u_ref), st_ref, cprev)

    @pl.when(step == nsteps - 1)
    def _():
        sout_ref[0] = st_ref[...].T
        cso_ref[0] = cprev[lo:CONV_PAD, :]


def ssd_scan(proj3, conv_state, conv_w, conv_b, p):
    nb, seq, _ = proj3.shape
    nsub = SCAN_ROWS_PER_STEP // SSD_CHUNK if seq % SCAN_ROWS_PER_STEP == 0 else 1
    q = nsub * SSD_CHUNK
    nsteps = seq // q
    vec = lambda n: pl.BlockSpec((1, n), lambda b, c: (0, 0))
    u, s_out, conv_new = pl.pallas_call(
        functools.partial(_ssd_scan_kernel, nsteps=nsteps, nsub=nsub),
        grid=(nb, nsteps),
        in_specs=[
            pl.BlockSpec((1, q, D_MODEL), lambda b, c: (b, c, COL_XBC // D_MODEL)),
            pl.BlockSpec((1, q, BC_W), lambda b, c: (b, c, (COL_XBC + D_MODEL) // BC_W)),
            pl.BlockSpec((1, q, BC_W), lambda b, c: (b, c, (COL_XBC + D_MODEL) // BC_W + 1)),
            pl.BlockSpec((1, q, LANE), lambda b, c: (b, c, COL_DT // LANE)),
            pl.BlockSpec((1, q, D_MODEL), lambda b, c: (b, c, COL_Z // D_MODEL)),
            pl.BlockSpec((1, SSD_CONV - 1, CONV_DIM), lambda b, c: (b, 0, 0)),
            pl.BlockSpec((SSD_CONV, CONV_DIM), lambda b, c: (0, 0)),
            vec(CONV_DIM),
            vec(LANE), vec(LANE), vec(D_MODEL), vec(D_MODEL),
            pl.BlockSpec((LANE, D_MODEL), lambda b, c: (0, 0)),
        ],
        out_specs=[
            pl.BlockSpec((1, q, D_MODEL), lambda b, c: (b, c, 0)),
            pl.BlockSpec((1, D_MODEL, D_STATE), lambda b, c: (b, 0, 0)),
            pl.BlockSpec((1, SSD_CONV - 1, CONV_DIM), lambda b, c: (b, 0, 0)),
        ],
        out_shape=[
            jax.ShapeDtypeStruct((nb, seq, D_MODEL), BF16),
            jax.ShapeDtypeStruct((nb, D_MODEL, D_STATE), F32),
            jax.ShapeDtypeStruct((nb, SSD_CONV - 1, CONV_DIM), F32),
        ],
        scratch_shapes=[pltpu.VMEM((D_STATE, D_MODEL), F32), pltpu.VMEM((CONV_PAD, CONV_DIM), F32)],
        compiler_params=_cp("parallel", "arbitrary"),
        name="ssd_scan",
    )(proj3, proj3, proj3, proj3, proj3, conv_state, conv_w, conv_b.reshape(1, CONV_DIM),
      p["dt_bias"], p["a_log"], p["d_exp"], p["ssd_norm"], p["e_head"])
    return u.reshape(nb * seq, D_MODEL), s_out.reshape(nb, SSD_HEADS, SSD_HEAD_DIM, D_STATE), conv_new


def _gla_gate_log(glr, wg, bg):
    x = _dot(glr.astype(BF16), wg) + bg
    return -_softplus(-x) / GATE_TAU


def _gla_chunk(q_ref, k_ref, v_ref, r_ref, glr_ref, wg_ref, bg_ref, nrm_ref, o_ref, st_ref):
    q = GLA_CHUNK
    glog = _gla_gate_log(glr_ref[...], wg_ref[...], bg_ref[...])
    row = lax.broadcasted_iota(jnp.int32, (q, q), 0)
    col = lax.broadcasted_iota(jnp.int32, (q, q), 1)
    tril = row >= col
    bc = _dot_sel(glog, tril.astype(BF16), sel_first=True)
    last = bc[q - 1 : q, :]
    kk = k_ref[...]
    qe = q_ref[...] * (GLA_HEAD_K ** -0.5) * jnp.exp(bc)
    ke = kk * jnp.exp(-bc)
    kd = kk * jnp.exp(last - bc)
    elast = jnp.exp(last)
    v16 = v_ref[...].astype(BF16)
    zeros_v = jnp.zeros((q, GLA_HEAD_V), BF16)
    outs = []
    for h in range(GLA_HEADS):
        ks = slice(h * GLA_HEAD_K, (h + 1) * GLA_HEAD_K)
        vs = slice(h * GLA_HEAD_V, (h + 1) * GLA_HEAD_V)
        qh = qe[:, ks].astype(BF16)
        kh = ke[:, ks].astype(BF16)
        att = jnp.where(tril, _dot_nt(qh, kh), 0.0)
        s_h = st_ref[ks, :]
        outs.append(_dot(att.astype(BF16), v16[:, vs]) + _dot(qh, s_h.astype(BF16)))
        xt = jnp.concatenate([kd[:, ks], jnp.broadcast_to(elast[:, ks], (q, GLA_HEAD_K))], axis=0).T
        v2 = jnp.concatenate([v16[:, vs], zeros_v], axis=0)
        st_ref[ks, :] = xt[:, q : q + 1] * s_h + _dot(xt.astype(BF16), v2)
    o = jnp.concatenate(outs, axis=1)
    o_ref[...] = _gla_out_norm(o, r_ref[...], nrm_ref[...]).astype(o_ref.dtype)


def _gla_scan_kernel(q_ref, k_ref, v_ref, r_ref, glr_ref, wg_ref, bg_ref, nrm_ref,
                     o_ref, sout_ref, st_ref, *, nsteps, nsub):
    step = pl.program_id(1)

    @pl.when(step == 0)
    def _():
        st_ref[...] = jnp.zeros_like(st_ref)

    for sub in range(nsub):
        rows = lambda ref: ref.at[0, pl.ds(sub * GLA_CHUNK, GLA_CHUNK)]
        _gla_chunk(rows(q_ref), rows(k_ref), rows(v_ref), rows(r_ref), rows(glr_ref), wg_ref, bg_ref, nrm_ref,
                   rows(o_ref), st_ref)

    @pl.when(step == nsteps - 1)
    def _():
        sout_ref[0] = st_ref[...]


def gla_scan(proj3, p):
    nb, seq, _ = proj3.shape
    nsub = SCAN_ROWS_PER_STEP // GLA_CHUNK if seq % SCAN_ROWS_PER_STEP == 0 else 1
    q = nsub * GLA_CHUNK
    nsteps = seq // q
    o, s_out = pl.pallas_call(
        functools.partial(_gla_scan_kernel, nsteps=nsteps, nsub=nsub),
        grid=(nb, nsteps),
        in_specs=[
            pl.BlockSpec((1, q, GLA_KEY_DIM), lambda b, c: (b, c, COL_Q // GLA_KEY_DIM)),
            pl.BlockSpec((1, q, GLA_KEY_DIM), lambda b, c: (b, c, COL_K // GLA_KEY_DIM)),
            pl.BlockSpec((1, q, D_MODEL), lambda b, c: (b, c, COL_V // D_MODEL)),
            pl.BlockSpec((1, q, D_MODEL), lambda b, c: (b, c, COL_R // D_MODEL)),
            pl.BlockSpec((1, q, LANE), lambda b, c: (b, c, COL_GLR // LANE)),
            pl.BlockSpec((LANE, GLA_KEY_DIM), lambda b, c: (0, 0)),
            pl.BlockSpec((1, GLA_KEY_DIM), lambda b, c: (0, 0)),
            pl.BlockSpec((1, GLA_HEAD_V), lambda b, c: (0, 0)),
        ],
        out_specs=[
            pl.BlockSpec((1, q, D_MODEL), lambda b, c: (b, c, 0)),
            pl.BlockSpec((1, GLA_KEY_DIM, GLA_HEAD_V), lambda b, c: (b, 0, 0)),
        ],
        out_shape=[
            jax.ShapeDtypeStruct((nb, seq, D_MODEL), BF16),
            jax.ShapeDtypeStruct((nb, GLA_KEY_DIM, GLA_HEAD_V), F32),
        ],
        scratch_shapes=[pltpu.VMEM((GLA_KEY_DIM, GLA_HEAD_V), F32)],
        compiler_params=_cp("parallel", "arbitrary"),
        name="gla_scan",
    )(proj3, proj3, proj3, proj3, proj3, p["w_gate"], p["b_gate"], p["gla_norm"])
    return o.reshape(nb * seq, D_MODEL), s_out.reshape(nb, GLA_HEADS, GLA_HEAD_K, GLA_HEAD_V)


TOK_BLOCK = 128


def _row_shift(x, d, tpos):
    return jnp.where(tpos >= d, pltpu.roll(x, d, 0), 0.0)


def _seq_cumsum_and_last(a, seq, tpos):
    nrows = a.shape[0]
    acum = a
    for d in range(1, seq):
        acum = acum + _row_shift(a, d, tpos)
    last = jnp.where(tpos == seq - 1, acum, 0.0)
    for d in range(1, seq):
        last = last + jnp.where(tpos == seq - 1 - d, pltpu.roll(acum, nrows - d, 0), 0.0)
    return acum, last


def _ssd_step_pre_kernel(xs_ref, b_ref, c_ref, dt_ref, dtb_ref, alog_ref, dexp_ref, e_ref, gh_ref,
                         ypart_ref, eexp_ref, xwt_ref, el_ref, *, seq):
    nrows = xs_ref.shape[0]
    xs = xs_ref[...]
    bm = b_ref[...]
    cm = c_ref[...]
    dt = _softplus(dt_ref[...] + dtb_ref[...])
    a = dt * (-jnp.exp(alog_ref[...]))
    pos = lambda w: lax.broadcasted_iota(jnp.int32, (nrows, w), 0) % seq
    t_h, t_c, t_x = pos(LANE), pos(BC_W), pos(D_MODEL)
    acum, last = _seq_cumsum_and_last(a, seq, t_h)
    e_mat = e_ref[...]
    eexp_ref[...] = _dot_sel(jnp.exp(acum), e_mat)
    wexp = _dot_sel(jnp.exp(last - acum) * dt, e_mat)
    xwt_ref[...] = (xs * wexp).T.astype(xwt_ref.dtype)
    el_ref[...] = jnp.exp(last)
    y = dexp_ref[...] * xs
    for d in range(seq):
        if d == 0:
            cbh = _dot_sel(cm * bm, gh_ref[...])
            coef = dt
            xd = xs
        else:
            cbh = _dot_sel(cm * _row_shift(bm, d, t_c), gh_ref[...])
            coef = jnp.where(t_h >= d, jnp.exp(acum - pltpu.roll(acum, d, 0)) * pltpu.roll(dt, d, 0), 0.0)
            xd = _row_shift(xs, d, t_x)
        y = y + _dot_sel(cbh * coef, e_mat) * xd
    ypart_ref[...] = y


def _ssd_step_state_kernel(st_ref, c_ref, b_ref, xwt_ref, el_ref, ypart_ref, eexp_ref, z_ref, nrm_ref,
                           u_ref, so_ref, *, sb, seq):
    i = pl.program_id(0)
    rows = sb * seq
    steps_per_block = TOK_BLOCK // rows
    base = (i % steps_per_block) * rows
    c16 = c_ref[...].astype(BF16)
    btok = b_ref[...]
    tok = lax.broadcasted_iota(jnp.int32, (TOK_BLOCK, LANE), 0)
    rsel = lax.broadcasted_iota(jnp.int32, (rows, GROUP_W), 0)
    heads_per_group = SSD_HEADS // N_GROUPS
    yoff = [jnp.zeros((rows, GROUP_W), F32) for _ in range(N_GROUPS)]
    for s in range(sb):
        lo = base + seq * s
        own = (tok >= lo) & (tok < lo + seq)
        mine = (rsel >= seq * s) & (rsel < seq * (s + 1))
        for g in range(N_GROUPS):
            sl = slice(g * GROUP_W, (g + 1) * GROUP_W)
            yo = _dot_nt(c16[:, g * D_STATE : (g + 1) * D_STATE], st_ref[s, sl, :].astype(BF16))
            yoff[g] = jnp.where(mine, yo, yoff[g])
            bsel = jnp.where(own, btok[:, g * D_STATE : (g + 1) * D_STATE], 0.0).astype(BF16)
            upd = _dot(xwt_ref[sl, :], bsel)
            for r in range(heads_per_group):
                h = g * heads_per_group + r
                hs = slice(h * SSD_HEAD_DIM, (h + 1) * SSD_HEAD_DIM)
                so_ref[s, hs, :] = el_ref[seq * s, h] * st_ref[s, hs, :] + upd[r * SSD_HEAD_DIM : (r + 1) * SSD_HEAD_DIM]
    y = ypart_ref[...] + jnp.concatenate(yoff, axis=1) * eexp_ref[...]
    u_ref[...] = _ssd_gate_norm(y, z_ref[...], nrm_ref[...]).astype(u_ref.dtype)


def ssd_step(xc2, proj2, state, p, seq, hosted=None):
    ntok = xc2.shape[0]
    nseq = ntok // seq
    full = lambda shape: pl.BlockSpec(shape, lambda i: (0,) * len(shape))
    ypart, eexp, xwt, elast = pl.pallas_call(
        functools.partial(_ssd_step_pre_kernel, seq=seq),
        grid=(1,),
        in_specs=[
            pl.BlockSpec((ntok, D_MODEL), lambda i: (0, 0)),
            pl.BlockSpec((ntok, BC_W), lambda i: (0, D_MODEL // BC_W)),
            pl.BlockSpec((ntok, BC_W), lambda i: (0, D_MODEL // BC_W + 1)),
            pl.BlockSpec((ntok, LANE), lambda i: (0, COL_DT // LANE)),
            full((1, LANE)), full((1, LANE)), full((1, D_MODEL)),
            full((LANE, D_MODEL)), full((BC_W, LANE)),
        ],
        out_specs=[full((ntok, D_MODEL)), full((ntok, D_MODEL)), full((D_MODEL, ntok)), full((ntok, LANE))],
        out_shape=[
            jax.ShapeDtypeStruct((ntok, D_MODEL), F32),
            jax.ShapeDtypeStruct((ntok, D_MODEL), F32),
            jax.ShapeDtypeStruct((D_MODEL, ntok), BF16),
            jax.ShapeDtypeStruct((ntok, LANE), F32),
        ],
        compiler_params=_cp("arbitrary"),
        name="ssd_step_pre",
    )(xc2, xc2, xc2, proj2, p["dt_bias"], p["a_log"], p["d_exp"], p["e_head"], p["g_head"])

    sb = 4
    rows = sb * seq
    spb = TOK_BLOCK // rows
    st3 = state.reshape(nseq, D_MODEL, D_STATE)
    nsteps = nseq // sb
    c = lambda i: jnp.minimum(i, nsteps - 1)
    (u, s_new), hosted_out = call_with_hosted_matmul(
        functools.partial(_ssd_step_state_kernel, sb=sb, seq=seq),
        nsteps,
        in_specs=[
            pl.BlockSpec((sb, D_MODEL, D_STATE), lambda i: (c(i), 0, 0)),
            pl.BlockSpec((rows, BC_W), lambda i: (c(i), D_MODEL // BC_W + 1)),
            pl.BlockSpec((TOK_BLOCK, BC_W), lambda i: (c(i) // spb, D_MODEL // BC_W)),
            pl.BlockSpec((D_MODEL, TOK_BLOCK), lambda i: (0, c(i) // spb)),
            pl.BlockSpec((rows, LANE), lambda i: (c(i), 0), memory_space=pltpu.SMEM),
            pl.BlockSpec((rows, D_MODEL), lambda i: (c(i), 0)),
            pl.BlockSpec((rows, D_MODEL), lambda i: (c(i), 0)),
            pl.BlockSpec((rows, D_MODEL), lambda i: (c(i), COL_Z // D_MODEL)),
            pl.BlockSpec((1, D_MODEL), lambda i: (0, 0)),
        ],
        out_specs=[
            pl.BlockSpec((rows, D_MODEL), lambda i: (c(i), 0)),
            pl.BlockSpec((sb, D_MODEL, D_STATE), lambda i: (c(i), 0, 0)),
        ],
        out_shape=[
            jax.ShapeDtypeStruct((ntok, D_MODEL), BF16),
            jax.ShapeDtypeStruct((nseq, D_MODEL, D_STATE), F32),
        ],
        args=(st3, xc2, xc2, xwt, elast, ypart, eexp, proj2, p["ssd_norm"]),
        name="ssd_step_state",
        hosted=hosted,
    )
    return u, s_new.reshape(nseq, SSD_HEADS, SSD_HEAD_DIM, D_STATE), hosted_out


def _gla_step_pre_kernel(q_ref, k_ref, v_ref, glr_ref, wg_ref, bg_ref, gv_ref,
                         oin_ref, qe_ref, kdt_ref, elt_ref, *, seq):
    nrows = q_ref.shape[0]
    glog = _gla_gate_log(glr_ref[...], wg_ref[...], bg_ref[...])
    pos = lambda w: lax.broadcasted_iota(jnp.int32, (nrows, w), 0) % seq
    t_k, t_v = pos(GLA_KEY_DIM), pos(D_MODEL)
    bc, last = _seq_cumsum_and_last(glog, seq, t_k)
    kk = k_ref[...]
    qe = q_ref[...] * (GLA_HEAD_K ** -0.5) * jnp.exp(bc)
    ke = kk * jnp.exp(-bc)
    qe_ref[...] = qe
    kdt_ref[...] = (kk * jnp.exp(last - bc)).T
    elt_ref[...] = jnp.exp(last).T
    v = v_ref[...]
    gv = gv_ref[...]
    o = jnp.zeros((nrows, D_MODEL), F32)
    for d in range(seq):
        ked = ke if d == 0 else _row_shift(ke, d, t_k)
        vd = v if d == 0 else _row_shift(v, d, t_v)
        att = _dot((qe * ked).astype(BF16), gv)
        o = o + att * vd
    oin_ref[...] = o


def _gla_step_state_kernel(st_ref, qe_ref, v_ref, kdt_ref, elt_ref, oin_ref, r_ref, nrm_ref,
                           o_ref, so_ref, *, sb, seq):
    i = pl.program_id(0)
    rows = sb * seq
    spb = TOK_BLOCK // rows
    base = (i % spb) * rows
    qe16 = qe_ref[...].astype(BF16)
    vtok = v_ref[...]
    tokv = lax.broadcasted_iota(jnp.int32, (TOK_BLOCK, GLA_HEAD_V), 0)
    tok = lax.broadcasted_iota(jnp.int32, (TOK_BLOCK, LANE), 0)
    rsel = lax.broadcasted_iota(jnp.int32, (rows, GLA_HEAD_V), 0)
    ooff = [jnp.zeros((rows, GLA_HEAD_V), F32) for _ in range(GLA_HEADS)]
    for s in range(sb):
        lo = base + seq * s
        own = (tokv >= lo) & (tokv < lo + seq)
        first = jnp.where(tok == lo, 1.0, 0.0).astype(BF16)
        mine = (rsel >= seq * s) & (rsel < seq * (s + 1))
        for h in range(GLA_HEADS):
            ks = slice(h * GLA_HEAD_K, (h + 1) * GLA_HEAD_K)
            vs = slice(h * GLA_HEAD_V, (h + 1) * GLA_HEAD_V)
            s_h = st_ref[s, ks, :]
            oo = _dot(qe16[:, ks], s_h.astype(BF16))
            ooff[h] = jnp.where(mine, oo, ooff[h])
            vsel = jnp.where(own, vtok[:, vs], 0.0).astype(BF16)
            upd = _dot(kdt_ref[ks, :].astype(BF16), vsel)
            ecol = _dot_sel(elt_ref[ks, :], first)
            so_ref[s, ks, :] = jnp.concatenate([ecol] * (GLA_HEAD_V // LANE), axis=1) * s_h + upd
    o = oin_ref[...] + jnp.concatenate(ooff, axis=1)
    o_ref[0] = _gla_out_norm(o, r_ref[...], nrm_ref[...]).astype(o_ref.dtype)


def gla_step(proj2, state, p, seq, hosted=None):
    ntok = proj2.shape[0]
    nseq = ntok // seq
    full = lambda shape: pl.BlockSpec(shape, lambda i: (0,) * len(shape))
    oin, qe, kdt, elt = pl.pallas_call(
        functools.partial(_gla_step_pre_kernel, seq=seq),
        grid=(1,),
        in_specs=[
            pl.BlockSpec((ntok, GLA_KEY_DIM), lambda i: (0, COL_Q // GLA_KEY_DIM)),
            pl.BlockSpec((ntok, GLA_KEY_DIM), lambda i: (0, COL_K // GLA_KEY_DIM)),
            pl.BlockSpec((ntok, D_MODEL), lambda i: (0, COL_V // D_MODEL)),
            pl.BlockSpec((ntok, LANE), lambda i: (0, COL_GLR // LANE)),
            full((LANE, GLA_KEY_DIM)), full((1, GLA_KEY_DIM)), full((GLA_KEY_DIM, D_MODEL)),
        ],
        out_specs=[full((ntok, D_MODEL)), full((ntok, GLA_KEY_DIM)), full((GLA_KEY_DIM, ntok)),
                   full((GLA_KEY_DIM, ntok))],
        out_shape=[
            jax.ShapeDtypeStruct((ntok, D_MODEL), F32),
            jax.ShapeDtypeStruct((ntok, GLA_KEY_DIM), F32),
            jax.ShapeDtypeStruct((GLA_KEY_DIM, ntok), F32),
            jax.ShapeDtypeStruct((GLA_KEY_DIM, ntok), F32),
        ],
        compiler_params=_cp("arbitrary"),
        name="gla_step_pre",
    )(proj2, proj2, proj2, proj2, p["w_gate"], p["b_gate"], p["g_val"])

    sb = 2 if hosted is not None else 4
    rows = sb * seq
    spb = TOK_BLOCK // rows
    st3 = state.reshape(nseq, GLA_KEY_DIM, GLA_HEAD_V)
    nsteps = nseq // sb
    c = lambda i: jnp.minimum(i, nsteps - 1)
    (o, s_new), hosted_out = call_with_hosted_matmul(
        functools.partial(_gla_step_state_kernel, sb=sb, seq=seq),
        nsteps,
        in_specs=[
            pl.BlockSpec((sb, GLA_KEY_DIM, GLA_HEAD_V), lambda i: (c(i), 0, 0)),
            pl.BlockSpec((rows, GLA_KEY_DIM), lambda i: (c(i), 0)),
            pl.BlockSpec((TOK_BLOCK, D_MODEL), lambda i: (c(i) // spb, COL_V // D_MODEL)),
            pl.BlockSpec((GLA_KEY_DIM, TOK_BLOCK), lambda i: (0, c(i) // spb)),
            pl.BlockSpec((GLA_KEY_DIM, TOK_BLOCK), lambda i: (0, c(i) // spb)),
            pl.BlockSpec((rows, D_MODEL), lambda i: (c(i), 0)),
            pl.BlockSpec((rows, D_MODEL), lambda i: (c(i), COL_R // D_MODEL)),
            pl.BlockSpec((1, GLA_HEAD_V), lambda i: (0, 0)),
        ],
        out_specs=[
            pl.BlockSpec((1, rows, D_MODEL), lambda i: (c(i), 0, 0)),
            pl.BlockSpec((sb, GLA_KEY_DIM, GLA_HEAD_V), lambda i: (c(i), 0, 0)),
        ],
        out_shape=[
            jax.ShapeDtypeStruct((nseq // sb, rows, D_MODEL), BF16),
            jax.ShapeDtypeStruct((nseq, GLA_KEY_DIM, GLA_HEAD_V), F32),
        ],
        args=(st3, qe, proj2, kdt, elt, oin, proj2, p["gla_norm"]),
        name="gla_step_state",
        hosted=hosted,
    )
    return o.reshape(ntok, D_MODEL), s_new.reshape(nseq, GLA_HEADS, GLA_HEAD_K, GLA_HEAD_V), hosted_out


def _softmax_rows(sc):
    e = jnp.exp(sc - jnp.max(sc, axis=-1, keepdims=True))
    return e / jnp.sum(e, axis=-1, keepdims=True)


def _cross_block_kernel(hn_ref, h_ref, k_ref, v_ref, wq_ref, wo_ref, g_ref, h2_ref, hn2_ref):
    q16 = _dot(hn_ref[...], wq_ref[...]).astype(BF16)
    outs = []
    for h in range(CROSS_HEADS):
        hs = slice(h * CROSS_HEAD_DIM, (h + 1) * CROSS_HEAD_DIM)
        sc = _dot_nt(q16[:, hs], k_ref[0, :, hs].astype(BF16)) * (CROSS_HEAD_DIM ** -0.5)
        outs.append(_dot(_softmax_rows(sc).astype(BF16), v_ref[0, :, hs].astype(BF16)))
    att = jnp.concatenate(outs, axis=1).astype(BF16)
    h2 = h_ref[...] + _dot(att, wo_ref[...])
    h2_ref[...] = h2
    hn2_ref[...] = _rms(h2, g_ref[...]).astype(hn2_ref.dtype)


def cross_block(hn, h, mem_k, mem_v, w_cq, w_co, g, seq):
    ntok, d = hn.shape
    n_mem = mem_k.shape[1]
    tl = min(seq, 512)
    lt = seq // tl
    row_tile = pl.BlockSpec((tl, d), lambda i: (i, 0))
    resident = lambda shape: pl.BlockSpec(shape, lambda i: (0,) * len(shape), pipeline_mode=pl.Buffered(1))
    kv_spec = pl.BlockSpec((1, n_mem, d), lambda i: (i // lt, 0, 0))
    return pl.pallas_call(
        _cross_block_kernel,
        grid=(ntok // tl,),
        in_specs=[row_tile, row_tile, kv_spec, kv_spec, resident((d, d)), resident((d, d)),
                  pl.BlockSpec((1, d), lambda i: (0, 0))],
        out_specs=[row_tile, row_tile],
        out_shape=[jax.ShapeDtypeStruct((ntok, d), F32), jax.ShapeDtypeStruct((ntok, d), BF16)],
        compiler_params=_cp("parallel"),
        name="cross_block",
    )(hn, h, mem_k, mem_v, w_cq, w_co, g.reshape(1, d))


def _xattn_step_kernel(q_ref, k_ref, v_ref, o_ref, *, nseq, tl):
    rows = nseq * tl
    n_mem = k_ref.shape[1]
    q = q_ref[0]
    qs = jnp.concatenate([q[:, h * CROSS_HEAD_DIM : (h + 1) * CROSS_HEAD_DIM] for h in range(CROSS_HEADS)],
                         axis=0).astype(BF16)
    shape = (CROSS_HEADS * rows, n_mem * CROSS_HEADS)
    col_head = lax.broadcasted_iota(jnp.int32, shape, 1) % CROSS_HEADS
    row_head = lax.broadcasted_iota(jnp.int32, shape, 0) // rows
    same_head = col_head == row_head
    rsel = lax.broadcasted_iota(jnp.int32, (CROSS_HEADS * rows, CROSS_HEAD_DIM), 0) % rows
    out = jnp.zeros((CROSS_HEADS * rows, CROSS_HEAD_DIM), F32)
    for s in range(nseq):
        kall = k_ref[s].reshape(n_mem * CROSS_HEADS, CROSS_HEAD_DIM).astype(BF16)
        vall = v_ref[s].reshape(n_mem * CROSS_HEADS, CROSS_HEAD_DIM).astype(BF16)
        sc = jnp.where(same_head, _dot_nt(qs, kall) * (CROSS_HEAD_DIM ** -0.5), NEG_BIG)
        oh = _dot(_softmax_rows(sc).astype(BF16), vall)
        out = jnp.where((rsel >= tl * s) & (rsel < tl * (s + 1)), oh, out)
    o_ref[0] = jnp.concatenate([out[h * rows : (h + 1) * rows] for h in range(CROSS_HEADS)],
                               axis=1).astype(o_ref.dtype)


def cross_attend_rider(q2, mem_k, mem_v, nb, seq):
    n_mem = mem_k.shape[1]
    nseq, tl = 8 // seq, seq
    rows = nseq * tl
    nblk = nb * seq // rows
    c = lambda i: jnp.minimum(i, nblk - 1)
    kv_spec = pl.BlockSpec((nseq, n_mem, CROSS_HEADS, CROSS_HEAD_DIM), lambda i: (c(i), 0, 0, 0))
    return RiderKernel(
        body=functools.partial(_xattn_step_kernel, nseq=nseq, tl=tl),
        nsteps=nblk,
        in_specs=[pl.BlockSpec((1, rows, D_MODEL), lambda i: (c(i), 0, 0)), kv_spec, kv_spec],
        out_specs=[pl.BlockSpec((1, rows, D_MODEL), lambda i: (c(i), 0, 0))],
        out_shape=[jax.ShapeDtypeStruct((nblk, rows, D_MODEL), BF16)],
        args=(q2.reshape(nblk, rows, D_MODEL), mem_k, mem_v),
    )


PACK_TILE = 512


SUBLANE = 8


def _cast_rows_kernel(w_ref, o_ref, *, nrows):
    j = pl.program_id(0)
    x = w_ref[...]
    row = lax.broadcasted_iota(jnp.int32, x.shape, 0) + j * PACK_TILE
    o_ref[...] = jnp.where(row < nrows, x, 0.0).astype(o_ref.dtype)


def cast_rows(wt, lo, hi, n_out, name):
    kdim = wt.shape[1]
    assert lo % SUBLANE == 0 and lo + n_out <= wt.shape[0]
    return pl.pallas_call(
        functools.partial(_cast_rows_kernel, nrows=hi - lo),
        grid=(n_out // PACK_TILE,),
        in_specs=[pl.BlockSpec((pl.Element(PACK_TILE), pl.Element(kdim)),
                               lambda j: (pl.multiple_of(lo + j * PACK_TILE, SUBLANE), 0))],
        out_specs=pl.BlockSpec((PACK_TILE, kdim), lambda j: (j, 0)),
        out_shape=jax.ShapeDtypeStruct((n_out, kdim), BF16),
        compiler_params=_cp("parallel"),
        name=name,
    )(wt)


def _cast_tile_kernel(w_ref, o_ref):
    o_ref[...] = w_ref[...].astype(o_ref.dtype)


def cast_hosting_matmul(w, hosted, name):
    kdim, ncol = w.shape
    nsteps = ncol // PACK_TILE
    c = lambda i: jnp.minimum(i, nsteps - 1)
    tile = pl.BlockSpec((kdim, PACK_TILE), lambda i: (0, c(i)))
    (wb,), hosted_out = call_with_hosted_matmul(
        _cast_tile_kernel, nsteps, [tile], [tile], [jax.ShapeDtypeStruct((kdim, ncol), BF16)], (w,), name, hosted)
    return wb, hosted_out


def _split_w_in(w_in):
    wt = w_in.T
    ssd_end = D_MODEL + CONV_DIM + SSD_HEADS
    gla_end = ssd_end + 2 * GLA_KEY_DIM + 2 * D_MODEL + GATE_RANK
    return (cast_rows(wt, 0, ssd_end, N_SSD_PROJ, "pack_ssd_in"),
            cast_rows(wt, ssd_end, gla_end, N_GLA_PROJ, "pack_gla_in"),
            cast_rows(wt, gla_end, wt.shape[0], 2 * D_MODEL, "pack_gate_in"))


def _params(ssd_dt_bias, ssd_A_log, ssd_D, ssd_norm, w_gla_gate, b_gla_gate, gla_norm):
    padv = lambda a: jnp.pad(a.astype(F32), (0, LANE - a.shape[0])).reshape(1, LANE)
    head_of_chan = jnp.arange(D_MODEL, dtype=jnp.int32) // SSD_HEAD_DIM
    e_head = (jnp.arange(LANE, dtype=jnp.int32)[:, None] == head_of_chan[None, :]).astype(BF16)
    group_of_bc = jnp.arange(BC_W, dtype=jnp.int32) // D_STATE
    lane_h = jnp.arange(LANE, dtype=jnp.int32)
    g_head = ((lane_h[None, :] // (SSD_HEADS // N_GROUPS) == group_of_bc[:, None])
              & (lane_h[None, :] < SSD_HEADS)).astype(BF16)
    khead = jnp.arange(GLA_KEY_DIM, dtype=jnp.int32) // GLA_HEAD_K
    vhead = jnp.arange(D_MODEL, dtype=jnp.int32) // GLA_HEAD_V
    g_val = (khead[:, None] == vhead[None, :]).astype(BF16)
    return dict(
        dt_bias=padv(ssd_dt_bias), a_log=padv(ssd_A_log),
        d_exp=jnp.repeat(ssd_D.astype(F32), SSD_HEAD_DIM).reshape(1, D_MODEL),
        ssd_norm=ssd_norm.astype(F32).reshape(1, D_MODEL),
        e_head=e_head, g_head=g_head, g_val=g_val,
        w_gate=jnp.pad(w_gla_gate, ((0, LANE - GATE_RANK), (0, 0))).astype(BF16),
        b_gate=b_gla_gate.astype(F32).reshape(1, GLA_KEY_DIM),
        gla_norm=gla_norm.astype(F32).reshape(1, GLA_HEAD_V),
    )


def _mixers(x3, xn, ssd_conv, ssd_state, gla_state, w, p, long_seq, proj_ssd=None, proj_gla=None, gates=None,
            other_xn=None):
    nb, seq, d = x3.shape
    ntok = nb * seq
    x2 = x3.reshape(ntok, d)
    w_ssd_in, w_gla_in, w_gate_in = w["w_in"]
    in_proj = functools.partial(matmul, xn, tm=2048, w_rows_are_outputs=True)
    proj_ssd = in_proj(w_ssd_in, name="in_proj_ssd") if proj_ssd is None else proj_ssd
    proj_gla = in_proj(w_gla_in, name="in_proj_gla") if proj_gla is None else proj_gla
    gates = in_proj(w_gate_in, name="in_proj_gates") if gates is None else gates
    hosted = (lambda wt: HostedMatmul(other_xn, wt)) if other_xn is not None else (lambda wt: None)
    other_ssd = other_gla = None
    if long_seq:
        u, ssd_new, ssd_conv_new = ssd_scan(proj_ssd.reshape(nb, seq, N_SSD_PROJ), ssd_conv,
                                            w["ssd_conv_w"], w["ssd_conv_b"], p)
        o, gla_new = gla_scan(proj_gla.reshape(nb, seq, N_GLA_PROJ), p)
    else:
        xc2, ssd_conv_new = short_conv(proj_ssd, seq, [COL_XBC], CONV_DIM, [ssd_conv], [w["ssd_conv_w"]],
                                       [w["ssd_conv_b"]], SSD_CONV, False, F32, "ssd_conv")
        u, ssd_new, other_ssd = ssd_step(xc2, proj_ssd, ssd_state, p, seq, hosted(w_ssd_in))
        o, gla_new, other_gla = gla_step(proj_gla, gla_state, p, seq, hosted(w_gla_in))
    merged = merge_branches(u, o, w["w_ssd_out"], w["w_gla_out"], gates)
    h, hn = mm_res_norm(merged, w["w_mix_out"], x2, w["norm_cross"], True, BF16, "mix_out")
    return h, hn, ssd_conv_new, ssd_new, gla_new, (other_ssd, other_gla)


def _conv_ffn(h2, hn2, ffn_conv, w, nb, seq, long_seq, rider=None):
    ntok, d = h2.shape
    ffn = w["w_down"].shape[0]
    cw, cbias = w["ffn_conv_w"], w["ffn_conv_b"]
    rider_out = None
    if long_seq:
        act, ffn_conv_new, rider_out = ffn_up_conv_act(hn2.reshape(nb, seq, d), w["w_up"], cw, cbias, ffn_conv, rider)
    else:
        up = matmul(hn2, w["w_up"], name="ffn_up")
        act, fa, fg = short_conv(up, seq, [0, ffn], ffn, [ffn_conv[:, :, :ffn], ffn_conv[:, :, ffn:]],
                                 [cw[:, :ffn], cw[:, ffn:]], [cbias[:ffn], cbias[ffn:]],
                                 FFN_CONV, True, BF16, "ffn_conv")
        ffn_conv_new = jnp.concatenate([fa, fg], axis=-1)
    y = mm_res_norm(act.reshape(ntok, ffn), w["w_down"], h2, w["norm_final"], False, F32, "ffn_down")
    return y.reshape(nb, seq, d), ffn_conv_new, rider_out


def kernel(x_prompt, x_sample, cache_mem_k, cache_mem_v, state_ssd_conv, state_ssd, state_gla, state_ffn_conv, mem_prompt, norm_mix, w_in, ssd_conv_w, ssd_conv_b, ssd_dt_bias, ssd_A_log, ssd_D, ssd_norm, w_ssd_out, w_gla_gate, b_gla_gate, gla_norm, w_gla_out, w_mix_out, norm_cross, norm_mem, w_cq, w_ck, w_cv, w_co, norm_ffn, w_up, ffn_conv_w, ffn_conv_b, w_down, norm_final):
    nb, seq, d = x_prompt.shape
    n_mem = mem_prompt.shape[1]
    ffn2 = w_up.shape[1]
    w = dict(
        norm_mix=norm_mix, norm_cross=norm_cross, norm_ffn=norm_ffn, norm_final=norm_final,
        w_in=_split_w_in(w_in), ssd_conv_w=ssd_conv_w, ssd_conv_b=ssd_conv_b,
        w_ssd_out=w_ssd_out.astype(BF16), w_gla_out=w_gla_out.astype(BF16), w_mix_out=w_mix_out.astype(BF16),
        w_cq=w_cq.astype(BF16), w_co=w_co.astype(BF16), w_down=w_down.astype(BF16),
        ffn_conv_w=ffn_conv_w, ffn_conv_b=ffn_conv_b,
    )
    p = _params(ssd_dt_bias, ssd_A_log, ssd_D, ssd_norm, w_gla_gate, b_gla_gate, gla_norm)

    mn = rmsnorm_cast(mem_prompt.reshape(nb * n_mem, d), norm_mem)
    p_mem_k, p_mem_v, mem_k_heads, mem_v_heads = mem_kv(mn, w_ck.astype(BF16), w_cv.astype(BF16))
    p_mem_k = p_mem_k.reshape(nb, n_mem, d)
    p_mem_v = p_mem_v.reshape(nb, n_mem, d)
    zeros_ssd_conv = jnp.zeros((nb, SSD_CONV - 1, CONV_DIM), F32)
    zeros_ffn_conv = jnp.zeros((nb, FFN_CONV - 1, ffn2), F32)
    ns, sseq, _ = x_sample.shape
    xn_prompt = rmsnorm_cast(x_prompt.reshape(nb * seq, d), norm_mix)
    xn_sample = rmsnorm_cast(x_sample.reshape(ns * sseq, d), norm_mix)
    w["w_up"], gates_p = cast_hosting_matmul(w_up, HostedMatmul(xn_prompt, w["w_in"][2]), "cast_w_up")

    h_s, hn_s, s_ssd_conv, s_ssd, s_gla, (proj_ssd_p, proj_gla_p) = _mixers(
        x_sample, xn_sample, state_ssd_conv, state_ssd, state_gla, w, p, False, other_xn=xn_prompt)

    h_p, hn_p, p_ssd_conv, p_ssd, p_gla, _ = _mixers(
        x_prompt, xn_prompt, zeros_ssd_conv, None, None, w, p, True, proj_ssd=proj_ssd_p, proj_gla=proj_gla_p,
        gates=gates_p)
    h2_p, hn2_p = cross_block(hn_p, h_p, p_mem_k, p_mem_v, w["w_cq"], w["w_co"], norm_ffn, seq)

    attend_s = cross_attend_rider(matmul(hn_s, w["w_cq"], name="cross_q"), cache_mem_k, cache_mem_v, ns, sseq)
    y_prompt, p_ffn_conv, (att_s,) = _conv_ffn(h2_p, hn2_p, zeros_ffn_conv, w, nb, seq, True, rider=attend_s)

    h2_s, hn2_s = mm_res_norm(att_s.reshape(ns * sseq, d), w["w_co"], h_s, norm_ffn, True, BF16, "cross_out")
    y_sample, s_ffn_conv, _ = _conv_ffn(h2_s, hn2_s, state_ffn_conv, w, ns, sseq, False)

    head_shape = (n_mem, CROSS_HEADS, CROSS_HEAD_DIM)
    return (y_prompt, y_sample, p_ssd_conv, p_ssd, p_gla, p_ffn_conv,
            mem_k_heads.reshape((nb,) + head_shape), mem_v_heads.reshape((nb,) + head_shape),
            s_ssd_conv, s_ssd, s_gla, s_ffn_conv)
```

```python
import dataclasses
import functools
from typing import NamedTuple

import jax
import jax.numpy as jnp
from jax import lax
from jax.experimental import pallas as pl
from jax.experimental.pallas import tpu as pltpu

F32 = jnp.float32
BF16 = jnp.bfloat16
EPS = 1e-6
NEG_BIG = -1e30

D_MODEL = 2048
SSD_HEAD_DIM = 64
SSD_HEADS = 32
D_STATE = 128
N_GROUPS = 4
GROUP_W = D_MODEL // N_GROUPS
BC_W = N_GROUPS * D_STATE
CONV_DIM = D_MODEL + 2 * BC_W
SSD_CONV = 4
SSD_CHUNK = 128
GLA_HEADS = 4
GLA_KEY_DIM = 1024
GLA_HEAD_K = 256
GLA_HEAD_V = 512
GATE_RANK = 16
GATE_TAU = 16.0
GLA_CHUNK = 64
CROSS_HEADS = 4
CROSS_HEAD_DIM = 512
FFN_CONV = 3
LANE = 128

COL_Z, COL_XBC, COL_DT, N_SSD_PROJ = 0, 2048, 5120, 5632
COL_Q, COL_K, COL_V, COL_R, COL_GLR, N_GLA_PROJ = 0, 1024, 2048, 4096, 6144, 6656
COL_GA, COL_GB = 0, 2048

VMEM_LIMIT = 56 * 1024 * 1024


def _cp(*sem):
    return pltpu.CompilerParams(dimension_semantics=sem, vmem_limit_bytes=VMEM_LIMIT)


def _dot(a, b, prec=None):
    return jnp.dot(a, b, preferred_element_type=F32, precision=prec)


def _dot_nt(a, b):
    return lax.dot_general(a, b, (((1,), (1,)), ((), ())), preferred_element_type=F32)


def _split3(x):
    x1 = x.astype(BF16)
    r1 = x - x1.astype(F32)
    x2 = r1.astype(BF16)
    x3 = (r1 - x2.astype(F32)).astype(BF16)
    return x1, x2, x3


def _dot_sel(x, sel, sel_first=False):
    parts = _split3(x)
    if sel_first:
        return _dot(sel, parts[0]) + _dot(sel, parts[1]) + _dot(sel, parts[2])
    return _dot(parts[0], sel) + _dot(parts[1], sel) + _dot(parts[2], sel)


def _sigmoid(x):
    return 1.0 / (1.0 + jnp.exp(-x))


def _silu(x):
    return x * _sigmoid(x)


def _softplus(x):
    return jnp.maximum(x, 0.0) + jnp.log(1.0 + jnp.exp(-jnp.abs(x)))


def _rms(x, g):
    ms = jnp.mean(x * x, axis=-1, keepdims=True)
    return x * lax.rsqrt(ms + EPS) * g


def _rmsnorm_kernel(x_ref, g_ref, o_ref):
    o_ref[...] = _rms(x_ref[...], g_ref[...]).astype(o_ref.dtype)


def rmsnorm_cast(x2, g):
    m, d = x2.shape
    tm = min(m, 512)
    return pl.pallas_call(
        _rmsnorm_kernel,
        grid=(m // tm,),
        in_specs=[pl.BlockSpec((tm, d), lambda i: (i, 0)), pl.BlockSpec((1, d), lambda i: (0, 0))],
        out_specs=pl.BlockSpec((tm, d), lambda i: (i, 0)),
        out_shape=jax.ShapeDtypeStruct((m, d), BF16),
        compiler_params=_cp("parallel"),
        name="rmsnorm_cast",
    )(x2, g.reshape(1, d))


def _mm_kernel(a_ref, w_ref, o_ref, *, w_rows_are_outputs):
    dot = _dot_nt if w_rows_are_outputs else _dot
    o_ref[...] = dot(a_ref[...], w_ref[...]).astype(o_ref.dtype)


def matmul(a, w, out_dtype=F32, tm=1024, tn=512, w_rows_are_outputs=False, name="matmul"):
    m, k = a.shape
    n = w.shape[0] if w_rows_are_outputs else w.shape[1]
    tm = min(m, tm)
    if w_rows_are_outputs:
        w_spec = pl.BlockSpec((tn, k), lambda i, j: (j, 0))
    else:
        w_spec = pl.BlockSpec((k, tn), lambda i, j: (0, j))
    return pl.pallas_call(
        functools.partial(_mm_kernel, w_rows_are_outputs=w_rows_are_outputs),
        grid=(m // tm, n // tn),
        in_specs=[pl.BlockSpec((tm, k), lambda i, j: (i, 0)), w_spec],
        out_specs=pl.BlockSpec((tm, tn), lambda i, j: (i, j)),
        out_shape=jax.ShapeDtypeStruct((m, n), out_dtype),
        compiler_params=_cp("parallel", "arbitrary"),
        name=name,
    )(a, w)


def _mem_kv_kernel(a_ref, wk_ref, wv_ref, k_ref, v_ref, k4_ref, v4_ref):
    a = a_ref[...]
    for w_ref, flat_ref, heads_ref in ((wk_ref, k_ref, k4_ref), (wv_ref, v_ref, v4_ref)):
        y = _dot(a, w_ref[...])
        flat_ref[...] = y
        heads_ref[...] = y.reshape(heads_ref.shape)


def mem_kv(mn, w_ck, w_cv):
    m, d = mn.shape
    tm = min(m, 256)
    flat = pl.BlockSpec((tm, d), lambda i: (i, 0))
    heads = pl.BlockSpec((tm, CROSS_HEADS, CROSS_HEAD_DIM), lambda i: (i, 0, 0))
    weight = pl.BlockSpec((d, d), lambda i: (0, 0), pipeline_mode=pl.Buffered(1))
    return pl.pallas_call(
        _mem_kv_kernel,
        grid=(m // tm,),
        in_specs=[flat, weight, weight],
        out_specs=[flat, flat, heads, heads],
        out_shape=[jax.ShapeDtypeStruct((m, d), F32)] * 2
        + [jax.ShapeDtypeStruct((m, CROSS_HEADS, CROSS_HEAD_DIM), F32)] * 2,
        compiler_params=_cp("parallel"),
        name="mem_kv",
    )(mn, w_ck, w_cv)


class HostedMatmul(NamedTuple):
    a: jax.Array
    w: jax.Array
    tm: int = 2048
    tn: int = 512


def call_with_hosted_matmul(body, nsteps, in_specs, out_specs, out_shape, args, name, hosted=None):
    if hosted is None:
        outs = pl.pallas_call(body, grid=(nsteps,), in_specs=in_specs, out_specs=out_specs, out_shape=out_shape,
                              compiler_params=_cp("arbitrary"), name=name)(*args)
        return list(outs), None
    m, k = hosted.a.shape
    n = hosted.w.shape[0]
    tm, tn = min(m, hosted.tm), hosted.tn
    nj = n // tn
    hsteps = (m // tm) * nj
    total = max(nsteps, hsteps)
    hc = lambda i: jnp.minimum(i, hsteps - 1)
    h_in = [pl.BlockSpec((tm, k), lambda i: (hc(i) // nj, 0)), pl.BlockSpec((tn, k), lambda i: (hc(i) % nj, 0))]
    h_out = pl.BlockSpec((tm, tn), lambda i: (hc(i) // nj, hc(i) % nj))
    n_in, n_out = len(in_specs), len(out_specs)

    def fused(*refs):
        g_in = refs[:n_in]
        ha_ref, hw_ref = refs[n_in : n_in + 2]
        g_out = refs[n_in + 2 : n_in + 2 + n_out]
        ho_ref = refs[n_in + 2 + n_out]
        step = pl.program_id(0)

        def guest():
            body(*g_in, *g_out)

        def host():
            ho_ref[...] = _dot_nt(ha_ref[...], hw_ref[...])

        guest() if nsteps == total else pl.when(step < nsteps)(guest)
        host() if hsteps == total else pl.when(step < hsteps)(host)

    outs = pl.pallas_call(
        fused,
        grid=(total,),
        in_specs=list(in_specs) + h_in,
        out_specs=list(out_specs) + [h_out],
        out_shape=list(out_shape) + [jax.ShapeDtypeStruct((m, n), F32)],
        compiler_params=_cp("arbitrary"),
        name=name,
    )(*args, hosted.a, hosted.w)
    return list(outs[:-1]), outs[-1]


def _mm_res_norm_kernel(a_ref, w_ref, res_ref, g_ref, *out_refs):
    h = res_ref[...] + _dot(a_ref[...], w_ref[...])
    if len(out_refs) == 2:
        out_refs[0][...] = h
    out_refs[-1][...] = _rms(h, g_ref[...]).astype(out_refs[-1].dtype)


def mm_res_norm(a, w, res, g, emit_h, norm_dtype, name):
    m, kdim = a.shape
    n = w.shape[1]
    tm = min(m, 512)
    row_tile = lambda width: pl.BlockSpec((tm, width), lambda i: (i, 0))
    out_shape = [jax.ShapeDtypeStruct((m, n), norm_dtype)]
    out_specs = [row_tile(n)]
    if emit_h:
        out_shape = [jax.ShapeDtypeStruct((m, n), F32)] + out_shape
        out_specs = [row_tile(n)] + out_specs
    outs = pl.pallas_call(
        _mm_res_norm_kernel,
        grid=(m // tm,),
        in_specs=[row_tile(kdim), pl.BlockSpec((kdim, n), lambda i: (0, 0), pipeline_mode=pl.Buffered(1)),
                  row_tile(n), pl.BlockSpec((1, n), lambda i: (0, 0))],
        out_specs=out_specs,
        out_shape=out_shape,
        compiler_params=_cp("parallel"),
        name=name,
    )(a, w, res, g.reshape(1, n))
    return outs if emit_h else outs[0]


def _merge_kernel(u_ref, o_ref, wa_ref, wb_ref, ga_ref, gb_ref, out_ref):
    a = _dot(u_ref[...], wa_ref[...])
    b = _dot(o_ref[...], wb_ref[...])
    out_ref[...] = (_sigmoid(ga_ref[...]) * a + _sigmoid(gb_ref[...]) * b).astype(out_ref.dtype)


def merge_branches(u, o, wa, wb, proj):
    m, d = u.shape
    tm, tn = min(m, 1024), 512
    return pl.pallas_call(
        _merge_kernel,
        grid=(m // tm, d // tn),
        in_specs=[
            pl.BlockSpec((tm, d), lambda i, j: (i, 0)),
            pl.BlockSpec((tm, d), lambda i, j: (i, 0)),
            pl.BlockSpec((d, tn), lambda i, j: (0, j)),
            pl.BlockSpec((d, tn), lambda i, j: (0, j)),
            pl.BlockSpec((tm, tn), lambda i, j: (i, COL_GA // tn + j)),
            pl.BlockSpec((tm, tn), lambda i, j: (i, COL_GB // tn + j)),
        ],
        out_specs=pl.BlockSpec((tm, tn), lambda i, j: (i, j)),
        out_shape=jax.ShapeDtypeStruct((m, d), BF16),
        compiler_params=_cp("parallel", "arbitrary"),
        name="merge_branches",
    )(u, o, wa, wb, proj, proj)


CONV_PAD = 8


def _short_conv_kernel(*refs, taps, seq, nstreams, swiglu):
    ins = refs[: 4 * nstreams]
    out_ref = refs[4 * nstreams]
    ns_refs = refs[4 * nstreams + 1 :]
    vals = []
    for s in range(nstreams):
        u_ref, st_ref, w_ref, b_ref = ins[4 * s : 4 * s + 4]
        nseq, _, tc = st_ref.shape
        full = jnp.concatenate([st_ref[...], u_ref[...].reshape(nseq, seq, tc)], axis=1)
        acc = b_ref[...]
        for k in range(taps):
            acc = acc + full[:, k : k + seq] * w_ref[k : k + 1, :]
        vals.append(acc)
        ns_refs[s][...] = full[:, seq : seq + taps - 1]
    out = _silu(vals[1]) * vals[0] if swiglu else _silu(vals[0])
    out_ref[...] = out.reshape(out_ref.shape).astype(out_ref.dtype)


def short_conv(u2, seq, col_offs, width, states, ws, bs, taps, swiglu, out_dtype, name):
    ntok = u2.shape[0]
    nseq = ntok // seq
    tc = 512
    nstreams = len(col_offs)
    in_specs, args = [], []
    for s in range(nstreams):
        cb = col_offs[s] // tc
        in_specs += [
            pl.BlockSpec((ntok, tc), lambda c, cb=cb: (0, cb + c)),
            pl.BlockSpec((nseq, taps - 1, tc), lambda c: (0, 0, c)),
            pl.BlockSpec((taps, tc), lambda c: (0, c)),
            pl.BlockSpec((1, tc), lambda c: (0, c)),
        ]
        args += [u2, states[s], ws[s], bs[s].reshape(1, width)]
    out_specs = [pl.BlockSpec((ntok, tc), lambda c: (0, c))]
    out_shape = [jax.ShapeDtypeStruct((ntok, width), out_dtype)]
    for s in range(nstreams):
        out_specs.append(pl.BlockSpec((nseq, taps - 1, tc), lambda c: (0, 0, c)))
        out_shape.append(jax.ShapeDtypeStruct((nseq, taps - 1, width), F32))
    return pl.pallas_call(
        functools.partial(_short_conv_kernel, taps=taps, seq=seq, nstreams=nstreams, swiglu=swiglu),
        grid=(width // tc,),
        in_specs=in_specs,
        out_specs=out_specs,
        out_shape=out_shape,
        compiler_params=_cp("parallel"),
        name=name,
    )(*args)


def _shift_rows(u, d, prev):
    x = pltpu.roll(u, d, 0)
    r = lax.broadcasted_iota(jnp.int32, prev.shape, 0)
    head = jnp.where(r < d, pltpu.roll(prev, d, 0), x[:CONV_PAD])
    return jnp.concatenate([head, x[CONV_PAD:]], axis=0)


def _causal_taps(u, prev, cw, bias):
    taps = cw.shape[0]
    acc = bias
    for k in range(taps):
        d = taps - 1 - k
        acc = acc + (_shift_rows(u, d, prev) if d else u) * cw[k : k + 1, :]
    return acc


class RiderKernel(NamedTuple):
    body: object
    nsteps: int
    in_specs: list
    out_specs: list
    out_shape: list
    args: tuple


def _ffn_up_kernel(*refs, tm, rider_body, n_rider_in, n_rider_out, rider_steps):
    n_in = 9
    hn_ref, wa_ref, wg_ref, cwa_ref, cwg_ref, cba_ref, cbg_ref, sta_ref, stg_ref = refs[:n_in]
    rider_in = refs[n_in : n_in + n_rider_in]
    act_ref, nsa_ref, nsg_ref = refs[n_in + n_rider_in : n_in + n_rider_in + 3]
    rider_out = refs[n_in + n_rider_in + 3 : n_in + n_rider_in + 3 + n_rider_out]
    preva, prevg = refs[-2:]
    m = pl.program_id(2)
    nm = pl.num_programs(2)
    lo = CONV_PAD - (FFN_CONV - 1)

    @pl.when(m == 0)
    def _():
        for prev, st_ref in ((preva, sta_ref), (prevg, stg_ref)):
            prev[0:lo, :] = jnp.zeros((lo, prev.shape[1]), F32)
            prev[lo:CONV_PAD, :] = st_ref[0]

    hn = hn_ref[0]
    vals = []
    for w_ref, cw_ref, cb_ref, prev in ((wa_ref, cwa_ref, cba_ref, preva), (wg_ref, cwg_ref, cbg_ref, prevg)):
        u = _dot(hn, w_ref[...])
        vals.append(_causal_taps(u, prev[...], cw_ref[...], cb_ref[...]))
        prev[...] = u[tm - CONV_PAD : tm]
    act_ref[0] = (_silu(vals[1]) * vals[0]).astype(act_ref.dtype)

    @pl.when(m == nm - 1)
    def _():
        nsa_ref[0] = preva[lo:CONV_PAD, :]
        nsg_ref[0] = prevg[lo:CONV_PAD, :]

    if rider_body is not None:
        step = (pl.program_id(0) * pl.num_programs(1) + pl.program_id(1)) * nm + m

        @pl.when(step < rider_steps)
        def _():
            rider_body(*rider_in, *rider_out)


def ffn_up_conv_act(hn3, w_up, conv_w, conv_b, conv_state, rider=None):
    nb, seq, d = hn3.shape
    ffn = w_up.shape[1] // 2
    tm, tn = min(seq, 1024), 512
    nn, nm = ffn // tn, seq // tm
    half = lambda off: (lambda b, n, m: (0, off + n))
    st_spec = lambda off: pl.BlockSpec((1, FFN_CONV - 1, tn), lambda b, n, m: (b, 0, off + n))
    cb2 = conv_b.reshape(1, 2 * ffn)
    in_specs = [
        pl.BlockSpec((1, tm, d), lambda b, n, m: (b, m, 0)),
        pl.BlockSpec((d, tn), half(0)), pl.BlockSpec((d, tn), half(nn)),
        pl.BlockSpec((FFN_CONV, tn), half(0)), pl.BlockSpec((FFN_CONV, tn), half(nn)),
        pl.BlockSpec((1, tn), half(0)), pl.BlockSpec((1, tn), half(nn)),
        st_spec(0), st_spec(nn),
    ]
    out_specs = [
        pl.BlockSpec((1, tm, tn), lambda b, n, m: (b, m, n)),
        pl.BlockSpec((1, FFN_CONV - 1, tn), lambda b, n, m: (b, 0, n)),
        pl.BlockSpec((1, FFN_CONV - 1, tn), lambda b, n, m: (b, 0, n)),
    ]
    out_shape = [
        jax.ShapeDtypeStruct((nb, seq, ffn), BF16),
        jax.ShapeDtypeStruct((nb, FFN_CONV - 1, ffn), F32),
        jax.ShapeDtypeStruct((nb, FFN_CONV - 1, ffn), F32),
    ]
    args = [hn3, w_up, w_up, conv_w, conv_w, cb2, cb2, conv_state, conv_state]
    rider_kw = dict(rider_body=None, n_rider_in=0, n_rider_out=0, rider_steps=0)
    if rider is not None and rider.nsteps > nb * nn * nm:
        rider_outs, _ = call_with_hosted_matmul(rider.body, rider.nsteps, rider.in_specs, rider.out_specs,
                                                rider.out_shape, rider.args, "rider")
        act, conv_new, _ = ffn_up_conv_act(hn3, w_up, conv_w, conv_b, conv_state)
        return act, conv_new, rider_outs
    if rider is not None:
        on_grid = lambda spec: dataclasses.replace(
            spec, index_map=lambda b, n, m, im=spec.index_map: im((b * nn + n) * nm + m))
        in_specs += [on_grid(s) for s in rider.in_specs]
        out_specs += [on_grid(s) for s in rider.out_specs]
        out_shape += list(rider.out_shape)
        args += list(rider.args)
        rider_kw = dict(rider_body=rider.body, n_rider_in=len(rider.in_specs), n_rider_out=len(rider.out_specs),
                        rider_steps=rider.nsteps)
    outs = pl.pallas_call(
        functools.partial(_ffn_up_kernel, tm=tm, **rider_kw),
        grid=(nb, nn, nm),
        in_specs=in_specs,
        out_specs=out_specs,
        out_shape=out_shape,
        scratch_shapes=[pltpu.VMEM((CONV_PAD, tn), F32), pltpu.VMEM((CONV_PAD, tn), F32)],
        compiler_params=_cp("parallel" if rider is None else "arbitrary", "parallel" if rider is None else "arbitrary",
                            "arbitrary"),
        name="ffn_up_conv_act",
    )(*args)
    act, nsa, nsg = outs[:3]
    return act, jnp.concatenate([nsa, nsg], axis=-1), list(outs[3:])


def _ssd_gate_norm(y, z, nrm):
    ug = y * _silu(z)
    outs = []
    for g in range(N_GROUPS):
        ugg = ug[:, g * GROUP_W : (g + 1) * GROUP_W]
        ms = jnp.mean(ugg * ugg, axis=-1, keepdims=True)
        outs.append(ugg * lax.rsqrt(ms + EPS))
    return jnp.concatenate(outs, axis=1) * nrm


def _gla_out_norm(o, r, nrm):
    outs = []
    for h in range(GLA_HEADS):
        oh = o[:, h * GLA_HEAD_V : (h + 1) * GLA_HEAD_V]
        rh = r[:, h * GLA_HEAD_V : (h + 1) * GLA_HEAD_V]
        outs.append(_rms(oh, nrm) * _silu(rh))
    return jnp.concatenate(outs, axis=1)


SCAN_ROWS_PER_STEP = 256


def _ssd_chunk(xs_ref, b_ref, c_ref, dt_ref, z_ref, cw_ref, cbias_ref, dtb_ref, alog_ref, dexp_ref, nrm_ref, e_ref,
               u_ref, st_ref, cprev):
    q = SSD_CHUNK
    conv = []
    for raw_ref, c0 in ((xs_ref, 0), (b_ref, D_MODEL), (c_ref, D_MODEL + BC_W)):
        raw = raw_ref[...]
        cols = slice(c0, c0 + raw.shape[1])
        conv.append(_silu(_causal_taps(raw, cprev[:, cols], cw_ref[:, cols], cbias_ref[:, cols])))
        cprev[:, cols] = raw[q - CONV_PAD : q]
    xs, bm, cm = conv
    dt = _softplus(dt_ref[...] + dtb_ref[...])
    a = dt * (-jnp.exp(alog_ref[...]))
    row = lax.broadcasted_iota(jnp.int32, (q, q), 0)
    col = lax.broadcasted_iota(jnp.int32, (q, q), 1)
    tril = row >= col
    acum = _dot_sel(a, tril.astype(BF16), sel_first=True)
    acum_t = acum.T
    dt_t = dt.T
    last = acum[q - 1 : q, :]
    e_mat = e_ref[...]
    eexp = _dot_sel(jnp.exp(acum), e_mat)
    wexp = _dot_sel(jnp.exp(last - acum) * dt, e_mat)
    s_bf = st_ref[...].astype(BF16)
    cb16 = cm.astype(BF16)
    bb16 = bm.astype(BF16)
    x16 = xs.astype(BF16)
    xw16 = (xs * wexp).astype(BF16)
    lane_lo = lax.broadcasted_iota(jnp.int32, (q, LANE), 1) < SSD_HEAD_DIM
    ys = []
    for g in range(N_GROUPS):
        cg = cb16[:, g * D_STATE : (g + 1) * D_STATE]
        bg = bb16[:, g * D_STATE : (g + 1) * D_STATE]
        cb = _dot_nt(cg, bg)
        yoff = _dot(cg, s_bf[:, g * GROUP_W : (g + 1) * GROUP_W])
        pieces = []
        for pr in range(GROUP_W // LANE):
            h0 = g * (SSD_HEADS // N_GROUPS) + 2 * pr
            xp = x16[:, h0 * SSD_HEAD_DIM : h0 * SSD_HEAD_DIM + LANE]
            yh = []
            for h in (h0, h0 + 1):
                diff = acum[:, h : h + 1] - acum_t[h : h + 1, :]
                dec = jnp.exp(jnp.where(tril, diff, NEG_BIG))
                m = (cb * dec * dt_t[h : h + 1, :]).astype(BF16)
                yh.append(_dot(m, xp))
            pieces.append(jnp.where(lane_lo, yh[0], yh[1]))
        sl = slice(g * GROUP_W, (g + 1) * GROUP_W)
        ys.append(jnp.concatenate(pieces, axis=1) + yoff * eexp[:, sl])
        bg_t = bm[:, g * D_STATE : (g + 1) * D_STATE].T.astype(BF16)
        upd = _dot(bg_t, xw16[:, sl])
        st_ref[:, sl] = eexp[q - 1 : q, sl] * st_ref[:, sl] + upd
    y = jnp.concatenate(ys, axis=1) + dexp_ref[...] * xs
    u_ref[...] = _ssd_gate_norm(y, z_ref[...], nrm_ref[...]).astype(u_ref.dtype)


def _ssd_scan_kernel(xs_ref, b_ref, c_ref, dt_ref, z_ref, cst_ref, cw_ref, cbias_ref,
                     dtb_ref, alog_ref, dexp_ref, nrm_ref, e_ref,
                     u_ref, sout_ref, cso_ref, st_ref, cprev, *, nsteps, nsub):
    step = pl.program_id(1)
    lo = CONV_PAD - (SSD_CONV - 1)

    @pl.when(step == 0)
    def _():
        st_ref[...] = jnp.zeros_like(st_ref)
        cprev[0:lo, :] = jnp.zeros((lo, CONV_DIM), F32)
        cprev[lo:CONV_PAD, :] = cst_ref[0]

    for sub in range(nsub):
        rows = lambda ref: ref.at[0, pl.ds(sub * SSD_CHUNK, SSD_CHUNK)]
        _ssd_chunk(rows(xs_ref), rows(b_ref), rows(c_ref), rows(dt_ref), rows(z_ref), cw_ref, cbias_ref,
                   dtb_ref, alog_ref, dexp_ref, nrm_ref, e_ref, rows(u_ref), st_ref, cprev)

    @pl.when(step == nsteps - 1)
    def _():
        sout_ref[0] = st_ref[...].T
        cso_ref[0] = cprev[lo:CONV_PAD, :]


def ssd_scan(proj3, conv_state, conv_w, conv_b, p):
    nb, seq, _ = proj3.shape
    nsub = SCAN_ROWS_PER_STEP // SSD_CHUNK if seq % SCAN_ROWS_PER_STEP == 0 else 1
    q = nsub * SSD_CHUNK
    nsteps = seq // q
    vec = lambda n: pl.BlockSpec((1, n), lambda b, c: (0, 0))
    u, s_out, conv_new = pl.pallas_call(
        functools.partial(_ssd_scan_kernel, nsteps=nsteps, nsub=nsub),
        grid=(nb, nsteps),
        in_specs=[
            pl.BlockSpec((1, q, D_MODEL), lambda b, c: (b, c, COL_XBC // D_MODEL)),
            pl.BlockSpec((1, q, BC_W), lambda b, c: (b, c, (COL_XBC + D_MODEL) // BC_W)),
            pl.BlockSpec((1, q, BC_W), lambda b, c: (b, c, (COL_XBC + D_MODEL) // BC_W + 1)),
            pl.BlockSpec((1, q, LANE), lambda b, c: (b, c, COL_DT // LANE)),
            pl.BlockSpec((1, q, D_MODEL), lambda b, c: (b, c, COL_Z // D_MODEL)),
            pl.BlockSpec((1, SSD_CONV - 1, CONV_DIM), lambda b, c: (b, 0, 0)),
            pl.BlockSpec((SSD_CONV, CONV_DIM), lambda b, c: (0, 0)),
            vec(CONV_DIM),
            vec(LANE), vec(LANE), vec(D_MODEL), vec(D_MODEL),
            pl.BlockSpec((LANE, D_MODEL), lambda b, c: (0, 0)),
        ],
        out_specs=[
            pl.BlockSpec((1, q, D_MODEL), lambda b, c: (b, c, 0)),
            pl.BlockSpec((1, D_MODEL, D_STATE), lambda b, c: (b, 0, 0)),
            pl.BlockSpec((1, SSD_CONV - 1, CONV_DIM), lambda b, c: (b, 0, 0)),
        ],
        out_shape=[
            jax.ShapeDtypeStruct((nb, seq, D_MODEL), BF16),
            jax.ShapeDtypeStruct((nb, D_MODEL, D_STATE), F32),
            jax.ShapeDtypeStruct((nb, SSD_CONV - 1, CONV_DIM), F32),
        ],
        scratch_shapes=[pltpu.VMEM((D_STATE, D_MODEL), F32), pltpu.VMEM((CONV_PAD, CONV_DIM), F32)],
        compiler_params=_cp("parallel", "arbitrary"),
        name="ssd_scan",
    )(proj3, proj3, proj3, proj3, proj3, conv_state, conv_w, conv_b.reshape(1, CONV_DIM),
      p["dt_bias"], p["a_log"], p["d_exp"], p["ssd_norm"], p["e_head"])
    return u.reshape(nb * seq, D_MODEL), s_out.reshape(nb, SSD_HEADS, SSD_HEAD_DIM, D_STATE), conv_new


def _gla_gate_log(glr, wg, bg):
    x = _dot(glr.astype(BF16), wg) + bg
    return -_softplus(-x) / GATE_TAU


def _gla_chunk(q_ref, k_ref, v_ref, r_ref, glr_ref, wg_ref, bg_ref, nrm_ref, o_ref, st_ref):
    q = GLA_CHUNK
    glog = _gla_gate_log(glr_ref[...], wg_ref[...], bg_ref[...])
    row = lax.broadcasted_iota(jnp.int32, (q, q), 0)
    col = lax.broadcasted_iota(jnp.int32, (q, q), 1)
    tril = row >= col
    bc = _dot_sel(glog, tril.astype(BF16), sel_first=True)
    last = bc[q - 1 : q, :]
    kk = k_ref[...]
    qe = q_ref[...] * (GLA_HEAD_K ** -0.5) * jnp.exp(bc)
    ke = kk * jnp.exp(-bc)
    kd = kk * jnp.exp(last - bc)
    elast = jnp.exp(last)
    v16 = v_ref[...].astype(BF16)
    zeros_v = jnp.zeros((q, GLA_HEAD_V), BF16)
    outs = []
    for h in range(GLA_HEADS):
        ks = slice(h * GLA_HEAD_K, (h + 1) * GLA_HEAD_K)
        vs = slice(h * GLA_HEAD_V, (h + 1) * GLA_HEAD_V)
        qh = qe[:, ks].astype(BF16)
        kh = ke[:, ks].astype(BF16)
        att = jnp.where(tril, _dot_nt(qh, kh), 0.0)
        s_h = st_ref[ks, :]
        outs.append(_dot(att.astype(BF16), v16[:, vs]) + _dot(qh, s_h.astype(BF16)))
        xt = jnp.concatenate([kd[:, ks], jnp.broadcast_to(elast[:, ks], (q, GLA_HEAD_K))], axis=0).T
        v2 = jnp.concatenate([v16[:, vs], zeros_v], axis=0)
        st_ref[ks, :] = xt[:, q : q + 1] * s_h + _dot(xt.astype(BF16), v2)
    o = jnp.concatenate(outs, axis=1)
    o_ref[...] = _gla_out_norm(o, r_ref[...], nrm_ref[...]).astype(o_ref.dtype)


def _gla_scan_kernel(q_ref, k_ref, v_ref, r_ref, glr_ref, wg_ref, bg_ref, nrm_ref,
                     o_ref, sout_ref, st_ref, *, nsteps, nsub):
    step = pl.program_id(1)

    @pl.when(step == 0)
    def _():
        st_ref[...] = jnp.zeros_like(st_ref)

    for sub in range(nsub):
        rows = lambda ref: ref.at[0, pl.ds(sub * GLA_CHUNK, GLA_CHUNK)]
        _gla_chunk(rows(q_ref), rows(k_ref), rows(v_ref), rows(r_ref), rows(glr_ref), wg_ref, bg_ref, nrm_ref,
                   rows(o_ref), st_ref)

    @pl.when(step == nsteps - 1)
    def _():
        sout_ref[0] = st_ref[...]


def gla_scan(proj3, p):
    nb, seq, _ = proj3.shape
    nsub = SCAN_ROWS_PER_STEP // GLA_CHUNK if seq % SCAN_ROWS_PER_STEP == 0 else 1
    q = nsub * GLA_CHUNK
    nsteps = seq // q
    o, s_out = pl.pallas_call(
        functools.partial(_gla_scan_kernel, nsteps=nsteps, nsub=nsub),
        grid=(nb, nsteps),
        in_specs=[
            pl.BlockSpec((1, q, GLA_KEY_DIM), lambda b, c: (b, c, COL_Q // GLA_KEY_DIM)),
            pl.BlockSpec((1, q, GLA_KEY_DIM), lambda b, c: (b, c, COL_K // GLA_KEY_DIM)),
            pl.BlockSpec((1, q, D_MODEL), lambda b, c: (b, c, COL_V // D_MODEL)),
            pl.BlockSpec((1, q, D_MODEL), lambda b, c: (b, c, COL_R // D_MODEL)),
            pl.BlockSpec((1, q, LANE), lambda b, c: (b, c, COL_GLR // LANE)),
            pl.BlockSpec((LANE, GLA_KEY_DIM), lambda b, c: (0, 0)),
            pl.BlockSpec((1, GLA_KEY_DIM), lambda b, c: (0, 0)),
            pl.BlockSpec((1, GLA_HEAD_V), lambda b, c: (0, 0)),
        ],
        out_specs=[
            pl.BlockSpec((1, q, D_MODEL), lambda b, c: (b, c, 0)),
            pl.BlockSpec((1, GLA_KEY_DIM, GLA_HEAD_V), lambda b, c: (b, 0, 0)),
        ],
        out_shape=[
            jax.ShapeDtypeStruct((nb, seq, D_MODEL), BF16),
            jax.ShapeDtypeStruct((nb, GLA_KEY_DIM, GLA_HEAD_V), F32),
        ],
        scratch_shapes=[pltpu.VMEM((GLA_KEY_DIM, GLA_HEAD_V), F32)],
        compiler_params=_cp("parallel", "arbitrary"),
        name="gla_scan",
    )(proj3, proj3, proj3, proj3, proj3, p["w_gate"], p["b_gate"], p["gla_norm"])
    return o.reshape(nb * seq, D_MODEL), s_out.reshape(nb, GLA_HEADS, GLA_HEAD_K, GLA_HEAD_V)


TOK_BLOCK = 128


def _row_shift(x, d, tpos):
    return jnp.where(tpos >= d, pltpu.roll(x, d, 0), 0.0)


def _seq_cumsum_and_last(a, seq, tpos):
    nrows = a.shape[0]
    acum = a
    for d in range(1, seq):
        acum = acum + _row_shift(a, d, tpos)
    last = jnp.where(tpos == seq - 1, acum, 0.0)
    for d in range(1, seq):
        last = last + jnp.where(tpos == seq - 1 - d, pltpu.roll(acum, nrows - d, 0), 0.0)
    return acum, last


def _ssd_step_pre_kernel(xs_ref, b_ref, c_ref, dt_ref, dtb_ref, alog_ref, dexp_ref, e_ref, gh_ref,
                         ypart_ref, eexp_ref, xwt_ref, el_ref, *, seq):
    nrows = xs_ref.shape[0]
    xs = xs_ref[...]
    bm = b_ref[...]
    cm = c_ref[...]
    dt = _softplus(dt_ref[...] + dtb_ref[...])
    a = dt * (-jnp.exp(alog_ref[...]))
    pos = lambda w: lax.broadcasted_iota(jnp.int32, (nrows, w), 0) % seq
    t_h, t_c, t_x = pos(LANE), pos(BC_W), pos(D_MODEL)
    acum, last = _seq_cumsum_and_last(a, seq, t_h)
    e_mat = e_ref[...]
    eexp_ref[...] = _dot_sel(jnp.exp(acum), e_mat)
    wexp = _dot_sel(jnp.exp(last - acum) * dt, e_mat)
    xwt_ref[...] = (xs * wexp).T.astype(xwt_ref.dtype)
    el_ref[...] = jnp.exp(last)
    y = dexp_ref[...] * xs
    for d in range(seq):
        if d == 0:
            cbh = _dot_sel(cm * bm, gh_ref[...])
            coef = dt
            xd = xs
        else:
            cbh = _dot_sel(cm * _row_shift(bm, d, t_c), gh_ref[...])
            coef = jnp.where(t_h >= d, jnp.exp(acum - pltpu.roll(acum, d, 0)) * pltpu.roll(dt, d, 0), 0.0)
            xd = _row_shift(xs, d, t_x)
        y = y + _dot_sel(cbh * coef, e_mat) * xd
    ypart_ref[...] = y


def _ssd_step_state_kernel(st_ref, c_ref, b_ref, xwt_ref, el_ref, ypart_ref, eexp_ref, z_ref, nrm_ref,
                           u_ref, so_ref, *, sb, seq):
    i = pl.program_id(0)
    rows = sb * seq
    steps_per_block = TOK_BLOCK // rows
    base = (i % steps_per_block) * rows
    c16 = c_ref[...].astype(BF16)
    btok = b_ref[...]
    tok = lax.broadcasted_iota(jnp.int32, (TOK_BLOCK, LANE), 0)
    rsel = lax.broadcasted_iota(jnp.int32, (rows, GROUP_W), 0)
    heads_per_group = SSD_HEADS // N_GROUPS
    yoff = [jnp.zeros((rows, GROUP_W), F32) for _ in range(N_GROUPS)]
    for s in range(sb):
        lo = base + seq * s
        own = (tok >= lo) & (tok < lo + seq)
        mine = (rsel >= seq * s) & (rsel < seq * (s + 1))
        for g in range(N_GROUPS):
            sl = slice(g * GROUP_W, (g + 1) * GROUP_W)
            yo = _dot_nt(c16[:, g * D_STATE : (g + 1) * D_STATE], st_ref[s, sl, :].astype(BF16))
            yoff[g] = jnp.where(mine, yo, yoff[g])
            bsel = jnp.where(own, btok[:, g * D_STATE : (g + 1) * D_STATE], 0.0).astype(BF16)
            upd = _dot(xwt_ref[sl, :], bsel)
            for r in range(heads_per_group):
                h = g * heads_per_group + r
                hs = slice(h * SSD_HEAD_DIM, (h + 1) * SSD_HEAD_DIM)
                so_ref[s, hs, :] = el_ref[seq * s, h] * st_ref[s, hs, :] + upd[r * SSD_HEAD_DIM : (r + 1) * SSD_HEAD_DIM]
    y = ypart_ref[...] + jnp.concatenate(yoff, axis=1) * eexp_ref[...]
    u_ref[...] = _ssd_gate_norm(y, z_ref[...], nrm_ref[...]).astype(u_ref.dtype)


def ssd_step(xc2, proj2, state, p, seq, hosted=None):
    ntok = xc2.shape[0]
    nseq = ntok // seq
    full = lambda shape: pl.BlockSpec(shape, lambda i: (0,) * len(shape))
    ypart, eexp, xwt, elast = pl.pallas_call(
        functools.partial(_ssd_step_pre_kernel, seq=seq),
        grid=(1,),
        in_specs=[
            pl.BlockSpec((ntok, D_MODEL), lambda i: (0, 0)),
            pl.BlockSpec((ntok, BC_W), lambda i: (0, D_MODEL // BC_W)),
            pl.BlockSpec((ntok, BC_W), lambda i: (0, D_MODEL // BC_W + 1)),
            pl.BlockSpec((ntok, LANE), lambda i: (0, COL_DT // LANE)),
            full((1, LANE)), full((1, LANE)), full((1, D_MODEL)),
            full((LANE, D_MODEL)), full((BC_W, LANE)),
        ],
        out_specs=[full((ntok, D_MODEL)), full((ntok, D_MODEL)), full((D_MODEL, ntok)), full((ntok, LANE))],
        out_shape=[
            jax.ShapeDtypeStruct((ntok, D_MODEL), F32),
            jax.ShapeDtypeStruct((ntok, D_MODEL), F32),
            jax.ShapeDtypeStruct((D_MODEL, ntok), BF16),
            jax.ShapeDtypeStruct((ntok, LANE), F32),
        ],
        compiler_params=_cp("arbitrary"),
        name="ssd_step_pre",
    )(xc2, xc2, xc2, proj2, p["dt_bias"], p["a_log"], p["d_exp"], p["e_head"], p["g_head"])

    sb = 4
    rows = sb * seq
    spb = TOK_BLOCK // rows
    st3 = state.reshape(nseq, D_MODEL, D_STATE)
    nsteps = nseq // sb
    c = lambda i: jnp.minimum(i, nsteps - 1)
    (u, s_new), hosted_out = call_with_hosted_matmul(
        functools.partial(_ssd_step_state_kernel, sb=sb, seq=seq),
        nsteps,
        in_specs=[
            pl.BlockSpec((sb, D_MODEL, D_STATE), lambda i: (c(i), 0, 0)),
            pl.BlockSpec((rows, BC_W), lambda i: (c(i), D_MODEL // BC_W + 1)),
            pl.BlockSpec((TOK_BLOCK, BC_W), lambda i: (c(i) // spb, D_MODEL // BC_W)),
            pl.BlockSpec((D_MODEL, TOK_BLOCK), lambda i: (0, c(i) // spb)),
            pl.BlockSpec((rows, LANE), lambda i: (c(i), 0), memory_space=pltpu.SMEM),
            pl.BlockSpec((rows, D_MODEL), lambda i: (c(i), 0)),
            pl.BlockSpec((rows, D_MODEL), lambda i: (c(i), 0)),
            pl.BlockSpec((rows, D_MODEL), lambda i: (c(i), COL_Z // D_MODEL)),
            pl.BlockSpec((1, D_MODEL), lambda i: (0, 0)),
        ],
        out_specs=[
            pl.BlockSpec((rows, D_MODEL), lambda i: (c(i), 0)),
            pl.BlockSpec((sb, D_MODEL, D_STATE), lambda i: (c(i), 0, 0)),
        ],
        out_shape=[
            jax.ShapeDtypeStruct((ntok, D_MODEL), BF16),
            jax.ShapeDtypeStruct((nseq, D_MODEL, D_STATE), F32),
        ],
        args=(st3, xc2, xc2, xwt, elast, ypart, eexp, proj2, p["ssd_norm"]),
        name="ssd_step_state",
        hosted=hosted,
    )
    return u, s_new.reshape(nseq, SSD_HEADS, SSD_HEAD_DIM, D_STATE), hosted_out


def _gla_step_pre_kernel(q_ref, k_ref, v_ref, glr_ref, wg_ref, bg_ref, gv_ref,
                         oin_ref, qe_ref, kdt_ref, elt_ref, *, seq):
    nrows = q_ref.shape[0]
    glog = _gla_gate_log(glr_ref[...], wg_ref[...], bg_ref[...])
    pos = lambda w: lax.broadcasted_iota(jnp.int32, (nrows, w), 0) % seq
    t_k, t_v = pos(GLA_KEY_DIM), pos(D_MODEL)
    bc, last = _seq_cumsum_and_last(glog, seq, t_k)
    kk = k_ref[...]
    qe = q_ref[...] * (GLA_HEAD_K ** -0.5) * jnp.exp(bc)
    ke = kk * jnp.exp(-bc)
    qe_ref[...] = qe
    kdt_ref[...] = (kk * jnp.exp(last - bc)).T
    elt_ref[...] = jnp.exp(last).T
    v = v_ref[...]
    gv = gv_ref[...]
    o = jnp.zeros((nrows, D_MODEL), F32)
    for d in range(seq):
        ked = ke if d == 0 else _row_shift(ke, d, t_k)
        vd = v if d == 0 else _row_shift(v, d, t_v)
        att = _dot((qe * ked).astype(BF16), gv)
        o = o + att * vd
    oin_ref[...] = o


def _gla_step_state_kernel(st_ref, qe_ref, v_ref, kdt_ref, elt_ref, oin_ref, r_ref, nrm_ref,
                           o_ref, so_ref, *, sb, seq):
    i = pl.program_id(0)
    rows = sb * seq
    spb = TOK_BLOCK // rows
    base = (i % spb) * rows
    qe16 = qe_ref[...].astype(BF16)
    vtok = v_ref[...]
    tokv = lax.broadcasted_iota(jnp.int32, (TOK_BLOCK, GLA_HEAD_V), 0)
    tok = lax.broadcasted_iota(jnp.int32, (TOK_BLOCK, LANE), 0)
    rsel = lax.broadcasted_iota(jnp.int32, (rows, GLA_HEAD_V), 0)
    ooff = [jnp.zeros((rows, GLA_HEAD_V), F32) for _ in range(GLA_HEADS)]
    for s in range(sb):
        lo = base + seq * s
        own = (tokv >= lo) & (tokv < lo + seq)
        first = jnp.where(tok == lo, 1.0, 0.0).astype(BF16)
        mine = (rsel >= seq * s) & (rsel < seq * (s + 1))
        for h in range(GLA_HEADS):
            ks = slice(h * GLA_HEAD_K, (h + 1) * GLA_HEAD_K)
            vs = slice(h * GLA_HEAD_V, (h + 1) * GLA_HEAD_V)
            s_h = st_ref[s, ks, :]
            oo = _dot(qe16[:, ks], s_h.astype(BF16))
            ooff[h] = jnp.where(mine, oo, ooff[h])
            vsel = jnp.where(own, vtok[:, vs], 0.0).astype(BF16)
            upd = _dot(kdt_ref[ks, :].astype(BF16), vsel)
            ecol = _dot_sel(elt_ref[ks, :], first)
            so_ref[s, ks, :] = jnp.concatenate([ecol] * (GLA_HEAD_V // LANE), axis=1) * s_h + upd
    o = oin_ref[...] + jnp.concatenate(ooff, axis=1)
    o_ref[0] = _gla_out_norm(o, r_ref[...], nrm_ref[...]).astype(o_ref.dtype)


def gla_step(proj2, state, p, seq, hosted=None):
    ntok = proj2.shape[0]
    nseq = ntok // seq
    full = lambda shape: pl.BlockSpec(shape, lambda i: (0,) * len(shape))
    oin, qe, kdt, elt = pl.pallas_call(
        functools.partial(_gla_step_pre_kernel, seq=seq),
        grid=(1,),
        in_specs=[
            pl.BlockSpec((ntok, GLA_KEY_DIM), lambda i: (0, COL_Q // GLA_KEY_DIM)),
            pl.BlockSpec((ntok, GLA_KEY_DIM), lambda i: (0, COL_K // GLA_KEY_DIM)),
            pl.BlockSpec((ntok, D_MODEL), lambda i: (0, COL_V // D_MODEL)),
            pl.BlockSpec((ntok, LANE), lambda i: (0, COL_GLR // LANE)),
            full((LANE, GLA_KEY_DIM)), full((1, GLA_KEY_DIM)), full((GLA_KEY_DIM, D_MODEL)),
        ],
        out_specs=[full((ntok, D_MODEL)), full((ntok, GLA_KEY_DIM)), full((GLA_KEY_DIM, ntok)),
                   full((GLA_KEY_DIM, ntok))],
        out_shape=[
            jax.ShapeDtypeStruct((ntok, D_MODEL), F32),
            jax.ShapeDtypeStruct((ntok, GLA_KEY_DIM), F32),
            jax.ShapeDtypeStruct((GLA_KEY_DIM, ntok), F32),
            jax.ShapeDtypeStruct((GLA_KEY_DIM, ntok), F32),
        ],
        compiler_params=_cp("arbitrary"),
        name="gla_step_pre",
    )(proj2, proj2, proj2, proj2, p["w_gate"], p["b_gate"], p["g_val"])

    sb = 2 if hosted is not None else 4
    rows = sb * seq
    spb = TOK_BLOCK // rows
    st3 = state.reshape(nseq, GLA_KEY_DIM, GLA_HEAD_V)
    nsteps = nseq // sb
    c = lambda i: jnp.minimum(i, nsteps - 1)
    (o, s_new), hosted_out = call_with_hosted_matmul(
        functools.partial(_gla_step_state_kernel, sb=sb, seq=seq),
        nsteps,
        in_specs=[
            pl.BlockSpec((sb, GLA_KEY_DIM, GLA_HEAD_V), lambda i: (c(i), 0, 0)),
            pl.BlockSpec((rows, GLA_KEY_DIM), lambda i: (c(i), 0)),
            pl.BlockSpec((TOK_BLOCK, D_MODEL), lambda i: (c(i) // spb, COL_V // D_MODEL)),
            pl.BlockSpec((GLA_KEY_DIM, TOK_BLOCK), lambda i: (0, c(i) // spb)),
            pl.BlockSpec((GLA_KEY_DIM, TOK_BLOCK), lambda i: (0, c(i) // spb)),
            pl.BlockSpec((rows, D_MODEL), lambda i: (c(i), 0)),
            pl.BlockSpec((rows, D_MODEL), lambda i: (c(i), COL_R // D_MODEL)),
            pl.BlockSpec((1, GLA_HEAD_V), lambda i: (0, 0)),
        ],
        out_specs=[
            pl.BlockSpec((1, rows, D_MODEL), lambda i: (c(i), 0, 0)),
            pl.BlockSpec((sb, GLA_KEY_DIM, GLA_HEAD_V), lambda i: (c(i), 0, 0)),
        ],
        out_shape=[
            jax.ShapeDtypeStruct((nseq // sb, rows, D_MODEL), BF16),
            jax.ShapeDtypeStruct((nseq, GLA_KEY_DIM, GLA_HEAD_V), F32),
        ],
        args=(st3, qe, proj2, kdt, elt, oin, proj2, p["gla_norm"]),
        name="gla_step_state",
        hosted=hosted,
    )
    return o.reshape(ntok, D_MODEL), s_new.reshape(nseq, GLA_HEADS, GLA_HEAD_K, GLA_HEAD_V), hosted_out


def _softmax_rows(sc):
    e = jnp.exp(sc - jnp.max(sc, axis=-1, keepdims=True))
    return e / jnp.sum(e, axis=-1, keepdims=True)


def _cross_block_kernel(hn_ref, h_ref, k_ref, v_ref, wq_ref, wo_ref, g_ref, h2_ref, hn2_ref):
    q16 = _dot(hn_ref[...], wq_ref[...]).astype(BF16)
    outs = []
    for h in range(CROSS_HEADS):
        hs = slice(h * CROSS_HEAD_DIM, (h + 1) * CROSS_HEAD_DIM)
        sc = _dot_nt(q16[:, hs], k_ref[0, :, hs].astype(BF16)) * (CROSS_HEAD_DIM ** -0.5)
        outs.append(_dot(_softmax_rows(sc).astype(BF16), v_ref[0, :, hs].astype(BF16)))
    att = jnp.concatenate(outs, axis=1).astype(BF16)
    h2 = h_ref[...] + _dot(att, wo_ref[...])
    h2_ref[...] = h2
    hn2_ref[...] = _rms(h2, g_ref[...]).astype(hn2_ref.dtype)


def cross_block(hn, h, mem_k, mem_v, w_cq, w_co, g, seq):
    ntok, d = hn.shape
    n_mem = mem_k.shape[1]
    tl = min(seq, 512)
    lt = seq // tl
    row_tile = pl.BlockSpec((tl, d), lambda i: (i, 0))
    resident = lambda shape: pl.BlockSpec(shape, lambda i: (0,) * len(shape), pipeline_mode=pl.Buffered(1))
    kv_spec = pl.BlockSpec((1, n_mem, d), lambda i: (i // lt, 0, 0))
    return pl.pallas_call(
        _cross_block_kernel,
        grid=(ntok // tl,),
        in_specs=[row_tile, row_tile, kv_spec, kv_spec, resident((d, d)), resident((d, d)),
                  pl.BlockSpec((1, d), lambda i: (0, 0))],
        out_specs=[row_tile, row_tile],
        out_shape=[jax.ShapeDtypeStruct((ntok, d), F32), jax.ShapeDtypeStruct((ntok, d), BF16)],
        compiler_params=_cp("parallel"),
        name="cross_block",
    )(hn, h, mem_k, mem_v, w_cq, w_co, g.reshape(1, d))


def _xattn_step_kernel(q_ref, k_ref, v_ref, o_ref, *, nseq, tl):
    rows = nseq * tl
    n_mem = k_ref.shape[1]
    q = q_ref[0]
    qs = jnp.concatenate([q[:, h * CROSS_HEAD_DIM : (h + 1) * CROSS_HEAD_DIM] for h in range(CROSS_HEADS)],
                         axis=0).astype(BF16)
    shape = (CROSS_HEADS * rows, n_mem * CROSS_HEADS)
    col_head = lax.broadcasted_iota(jnp.int32, shape, 1) % CROSS_HEADS
    row_head = lax.broadcasted_iota(jnp.int32, shape, 0) // rows
    same_head = col_head == row_head
    rsel = lax.broadcasted_iota(jnp.int32, (CROSS_HEADS * rows, CROSS_HEAD_DIM), 0) % rows
    out = jnp.zeros((CROSS_HEADS * rows, CROSS_HEAD_DIM), F32)
    for s in range(nseq):
        kall = k_ref[s].reshape(n_mem * CROSS_HEADS, CROSS_HEAD_DIM).astype(BF16)
        vall = v_ref[s].reshape(n_mem * CROSS_HEADS, CROSS_HEAD_DIM).astype(BF16)
        sc = jnp.where(same_head, _dot_nt(qs, kall) * (CROSS_HEAD_DIM ** -0.5), NEG_BIG)
        oh = _dot(_softmax_rows(sc).astype(BF16), vall)
        out = jnp.where((rsel >= tl * s) & (rsel < tl * (s + 1)), oh, out)
    o_ref[0] = jnp.concatenate([out[h * rows : (h + 1) * rows] for h in range(CROSS_HEADS)],
                               axis=1).astype(o_ref.dtype)


def cross_attend_rider(q2, mem_k, mem_v, nb, seq):
    n_mem = mem_k.shape[1]
    nseq, tl = 8 // seq, seq
    rows = nseq * tl
    nblk = nb * seq // rows
    c = lambda i: jnp.minimum(i, nblk - 1)
    kv_spec = pl.BlockSpec((nseq, n_mem, CROSS_HEADS, CROSS_HEAD_DIM), lambda i: (c(i), 0, 0, 0))
    return RiderKernel(
        body=functools.partial(_xattn_step_kernel, nseq=nseq, tl=tl),
        nsteps=nblk,
        in_specs=[pl.BlockSpec((1, rows, D_MODEL), lambda i: (c(i), 0, 0)), kv_spec, kv_spec],
        out_specs=[pl.BlockSpec((1, rows, D_MODEL), lambda i: (c(i), 0, 0))],
        out_shape=[jax.ShapeDtypeStruct((nblk, rows, D_MODEL), BF16)],
        args=(q2.reshape(nblk, rows, D_MODEL), mem_k, mem_v),
    )


PACK_TILE = 512


SUBLANE = 8


def _cast_rows_kernel(w_ref, o_ref, *, nrows):
    j = pl.program_id(0)
    x = w_ref[...]
    row = lax.broadcasted_iota(jnp.int32, x.shape, 0) + j * PACK_TILE
    o_ref[...] = jnp.where(row < nrows, x, 0.0).astype(o_ref.dtype)


def cast_rows(wt, lo, hi, n_out, name):
    kdim = wt.shape[1]
    assert lo % SUBLANE == 0 and lo + n_out <= wt.shape[0]
    return pl.pallas_call(
        functools.partial(_cast_rows_kernel, nrows=hi - lo),
        grid=(n_out // PACK_TILE,),
        in_specs=[pl.BlockSpec((pl.Element(PACK_TILE), pl.Element(kdim)),
                               lambda j: (pl.multiple_of(lo + j * PACK_TILE, SUBLANE), 0))],
        out_specs=pl.BlockSpec((PACK_TILE, kdim), lambda j: (j, 0)),
        out_shape=jax.ShapeDtypeStruct((n_out, kdim), BF16),
        compiler_params=_cp("parallel"),
        name=name,
    )(wt)


def _cast_two_kernel(a_ref, b_ref, oa_ref, ob_ref, *, a_steps):
    step = pl.program_id(0)

    @pl.when(step < a_steps)
    def _():
        oa_ref[...] = a_ref[...].astype(oa_ref.dtype)

    @pl.when(step >= a_steps)
    def _():
        ob_ref[...] = b_ref[...].astype(ob_ref.dtype)


def cast_two_hosting_matmul(wa, wb, hosted, name):
    a_steps, b_steps = wa.shape[1] // PACK_TILE, wb.shape[0] // PACK_TILE
    a_tile = pl.BlockSpec((wa.shape[0], PACK_TILE), lambda i: (0, jnp.minimum(i, a_steps - 1)))
    b_tile = pl.BlockSpec((PACK_TILE, wb.shape[1]), lambda i: (jnp.clip(i - a_steps, 0, b_steps - 1), 0))
    (ca, cb), hosted_out = call_with_hosted_matmul(
        functools.partial(_cast_two_kernel, a_steps=a_steps), a_steps + b_steps, [a_tile, b_tile], [a_tile, b_tile],
        [jax.ShapeDtypeStruct(wa.shape, BF16), jax.ShapeDtypeStruct(wb.shape, BF16)], (wa, wb), name, hosted)
    return ca, cb, hosted_out


def _split_w_in(w_in):
    wt = w_in.T
    ssd_end = D_MODEL + CONV_DIM + SSD_HEADS
    gla_end = ssd_end + 2 * GLA_KEY_DIM + 2 * D_MODEL + GATE_RANK
    return (cast_rows(wt, 0, ssd_end, N_SSD_PROJ, "pack_ssd_in"),
            cast_rows(wt, ssd_end, gla_end, N_GLA_PROJ, "pack_gla_in"),
            cast_rows(wt, gla_end, wt.shape[0], 2 * D_MODEL, "pack_gate_in"))


def _params(ssd_dt_bias, ssd_A_log, ssd_D, ssd_norm, w_gla_gate, b_gla_gate, gla_norm):
    padv = lambda a: jnp.pad(a.astype(F32), (0, LANE - a.shape[0])).reshape(1, LANE)
    head_of_chan = jnp.arange(D_MODEL, dtype=jnp.int32) // SSD_HEAD_DIM
    e_head = (jnp.arange(LANE, dtype=jnp.int32)[:, None] == head_of_chan[None, :]).astype(BF16)
    group_of_bc = jnp.arange(BC_W, dtype=jnp.int32) // D_STATE
    lane_h = jnp.arange(LANE, dtype=jnp.int32)
    g_head = ((lane_h[None, :] // (SSD_HEADS // N_GROUPS) == group_of_bc[:, None])
              & (lane_h[None, :] < SSD_HEADS)).astype(BF16)
    khead = jnp.arange(GLA_KEY_DIM, dtype=jnp.int32) // GLA_HEAD_K
    vhead = jnp.arange(D_MODEL, dtype=jnp.int32) // GLA_HEAD_V
    g_val = (khead[:, None] == vhead[None, :]).astype(BF16)
    return dict(
        dt_bias=padv(ssd_dt_bias), a_log=padv(ssd_A_log),
        d_exp=jnp.repeat(ssd_D.astype(F32), SSD_HEAD_DIM).reshape(1, D_MODEL),
        ssd_norm=ssd_norm.astype(F32).reshape(1, D_MODEL),
        e_head=e_head, g_head=g_head, g_val=g_val,
        w_gate=jnp.pad(w_gla_gate, ((0, LANE - GATE_RANK), (0, 0))).astype(BF16),
        b_gate=b_gla_gate.astype(F32).reshape(1, GLA_KEY_DIM),
        gla_norm=gla_norm.astype(F32).reshape(1, GLA_HEAD_V),
    )


def _mixers(x3, xn, ssd_conv, ssd_state, gla_state, w, p, long_seq, proj_ssd=None, proj_gla=None, gates=None,
            other_xn=None):
    nb, seq, d = x3.shape
    ntok = nb * seq
    x2 = x3.reshape(ntok, d)
    w_ssd_in, w_gla_in, w_gate_in = w["w_in"]
    in_proj = functools.partial(matmul, xn, tm=2048, w_rows_are_outputs=True)
    proj_ssd = in_proj(w_ssd_in, name="in_proj_ssd") if proj_ssd is None else proj_ssd
    proj_gla = in_proj(w_gla_in, name="in_proj_gla") if proj_gla is None else proj_gla
    gates = in_proj(w_gate_in, name="in_proj_gates") if gates is None else gates
    hosted = (lambda wt: HostedMatmul(other_xn, wt)) if other_xn is not None else (lambda wt: None)
    other_ssd = other_gla = None
    if long_seq:
        u, ssd_new, ssd_conv_new = ssd_scan(proj_ssd.reshape(nb, seq, N_SSD_PROJ), ssd_conv,
                                            w["ssd_conv_w"], w["ssd_conv_b"], p)
        o, gla_new = gla_scan(proj_gla.reshape(nb, seq, N_GLA_PROJ), p)
    else:
        xc2, ssd_conv_new = short_conv(proj_ssd, seq, [COL_XBC], CONV_DIM, [ssd_conv], [w["ssd_conv_w"]],
                                       [w["ssd_conv_b"]], SSD_CONV, False, F32, "ssd_conv")
        u, ssd_new, other_ssd = ssd_step(xc2, proj_ssd, ssd_state, p, seq, hosted(w_ssd_in))
        o, gla_new, other_gla = gla_step(proj_gla, gla_state, p, seq, hosted(w_gla_in))
    merged = merge_branches(u, o, w["w_ssd_out"], w["w_gla_out"], gates)
    h, hn = mm_res_norm(merged, w["w_mix_out"], x2, w["norm_cross"], True, BF16, "mix_out")
    return h, hn, ssd_conv_new, ssd_new, gla_new, (other_ssd, other_gla)


def _conv_ffn(h2, hn2, ffn_conv, w, nb, seq, long_seq, rider=None):
    ntok, d = h2.shape
    ffn = w["w_down"].shape[0]
    cw, cbias = w["ffn_conv_w"], w["ffn_conv_b"]
    rider_out = None
    if long_seq:
        act, ffn_conv_new, rider_out = ffn_up_conv_act(hn2.reshape(nb, seq, d), w["w_up"], cw, cbias, ffn_conv, rider)
    else:
        up = matmul(hn2, w["w_up"], name="ffn_up")
        act, fa, fg = short_conv(up, seq, [0, ffn], ffn, [ffn_conv[:, :, :ffn], ffn_conv[:, :, ffn:]],
                                 [cw[:, :ffn], cw[:, ffn:]], [cbias[:ffn], cbias[ffn:]],
                                 FFN_CONV, True, BF16, "ffn_conv")
        ffn_conv_new = jnp.concatenate([fa, fg], axis=-1)
    y = mm_res_norm(act.reshape(ntok, ffn), w["w_down"], h2, w["norm_final"], False, F32, "ffn_down")
    return y.reshape(nb, seq, d), ffn_conv_new, rider_out


def kernel(x_prompt, x_sample, cache_mem_k, cache_mem_v, state_ssd_conv, state_ssd, state_gla, state_ffn_conv, mem_prompt, norm_mix, w_in, ssd_conv_w, ssd_conv_b, ssd_dt_bias, ssd_A_log, ssd_D, ssd_norm, w_ssd_out, w_gla_gate, b_gla_gate, gla_norm, w_gla_out, w_mix_out, norm_cross, norm_mem, w_cq, w_ck, w_cv, w_co, norm_ffn, w_up, ffn_conv_w, ffn_conv_b, w_down, norm_final):
    nb, seq, d = x_prompt.shape
    n_mem = mem_prompt.shape[1]
    ffn2 = w_up.shape[1]
    w = dict(
        norm_mix=norm_mix, norm_cross=norm_cross, norm_ffn=norm_ffn, norm_final=norm_final,
        w_in=_split_w_in(w_in), ssd_conv_w=ssd_conv_w, ssd_conv_b=ssd_conv_b,
        w_ssd_out=w_ssd_out.astype(BF16), w_gla_out=w_gla_out.astype(BF16), w_mix_out=w_mix_out.astype(BF16),
        w_cq=w_cq.astype(BF16), w_co=w_co.astype(BF16),
        ffn_conv_w=ffn_conv_w, ffn_conv_b=ffn_conv_b,
    )
    p = _params(ssd_dt_bias, ssd_A_log, ssd_D, ssd_norm, w_gla_gate, b_gla_gate, gla_norm)

    mn = rmsnorm_cast(mem_prompt.reshape(nb * n_mem, d), norm_mem)
    p_mem_k, p_mem_v, mem_k_heads, mem_v_heads = mem_kv(mn, w_ck.astype(BF16), w_cv.astype(BF16))
    p_mem_k = p_mem_k.reshape(nb, n_mem, d)
    p_mem_v = p_mem_v.reshape(nb, n_mem, d)
    zeros_ssd_conv = jnp.zeros((nb, SSD_CONV - 1, CONV_DIM), F32)
    zeros_ffn_conv = jnp.zeros((nb, FFN_CONV - 1, ffn2), F32)
    ns, sseq, _ = x_sample.shape
    xn_prompt = rmsnorm_cast(x_prompt.reshape(nb * seq, d), norm_mix)
    xn_sample = rmsnorm_cast(x_sample.reshape(ns * sseq, d), norm_mix)
    w["w_up"], w["w_down"], gates_p = cast_two_hosting_matmul(
        w_up, w_down, HostedMatmul(xn_prompt, w["w_in"][2]), "cast_w_ffn")

    h_s, hn_s, s_ssd_conv, s_ssd, s_gla, (proj_ssd_p, proj_gla_p) = _mixers(
        x_sample, xn_sample, state_ssd_conv, state_ssd, state_gla, w, p, False, other_xn=xn_prompt)

    h_p, hn_p, p_ssd_conv, p_ssd, p_gla, _ = _mixers(
        x_prompt, xn_prompt, zeros_ssd_conv, None, None, w, p, True, proj_ssd=proj_ssd_p, proj_gla=proj_gla_p,
        gates=gates_p)
    h2_p, hn2_p = cross_block(hn_p, h_p, p_mem_k, p_mem_v, w["w_cq"], w["w_co"], norm_ffn, seq)

    attend_s = cross_attend_rider(matmul(hn_s, w["w_cq"], name="cross_q"), cache_mem_k, cache_mem_v, ns, sseq)
    y_prompt, p_ffn_conv, (att_s,) = _conv_ffn(h2_p, hn2_p, zeros_ffn_conv, w, nb, seq, True, rider=attend_s)

    h2_s, hn2_s = mm_res_norm(att_s.reshape(ns * sseq, d), w["w_co"], h_s, norm_ffn, True, BF16, "cross_out")
    y_sample, s_ffn_conv, _ = _conv_ffn(h2_s, hn2_s, state_ffn_conv, w, ns, sseq, False)

    head_shape = (n_mem, CROSS_HEADS, CROSS_HEAD_DIM)
    return (y_prompt, y_sample, p_ssd_conv, p_ssd, p_gla, p_ffn_conv,
            mem_k_heads.reshape((nb,) + head_shape), mem_v_heads.reshape((nb,) + head_shape),
            s_ssd_conv, s_ssd, s_gla, s_ffn_conv)
```

```python
import dataclasses
import functools
from typing import NamedTuple

import jax
import jax.numpy as jnp
from jax import lax
from jax.experimental import pallas as pl
from jax.experimental.pallas import tpu as pltpu

F32 = jnp.float32
BF16 = jnp.bfloat16
EPS = 1e-6
NEG_BIG = -1e30

D_MODEL = 2048
SSD_HEAD_DIM = 64
SSD_HEADS = 32
D_STATE = 128
N_GROUPS = 4
GROUP_W = D_MODEL // N_GROUPS
BC_W = N_GROUPS * D_STATE
CONV_DIM = D_MODEL + 2 * BC_W
SSD_CONV = 4
SSD_CHUNK = 128
GLA_HEADS = 4
GLA_KEY_DIM = 1024
GLA_HEAD_K = 256
GLA_HEAD_V = 512
GATE_RANK = 16
GATE_TAU = 16.0
GLA_CHUNK = 64
CROSS_HEADS = 4
CROSS_HEAD_DIM = 512
FFN_CONV = 3
LANE = 128

COL_Z, COL_XBC, COL_DT, N_SSD_PROJ = 0, 2048, 5120, 5632
COL_Q, COL_K, COL_V, COL_R, COL_GLR, N_GLA_PROJ = 0, 1024, 2048, 4096, 6144, 6656
COL_GA, COL_GB = 0, 2048

VMEM_LIMIT = 56 * 1024 * 1024


def _cp(*sem):
    return pltpu.CompilerParams(dimension_semantics=sem, vmem_limit_bytes=VMEM_LIMIT)


def _dot(a, b, prec=None):
    return jnp.dot(a, b, preferred_element_type=F32, precision=prec)


def _dot_nt(a, b):
    return lax.dot_general(a, b, (((1,), (1,)), ((), ())), preferred_element_type=F32)


def _split3(x):
    x1 = x.astype(BF16)
    r1 = x - x1.astype(F32)
    x2 = r1.astype(BF16)
    x3 = (r1 - x2.astype(F32)).astype(BF16)
    return x1, x2, x3


def _dot_sel(x, sel, sel_first=False):
    parts = _split3(x)
    if sel_first:
        return _dot(sel, parts[0]) + _dot(sel, parts[1]) + _dot(sel, parts[2])
    return _dot(parts[0], sel) + _dot(parts[1], sel) + _dot(parts[2], sel)


def _sigmoid(x):
    return 1.0 / (1.0 + jnp.exp(-x))


def _silu(x):
    return x * _sigmoid(x)


def _softplus(x):
    return jnp.maximum(x, 0.0) + jnp.log(1.0 + jnp.exp(-jnp.abs(x)))


def _rms(x, g):
    ms = jnp.mean(x * x, axis=-1, keepdims=True)
    return x * lax.rsqrt(ms + EPS) * g


def _rmsnorm_kernel(x_ref, g_ref, o_ref):
    o_ref[...] = _rms(x_ref[...], g_ref[...]).astype(o_ref.dtype)


def rmsnorm_cast(x2, g):
    m, d = x2.shape
    tm = min(m, 512)
    return pl.pallas_call(
        _rmsnorm_kernel,
        grid=(m // tm,),
        in_specs=[pl.BlockSpec((tm, d), lambda i: (i, 0)), pl.BlockSpec((1, d), lambda i: (0, 0))],
        out_specs=pl.BlockSpec((tm, d), lambda i: (i, 0)),
        out_shape=jax.ShapeDtypeStruct((m, d), BF16),
        compiler_params=_cp("parallel"),
        name="rmsnorm_cast",
    )(x2, g.reshape(1, d))


def _mm_kernel(a_ref, w_ref, o_ref, *, w_rows_are_outputs):
    dot = _dot_nt if w_rows_are_outputs else _dot
    o_ref[...] = dot(a_ref[...], w_ref[...]).astype(o_ref.dtype)


def matmul(a, w, out_dtype=F32, tm=1024, tn=512, w_rows_are_outputs=False, name="matmul"):
    m, k = a.shape
    n = w.shape[0] if w_rows_are_outputs else w.shape[1]
    tm = min(m, tm)
    if w_rows_are_outputs:
        w_spec = pl.BlockSpec((tn, k), lambda i, j: (j, 0))
    else:
        w_spec = pl.BlockSpec((k, tn), lambda i, j: (0, j))
    return pl.pallas_call(
        functools.partial(_mm_kernel, w_rows_are_outputs=w_rows_are_outputs),
        grid=(m // tm, n // tn),
        in_specs=[pl.BlockSpec((tm, k), lambda i, j: (i, 0)), w_spec],
        out_specs=pl.BlockSpec((tm, tn), lambda i, j: (i, j)),
        out_shape=jax.ShapeDtypeStruct((m, n), out_dtype),
        compiler_params=_cp("parallel", "arbitrary"),
        name=name,
    )(a, w)


def _mem_kv_kernel(a_ref, wk_ref, wv_ref, k_ref, v_ref, k4_ref, v4_ref):
    a = a_ref[...]
    for w_ref, flat_ref, heads_ref in ((wk_ref, k_ref, k4_ref), (wv_ref, v_ref, v4_ref)):
        y = _dot(a, w_ref[...])
        flat_ref[...] = y
        heads_ref[...] = y.reshape(heads_ref.shape)


def mem_kv(mn, w_ck, w_cv):
    m, d = mn.shape
    tm = min(m, 256)
    flat = pl.BlockSpec((tm, d), lambda i: (i, 0))
    heads = pl.BlockSpec((tm, CROSS_HEADS, CROSS_HEAD_DIM), lambda i: (i, 0, 0))
    weight = pl.BlockSpec((d, d), lambda i: (0, 0), pipeline_mode=pl.Buffered(1))
    return pl.pallas_call(
        _mem_kv_kernel,
        grid=(m // tm,),
        in_specs=[flat, weight, weight],
        out_specs=[flat, flat, heads, heads],
        out_shape=[jax.ShapeDtypeStruct((m, d), F32)] * 2
        + [jax.ShapeDtypeStruct((m, CROSS_HEADS, CROSS_HEAD_DIM), F32)] * 2,
        compiler_params=_cp("parallel"),
        name="mem_kv",
    )(mn, w_ck, w_cv)


class HostedMatmul(NamedTuple):
    a: jax.Array
    w: jax.Array
    tm: int = 2048
    tn: int = 512


def call_with_hosted_matmul(body, nsteps, in_specs, out_specs, out_shape, args, name, hosted=None):
    if hosted is None:
        outs = pl.pallas_call(body, grid=(nsteps,), in_specs=in_specs, out_specs=out_specs, out_shape=out_shape,
                              compiler_params=_cp("arbitrary"), name=name)(*args)
        return list(outs), None
    m, k = hosted.a.shape
    n = hosted.w.shape[0]
    tm, tn = min(m, hosted.tm), hosted.tn
    nj = n // tn
    hsteps = (m // tm) * nj
    total = max(nsteps, hsteps)
    hc = lambda i: jnp.minimum(i, hsteps - 1)
    h_in = [pl.BlockSpec((tm, k), lambda i: (hc(i) // nj, 0)), pl.BlockSpec((tn, k), lambda i: (hc(i) % nj, 0))]
    h_out = pl.BlockSpec((tm, tn), lambda i: (hc(i) // nj, hc(i) % nj))
    n_in, n_out = len(in_specs), len(out_specs)

    def fused(*refs):
        g_in = refs[:n_in]
        ha_ref, hw_ref = refs[n_in : n_in + 2]
        g_out = refs[n_in + 2 : n_in + 2 + n_out]
        ho_ref = refs[n_in + 2 + n_out]
        step = pl.program_id(0)

        def guest():
            body(*g_in, *g_out)

        def host():
            ho_ref[...] = _dot_nt(ha_ref[...], hw_ref[...])

        guest() if nsteps == total else pl.when(step < nsteps)(guest)
        host() if hsteps == total else pl.when(step < hsteps)(host)

    outs = pl.pallas_call(
        fused,
        grid=(total,),
        in_specs=list(in_specs) + h_in,
        out_specs=list(out_specs) + [h_out],
        out_shape=list(out_shape) + [jax.ShapeDtypeStruct((m, n), F32)],
        compiler_params=_cp("arbitrary"),
        name=name,
    )(*args, hosted.a, hosted.w)
    return list(outs[:-1]), outs[-1]


def _mm_res_norm_kernel(a_ref, w_ref, res_ref, g_ref, *out_refs):
    h = res_ref[...] + _dot(a_ref[...], w_ref[...])
    if len(out_refs) == 2:
        out_refs[0][...] = h
    out_refs[-1][...] = _rms(h, g_ref[...]).astype(out_refs[-1].dtype)


def mm_res_norm(a, w, res, g, emit_h, norm_dtype, name):
    m, kdim = a.shape
    n = w.shape[1]
    tm = min(m, 512)
    row_tile = lambda width: pl.BlockSpec((tm, width), lambda i: (i, 0))
    out_shape = [jax.ShapeDtypeStruct((m, n), norm_dtype)]
    out_specs = [row_tile(n)]
    if emit_h:
        out_shape = [jax.ShapeDtypeStruct((m, n), F32)] + out_shape
        out_specs = [row_tile(n)] + out_specs
    outs = pl.pallas_call(
        _mm_res_norm_kernel,
        grid=(m // tm,),
        in_specs=[row_tile(kdim), pl.BlockSpec((kdim, n), lambda i: (0, 0), pipeline_mode=pl.Buffered(1)),
                  row_tile(n), pl.BlockSpec((1, n), lambda i: (0, 0))],
        out_specs=out_specs,
        out_shape=out_shape,
        compiler_params=_cp("parallel"),
        name=name,
    )(a, w, res, g.reshape(1, n))
    return outs if emit_h else outs[0]


def _merge_kernel(u_ref, o_ref, wa_ref, wb_ref, ga_ref, gb_ref, out_ref):
    a = _dot(u_ref[...], wa_ref[...])
    b = _dot(o_ref[...], wb_ref[...])
    out_ref[...] = (_sigmoid(ga_ref[...]) * a + _sigmoid(gb_ref[...]) * b).astype(out_ref.dtype)


def merge_branches(u, o, wa, wb, proj):
    m, d = u.shape
    tm, tn = min(m, 1024), 512
    return pl.pallas_call(
        _merge_kernel,
        grid=(m // tm, d // tn),
        in_specs=[
            pl.BlockSpec((tm, d), lambda i, j: (i, 0)),
            pl.BlockSpec((tm, d), lambda i, j: (i, 0)),
            pl.BlockSpec((d, tn), lambda i, j: (0, j)),
            pl.BlockSpec((d, tn), lambda i, j: (0, j)),
            pl.BlockSpec((tm, tn), lambda i, j: (i, COL_GA // tn + j)),
            pl.BlockSpec((tm, tn), lambda i, j: (i, COL_GB // tn + j)),
        ],
        out_specs=pl.BlockSpec((tm, tn), lambda i, j: (i, j)),
        out_shape=jax.ShapeDtypeStruct((m, d), BF16),
        compiler_params=_cp("parallel", "arbitrary"),
        name="merge_branches",
    )(u, o, wa, wb, proj, proj)


CONV_PAD = 8


def _short_conv_kernel(*refs, taps, seq, nstreams, swiglu):
    ins = refs[: 4 * nstreams]
    out_ref = refs[4 * nstreams]
    ns_refs = refs[4 * nstreams + 1 :]
    vals = []
    for s in range(nstreams):
        u_ref, st_ref, w_ref, b_ref = ins[4 * s : 4 * s + 4]
        nseq, _, tc = st_ref.shape
        full = jnp.concatenate([st_ref[...], u_ref[...].reshape(nseq, seq, tc)], axis=1)
        acc = b_ref[...]
        for k in range(taps):
            acc = acc + full[:, k : k + seq] * w_ref[k : k + 1, :]
        vals.append(acc)
        ns_refs[s][...] = full[:, seq : seq + taps - 1]
    out = _silu(vals[1]) * vals[0] if swiglu else _silu(vals[0])
    out_ref[...] = out.reshape(out_ref.shape).astype(out_ref.dtype)


def short_conv(u2, seq, col_offs, width, states, ws, bs, taps, swiglu, out_dtype, name):
    ntok = u2.shape[0]
    nseq = ntok // seq
    tc = 512
    nstreams = len(col_offs)
    in_specs, args = [], []
    for s in range(nstreams):
        cb = col_offs[s] // tc
        in_specs += [
            pl.BlockSpec((ntok, tc), lambda c, cb=cb: (0, cb + c)),
            pl.BlockSpec((nseq, taps - 1, tc), lambda c: (0, 0, c)),
            pl.BlockSpec((taps, tc), lambda c: (0, c)),
            pl.BlockSpec((1, tc), lambda c: (0, c)),
        ]
        args += [u2, states[s], ws[s], bs[s].reshape(1, width)]
    out_specs = [pl.BlockSpec((ntok, tc), lambda c: (0, c))]
    out_shape = [jax.ShapeDtypeStruct((ntok, width), out_dtype)]
    for s in range(nstreams):
        out_specs.append(pl.BlockSpec((nseq, taps - 1, tc), lambda c: (0, 0, c)))
        out_shape.append(jax.ShapeDtypeStruct((nseq, taps - 1, width), F32))
    return pl.pallas_call(
        functools.partial(_short_conv_kernel, taps=taps, seq=seq, nstreams=nstreams, swiglu=swiglu),
        grid=(width // tc,),
        in_specs=in_specs,
        out_specs=out_specs,
        out_shape=out_shape,
        compiler_params=_cp("parallel"),
        name=name,
    )(*args)


def _shift_rows(u, d, prev):
    x = pltpu.roll(u, d, 0)
    r = lax.broadcasted_iota(jnp.int32, prev.shape, 0)
    head = jnp.where(r < d, pltpu.roll(prev, d, 0), x[:CONV_PAD])
    return jnp.concatenate([head, x[CONV_PAD:]], axis=0)


def _causal_taps(u, prev, cw, bias):
    taps = cw.shape[0]
    acc = bias
    for k in range(taps):
        d = taps - 1 - k
        acc = acc + (_shift_rows(u, d, prev) if d else u) * cw[k : k + 1, :]
    return acc


class RiderKernel(NamedTuple):
    body: object
    nsteps: int
    in_specs: list
    out_specs: list
    out_shape: list
    args: tuple


FFN_EPILOGUE_STRIP = 128


def _ffn_up_kernel(*refs, tm, rider_body, n_rider_in, n_rider_out, rider_steps):
    n_in = 9
    hn_ref, wa_ref, wg_ref, cwa_ref, cwg_ref, cba_ref, cbg_ref, sta_ref, stg_ref = refs[:n_in]
    rider_in = refs[n_in : n_in + n_rider_in]
    act_ref, nsa_ref, nsg_ref = refs[n_in + n_rider_in : n_in + n_rider_in + 3]
    rider_out = refs[n_in + n_rider_in + 3 : n_in + n_rider_in + 3 + n_rider_out]
    preva, prevg = refs[-2:]
    m = pl.program_id(2)
    nm = pl.num_programs(2)
    lo = CONV_PAD - (FFN_CONV - 1)

    @pl.when(m == 0)
    def _():
        for prev, st_ref in ((preva, sta_ref), (prevg, stg_ref)):
            prev[0:lo, :] = jnp.zeros((lo, prev.shape[1]), F32)
            prev[lo:CONV_PAD, :] = st_ref[0]

    hn = hn_ref[0]
    ua = _dot(hn, wa_ref[...])
    ug = _dot(hn, wg_ref[...])
    for c0 in range(0, ua.shape[1], FFN_EPILOGUE_STRIP):
        cs = slice(c0, c0 + FFN_EPILOGUE_STRIP)
        a = _causal_taps(ua[:, cs], preva[:, cs], cwa_ref[:, cs], cba_ref[:, cs])
        g = _causal_taps(ug[:, cs], prevg[:, cs], cwg_ref[:, cs], cbg_ref[:, cs])
        act_ref[0, :, cs] = (_silu(g) * a).astype(act_ref.dtype)
    preva[...] = ua[tm - CONV_PAD : tm]
    prevg[...] = ug[tm - CONV_PAD : tm]

    @pl.when(m == nm - 1)
    def _():
        nsa_ref[0] = preva[lo:CONV_PAD, :]
        nsg_ref[0] = prevg[lo:CONV_PAD, :]

    if rider_body is not None:
        step = (pl.program_id(0) * pl.num_programs(1) + pl.program_id(1)) * nm + m

        @pl.when(step < rider_steps)
        def _():
            rider_body(*rider_in, *rider_out)


def ffn_up_conv_act(hn3, w_up, conv_w, conv_b, conv_state, rider=None):
    nb, seq, d = hn3.shape
    ffn = w_up.shape[1] // 2
    tm, tn = min(seq, 1024), 512
    nn, nm = ffn // tn, seq // tm
    half = lambda off: (lambda b, n, m: (0, off + n))
    st_spec = lambda off: pl.BlockSpec((1, FFN_CONV - 1, tn), lambda b, n, m: (b, 0, off + n))
    cb2 = conv_b.reshape(1, 2 * ffn)
    in_specs = [
        pl.BlockSpec((1, tm, d), lambda b, n, m: (b, m, 0)),
        pl.BlockSpec((d, tn), half(0)), pl.BlockSpec((d, tn), half(nn)),
        pl.BlockSpec((FFN_CONV, tn), half(0)), pl.BlockSpec((FFN_CONV, tn), half(nn)),
        pl.BlockSpec((1, tn), half(0)), pl.BlockSpec((1, tn), half(nn)),
        st_spec(0), st_spec(nn),
    ]
    out_specs = [
        pl.BlockSpec((1, tm, tn), lambda b, n, m: (b, m, n)),
        pl.BlockSpec((1, FFN_CONV - 1, tn), lambda b, n, m: (b, 0, n)),
        pl.BlockSpec((1, FFN_CONV - 1, tn), lambda b, n, m: (b, 0, n)),
    ]
    out_shape = [
        jax.ShapeDtypeStruct((nb, seq, ffn), BF16),
        jax.ShapeDtypeStruct((nb, FFN_CONV - 1, ffn), F32),
        jax.ShapeDtypeStruct((nb, FFN_CONV - 1, ffn), F32),
    ]
    args = [hn3, w_up, w_up, conv_w, conv_w, cb2, cb2, conv_state, conv_state]
    rider_kw = dict(rider_body=None, n_rider_in=0, n_rider_out=0, rider_steps=0)
    if rider is not None and rider.nsteps > nb * nn * nm:
        rider_outs, _ = call_with_hosted_matmul(rider.body, rider.nsteps, rider.in_specs, rider.out_specs,
                                                rider.out_shape, rider.args, "rider")
        act, conv_new, _ = ffn_up_conv_act(hn3, w_up, conv_w, conv_b, conv_state)
        return act, conv_new, rider_outs
    if rider is not None:
        on_grid = lambda spec: dataclasses.replace(
            spec, index_map=lambda b, n, m, im=spec.index_map: im((b * nn + n) * nm + m))
        in_specs += [on_grid(s) for s in rider.in_specs]
        out_specs += [on_grid(s) for s in rider.out_specs]
        out_shape += list(rider.out_shape)
        args += list(rider.args)
        rider_kw = dict(rider_body=rider.body, n_rider_in=len(rider.in_specs), n_rider_out=len(rider.out_specs),
                        rider_steps=rider.nsteps)
    outs = pl.pallas_call(
        functools.partial(_ffn_up_kernel, tm=tm, **rider_kw),
        grid=(nb, nn, nm),
        in_specs=in_specs,
        out_specs=out_specs,
        out_shape=out_shape,
        scratch_shapes=[pltpu.VMEM((CONV_PAD, tn), F32), pltpu.VMEM((CONV_PAD, tn), F32)],
        compiler_params=_cp("parallel" if rider is None else "arbitrary", "parallel" if rider is None else "arbitrary",
                            "arbitrary"),
        name="ffn_up_conv_act",
    )(*args)
    act, nsa, nsg = outs[:3]
    return act, jnp.concatenate([nsa, nsg], axis=-1), list(outs[3:])


def _ssd_gate_norm(y, z, nrm):
    ug = y * _silu(z)
    outs = []
    for g in range(N_GROUPS):
        ugg = ug[:, g * GROUP_W : (g + 1) * GROUP_W]
        ms = jnp.mean(ugg * ugg, axis=-1, keepdims=True)
        outs.append(ugg * lax.rsqrt(ms + EPS))
    return jnp.concatenate(outs, axis=1) * nrm


def _gla_out_norm(o, r, nrm):
    outs = []
    for h in range(GLA_HEADS):
        oh = o[:, h * GLA_HEAD_V : (h + 1) * GLA_HEAD_V]
        rh = r[:, h * GLA_HEAD_V : (h + 1) * GLA_HEAD_V]
        outs.append(_rms(oh, nrm) * _silu(rh))
    return jnp.concatenate(outs, axis=1)


SCAN_ROWS_PER_STEP = 256


def _ssd_chunk(xs_ref, b_ref, c_ref, dt_ref, z_ref, cw_ref, cbias_ref, dtb_ref, alog_ref, dexp_ref, nrm_ref, e_ref,
               u_ref, st_ref, cprev):
    q = SSD_CHUNK
    conv = []
    for raw_ref, c0 in ((xs_ref, 0), (b_ref, D_MODEL), (c_ref, D_MODEL + BC_W)):
        raw = raw_ref[...]
        cols = slice(c0, c0 + raw.shape[1])
        conv.append(_silu(_causal_taps(raw, cprev[:, cols], cw_ref[:, cols], cbias_ref[:, cols])))
        cprev[:, cols] = raw[q - CONV_PAD : q]
    xs, bm, cm = conv
    dt = _softplus(dt_ref[...] + dtb_ref[...])
    a = dt * (-jnp.exp(alog_ref[...]))
    row = lax.broadcasted_iota(jnp.int32, (q, q), 0)
    col = lax.broadcasted_iota(jnp.int32, (q, q), 1)
    tril = row >= col
    acum = _dot_sel(a, tril.astype(BF16), sel_first=True)
    acum_t = acum.T
    dt_t = dt.T
    last = acum[q - 1 : q, :]
    e_mat = e_ref[...]
    eexp = _dot_sel(jnp.exp(acum), e_mat)
    wexp = _dot_sel(jnp.exp(last - acum) * dt, e_mat)
    s_bf = st_ref[...].astype(BF16)
    cb16 = cm.astype(BF16)
    bb16 = bm.astype(BF16)
    x16 = xs.astype(BF16)
    xw16 = (xs * wexp).astype(BF16)
    lane_lo = lax.broadcasted_iota(jnp.int32, (q, LANE), 1) < SSD_HEAD_DIM
    ys = []
    for g in range(N_GROUPS):
        cg = cb16[:, g * D_STATE : (g + 1) * D_STATE]
        bg = bb16[:, g * D_STATE : (g + 1) * D_STATE]
        cb = _dot_nt(cg, bg)
        yoff = _dot(cg, s_bf[:, g * GROUP_W : (g + 1) * GROUP_W])
        pieces = []
        for pr in range(GROUP_W // LANE):
            h0 = g * (SSD_HEADS // N_GROUPS) + 2 * pr
            xp = x16[:, h0 * SSD_HEAD_DIM : h0 * SSD_HEAD_DIM + LANE]
            yh = []
            for h in (h0, h0 + 1):
                diff = acum[:, h : h + 1] - acum_t[h : h + 1, :]
                dec = jnp.exp(jnp.where(tril, diff, NEG_BIG))
                m = (cb * dec * dt_t[h : h + 1, :]).astype(BF16)
                yh.append(_dot(m, xp))
            pieces.append(jnp.where(lane_lo, yh[0], yh[1]))
        sl = slice(g * GROUP_W, (g + 1) * GROUP_W)
        ys.append(jnp.concatenate(pieces, axis=1) + yoff * eexp[:, sl])
        bg_t = bm[:, g * D_STATE : (g + 1) * D_STATE].T.astype(BF16)
        upd = _dot(bg_t, xw16[:, sl])
        st_ref[:, sl] = eexp[q - 1 : q, sl] * st_ref[:, sl] + upd
    y = jnp.concatenate(ys, axis=1) + dexp_ref[...] * xs
    u_ref[...] = _ssd_gate_norm(y, z_ref[...], nrm_ref[...]).astype(u_ref.dtype)


def _ssd_scan_kernel(xs_ref, b_ref, c_ref, dt_ref, z_ref, cst_ref, cw_ref, cbias_ref,
                     dtb_ref, alog_ref, dexp_ref, nrm_ref, e_ref,
                     u_ref, sout_ref, cso_ref, st_ref, cprev, *, nsteps, nsub):
    step = pl.program_id(1)
    lo = CONV_PAD - (SSD_CONV - 1)

    @pl.when(step == 0)
    def _():
        st_ref[...] = jnp.zeros_like(st_ref)
        cprev[0:lo, :] = jnp.zeros((lo, CONV_DIM), F32)
        cprev[lo:CONV_PAD, :] = cst_ref[0]

    for sub in range(nsub):
        rows = lambda ref: ref.at[0, pl.ds(sub * SSD_CHUNK, SSD_CHUNK)]
        _ssd_chunk(rows(xs_ref), rows(b_ref), rows(c_ref), rows(dt_ref), rows(z_ref), cw_ref, cbias_ref,
                   dtb_ref, alog_ref, dexp_ref, nrm_ref, e_ref, rows(u_ref), st_ref, cprev)

    @pl.when(step == nsteps - 1)
    def _():
        sout_ref[0] = st_ref[...].T
        cso_ref[0] = cprev[lo:CONV_PAD, :]


def ssd_scan(proj3, conv_state, conv_w, conv_b, p):
    nb, seq, _ = proj3.shape
    nsub = SCAN_ROWS_PER_STEP // SSD_CHUNK if seq % SCAN_ROWS_PER_STEP == 0 else 1
    q = nsub * SSD_CHUNK
    nsteps = seq // q
    vec = lambda n: pl.BlockSpec((1, n), lambda b, c: (0, 0))
    u, s_out, conv_new = pl.pallas_call(
        functools.partial(_ssd_scan_kernel, nsteps=nsteps, nsub=nsub),
        grid=(nb, nsteps),
        in_specs=[
            pl.BlockSpec((1, q, D_MODEL), lambda b, c: (b, c, COL_XBC // D_MODEL)),
            pl.BlockSpec((1, q, BC_W), lambda b, c: (b, c, (COL_XBC + D_MODEL) // BC_W)),
            pl.BlockSpec((1, q, BC_W), lambda b, c: (b, c, (COL_XBC + D_MODEL) // BC_W + 1)),
            pl.BlockSpec((1, q, LANE), lambda b, c: (b, c, COL_DT // LANE)),
            pl.BlockSpec((1, q, D_MODEL), lambda b, c: (b, c, COL_Z // D_MODEL)),
            pl.BlockSpec((1, SSD_CONV - 1, CONV_DIM), lambda b, c: (b, 0, 0)),
            pl.BlockSpec((SSD_CONV, CONV_DIM), lambda b, c: (0, 0)),
            vec(CONV_DIM),
            vec(LANE), vec(LANE), vec(D_MODEL), vec(D_MODEL),
            pl.BlockSpec((LANE, D_MODEL), lambda b, c: (0, 0)),
        ],
        out_specs=[
            pl.BlockSpec((1, q, D_MODEL), lambda b, c: (b, c, 0)),
            pl.BlockSpec((1, D_MODEL, D_STATE), lambda b, c: (b, 0, 0)),
            pl.BlockSpec((1, SSD_CONV - 1, CONV_DIM), lambda b, c: (b, 0, 0)),
        ],
        out_shape=[
            jax.ShapeDtypeStruct((nb, seq, D_MODEL), BF16),
            jax.ShapeDtypeStruct((nb, D_MODEL, D_STATE), F32),
            jax.ShapeDtypeStruct((nb, SSD_CONV - 1, CONV_DIM), F32),
        ],
        scratch_shapes=[pltpu.VMEM((D_STATE, D_MODEL), F32), pltpu.VMEM((CONV_PAD, CONV_DIM), F32)],
        compiler_params=_cp("parallel", "arbitrary"),
        name="ssd_scan",
    )(proj3, proj3, proj3, proj3, proj3, conv_state, conv_w, conv_b.reshape(1, CONV_DIM),
      p["dt_bias"], p["a_log"], p["d_exp"], p["ssd_norm"], p["e_head"])
    return u.reshape(nb * seq, D_MODEL), s_out.reshape(nb, SSD_HEADS, SSD_HEAD_DIM, D_STATE), conv_new


def _gla_gate_log(glr, wg, bg):
    x = _dot(glr.astype(BF16), wg) + bg
    return -_softplus(-x) / GATE_TAU


def _gla_chunk(q_ref, k_ref, v_ref, r_ref, glr_ref, wg_ref, bg_ref, nrm_ref, o_ref, st_ref):
    q = GLA_CHUNK
    glog = _gla_gate_log(glr_ref[...], wg_ref[...], bg_ref[...])
    row = lax.broadcasted_iota(jnp.int32, (q, q), 0)
    col = lax.broadcasted_iota(jnp.int32, (q, q), 1)
    tril = row >= col
    bc = _dot_sel(glog, tril.astype(BF16), sel_first=True)
    last = bc[q - 1 : q, :]
    kk = k_ref[...]
    qe = q_ref[...] * (GLA_HEAD_K ** -0.5) * jnp.exp(bc)
    ke = kk * jnp.exp(-bc)
    kd = kk * jnp.exp(last - bc)
    elast = jnp.exp(last)
    v16 = v_ref[...].astype(BF16)
    zeros_v = jnp.zeros((q, GLA_HEAD_V), BF16)
    outs = []
    for h in range(GLA_HEADS):
        ks = slice(h * GLA_HEAD_K, (h + 1) * GLA_HEAD_K)
        vs = slice(h * GLA_HEAD_V, (h + 1) * GLA_HEAD_V)
        qh = qe[:, ks].astype(BF16)
        kh = ke[:, ks].astype(BF16)
        att = jnp.where(tril, _dot_nt(qh, kh), 0.0)
        s_h = st_ref[ks, :]
        outs.append(_dot(att.astype(BF16), v16[:, vs]) + _dot(qh, s_h.astype(BF16)))
        xt = jnp.concatenate([kd[:, ks], jnp.broadcast_to(elast[:, ks], (q, GLA_HEAD_K))], axis=0).T
        v2 = jnp.concatenate([v16[:, vs], zeros_v], axis=0)
        st_ref[ks, :] = xt[:, q : q + 1] * s_h + _dot(xt.astype(BF16), v2)
    o = jnp.concatenate(outs, axis=1)
    o_ref[...] = _gla_out_norm(o, r_ref[...], nrm_ref[...]).astype(o_ref.dtype)


def _gla_scan_kernel(q_ref, k_ref, v_ref, r_ref, glr_ref, wg_ref, bg_ref, nrm_ref,
                     o_ref, sout_ref, st_ref, *, nsteps, nsub):
    step = pl.program_id(1)

    @pl.when(step == 0)
    def _():
        st_ref[...] = jnp.zeros_like(st_ref)

    for sub in range(nsub):
        rows = lambda ref: ref.at[0, pl.ds(sub * GLA_CHUNK, GLA_CHUNK)]
        _gla_chunk(rows(q_ref), rows(k_ref), rows(v_ref), rows(r_ref), rows(glr_ref), wg_ref, bg_ref, nrm_ref,
                   rows(o_ref), st_ref)

    @pl.when(step == nsteps - 1)
    def _():
        sout_ref[0] = st_ref[...]


def gla_scan(proj3, p):
    nb, seq, _ = proj3.shape
    nsub = SCAN_ROWS_PER_STEP // GLA_CHUNK if seq % SCAN_ROWS_PER_STEP == 0 else 1
    q = nsub * GLA_CHUNK
    nsteps = seq // q
    o, s_out = pl.pallas_call(
        functools.partial(_gla_scan_kernel, nsteps=nsteps, nsub=nsub),
        grid=(nb, nsteps),
        in_specs=[
            pl.BlockSpec((1, q, GLA_KEY_DIM), lambda b, c: (b, c, COL_Q // GLA_KEY_DIM)),
            pl.BlockSpec((1, q, GLA_KEY_DIM), lambda b, c: (b, c, COL_K // GLA_KEY_DIM)),
            pl.BlockSpec((1, q, D_MODEL), lambda b, c: (b, c, COL_V // D_MODEL)),
            pl.BlockSpec((1, q, D_MODEL), lambda b, c: (b, c, COL_R // D_MODEL)),
            pl.BlockSpec((1, q, LANE), lambda b, c: (b, c, COL_GLR // LANE)),
            pl.BlockSpec((LANE, GLA_KEY_DIM), lambda b, c: (0, 0)),
            pl.BlockSpec((1, GLA_KEY_DIM), lambda b, c: (0, 0)),
            pl.BlockSpec((1, GLA_HEAD_V), lambda b, c: (0, 0)),
        ],
        out_specs=[
            pl.BlockSpec((1, q, D_MODEL), lambda b, c: (b, c, 0)),
            pl.BlockSpec((1, GLA_KEY_DIM, GLA_HEAD_V), lambda b, c: (b, 0, 0)),
        ],
        out_shape=[
            jax.ShapeDtypeStruct((nb, seq, D_MODEL), BF16),
            jax.ShapeDtypeStruct((nb, GLA_KEY_DIM, GLA_HEAD_V), F32),
        ],
        scratch_shapes=[pltpu.VMEM((GLA_KEY_DIM, GLA_HEAD_V), F32)],
        compiler_params=_cp("parallel", "arbitrary"),
        name="gla_scan",
    )(proj3, proj3, proj3, proj3, proj3, p["w_gate"], p["b_gate"], p["gla_norm"])
    return o.reshape(nb * seq, D_MODEL), s_out.reshape(nb, GLA_HEADS, GLA_HEAD_K, GLA_HEAD_V)


TOK_BLOCK = 128


def _row_shift(x, d, tpos):
    return jnp.where(tpos >= d, pltpu.roll(x, d, 0), 0.0)


def _seq_cumsum_and_last(a, seq, tpos):
    nrows = a.shape[0]
    acum = a
    for d in range(1, seq):
        acum = acum + _row_shift(a, d, tpos)
    last = jnp.where(tpos == seq - 1, acum, 0.0)
    for d in range(1, seq):
        last = last + jnp.where(tpos == seq - 1 - d, pltpu.roll(acum, nrows - d, 0), 0.0)
    return acum, last


def _ssd_step_pre_kernel(xs_ref, b_ref, c_ref, dt_ref, dtb_ref, alog_ref, dexp_ref, e_ref, gh_ref,
                         ypart_ref, eexp_ref, xwt_ref, el_ref, *, seq):
    nrows = xs_ref.shape[0]
    xs = xs_ref[...]
    bm = b_ref[...]
    cm = c_ref[...]
    dt = _softplus(dt_ref[...] + dtb_ref[...])
    a = dt * (-jnp.exp(alog_ref[...]))
    pos = lambda w: lax.broadcasted_iota(jnp.int32, (nrows, w), 0) % seq
    t_h, t_c, t_x = pos(LANE), pos(BC_W), pos(D_MODEL)
    acum, last = _seq_cumsum_and_last(a, seq, t_h)
    e_mat = e_ref[...]
    eexp_ref[...] = _dot_sel(jnp.exp(acum), e_mat)
    wexp = _dot_sel(jnp.exp(last - acum) * dt, e_mat)
    xwt_ref[...] = (xs * wexp).T.astype(xwt_ref.dtype)
    el_ref[...] = jnp.exp(last)
    y = dexp_ref[...] * xs
    for d in range(seq):
        if d == 0:
            cbh = _dot_sel(cm * bm, gh_ref[...])
            coef = dt
            xd = xs
        else:
            cbh = _dot_sel(cm * _row_shift(bm, d, t_c), gh_ref[...])
            coef = jnp.where(t_h >= d, jnp.exp(acum - pltpu.roll(acum, d, 0)) * pltpu.roll(dt, d, 0), 0.0)
            xd = _row_shift(xs, d, t_x)
        y = y + _dot_sel(cbh * coef, e_mat) * xd
    ypart_ref[...] = y


def _ssd_step_state_kernel(st_ref, c_ref, b_ref, xwt_ref, el_ref, ypart_ref, eexp_ref, z_ref, nrm_ref,
                           u_ref, so_ref, *, sb, seq):
    i = pl.program_id(0)
    rows = sb * seq
    steps_per_block = TOK_BLOCK // rows
    base = (i % steps_per_block) * rows
    c16 = c_ref[...].astype(BF16)
    btok = b_ref[...]
    tok = lax.broadcasted_iota(jnp.int32, (TOK_BLOCK, LANE), 0)
    rsel = lax.broadcasted_iota(jnp.int32, (rows, GROUP_W), 0)
    heads_per_group = SSD_HEADS // N_GROUPS
    yoff = [jnp.zeros((rows, GROUP_W), F32) for _ in range(N_GROUPS)]
    for s in range(sb):
        lo = base + seq * s
        own = (tok >= lo) & (tok < lo + seq)
        mine = (rsel >= seq * s) & (rsel < seq * (s + 1))
        for g in range(N_GROUPS):
            sl = slice(g * GROUP_W, (g + 1) * GROUP_W)
            yo = _dot_nt(c16[:, g * D_STATE : (g + 1) * D_STATE], st_ref[s, sl, :].astype(BF16))
            yoff[g] = jnp.where(mine, yo, yoff[g])
            bsel = jnp.where(own, btok[:, g * D_STATE : (g + 1) * D_STATE], 0.0).astype(BF16)
            upd = _dot(xwt_ref[sl, :], bsel)
            for r in range(heads_per_group):
                h = g * heads_per_group + r
                hs = slice(h * SSD_HEAD_DIM, (h + 1) * SSD_HEAD_DIM)
                so_ref[s, hs, :] = el_ref[seq * s, h] * st_ref[s, hs, :] + upd[r * SSD_HEAD_DIM : (r + 1) * SSD_HEAD_DIM]
    y = ypart_ref[...] + jnp.concatenate(yoff, axis=1) * eexp_ref[...]
    u_ref[...] = _ssd_gate_norm(y, z_ref[...], nrm_ref[...]).astype(u_ref.dtype)


def ssd_step(xc2, proj2, state, p, seq, hosted=None):
    ntok = xc2.shape[0]
    nseq = ntok // seq
    full = lambda shape: pl.BlockSpec(shape, lambda i: (0,) * len(shape))
    ypart, eexp, xwt, elast = pl.pallas_call(
        functools.partial(_ssd_step_pre_kernel, seq=seq),
        grid=(1,),
        in_specs=[
            pl.BlockSpec((ntok, D_MODEL), lambda i: (0, 0)),
            pl.BlockSpec((ntok, BC_W), lambda i: (0, D_MODEL // BC_W)),
            pl.BlockSpec((ntok, BC_W), lambda i: (0, D_MODEL // BC_W + 1)),
            pl.BlockSpec((ntok, LANE), lambda i: (0, COL_DT // LANE)),
            full((1, LANE)), full((1, LANE)), full((1, D_MODEL)),
            full((LANE, D_MODEL)), full((BC_W, LANE)),
        ],
        out_specs=[full((ntok, D_MODEL)), full((ntok, D_MODEL)), full((D_MODEL, ntok)), full((ntok, LANE))],
        out_shape=[
            jax.ShapeDtypeStruct((ntok, D_MODEL), F32),
            jax.ShapeDtypeStruct((ntok, D_MODEL), F32),
            jax.ShapeDtypeStruct((D_MODEL, ntok), BF16),
            jax.ShapeDtypeStruct((ntok, LANE), F32),
        ],
        compiler_params=_cp("arbitrary"),
        name="ssd_step_pre",
    )(xc2, xc2, xc2, proj2, p["dt_bias"], p["a_log"], p["d_exp"], p["e_head"], p["g_head"])

    sb = 4
    rows = sb * seq
    spb = TOK_BLOCK // rows
    st3 = state.reshape(nseq, D_MODEL, D_STATE)
    nsteps = nseq // sb
    c = lambda i: jnp.minimum(i, nsteps - 1)
    (u, s_new), hosted_out = call_with_hosted_matmul(
        functools.partial(_ssd_step_state_kernel, sb=sb, seq=seq),
        nsteps,
        in_specs=[
            pl.BlockSpec((sb, D_MODEL, D_STATE), lambda i: (c(i), 0, 0)),
            pl.BlockSpec((rows, BC_W), lambda i: (c(i), D_MODEL // BC_W + 1)),
            pl.BlockSpec((TOK_BLOCK, BC_W), lambda i: (c(i) // spb, D_MODEL // BC_W)),
            pl.BlockSpec((D_MODEL, TOK_BLOCK), lambda i: (0, c(i) // spb)),
            pl.BlockSpec((rows, LANE), lambda i: (c(i), 0), memory_space=pltpu.SMEM),
            pl.BlockSpec((rows, D_MODEL), lambda i: (c(i), 0)),
            pl.BlockSpec((rows, D_MODEL), lambda i: (c(i), 0)),
            pl.BlockSpec((rows, D_MODEL), lambda i: (c(i), COL_Z // D_MODEL)),
            pl.BlockSpec((1, D_MODEL), lambda i: (0, 0)),
        ],
        out_specs=[
            pl.BlockSpec((rows, D_MODEL), lambda i: (c(i), 0)),
            pl.BlockSpec((sb, D_MODEL, D_STATE), lambda i: (c(i), 0, 0)),
        ],
        out_shape=[
            jax.ShapeDtypeStruct((ntok, D_MODEL), BF16),
            jax.ShapeDtypeStruct((nseq, D_MODEL, D_STATE), F32),
        ],
        args=(st3, xc2, xc2, xwt, elast, ypart, eexp, proj2, p["ssd_norm"]),
        name="ssd_step_state",
        hosted=hosted,
    )
    return u, s_new.reshape(nseq, SSD_HEADS, SSD_HEAD_DIM, D_STATE), hosted_out


def _gla_step_pre_kernel(q_ref, k_ref, v_ref, glr_ref, wg_ref, bg_ref, gv_ref,
                         oin_ref, qe_ref, kdt_ref, elt_ref, *, seq):
    nrows = q_ref.shape[0]
    glog = _gla_gate_log(glr_ref[...], wg_ref[...], bg_ref[...])
    pos = lambda w: lax.broadcasted_iota(jnp.int32, (nrows, w), 0) % seq
    t_k, t_v = pos(GLA_KEY_DIM), pos(D_MODEL)
    bc, last = _seq_cumsum_and_last(glog, seq, t_k)
    kk = k_ref[...]
    qe = q_ref[...] * (GLA_HEAD_K ** -0.5) * jnp.exp(bc)
    ke = kk * jnp.exp(-bc)
    qe_ref[...] = qe
    kdt_ref[...] = (kk * jnp.exp(last - bc)).T
    elt_ref[...] = jnp.exp(last).T
    v = v_ref[...]
    gv = gv_ref[...]
    o = jnp.zeros((nrows, D_MODEL), F32)
    for d in range(seq):
        ked = ke if d == 0 else _row_shift(ke, d, t_k)
        vd = v if d == 0 else _row_shift(v, d, t_v)
        att = _dot((qe * ked).astype(BF16), gv)
        o = o + att * vd
    oin_ref[...] = o


def _gla_step_state_kernel(st_ref, qe_ref, v_ref, kdt_ref, elt_ref, oin_ref, r_ref, nrm_ref,
                           o_ref, so_ref, *, sb, seq):
    i = pl.program_id(0)
    rows = sb * seq
    spb = TOK_BLOCK // rows
    base = (i % spb) * rows
    qe16 = qe_ref[...].astype(BF16)
    vtok = v_ref[...]
    tokv = lax.broadcasted_iota(jnp.int32, (TOK_BLOCK, GLA_HEAD_V), 0)
    tok = lax.broadcasted_iota(jnp.int32, (TOK_BLOCK, LANE), 0)
    rsel = lax.broadcasted_iota(jnp.int32, (rows, GLA_HEAD_V), 0)
    ooff = [jnp.zeros((rows, GLA_HEAD_V), F32) for _ in range(GLA_HEADS)]
    for s in range(sb):
        lo = base + seq * s
        own = (tokv >= lo) & (tokv < lo + seq)
        first = jnp.where(tok == lo, 1.0, 0.0).astype(BF16)
        mine = (rsel >= seq * s) & (rsel < seq * (s + 1))
        for h in range(GLA_HEADS):
            ks = slice(h * GLA_HEAD_K, (h + 1) * GLA_HEAD_K)
            vs = slice(h * GLA_HEAD_V, (h + 1) * GLA_HEAD_V)
            s_h = st_ref[s, ks, :]
            oo = _dot(qe16[:, ks], s_h.astype(BF16))
            ooff[h] = jnp.where(mine, oo, ooff[h])
            vsel = jnp.where(own, vtok[:, vs], 0.0).astype(BF16)
            upd = _dot(kdt_ref[ks, :].astype(BF16), vsel)
            ecol = _dot_sel(elt_ref[ks, :], first)
            so_ref[s, ks, :] = jnp.concatenate([ecol] * (GLA_HEAD_V // LANE), axis=1) * s_h + upd
    o = oin_ref[...] + jnp.concatenate(ooff, axis=1)
    o_ref[0] = _gla_out_norm(o, r_ref[...], nrm_ref[...]).astype(o_ref.dtype)


def gla_step(proj2, state, p, seq, hosted=None):
    ntok = proj2.shape[0]
    nseq = ntok // seq
    full = lambda shape: pl.BlockSpec(shape, lambda i: (0,) * len(shape))
    oin, qe, kdt, elt = pl.pallas_call(
        functools.partial(_gla_step_pre_kernel, seq=seq),
        grid=(1,),
        in_specs=[
            pl.BlockSpec((ntok, GLA_KEY_DIM), lambda i: (0, COL_Q // GLA_KEY_DIM)),
            pl.BlockSpec((ntok, GLA_KEY_DIM), lambda i: (0, COL_K // GLA_KEY_DIM)),
            pl.BlockSpec((ntok, D_MODEL), lambda i: (0, COL_V // D_MODEL)),
            pl.BlockSpec((ntok, LANE), lambda i: (0, COL_GLR // LANE)),
            full((LANE, GLA_KEY_DIM)), full((1, GLA_KEY_DIM)), full((GLA_KEY_DIM, D_MODEL)),
        ],
        out_specs=[full((ntok, D_MODEL)), full((ntok, GLA_KEY_DIM)), full((GLA_KEY_DIM, ntok)),
                   full((GLA_KEY_DIM, ntok))],
        out_shape=[
            jax.ShapeDtypeStruct((ntok, D_MODEL), F32),
            jax.ShapeDtypeStruct((ntok, GLA_KEY_DIM), F32),
            jax.ShapeDtypeStruct((GLA_KEY_DIM, ntok), F32),
            jax.ShapeDtypeStruct((GLA_KEY_DIM, ntok), F32),
        ],
        compiler_params=_cp("arbitrary"),
        name="gla_step_pre",
    )(proj2, proj2, proj2, proj2, p["w_gate"], p["b_gate"], p["g_val"])

    sb = 2 if hosted is not None else 4
    rows = sb * seq
    spb = TOK_BLOCK // rows
    st3 = state.reshape(nseq, GLA_KEY_DIM, GLA_HEAD_V)
    nsteps = nseq // sb
    c = lambda i: jnp.minimum(i, nsteps - 1)
    (o, s_new), hosted_out = call_with_hosted_matmul(
        functools.partial(_gla_step_state_kernel, sb=sb, seq=seq),
        nsteps,
        in_specs=[
            pl.BlockSpec((sb, GLA_KEY_DIM, GLA_HEAD_V), lambda i: (c(i), 0, 0)),
            pl.BlockSpec((rows, GLA_KEY_DIM), lambda i: (c(i), 0)),
            pl.BlockSpec((TOK_BLOCK, D_MODEL), lambda i: (c(i) // spb, COL_V // D_MODEL)),
            pl.BlockSpec((GLA_KEY_DIM, TOK_BLOCK), lambda i: (0, c(i) // spb)),
            pl.BlockSpec((GLA_KEY_DIM, TOK_BLOCK), lambda i: (0, c(i) // spb)),
            pl.BlockSpec((rows, D_MODEL), lambda i: (c(i), 0)),
            pl.BlockSpec((rows, D_MODEL), lambda i: (c(i), COL_R // D_MODEL)),
            pl.BlockSpec((1, GLA_HEAD_V), lambda i: (0, 0)),
        ],
        out_specs=[
            pl.BlockSpec((1, rows, D_MODEL), lambda i: (c(i), 0, 0)),
            pl.BlockSpec((sb, GLA_KEY_DIM, GLA_HEAD_V), lambda i: (c(i), 0, 0)),
        ],
        out_shape=[
            jax.ShapeDtypeStruct((nseq // sb, rows, D_MODEL), BF16),
            jax.ShapeDtypeStruct((nseq, GLA_KEY_DIM, GLA_HEAD_V), F32),
        ],
        args=(st3, qe, proj2, kdt, elt, oin, proj2, p["gla_norm"]),
        name="gla_step_state",
        hosted=hosted,
    )
    return o.reshape(ntok, D_MODEL), s_new.reshape(nseq, GLA_HEADS, GLA_HEAD_K, GLA_HEAD_V), hosted_out


def _softmax_rows(sc):
    e = jnp.exp(sc - jnp.max(sc, axis=-1, keepdims=True))
    return e / jnp.sum(e, axis=-1, keepdims=True)


def _cross_block_kernel(hn_ref, h_ref, k_ref, v_ref, wq_ref, wo_ref, g_ref, h2_ref, hn2_ref):
    q16 = _dot(hn_ref[...], wq_ref[...]).astype(BF16)
    outs = []
    for h in range(CROSS_HEADS):
        hs = slice(h * CROSS_HEAD_DIM, (h + 1) * CROSS_HEAD_DIM)
        sc = _dot_nt(q16[:, hs], k_ref[0, :, hs].astype(BF16)) * (CROSS_HEAD_DIM ** -0.5)
        outs.append(_dot(_softmax_rows(sc).astype(BF16), v_ref[0, :, hs].astype(BF16)))
    att = jnp.concatenate(outs, axis=1).astype(BF16)
    h2 = h_ref[...] + _dot(att, wo_ref[...])
    h2_ref[...] = h2
    hn2_ref[...] = _rms(h2, g_ref[...]).astype(hn2_ref.dtype)


def cross_block(hn, h, mem_k, mem_v, w_cq, w_co, g, seq):
    ntok, d = hn.shape
    n_mem = mem_k.shape[1]
    tl = min(seq, 512)
    lt = seq // tl
    row_tile = pl.BlockSpec((tl, d), lambda i: (i, 0))
    resident = lambda shape: pl.BlockSpec(shape, lambda i: (0,) * len(shape), pipeline_mode=pl.Buffered(1))
    kv_spec = pl.BlockSpec((1, n_mem, d), lambda i: (i // lt, 0, 0))
    return pl.pallas_call(
        _cross_block_kernel,
        grid=(ntok // tl,),
        in_specs=[row_tile, row_tile, kv_spec, kv_spec, resident((d, d)), resident((d, d)),
                  pl.BlockSpec((1, d), lambda i: (0, 0))],
        out_specs=[row_tile, row_tile],
        out_shape=[jax.ShapeDtypeStruct((ntok, d), F32), jax.ShapeDtypeStruct((ntok, d), BF16)],
        compiler_params=_cp("parallel"),
        name="cross_block",
    )(hn, h, mem_k, mem_v, w_cq, w_co, g.reshape(1, d))


def _xattn_step_kernel(q_ref, k_ref, v_ref, o_ref, *, nseq, tl):
    rows = nseq * tl
    n_mem = k_ref.shape[1]
    q = q_ref[0]
    qs = jnp.concatenate([q[:, h * CROSS_HEAD_DIM : (h + 1) * CROSS_HEAD_DIM] for h in range(CROSS_HEADS)],
                         axis=0).astype(BF16)
    shape = (CROSS_HEADS * rows, n_mem * CROSS_HEADS)
    col_head = lax.broadcasted_iota(jnp.int32, shape, 1) % CROSS_HEADS
    row_head = lax.broadcasted_iota(jnp.int32, shape, 0) // rows
    same_head = col_head == row_head
    rsel = lax.broadcasted_iota(jnp.int32, (CROSS_HEADS * rows, CROSS_HEAD_DIM), 0) % rows
    out = jnp.zeros((CROSS_HEADS * rows, CROSS_HEAD_DIM), F32)
    for s in range(nseq):
        kall = k_ref[s].reshape(n_mem * CROSS_HEADS, CROSS_HEAD_DIM).astype(BF16)
        vall = v_ref[s].reshape(n_mem * CROSS_HEADS, CROSS_HEAD_DIM).astype(BF16)
        sc = jnp.where(same_head, _dot_nt(qs, kall) * (CROSS_HEAD_DIM ** -0.5), NEG_BIG)
        oh = _dot(_softmax_rows(sc).astype(BF16), vall)
        out = jnp.where((rsel >= tl * s) & (rsel < tl * (s + 1)), oh, out)
    o_ref[0] = jnp.concatenate([out[h * rows : (h + 1) * rows] for h in range(CROSS_HEADS)],
                               axis=1).astype(o_ref.dtype)


def cross_attend_rider(q2, mem_k, mem_v, nb, seq):
    n_mem = mem_k.shape[1]
    nseq, tl = 8 // seq, seq
    rows = nseq * tl
    nblk = nb * seq // rows
    c = lambda i: jnp.minimum(i, nblk - 1)
    kv_spec = pl.BlockSpec((nseq, n_mem, CROSS_HEADS, CROSS_HEAD_DIM), lambda i: (c(i), 0, 0, 0))
    return RiderKernel(
        body=functools.partial(_xattn_step_kernel, nseq=nseq, tl=tl),
        nsteps=nblk,
        in_specs=[pl.BlockSpec((1, rows, D_MODEL), lambda i: (c(i), 0, 0)), kv_spec, kv_spec],
        out_specs=[pl.BlockSpec((1, rows, D_MODEL), lambda i: (c(i), 0, 0))],
        out_shape=[jax.ShapeDtypeStruct((nblk, rows, D_MODEL), BF16)],
        args=(q2.reshape(nblk, rows, D_MODEL), mem_k, mem_v),
    )


PACK_TILE = 512


SUBLANE = 8


def _cast_rows_kernel(w_ref, o_ref, *, nrows):
    j = pl.program_id(0)
    x = w_ref[...]
    row = lax.broadcasted_iota(jnp.int32, x.shape, 0) + j * PACK_TILE
    o_ref[...] = jnp.where(row < nrows, x, 0.0).astype(o_ref.dtype)


def cast_rows(wt, lo, hi, n_out, name):
    kdim = wt.shape[1]
    assert lo % SUBLANE == 0 and lo + n_out <= wt.shape[0]
    return pl.pallas_call(
        functools.partial(_cast_rows_kernel, nrows=hi - lo),
        grid=(n_out // PACK_TILE,),
        in_specs=[pl.BlockSpec((pl.Element(PACK_TILE), pl.Element(kdim)),
                               lambda j: (pl.multiple_of(lo + j * PACK_TILE, SUBLANE), 0))],
        out_specs=pl.BlockSpec((PACK_TILE, kdim), lambda j: (j, 0)),
        out_shape=jax.ShapeDtypeStruct((n_out, kdim), BF16),
        compiler_params=_cp("parallel"),
        name=name,
    )(wt)


def _cast_two_kernel(a_ref, b_ref, oa_ref, ob_ref, *, a_steps):
    step = pl.program_id(0)

    @pl.when(step < a_steps)
    def _():
        oa_ref[...] = a_ref[...].astype(oa_ref.dtype)

    @pl.when(step >= a_steps)
    def _():
        ob_ref[...] = b_ref[...].astype(ob_ref.dtype)


def cast_two_hosting_matmul(wa, wb, hosted, name):
    a_steps, b_steps = wa.shape[1] // PACK_TILE, wb.shape[0] // PACK_TILE
    a_tile = pl.BlockSpec((wa.shape[0], PACK_TILE), lambda i: (0, jnp.minimum(i, a_steps - 1)))
    b_tile = pl.BlockSpec((PACK_TILE, wb.shape[1]), lambda i: (jnp.clip(i - a_steps, 0, b_steps - 1), 0))
    (ca, cb), hosted_out = call_with_hosted_matmul(
        functools.partial(_cast_two_kernel, a_steps=a_steps), a_steps + b_steps, [a_tile, b_tile], [a_tile, b_tile],
        [jax.ShapeDtypeStruct(wa.shape, BF16), jax.ShapeDtypeStruct(wb.shape, BF16)], (wa, wb), name, hosted)
    return ca, cb, hosted_out


def _split_w_in(w_in):
    wt = w_in.T
    ssd_end = D_MODEL + CONV_DIM + SSD_HEADS
    gla_end = ssd_end + 2 * GLA_KEY_DIM + 2 * D_MODEL + GATE_RANK
    return (cast_rows(wt, 0, ssd_end, N_SSD_PROJ, "pack_ssd_in"),
            cast_rows(wt, ssd_end, gla_end, N_GLA_PROJ, "pack_gla_in"),
            cast_rows(wt, gla_end, wt.shape[0], 2 * D_MODEL, "pack_gate_in"))


def _params(ssd_dt_bias, ssd_A_log, ssd_D, ssd_norm, w_gla_gate, b_gla_gate, gla_norm):
    padv = lambda a: jnp.pad(a.astype(F32), (0, LANE - a.shape[0])).reshape(1, LANE)
    head_of_chan = jnp.arange(D_MODEL, dtype=jnp.int32) // SSD_HEAD_DIM
    e_head = (jnp.arange(LANE, dtype=jnp.int32)[:, None] == head_of_chan[None, :]).astype(BF16)
    group_of_bc = jnp.arange(BC_W, dtype=jnp.int32) // D_STATE
    lane_h = jnp.arange(LANE, dtype=jnp.int32)
    g_head = ((lane_h[None, :] // (SSD_HEADS // N_GROUPS) == group_of_bc[:, None])
              & (lane_h[None, :] < SSD_HEADS)).astype(BF16)
    khead = jnp.arange(GLA_KEY_DIM, dtype=jnp.int32) // GLA_HEAD_K
    vhead = jnp.arange(D_MODEL, dtype=jnp.int32) // GLA_HEAD_V
    g_val = (khead[:, None] == vhead[None, :]).astype(BF16)
    return dict(
        dt_bias=padv(ssd_dt_bias), a_log=padv(ssd_A_log),
        d_exp=jnp.repeat(ssd_D.astype(F32), SSD_HEAD_DIM).reshape(1, D_MODEL),
        ssd_norm=ssd_norm.astype(F32).reshape(1, D_MODEL),
        e_head=e_head, g_head=g_head, g_val=g_val,
        w_gate=jnp.pad(w_gla_gate, ((0, LANE - GATE_RANK), (0, 0))).astype(BF16),
        b_gate=b_gla_gate.astype(F32).reshape(1, GLA_KEY_DIM),
        gla_norm=gla_norm.astype(F32).reshape(1, GLA_HEAD_V),
    )


def _mixers(x3, xn, ssd_conv, ssd_state, gla_state, w, p, long_seq, proj_ssd=None, proj_gla=None, gates=None,
            other_xn=None):
    nb, seq, d = x3.shape
    ntok = nb * seq
    x2 = x3.reshape(ntok, d)
    w_ssd_in, w_gla_in, w_gate_in = w["w_in"]
    in_proj = functools.partial(matmul, xn, tm=2048, w_rows_are_outputs=True)
    proj_ssd = in_proj(w_ssd_in, name="in_proj_ssd") if proj_ssd is None else proj_ssd
    proj_gla = in_proj(w_gla_in, name="in_proj_gla") if proj_gla is None else proj_gla
    gates = in_proj(w_gate_in, name="in_proj_gates") if gates is None else gates
    hosted = (lambda wt: HostedMatmul(other_xn, wt)) if other_xn is not None else (lambda wt: None)
    other_ssd = other_gla = None
    if long_seq:
        u, ssd_new, ssd_conv_new = ssd_scan(proj_ssd.reshape(nb, seq, N_SSD_PROJ), ssd_conv,
                                            w["ssd_conv_w"], w["ssd_conv_b"], p)
        o, gla_new = gla_scan(proj_gla.reshape(nb, seq, N_GLA_PROJ), p)
    else:
        xc2, ssd_conv_new = short_conv(proj_ssd, seq, [COL_XBC], CONV_DIM, [ssd_conv], [w["ssd_conv_w"]],
                                       [w["ssd_conv_b"]], SSD_CONV, False, F32, "ssd_conv")
        u, ssd_new, other_ssd = ssd_step(xc2, proj_ssd, ssd_state, p, seq, hosted(w_ssd_in))
        o, gla_new, other_gla = gla_step(proj_gla, gla_state, p, seq, hosted(w_gla_in))
    merged = merge_branches(u, o, w["w_ssd_out"], w["w_gla_out"], gates)
    h, hn = mm_res_norm(merged, w["w_mix_out"], x2, w["norm_cross"], True, BF16, "mix_out")
    return h, hn, ssd_conv_new, ssd_new, gla_new, (other_ssd, other_gla)


def _conv_ffn(h2, hn2, ffn_conv, w, nb, seq, long_seq, rider=None):
    ntok, d = h2.shape
    ffn = w["w_down"].shape[0]
    cw, cbias = w["ffn_conv_w"], w["ffn_conv_b"]
    rider_out = None
    if long_seq:
        act, ffn_conv_new, rider_out = ffn_up_conv_act(hn2.reshape(nb, seq, d), w["w_up"], cw, cbias, ffn_conv, rider)
    else:
        up = matmul(hn2, w["w_up"], name="ffn_up")
        act, fa, fg = short_conv(up, seq, [0, ffn], ffn, [ffn_conv[:, :, :ffn], ffn_conv[:, :, ffn:]],
                                 [cw[:, :ffn], cw[:, ffn:]], [cbias[:ffn], cbias[ffn:]],
                                 FFN_CONV, True, BF16, "ffn_conv")
        ffn_conv_new = jnp.concatenate([fa, fg], axis=-1)
    y = mm_res_norm(act.reshape(ntok, ffn), w["w_down"], h2, w["norm_final"], False, F32, "ffn_down")
    return y.reshape(nb, seq, d), ffn_conv_new, rider_out


def kernel(x_prompt, x_sample, cache_mem_k, cache_mem_v, state_ssd_conv, state_ssd, state_gla, state_ffn_conv, mem_prompt, norm_mix, w_in, ssd_conv_w, ssd_conv_b, ssd_dt_bias, ssd_A_log, ssd_D, ssd_norm, w_ssd_out, w_gla_gate, b_gla_gate, gla_norm, w_gla_out, w_mix_out, norm_cross, norm_mem, w_cq, w_ck, w_cv, w_co, norm_ffn, w_up, ffn_conv_w, ffn_conv_b, w_down, norm_final):
    nb, seq, d = x_prompt.shape
    n_mem = mem_prompt.shape[1]
    ffn2 = w_up.shape[1]
    w = dict(
        norm_mix=norm_mix, norm_cross=norm_cross, norm_ffn=norm_ffn, norm_final=norm_final,
        w_in=_split_w_in(w_in), ssd_conv_w=ssd_conv_w, ssd_conv_b=ssd_conv_b,
        w_ssd_out=w_ssd_out.astype(BF16), w_gla_out=w_gla_out.astype(BF16), w_mix_out=w_mix_out.astype(BF16),
        w_cq=w_cq.astype(BF16), w_co=w_co.astype(BF16),
        ffn_conv_w=ffn_conv_w, ffn_conv_b=ffn_conv_b,
    )
    p = _params(ssd_dt_bias, ssd_A_log, ssd_D, ssd_norm, w_gla_gate, b_gla_gate, gla_norm)

    mn = rmsnorm_cast(mem_prompt.reshape(nb * n_mem, d), norm_mem)
    p_mem_k, p_mem_v, mem_k_heads, mem_v_heads = mem_kv(mn, w_ck.astype(BF16), w_cv.astype(BF16))
    p_mem_k = p_mem_k.reshape(nb, n_mem, d)
    p_mem_v = p_mem_v.reshape(nb, n_mem, d)
    zeros_ssd_conv = jnp.zeros((nb, SSD_CONV - 1, CONV_DIM), F32)
    zeros_ffn_conv = jnp.zeros((nb, FFN_CONV - 1, ffn2), F32)
    ns, sseq, _ = x_sample.shape
    xn_prompt = rmsnorm_cast(x_prompt.reshape(nb * seq, d), norm_mix)
    xn_sample = rmsnorm_cast(x_sample.reshape(ns * sseq, d), norm_mix)
    w["w_up"], w["w_down"], gates_p = cast_two_hosting_matmul(
        w_up, w_down, HostedMatmul(xn_prompt, w["w_in"][2]), "cast_w_ffn")

    h_s, hn_s, s_ssd_conv, s_ssd, s_gla, (proj_ssd_p, proj_gla_p) = _mixers(
        x_sample, xn_sample, state_ssd_conv, state_ssd, state_gla, w, p, False, other_xn=xn_prompt)

    h_p, hn_p, p_ssd_conv, p_ssd, p_gla, _ = _mixers(
        x_prompt, xn_prompt, zeros_ssd_conv, None, None, w, p, True, proj_ssd=proj_ssd_p, proj_gla=proj_gla_p,
        gates=gates_p)
    h2_p, hn2_p = cross_block(hn_p, h_p, p_mem_k, p_mem_v, w["w_cq"], w["w_co"], norm_ffn, seq)

    attend_s = cross_attend_rider(matmul(hn_s, w["w_cq"], name="cross_q"), cache_mem_k, cache_mem_v, ns, sseq)
    y_prompt, p_ffn_conv, (att_s,) = _conv_ffn(h2_p, hn2_p, zeros_ffn_conv, w, nb, seq, True, rider=attend_s)

    h2_s, hn2_s = mm_res_norm(att_s.reshape(ns * sseq, d), w["w_co"], h_s, norm_ffn, True, BF16, "cross_out")
    y_sample, s_ffn_conv, _ = _conv_ffn(h2_s, hn2_s, state_ffn_conv, w, ns, sseq, False)

    head_shape = (n_mem, CROSS_HEADS, CROSS_HEAD_DIM)
    return (y_prompt, y_sample, p_ssd_conv, p_ssd, p_gla, p_ffn_conv,
            mem_k_heads.reshape((nb,) + head_shape), mem_v_heads.reshape((nb,) + head_shape),
            s_ssd_conv, s_ssd, s_gla, s_ffn_conv)
```
